```python
import math
import jax
import jax.numpy as jnp
from jax import lax
import numpy as np

D_MODEL = 1024
BATCH = 8
SEQ = 8192
DEPTH = 4

N_AB = (DEPTH + 1) // 2
N_CD = DEPTH // 2

SSD_HEADS = 8
SSD_HEAD_DIM = 64
SSD_D_INNER = SSD_HEADS * SSD_HEAD_DIM
SSD_GROUPS = 2
SSD_D_STATE = 64
SSD_CONV = 4
SSD_CHUNK = 128
SSD_CONV_DIM = SSD_D_INNER + 2 * SSD_GROUPS * SSD_D_STATE

HG_HEADS = 4
HG_KEY_DIM = 128
HG_VAL_DIM = 128
HG_WIDTH = HG_HEADS * HG_KEY_DIM
HG_CHUNK = 64

SWA_Q_HEADS = 8
SWA_KV_HEADS = 2
SWA_HEAD_DIM = 64
SWA_WINDOW = 128
SWA_BLOCK = 128

RG_WIDTH = 512
RG_BLOCKS = 8
RG_BLOCK_DIM = RG_WIDTH // RG_BLOCKS
RG_CONV = 4
RG_C = 8.0

FFN_DIM = 2816
FFN_CONV = 3

LN_EPS = 1e-5
RMS_EPS = 1e-6
MASK_VALUE = -1e9
ALPHA = (2 * DEPTH) ** 0.25
BETA = (8 * DEPTH) ** -0.25

AB_SIZES = (SSD_D_INNER, SSD_CONV_DIM, SSD_HEADS, HG_WIDTH, HG_WIDTH, HG_HEADS * HG_VAL_DIM, HG_HEADS * HG_VAL_DIM)
AB_IN = sum(AB_SIZES)
AB_OUT_IN = SSD_D_INNER + HG_HEADS * HG_VAL_DIM
CD_SIZES = (SWA_Q_HEADS * SWA_HEAD_DIM, SWA_KV_HEADS * SWA_HEAD_DIM, SWA_KV_HEADS * SWA_HEAD_DIM, RG_WIDTH, RG_WIDTH)
CD_IN = sum(CD_SIZES)
CD_OUT_IN = SWA_Q_HEADS * SWA_HEAD_DIM + RG_WIDTH

kernel_name = 'hybrid_ssd_hgrn2_swa_rglru_deepnorm'


def _layer_norm(x, g, b):
    xf = x.astype(jnp.float32)
    mu = jnp.mean(xf, -1, keepdims=True)
    var = jnp.mean(jnp.square(xf - mu), -1, keepdims=True)
    return ((xf - mu) * lax.rsqrt(var + LN_EPS) * g.astype(jnp.float32) + b.astype(jnp.float32)).astype(x.dtype)


def _rms_norm(x, w):
    xf = x.astype(jnp.float32)
    return xf * lax.rsqrt(jnp.mean(jnp.square(xf), -1, keepdims=True) + RMS_EPS) * w.astype(jnp.float32)


def _split(h, sizes):
    return jnp.split(h, np.cumsum(sizes)[:-1].tolist(), axis=-1)


def _causal_dwconv(x, w, b):
    width = w.shape[0]
    y = lax.conv_general_dilated(x, w[:, None, :].astype(x.dtype), window_strides=(1,),
                                 padding=((width - 1, 0),), dimension_numbers=('NWC', 'WIO', 'NWC'),
                                 feature_group_count=x.shape[-1])
    return y + b.astype(x.dtype)


def _masked_exp(diff, mask):
    return jnp.where(mask, jnp.exp(jnp.where(mask, diff, 0.0)), 0.0)


def _decay_matrix(cs):
    t = cs.shape[-1]
    mask = jnp.tril(jnp.ones((t, t), dtype=bool))
    return _masked_exp(cs[..., :, None] - cs[..., None, :], mask)


def _ssd_scan(x, dt, a, bm, cm):
    b, s, h, p = x.shape
    g, n = bm.shape[-2:]
    L = SSD_CHUNK
    nc = s // L
    rep = h // g
    bh = jnp.repeat(bm, rep, axis=2).reshape(b, nc, L, h, n)
    ch = jnp.repeat(cm, rep, axis=2).reshape(b, nc, L, h, n)
    xc = (x * dt[..., None]).reshape(b, nc, L, h, p)
    a_cs = jnp.cumsum((dt * a).reshape(b, nc, L, h).transpose(0, 3, 1, 2), axis=-1)
    scores = jnp.einsum('bclhn,bcshn->bhcls', ch, bh) * _decay_matrix(a_cs)
    y_diag = jnp.einsum('bhcls,bcshp->bclhp', scores, xc)
    decay_states = jnp.exp(a_cs[..., -1:] - a_cs)
    states = jnp.einsum('bclhn,bhcl,bclhp->bchpn', bh, decay_states, xc)
    states = jnp.concatenate([jnp.zeros_like(states[:, :1]), states], axis=1)
    chunk_cs = jnp.cumsum(jnp.pad(a_cs[..., -1], ((0, 0), (0, 0), (1, 0))), axis=-1)
    states = jnp.einsum('bhzc,bchpn->bzhpn', _decay_matrix(chunk_cs), states)[:, :-1]
    y_off = jnp.einsum('bclhn,bchpn,bhcl->bclhp', ch, states, jnp.exp(a_cs))
    return (y_diag + y_off).reshape(b, s, h, p)


def _ssd_mixer(z, xbc, dt_raw, conv_w, conv_b, dt_bias, a_log, d_skip, norm_w):
    bsz, seq, _ = z.shape
    xbc = jax.nn.silu(_causal_dwconv(xbc, conv_w, conv_b)).astype(jnp.float32)
    xs, bm, cm = _split(xbc, (SSD_D_INNER, SSD_GROUPS * SSD_D_STATE, SSD_GROUPS * SSD_D_STATE))
    xs = xs.reshape(bsz, seq, SSD_HEADS, SSD_HEAD_DIM)
    bm = bm.reshape(bsz, seq, SSD_GROUPS, SSD_D_STATE)
    cm = cm.reshape(bsz, seq, SSD_GROUPS, SSD_D_STATE)
    dt = jax.nn.softplus(dt_raw.astype(jnp.float32) + dt_bias.astype(jnp.float32))
    a = -jnp.exp(a_log.astype(jnp.float32))
    y = _ssd_scan(xs, dt, a, bm, cm) + d_skip.astype(jnp.float32)[:, None] * xs
    y = y.reshape(bsz, seq, SSD_D_INNER) * jax.nn.silu(z.astype(jnp.float32))
    y = _rms_norm(y.reshape(bsz, seq, SSD_GROUPS, -1), norm_w.reshape(SSD_GROUPS, -1))
    return y.reshape(bsz, seq, SSD_D_INNER)


def _hgrn2_scan(q, k, v, log_f):
    b, s, h, dk = q.shape
    dv = v.shape[-1]
    L = HG_CHUNK
    nc = s // L
    to_chunks = lambda t: t.reshape(b, nc, L, h, t.shape[-1]).transpose(1, 0, 3, 2, 4)
    mask = jnp.tril(jnp.ones((L, L), dtype=bool))[:, :, None]

    def step(state, inp):
        qc, kc, vc, gc = inp
        bc = jnp.cumsum(gc, axis=2)
        decay = _masked_exp(bc[:, :, :, None, :] - bc[:, :, None, :, :], mask)
        attn = jnp.einsum('bhtk,bhsk,bhtsk->bhts', qc, kc, decay)
        out = jnp.einsum('bhts,bhsv->bhtv', attn, vc) + jnp.einsum('bhtk,bhkv->bhtv', qc * jnp.exp(bc), state)
        b_last = bc[:, :, -1:, :]
        state = jnp.exp(b_last[:, :, 0])[..., None] * state + jnp.einsum('bhsk,bhsv->bhkv', kc * jnp.exp(b_last - bc), vc)
        return state, out

    state0 = jnp.zeros((b, h, dk, dv), jnp.float32)
    _, out = lax.scan(step, state0, (to_chunks(q), to_chunks(k), to_chunks(v), to_chunks(log_f)))
    return out.transpose(1, 0, 3, 2, 4).reshape(b, s, h, dv)


def _hgrn2_mixer(hq, hf, hi, hg, lb, norm_w):
    bsz, seq, _ = hq.shape
    q = jax.nn.silu(hq.astype(jnp.float32)).reshape(bsz, seq, HG_HEADS, HG_KEY_DIM)
    fx = hf.astype(jnp.float32).reshape(bsz, seq, HG_HEADS, HG_KEY_DIM)
    lb = lb.reshape(HG_HEADS, HG_KEY_DIM)
    log_f = jnp.log(lb + (1.0 - lb) * jax.nn.sigmoid(fx))
    k = (1.0 - lb) * jax.nn.sigmoid(-fx)
    v = hi.astype(jnp.float32).reshape(bsz, seq, HG_HEADS, HG_VAL_DIM)
    o = _hgrn2_scan(q, k, v, log_f)
    o = _rms_norm(o, norm_w) * jax.nn.silu(hg.astype(jnp.float32).reshape(bsz, seq, HG_HEADS, HG_VAL_DIM))
    return o.reshape(bsz, seq, HG_HEADS * HG_VAL_DIM)


def _swa_sink_attention(q, k, v, sinks):
    b, s, hq, d = q.shape
    hkv = k.shape[2]
    grp = hq // hkv
    T = SWA_BLOCK
    nb = s // T
    qb = q.reshape(b, nb, T, hkv, grp, d)

    def banded(t):
        tb = t.reshape(b, nb, T, hkv, d)
        prev = jnp.pad(tb, ((0, 0), (1, 0), (0, 0), (0, 0), (0, 0)))[:, :-1]
        return jnp.concatenate([prev, tb], axis=2)

    kb, vb = banded(k), banded(v)
    scores = jnp.einsum('bnqhgd,bnkhd->bnhgqk', qb, kb).astype(jnp.float32) * (d ** -0.5)
    rel = (jnp.arange(T)[:, None] + T) - jnp.arange(2 * T)[None, :]
    band = (rel >= 0) & (rel < SWA_WINDOW)
    valid = (jnp.arange(nb)[:, None] > 0) | (jnp.arange(2 * T)[None, :] >= T)
    mask = band[None] & valid[:, None, :]
    scores = jnp.where(mask[None, :, None, None], scores, MASK_VALUE)
    sink = sinks.astype(jnp.float32).reshape(hkv, grp)
    sink_col = jnp.broadcast_to(sink[None, None, :, :, None, None], scores.shape[:-1] + (1,))
    probs = jax.nn.softmax(jnp.concatenate([scores, sink_col], axis=-1), axis=-1)[..., :-1]
    out = jnp.einsum('bnhgqk,bnkhd->bnqhgd', probs.astype(v.dtype), vb)
    return out.reshape(b, s, hq * d)


def _rglru_mixer(gate, xr, conv_w, conv_b, wa, ba, wx, bx, lam):
    bsz, seq, _ = xr.shape
    xc = _causal_dwconv(xr, conv_w, conv_b).astype(jnp.float32)
    xblk = xc.reshape(bsz, seq, RG_BLOCKS, RG_BLOCK_DIM)
    r = jax.nn.sigmoid(jnp.einsum('bsgi,gij->bsgj', xblk, wa.astype(jnp.float32)).reshape(bsz, seq, RG_WIDTH) + ba.astype(jnp.float32))
    i = jax.nn.sigmoid(jnp.einsum('bsgi,gij->bsgj', xblk, wx.astype(jnp.float32)).reshape(bsz, seq, RG_WIDTH) + bx.astype(jnp.float32))
    log_a = -RG_C * r * jax.nn.softplus(-lam.astype(jnp.float32))
    a = jnp.exp(log_a)
    u = jnp.sqrt(jnp.maximum(-jnp.expm1(2.0 * log_a), 0.0)) * (i * xc)

    def combine(c1, c2):
        a1, u1 = c1
        a2, u2 = c2
        return a1 * a2, a2 * u1 + u2

    _, h = lax.associative_scan(combine, (a, u), axis=1)
    return h * jax.nn.gelu(gate.astype(jnp.float32))


def _ab_mixer(x, w_in, conv_w, conv_b, dt_bias, a_log, d_skip, ssd_norm_w, lb, hg_norm_w, w_out):
    h = x @ w_in
    z, xbc, dt_raw, hq, hf, hi, hg = _split(h, AB_SIZES)
    y_a = _ssd_mixer(z, xbc, dt_raw, conv_w, conv_b, dt_bias, a_log, d_skip, ssd_norm_w)
    y_b = _hgrn2_mixer(hq, hf, hi, hg, lb, hg_norm_w)
    return jnp.concatenate([y_a, y_b], axis=-1).astype(x.dtype) @ w_out


def _cd_mixer(x, w_in, sinks, conv_w, conv_b, wa, ba, wx, bx, lam, w_out):
    bsz, seq, _ = x.shape
    h = x @ w_in
    q, k, v, gate, xr = _split(h, CD_SIZES)
    y_c = _swa_sink_attention(q.reshape(bsz, seq, SWA_Q_HEADS, SWA_HEAD_DIM),
                              k.reshape(bsz, seq, SWA_KV_HEADS, SWA_HEAD_DIM),
                              v.reshape(bsz, seq, SWA_KV_HEADS, SWA_HEAD_DIM), sinks)
    y_d = _rglru_mixer(gate, xr, conv_w, conv_b, wa, ba, wx, bx, lam)
    return jnp.concatenate([y_c.astype(jnp.float32), y_d], axis=-1).astype(x.dtype) @ w_out


def _conv_ffn(x, w_up, conv_w, conv_b, w_down):
    h = _causal_dwconv(x @ w_up, conv_w, conv_b)
    g, u = jnp.split(h, 2, axis=-1)
    return (jax.nn.silu(g) * u) @ w_down


def _fwd_setup_inputs(seed: int = 0) -> dict:
    key = jax.random.key(seed)
    keys = list(jax.random.split(key, 40))

    def nrm(shape, scale):
        return jax.random.normal(keys.pop(), shape, jnp.float32) * scale

    def unif(shape, lo, hi):
        return jax.random.uniform(keys.pop(), shape, jnp.float32, minval=lo, maxval=hi)

    x = nrm((BATCH, SEQ, D_MODEL), 1.0)
    ab_w_in = nrm((N_AB, D_MODEL, AB_IN), D_MODEL ** -0.5)
    ssd_conv_w = nrm((N_AB, SSD_CONV, SSD_CONV_DIM), SSD_CONV ** -0.5)
    ssd_conv_b = nrm((N_AB, SSD_CONV_DIM), 0.02)
    dt0 = jnp.exp(unif((N_AB, SSD_HEADS), math.log(1e-3), math.log(1e-1)))
    ssd_dt_bias = dt0 + jnp.log(-jnp.expm1(-dt0))
    ssd_a_log = jnp.log(unif((N_AB, SSD_HEADS), 1.0, 16.0))
    ssd_d = 1.0 + nrm((N_AB, SSD_HEADS), 0.1)
    ssd_norm_w = 1.0 + nrm((N_AB, SSD_D_INNER), 0.1)
    hg_lower = nrm((N_AB, HG_WIDTH), 1.0)
    hg_norm_w = 1.0 + nrm((N_AB, HG_VAL_DIM), 0.1)
    ab_w_out = nrm((N_AB, AB_OUT_IN, D_MODEL), (AB_OUT_IN ** -0.5) * BETA)
    cd_w_in = nrm((N_CD, D_MODEL, CD_IN), D_MODEL ** -0.5)
    swa_sinks = nrm((N_CD, SWA_Q_HEADS), 0.5)
    rg_conv_w = nrm((N_CD, RG_CONV, RG_WIDTH), RG_CONV ** -0.5)
    rg_conv_b = nrm((N_CD, RG_WIDTH), 0.02)
    rg_wa = nrm((N_CD, RG_BLOCKS, RG_BLOCK_DIM, RG_BLOCK_DIM), RG_BLOCK_DIM ** -0.5)
    rg_ba = nrm((N_CD, RG_WIDTH), 0.02)
    rg_wx = nrm((N_CD, RG_BLOCKS, RG_BLOCK_DIM, RG_BLOCK_DIM), RG_BLOCK_DIM ** -0.5)
    rg_bx = nrm((N_CD, RG_WIDTH), 0.02)
    sig = unif((N_CD, RG_WIDTH), 0.9, 0.999) ** (1.0 / RG_C)
    rg_lambda = jnp.log(sig) - jnp.log1p(-sig)
    cd_w_out = nrm((N_CD, CD_OUT_IN, D_MODEL), (CD_OUT_IN ** -0.5) * BETA)
    ffn_w_up = nrm((DEPTH, D_MODEL, 2 * FFN_DIM), D_MODEL ** -0.5)
    ffn_conv_w = nrm((DEPTH, FFN_CONV, 2 * FFN_DIM), FFN_CONV ** -0.5)
    ffn_conv_b = nrm((DEPTH, 2 * FFN_DIM), 0.02)
    ffn_w_down = nrm((DEPTH, FFN_DIM, D_MODEL), (FFN_DIM ** -0.5) * BETA)
    ln_g = 1.0 + nrm((DEPTH, 2, D_MODEL), 0.05)
    ln_b = nrm((DEPTH, 2, D_MODEL), 0.02)
    return {'x': x, 'ab_w_in': ab_w_in, 'ssd_conv_w': ssd_conv_w, 'ssd_conv_b': ssd_conv_b,
            'ssd_dt_bias': ssd_dt_bias, 'ssd_a_log': ssd_a_log, 'ssd_d': ssd_d, 'ssd_norm_w': ssd_norm_w,
            'hg_lower': hg_lower, 'hg_norm_w': hg_norm_w, 'ab_w_out': ab_w_out, 'cd_w_in': cd_w_in,
            'swa_sinks': swa_sinks, 'rg_conv_w': rg_conv_w, 'rg_conv_b': rg_conv_b, 'rg_wa': rg_wa,
            'rg_ba': rg_ba, 'rg_wx': rg_wx, 'rg_bx': rg_bx, 'rg_lambda': rg_lambda, 'cd_w_out': cd_w_out,
            'ffn_w_up': ffn_w_up, 'ffn_conv_w': ffn_conv_w, 'ffn_conv_b': ffn_conv_b, 'ffn_w_down': ffn_w_down,
            'ln_g': ln_g, 'ln_b': ln_b}


def _fwd_reference(x, ab_w_in, ssd_conv_w, ssd_conv_b, ssd_dt_bias, ssd_a_log, ssd_d, ssd_norm_w,
              hg_lower, hg_norm_w, ab_w_out, cd_w_in, swa_sinks, rg_conv_w, rg_conv_b, rg_wa,
              rg_ba, rg_wx, rg_bx, rg_lambda, cd_w_out, ffn_w_up, ffn_conv_w, ffn_conv_b, ffn_w_down,
              ln_g, ln_b):
    sm = jax.nn.softmax(hg_lower.astype(jnp.float32), axis=0)
    lb_all = jnp.clip(jnp.cumsum(sm, axis=0) - sm[0], 0.0, 1.0)
    for layer in range(DEPTH):
        j = layer // 2
        if layer % 2 == 0:
            m = _ab_mixer(x, ab_w_in[j], ssd_conv_w[j], ssd_conv_b[j], ssd_dt_bias[j], ssd_a_log[j],
                          ssd_d[j], ssd_norm_w[j], lb_all[j], hg_norm_w[j], ab_w_out[j])
        else:
            m = _cd_mixer(x, cd_w_in[j], swa_sinks[j], rg_conv_w[j], rg_conv_b[j], rg_wa[j], rg_ba[j],
                          rg_wx[j], rg_bx[j], rg_lambda[j], cd_w_out[j])
        x = _layer_norm(ALPHA * x + m, ln_g[layer, 0], ln_b[layer, 0])
        f = _conv_ffn(x, ffn_w_up[layer], ffn_conv_w[layer], ffn_conv_b[layer], ffn_w_down[layer])
        x = _layer_norm(ALPHA * x + f, ln_g[layer, 1], ln_b[layer, 1])
    return x


import jax as _jax
import jax.numpy as _jnp

TWIN_FORMAT = 'train_step'
FWD_PARAMS = ['x', 'ab_w_in', 'ssd_conv_w', 'ssd_conv_b', 'ssd_dt_bias', 'ssd_a_log', 'ssd_d', 'ssd_norm_w', 'hg_lower', 'hg_norm_w', 'ab_w_out', 'cd_w_in', 'swa_sinks', 'rg_conv_w', 'rg_conv_b', 'rg_wa', 'rg_ba', 'rg_wx', 'rg_bx', 'rg_lambda', 'cd_w_out', 'ffn_w_up', 'ffn_conv_w', 'ffn_conv_b', 'ffn_w_down', 'ln_g', 'ln_b']
TWIN_WEIGHTS = ['ab_w_in', 'ssd_conv_w', 'ssd_conv_b', 'ssd_dt_bias', 'ssd_a_log', 'ssd_d', 'ssd_norm_w', 'hg_lower', 'hg_norm_w', 'ab_w_out', 'cd_w_in', 'swa_sinks', 'rg_conv_w', 'rg_conv_b', 'rg_wa', 'rg_ba', 'rg_wx', 'rg_bx', 'rg_lambda', 'cd_w_out', 'ffn_w_up', 'ffn_conv_w', 'ffn_conv_b', 'ffn_w_down', 'ln_g', 'ln_b']
TWIN_DIFF_INPUT = 'x'
TWIN_INPUTS = ['x', 'ab_w_in', 'ssd_conv_w', 'ssd_conv_b', 'ssd_dt_bias', 'ssd_a_log', 'ssd_d', 'ssd_norm_w', 'hg_lower', 'hg_norm_w', 'ab_w_out', 'cd_w_in', 'swa_sinks', 'rg_conv_w', 'rg_conv_b', 'rg_wa', 'rg_ba', 'rg_wx', 'rg_bx', 'rg_lambda', 'cd_w_out', 'ffn_w_up', 'ffn_conv_w', 'ffn_conv_b', 'ffn_w_down', 'ln_g', 'ln_b', 'loss_target', 'm_ab_w_in', 'm_ssd_conv_w', 'm_ssd_conv_b', 'm_ssd_dt_bias', 'm_ssd_a_log', 'm_ssd_d', 'm_ssd_norm_w', 'm_hg_lower', 'm_hg_norm_w', 'm_ab_w_out', 'm_cd_w_in', 'm_swa_sinks', 'm_rg_conv_w', 'm_rg_conv_b', 'm_rg_wa', 'm_rg_ba', 'm_rg_wx', 'm_rg_bx', 'm_rg_lambda', 'm_cd_w_out', 'm_ffn_w_up', 'm_ffn_conv_w', 'm_ffn_conv_b', 'm_ffn_w_down', 'm_ln_g', 'm_ln_b', 'v_ab_w_in', 'v_ssd_conv_w', 'v_ssd_conv_b', 'v_ssd_dt_bias', 'v_ssd_a_log', 'v_ssd_d', 'v_ssd_norm_w', 'v_hg_lower', 'v_hg_norm_w', 'v_ab_w_out', 'v_cd_w_in', 'v_swa_sinks', 'v_rg_conv_w', 'v_rg_conv_b', 'v_rg_wa', 'v_rg_ba', 'v_rg_wx', 'v_rg_bx', 'v_rg_lambda', 'v_cd_w_out', 'v_ffn_w_up', 'v_ffn_conv_w', 'v_ffn_conv_b', 'v_ffn_w_down', 'v_ln_g', 'v_ln_b']
TWIN_OUTPUTS = ['loss', 'grad_x', 'grad_ab_w_in', 'grad_ssd_conv_w', 'grad_ssd_conv_b', 'grad_ssd_dt_bias', 'grad_ssd_a_log', 'grad_ssd_d', 'grad_ssd_norm_w', 'grad_hg_lower', 'grad_hg_norm_w', 'grad_ab_w_out', 'grad_cd_w_in', 'grad_swa_sinks', 'grad_rg_conv_w', 'grad_rg_conv_b', 'grad_rg_wa', 'grad_rg_ba', 'grad_rg_wx', 'grad_rg_bx', 'grad_rg_lambda', 'grad_cd_w_out', 'grad_ffn_w_up', 'grad_ffn_conv_w', 'grad_ffn_conv_b', 'grad_ffn_w_down', 'grad_ln_g', 'grad_ln_b', 'delta_ab_w_in', 'delta_ssd_conv_w', 'delta_ssd_conv_b', 'delta_ssd_dt_bias', 'delta_ssd_a_log', 'delta_ssd_d', 'delta_ssd_norm_w', 'delta_hg_lower', 'delta_hg_norm_w', 'delta_ab_w_out', 'delta_cd_w_in', 'delta_swa_sinks', 'delta_rg_conv_w', 'delta_rg_conv_b', 'delta_rg_wa', 'delta_rg_ba', 'delta_rg_wx', 'delta_rg_bx', 'delta_rg_lambda', 'delta_cd_w_out', 'delta_ffn_w_up', 'delta_ffn_conv_w', 'delta_ffn_conv_b', 'delta_ffn_w_down', 'delta_ln_g', 'delta_ln_b', 'new_m_ab_w_in', 'new_m_ssd_conv_w', 'new_m_ssd_conv_b', 'new_m_ssd_dt_bias', 'new_m_ssd_a_log', 'new_m_ssd_d', 'new_m_ssd_norm_w', 'new_m_hg_lower', 'new_m_hg_norm_w', 'new_m_ab_w_out', 'new_m_cd_w_in', 'new_m_swa_sinks', 'new_m_rg_conv_w', 'new_m_rg_conv_b', 'new_m_rg_wa', 'new_m_rg_ba', 'new_m_rg_wx', 'new_m_rg_bx', 'new_m_rg_lambda', 'new_m_cd_w_out', 'new_m_ffn_w_up', 'new_m_ffn_conv_w', 'new_m_ffn_conv_b', 'new_m_ffn_w_down', 'new_m_ln_g', 'new_m_ln_b', 'new_v_ab_w_in', 'new_v_ssd_conv_w', 'new_v_ssd_conv_b', 'new_v_ssd_dt_bias', 'new_v_ssd_a_log', 'new_v_ssd_d', 'new_v_ssd_norm_w', 'new_v_hg_lower', 'new_v_hg_norm_w', 'new_v_ab_w_out', 'new_v_cd_w_in', 'new_v_swa_sinks', 'new_v_rg_conv_w', 'new_v_rg_conv_b', 'new_v_rg_wa', 'new_v_rg_ba', 'new_v_rg_wx', 'new_v_rg_bx', 'new_v_rg_lambda', 'new_v_cd_w_out', 'new_v_ffn_w_up', 'new_v_ffn_conv_w', 'new_v_ffn_conv_b', 'new_v_ffn_w_down', 'new_v_ln_g', 'new_v_ln_b']
TWIN_LEAF_KINDS = {'loss': 'loss', 'grad_x': 'grad_x', 'grad_ab_w_in': 'grad_w', 'grad_ssd_conv_w': 'grad_w', 'grad_ssd_conv_b': 'grad_w', 'grad_ssd_dt_bias': 'grad_w', 'grad_ssd_a_log': 'grad_w', 'grad_ssd_d': 'grad_w', 'grad_ssd_norm_w': 'grad_w', 'grad_hg_lower': 'grad_w', 'grad_hg_norm_w': 'grad_w', 'grad_ab_w_out': 'grad_w', 'grad_cd_w_in': 'grad_w', 'grad_swa_sinks': 'grad_w', 'grad_rg_conv_w': 'grad_w', 'grad_rg_conv_b': 'grad_w', 'grad_rg_wa': 'grad_w', 'grad_rg_ba': 'grad_w', 'grad_rg_wx': 'grad_w', 'grad_rg_bx': 'grad_w', 'grad_rg_lambda': 'grad_w', 'grad_cd_w_out': 'grad_w', 'grad_ffn_w_up': 'grad_w', 'grad_ffn_conv_w': 'grad_w', 'grad_ffn_conv_b': 'grad_w', 'grad_ffn_w_down': 'grad_w', 'grad_ln_g': 'grad_w', 'grad_ln_b': 'grad_w', 'delta_ab_w_in': 'delta_w', 'delta_ssd_conv_w': 'delta_w', 'delta_ssd_conv_b': 'delta_w', 'delta_ssd_dt_bias': 'delta_w', 'delta_ssd_a_log': 'delta_w', 'delta_ssd_d': 'delta_w', 'delta_ssd_norm_w': 'delta_w', 'delta_hg_lower': 'delta_w', 'delta_hg_norm_w': 'delta_w', 'delta_ab_w_out': 'delta_w', 'delta_cd_w_in': 'delta_w', 'delta_swa_sinks': 'delta_w', 'delta_rg_conv_w': 'delta_w', 'delta_rg_conv_b': 'delta_w', 'delta_rg_wa': 'delta_w', 'delta_rg_ba': 'delta_w', 'delta_rg_wx': 'delta_w', 'delta_rg_bx': 'delta_w', 'delta_rg_lambda': 'delta_w', 'delta_cd_w_out': 'delta_w', 'delta_ffn_w_up': 'delta_w', 'delta_ffn_conv_w': 'delta_w', 'delta_ffn_conv_b': 'delta_w', 'delta_ffn_w_down': 'delta_w', 'delta_ln_g': 'delta_w', 'delta_ln_b': 'delta_w', 'new_m_ab_w_in': 'new_m', 'new_m_ssd_conv_w': 'new_m', 'new_m_ssd_conv_b': 'new_m', 'new_m_ssd_dt_bias': 'new_m', 'new_m_ssd_a_log': 'new_m', 'new_m_ssd_d': 'new_m', 'new_m_ssd_norm_w': 'new_m', 'new_m_hg_lower': 'new_m', 'new_m_hg_norm_w': 'new_m', 'new_m_ab_w_out': 'new_m', 'new_m_cd_w_in': 'new_m', 'new_m_swa_sinks': 'new_m', 'new_m_rg_conv_w': 'new_m', 'new_m_rg_conv_b': 'new_m', 'new_m_rg_wa': 'new_m', 'new_m_rg_ba': 'new_m', 'new_m_rg_wx': 'new_m', 'new_m_rg_bx': 'new_m', 'new_m_rg_lambda': 'new_m', 'new_m_cd_w_out': 'new_m', 'new_m_ffn_w_up': 'new_m', 'new_m_ffn_conv_w': 'new_m', 'new_m_ffn_conv_b': 'new_m', 'new_m_ffn_w_down': 'new_m', 'new_m_ln_g': 'new_m', 'new_m_ln_b': 'new_m', 'new_v_ab_w_in': 'new_v', 'new_v_ssd_conv_w': 'new_v', 'new_v_ssd_conv_b': 'new_v', 'new_v_ssd_dt_bias': 'new_v', 'new_v_ssd_a_log': 'new_v', 'new_v_ssd_d': 'new_v', 'new_v_ssd_norm_w': 'new_v', 'new_v_hg_lower': 'new_v', 'new_v_hg_norm_w': 'new_v', 'new_v_ab_w_out': 'new_v', 'new_v_cd_w_in': 'new_v', 'new_v_swa_sinks': 'new_v', 'new_v_rg_conv_w': 'new_v', 'new_v_rg_conv_b': 'new_v', 'new_v_rg_wa': 'new_v', 'new_v_rg_ba': 'new_v', 'new_v_rg_wx': 'new_v', 'new_v_rg_bx': 'new_v', 'new_v_rg_lambda': 'new_v', 'new_v_cd_w_out': 'new_v', 'new_v_ffn_w_up': 'new_v', 'new_v_ffn_conv_w': 'new_v', 'new_v_ffn_conv_b': 'new_v', 'new_v_ffn_w_down': 'new_v', 'new_v_ln_g': 'new_v', 'new_v_ln_b': 'new_v'}


def _forward(args):
    return _fwd_reference(*[args[k] for k in FWD_PARAMS])


def _output_shape():
    def fwd():
        inp = _fwd_setup_inputs(0)
        return _fwd_reference(*[inp[k] for k in FWD_PARAMS])
    out = _jax.eval_shape(fwd)
    return out.shape, out.dtype

N_MICROBATCH = 1
ADAM_LR = 0.001
ADAM_B1 = 0.9
ADAM_B2 = 0.999
ADAM_EPS = 1e-08
ADAM_WD = 0.01
ADAM_STEP = 10
PER_EXAMPLE_BATCH_AXIS = {'x': 0, 'loss_target': 0}
SHARED_INPUTS = []
_WEIGHT_DTYPES = {'ab_w_in': _jnp.float32, 'ssd_conv_w': _jnp.float32, 'ssd_conv_b': _jnp.float32, 'ssd_dt_bias': _jnp.float32, 'ssd_a_log': _jnp.float32, 'ssd_d': _jnp.float32, 'ssd_norm_w': _jnp.float32, 'hg_lower': _jnp.float32, 'hg_norm_w': _jnp.float32, 'ab_w_out': _jnp.float32, 'cd_w_in': _jnp.float32, 'swa_sinks': _jnp.float32, 'rg_conv_w': _jnp.float32, 'rg_conv_b': _jnp.float32, 'rg_wa': _jnp.float32, 'rg_ba': _jnp.float32, 'rg_wx': _jnp.float32, 'rg_bx': _jnp.float32, 'rg_lambda': _jnp.float32, 'cd_w_out': _jnp.float32, 'ffn_w_up': _jnp.float32, 'ffn_conv_w': _jnp.float32, 'ffn_conv_b': _jnp.float32, 'ffn_w_down': _jnp.float32, 'ln_g': _jnp.float32, 'ln_b': _jnp.float32}
MOMENT_SCALE = {'ab_w_in': 4.445715e-02, 'ssd_conv_w': 6.267118e-02, 'ssd_conv_b': 1.127790e-01, 'ssd_dt_bias': 1.184958e-01, 'ssd_a_log': 3.055848e-01, 'ssd_d': 4.218663e-01, 'ssd_norm_w': 7.317810e-02, 'hg_lower': 3.009889e-03, 'hg_norm_w': 8.169608e-02, 'ab_w_out': 1.432267e-01, 'cd_w_in': 2.908382e-02, 'swa_sinks': 1.090458e-02, 'rg_conv_w': 5.046870e-02, 'rg_conv_b': 6.431503e-01, 'rg_wa': 2.154773e-02, 'rg_ba': 1.411056e-02, 'rg_wx': 4.019275e-02, 'rg_bx': 1.714320e-02, 'rg_lambda': 2.567135e-02, 'cd_w_out': 8.206033e-02, 'ffn_w_up': 2.418280e-02, 'ffn_conv_w': 2.406518e-02, 'ffn_conv_b': 3.245757e-02, 'ffn_w_down': 9.397831e-02, 'ln_g': 2.401519e+01, 'ln_b': 1.953987e+00}


def _to_microbatches(a, axis):
    t = _jnp.moveaxis(a, axis, 0)
    t = t.reshape((N_MICROBATCH, t.shape[0] // N_MICROBATCH) + t.shape[1:])
    return _jnp.moveaxis(t, 1, axis + 1)


def setup_inputs(seed: int = 0) -> dict:
    inp = _fwd_setup_inputs(seed)
    key = _jax.random.fold_in(_jax.random.key(seed), 7919)
    shape, _ = _output_shape()
    out = dict(inp)
    out["loss_target"] = _jax.random.normal(_jax.random.fold_in(key, 0), shape, _jnp.float32)
    for i, name in enumerate(TWIN_WEIGHTS):
        w = inp[name].astype(_jnp.float32)
        if MOMENT_SCALE is None:
            s = _jnp.sqrt(_jnp.mean(_jnp.square(w)) + 1e-30)
        else:
            s = MOMENT_SCALE[name]
        km, kv = _jax.random.split(_jax.random.fold_in(key, i + 1))
        out[name] = w
        out["m_" + name] = s * _jax.random.normal(km, w.shape, _jnp.float32)
        out["v_" + name] = (s * s) * _jax.random.uniform(kv, w.shape, _jnp.float32, 0.5, 1.5)
    if N_MICROBATCH > 1:
        for name, axis in PER_EXAMPLE_BATCH_AXIS.items():
            out[name] = _to_microbatches(out[name], axis)
    return {'x': out['x'], 'ab_w_in': out['ab_w_in'], 'ssd_conv_w': out['ssd_conv_w'], 'ssd_conv_b': out['ssd_conv_b'], 'ssd_dt_bias': out['ssd_dt_bias'], 'ssd_a_log': out['ssd_a_log'], 'ssd_d': out['ssd_d'], 'ssd_norm_w': out['ssd_norm_w'], 'hg_lower': out['hg_lower'], 'hg_norm_w': out['hg_norm_w'], 'ab_w_out': out['ab_w_out'], 'cd_w_in': out['cd_w_in'], 'swa_sinks': out['swa_sinks'], 'rg_conv_w': out['rg_conv_w'], 'rg_conv_b': out['rg_conv_b'], 'rg_wa': out['rg_wa'], 'rg_ba': out['rg_ba'], 'rg_wx': out['rg_wx'], 'rg_bx': out['rg_bx'], 'rg_lambda': out['rg_lambda'], 'cd_w_out': out['cd_w_out'], 'ffn_w_up': out['ffn_w_up'], 'ffn_conv_w': out['ffn_conv_w'], 'ffn_conv_b': out['ffn_conv_b'], 'ffn_w_down': out['ffn_w_down'], 'ln_g': out['ln_g'], 'ln_b': out['ln_b'], 'loss_target': out['loss_target'], 'm_ab_w_in': out['m_ab_w_in'], 'm_ssd_conv_w': out['m_ssd_conv_w'], 'm_ssd_conv_b': out['m_ssd_conv_b'], 'm_ssd_dt_bias': out['m_ssd_dt_bias'], 'm_ssd_a_log': out['m_ssd_a_log'], 'm_ssd_d': out['m_ssd_d'], 'm_ssd_norm_w': out['m_ssd_norm_w'], 'm_hg_lower': out['m_hg_lower'], 'm_hg_norm_w': out['m_hg_norm_w'], 'm_ab_w_out': out['m_ab_w_out'], 'm_cd_w_in': out['m_cd_w_in'], 'm_swa_sinks': out['m_swa_sinks'], 'm_rg_conv_w': out['m_rg_conv_w'], 'm_rg_conv_b': out['m_rg_conv_b'], 'm_rg_wa': out['m_rg_wa'], 'm_rg_ba': out['m_rg_ba'], 'm_rg_wx': out['m_rg_wx'], 'm_rg_bx': out['m_rg_bx'], 'm_rg_lambda': out['m_rg_lambda'], 'm_cd_w_out': out['m_cd_w_out'], 'm_ffn_w_up': out['m_ffn_w_up'], 'm_ffn_conv_w': out['m_ffn_conv_w'], 'm_ffn_conv_b': out['m_ffn_conv_b'], 'm_ffn_w_down': out['m_ffn_w_down'], 'm_ln_g': out['m_ln_g'], 'm_ln_b': out['m_ln_b'], 'v_ab_w_in': out['v_ab_w_in'], 'v_ssd_conv_w': out['v_ssd_conv_w'], 'v_ssd_conv_b': out['v_ssd_conv_b'], 'v_ssd_dt_bias': out['v_ssd_dt_bias'], 'v_ssd_a_log': out['v_ssd_a_log'], 'v_ssd_d': out['v_ssd_d'], 'v_ssd_norm_w': out['v_ssd_norm_w'], 'v_hg_lower': out['v_hg_lower'], 'v_hg_norm_w': out['v_hg_norm_w'], 'v_ab_w_out': out['v_ab_w_out'], 'v_cd_w_in': out['v_cd_w_in'], 'v_swa_sinks': out['v_swa_sinks'], 'v_rg_conv_w': out['v_rg_conv_w'], 'v_rg_conv_b': out['v_rg_conv_b'], 'v_rg_wa': out['v_rg_wa'], 'v_rg_ba': out['v_rg_ba'], 'v_rg_wx': out['v_rg_wx'], 'v_rg_bx': out['v_rg_bx'], 'v_rg_lambda': out['v_rg_lambda'], 'v_cd_w_out': out['v_cd_w_out'], 'v_ffn_w_up': out['v_ffn_w_up'], 'v_ffn_conv_w': out['v_ffn_conv_w'], 'v_ffn_conv_b': out['v_ffn_conv_b'], 'v_ffn_w_down': out['v_ffn_w_down'], 'v_ln_g': out['v_ln_g'], 'v_ln_b': out['v_ln_b']}


def _loss(weights, diff, rest, loss_target):
    with _jax.named_scope("forward"):
        args = {**rest, TWIN_DIFF_INPUT: diff, **{k: w.astype(_WEIGHT_DTYPES[k]) for k, w in weights.items()}}
        y = _forward(args)
    with _jax.named_scope("loss_head"):
        err = _jnp.square(y.astype(_jnp.float32) - loss_target)
        return 0.5 * _jnp.sum(_jnp.mean(err, axis=-1)) if err.ndim else 0.5 * err


def _adamw(w, g, m, v):
    m = ADAM_B1 * m + (1.0 - ADAM_B1) * g
    v = ADAM_B2 * v + (1.0 - ADAM_B2) * _jnp.square(g)
    m_hat = m / (1.0 - ADAM_B1 ** ADAM_STEP)
    v_hat = v / (1.0 - ADAM_B2 ** ADAM_STEP)
    delta = -ADAM_LR * (m_hat / (_jnp.sqrt(v_hat) + ADAM_EPS) + ADAM_WD * w)
    return delta, m, v


def reference(x, ab_w_in, ssd_conv_w, ssd_conv_b, ssd_dt_bias, ssd_a_log, ssd_d, ssd_norm_w, hg_lower, hg_norm_w, ab_w_out, cd_w_in, swa_sinks, rg_conv_w, rg_conv_b, rg_wa, rg_ba, rg_wx, rg_bx, rg_lambda, cd_w_out, ffn_w_up, ffn_conv_w, ffn_conv_b, ffn_w_down, ln_g, ln_b, loss_target, m_ab_w_in, m_ssd_conv_w, m_ssd_conv_b, m_ssd_dt_bias, m_ssd_a_log, m_ssd_d, m_ssd_norm_w, m_hg_lower, m_hg_norm_w, m_ab_w_out, m_cd_w_in, m_swa_sinks, m_rg_conv_w, m_rg_conv_b, m_rg_wa, m_rg_ba, m_rg_wx, m_rg_bx, m_rg_lambda, m_cd_w_out, m_ffn_w_up, m_ffn_conv_w, m_ffn_conv_b, m_ffn_w_down, m_ln_g, m_ln_b, v_ab_w_in, v_ssd_conv_w, v_ssd_conv_b, v_ssd_dt_bias, v_ssd_a_log, v_ssd_d, v_ssd_norm_w, v_hg_lower, v_hg_norm_w, v_ab_w_out, v_cd_w_in, v_swa_sinks, v_rg_conv_w, v_rg_conv_b, v_rg_wa, v_rg_ba, v_rg_wx, v_rg_bx, v_rg_lambda, v_cd_w_out, v_ffn_w_up, v_ffn_conv_w, v_ffn_conv_b, v_ffn_w_down, v_ln_g, v_ln_b):
    given = dict(x=x, ab_w_in=ab_w_in, ssd_conv_w=ssd_conv_w, ssd_conv_b=ssd_conv_b, ssd_dt_bias=ssd_dt_bias, ssd_a_log=ssd_a_log, ssd_d=ssd_d, ssd_norm_w=ssd_norm_w, hg_lower=hg_lower, hg_norm_w=hg_norm_w, ab_w_out=ab_w_out, cd_w_in=cd_w_in, swa_sinks=swa_sinks, rg_conv_w=rg_conv_w, rg_conv_b=rg_conv_b, rg_wa=rg_wa, rg_ba=rg_ba, rg_wx=rg_wx, rg_bx=rg_bx, rg_lambda=rg_lambda, cd_w_out=cd_w_out, ffn_w_up=ffn_w_up, ffn_conv_w=ffn_conv_w, ffn_conv_b=ffn_conv_b, ffn_w_down=ffn_w_down, ln_g=ln_g, ln_b=ln_b, loss_target=loss_target, m_ab_w_in=m_ab_w_in, m_ssd_conv_w=m_ssd_conv_w, m_ssd_conv_b=m_ssd_conv_b, m_ssd_dt_bias=m_ssd_dt_bias, m_ssd_a_log=m_ssd_a_log, m_ssd_d=m_ssd_d, m_ssd_norm_w=m_ssd_norm_w, m_hg_lower=m_hg_lower, m_hg_norm_w=m_hg_norm_w, m_ab_w_out=m_ab_w_out, m_cd_w_in=m_cd_w_in, m_swa_sinks=m_swa_sinks, m_rg_conv_w=m_rg_conv_w, m_rg_conv_b=m_rg_conv_b, m_rg_wa=m_rg_wa, m_rg_ba=m_rg_ba, m_rg_wx=m_rg_wx, m_rg_bx=m_rg_bx, m_rg_lambda=m_rg_lambda, m_cd_w_out=m_cd_w_out, m_ffn_w_up=m_ffn_w_up, m_ffn_conv_w=m_ffn_conv_w, m_ffn_conv_b=m_ffn_conv_b, m_ffn_w_down=m_ffn_w_down, m_ln_g=m_ln_g, m_ln_b=m_ln_b, v_ab_w_in=v_ab_w_in, v_ssd_conv_w=v_ssd_conv_w, v_ssd_conv_b=v_ssd_conv_b, v_ssd_dt_bias=v_ssd_dt_bias, v_ssd_a_log=v_ssd_a_log, v_ssd_d=v_ssd_d, v_ssd_norm_w=v_ssd_norm_w, v_hg_lower=v_hg_lower, v_hg_norm_w=v_hg_norm_w, v_ab_w_out=v_ab_w_out, v_cd_w_in=v_cd_w_in, v_swa_sinks=v_swa_sinks, v_rg_conv_w=v_rg_conv_w, v_rg_conv_b=v_rg_conv_b, v_rg_wa=v_rg_wa, v_rg_ba=v_rg_ba, v_rg_wx=v_rg_wx, v_rg_bx=v_rg_bx, v_rg_lambda=v_rg_lambda, v_cd_w_out=v_cd_w_out, v_ffn_w_up=v_ffn_w_up, v_ffn_conv_w=v_ffn_conv_w, v_ffn_conv_b=v_ffn_conv_b, v_ffn_w_down=v_ffn_w_down, v_ln_g=v_ln_g, v_ln_b=v_ln_b)
    weights = {n: given[n] for n in TWIN_WEIGHTS}
    shared = {n: given[n] for n in SHARED_INPUTS}
    per_example = {n: given[n] for n in ['x']}
    grad_fn = _jax.value_and_grad(_loss, argnums=(0, 1))

    def one_microbatch(ex, loss_target):
        ex = dict(ex)
        diff = ex.pop(TWIN_DIFF_INPUT)
        return grad_fn(weights, diff, {**shared, **ex}, loss_target)

    if N_MICROBATCH == 1:
        loss, (grad_w, grad_x) = one_microbatch(per_example, given["loss_target"])
    else:
        def body(carry, xs):
            loss_sum, grad_sum = carry
            l_k, (gw_k, gx_k) = one_microbatch(xs[0], xs[1])
            with _jax.named_scope("update"):
                return (loss_sum + l_k, _jax.tree.map(_jnp.add, grad_sum, gw_k)), gx_k

        init = (_jnp.zeros((), _jnp.float32), _jax.tree.map(_jnp.zeros_like, weights))
        (loss, grad_w), grad_x = _jax.lax.scan(body, init, (per_example, given["loss_target"]))
    with _jax.named_scope("update"):
        delta_w, new_m, new_v = {}, {}, {}
        for n in TWIN_WEIGHTS:
            delta_w[n], new_m[n], new_v[n] = _adamw(weights[n], grad_w[n], given["m_" + n], given["v_" + n])
    return (loss, grad_x, *[grad_w[n] for n in TWIN_WEIGHTS], *[delta_w[n] for n in TWIN_WEIGHTS],
            *[new_m[n] for n in TWIN_WEIGHTS], *[new_v[n] for n in TWIN_WEIGHTS])
```

```python
import functools
import math

import numpy as np
import jax
import jax.numpy as jnp
from jax import lax
from jax.experimental import pallas as pl
from jax.experimental.pallas import tpu as pltpu

F32 = jnp.float32
BF16 = jnp.bfloat16
HI = lax.Precision.HIGHEST
MM_DTYPE = BF16

DEPTH = 4
D_MODEL = 1024
N_DEV = 8
LN_EPS = 1e-5
RMS_EPS = 1e-6
MASK_VALUE = -1e9
ALPHA = (2 * DEPTH) ** 0.25
RG_C = 8.0
FFN_DIM = 2816
SSD_CHUNK = 128
HG_CHUNK = 64
SWA_BLOCK = 128
LANES = 128
VMEM_LIMIT = 56 * 1024 * 1024

ADAM_LR, ADAM_B1, ADAM_B2, ADAM_EPS, ADAM_WD, ADAM_STEP = 0.001, 0.9, 0.999, 1e-08, 0.01, 10

AB_Z, AB_DT, AB_HQ, AB_HF, AB_HI, AB_HG, AB_XBC, AB_PAD = 0, 512, 1024, 1536, 2048, 2560, 3072, 3840
CD_Q, CD_GATE, CD_XR, CD_K, CD_V, CD_IN = 0, 512, 1024, 1536, 1664, 1792


def _cols(x, lo, hi):
    n = x.shape[1]

    @jax.custom_vjp
    def f(x):
        return x[:, lo:hi]

    def bwd(_, g):
        parts = []
        if lo > 0:
            parts.append(jnp.zeros((g.shape[0], lo), g.dtype))
        parts.append(g)
        if hi < n:
            parts.append(jnp.zeros((g.shape[0], n - hi), g.dtype))
        return (jnp.concatenate(parts, axis=1) if len(parts) > 1 else g,)

    f.defvjp(lambda x: (f(x), None), bwd)
    return f(x)


def _rows(x, lo, hi):
    n = x.shape[0]

    @jax.custom_vjp
    def f(x):
        return x[lo:hi, :]

    def bwd(_, g):
        parts = []
        if lo > 0:
            parts.append(jnp.zeros((lo, g.shape[1]), g.dtype))
        parts.append(g)
        if hi < n:
            parts.append(jnp.zeros((n - hi, g.shape[1]), g.dtype))
        return (jnp.concatenate(parts, axis=0) if len(parts) > 1 else g,)

    f.defvjp(lambda x: (f(x), None), bwd)
    return f(x)


def _roll(x, shift, axis):
    n = x.shape[axis]
    shift = shift % n
    if shift == 0:
        return x

    @jax.custom_vjp
    def f(x):
        return pltpu.roll(x, shift, axis)

    f.defvjp(lambda x: (f(x), None), lambda _, g: (pltpu.roll(g, n - shift, axis),))
    return f(x)


def _dot(a, b, precision=None):
    return lax.dot_general(a, b, (((1,), (0,)), ((), ())), precision=precision, preferred_element_type=F32)


def _dot_nt(a, b, precision=None):
    return lax.dot_general(a, b, (((1,), (1,)), ((), ())), precision=precision, preferred_element_type=F32)


def _dot_tn(a, b, precision=None):
    return lax.dot_general(a, b, (((0,), (0,)), ((), ())), precision=precision, preferred_element_type=F32)


def _sigmoid(x):
    return 1.0 / (1.0 + jnp.exp(-x))


def _silu(x):
    return x * _sigmoid(x)


def _softplus(x):
    return jnp.maximum(x, 0.0) + jnp.log(1.0 + jnp.exp(-jnp.abs(x)))


def _gelu_tanh(x):
    c = math.sqrt(2.0 / math.pi)
    return 0.5 * x * (1.0 + jnp.tanh(c * (x + 0.044715 * (x * x * x))))


def _iota(shape, axis):
    return lax.broadcasted_iota(jnp.int32, shape, axis)


def _lane_mask(lo, hi, width=LANES):
    lane = _iota((1, width), 1)
    return ((lane >= lo) & (lane < hi)).astype(F32)


def _cparams(sem):
    return pltpu.CompilerParams(dimension_semantics=sem, vmem_limit_bytes=VMEM_LIMIT)


def _chunk_fwd(name, fn, grid, params, xs, outs, state_shapes):
    n_g, n_c = grid
    n_p, n_x, n_o, n_s = len(params), len(xs), len(outs), len(state_shapes)

    def body(*refs):
        p_refs, x_refs = refs[:n_p], refs[n_p:n_p + n_x]
        o_refs = refs[n_p + n_x:n_p + n_x + n_o]
        sv_refs = refs[n_p + n_x + n_o:n_p + n_x + n_o + n_s]
        st_refs = refs[n_p + n_x + n_o + n_s:]
        c = pl.program_id(1)

        @pl.when(c == 0)
        def _():
            for s in st_refs:
                s[...] = jnp.zeros(s.shape, s.dtype)

        st = [s[...] for s in st_refs]
        ys, new_st = fn(c, [p[...] for p in p_refs], [x[...] for x in x_refs], st)
        for o, y in zip(o_refs, ys):
            o[...] = y.astype(o.dtype)
        for sv, s in zip(sv_refs, st):
            sv[0, 0] = s
        for s_ref, s in zip(st_refs, new_st):
            s_ref[...] = s

    in_specs = [pl.BlockSpec(b, m) for _, b, m in params] + [pl.BlockSpec(b, m) for _, b, m in xs]
    out_specs = [pl.BlockSpec(b, m) for _, b, m in outs]
    out_shape = [jax.ShapeDtypeStruct(s, F32) for s, _, _ in outs]
    for shp in state_shapes:
        out_specs.append(pl.BlockSpec((1, 1) + shp, lambda g, c: (g, c) + (0,) * len(shp)))
        out_shape.append(jax.ShapeDtypeStruct((n_g, n_c) + shp, F32))
    res = pl.pallas_call(
        body, name=name, grid=grid, in_specs=in_specs, out_specs=out_specs, out_shape=out_shape,
        scratch_shapes=[pltpu.VMEM(shp, F32) for shp in state_shapes],
        compiler_params=_cparams(("arbitrary", "arbitrary")),
    )(*[a for a, _, _ in params], *[a for a, _, _ in xs])
    return list(res[:n_o]), list(res[n_o:])


def _chunk_bwd(name, fn, grid, params, xs, saved, dys, state_shapes):
    n_g, n_c = grid
    n_p, n_x, n_s, n_y = len(params), len(xs), len(state_shapes), len(dys)

    def rev(m):
        return lambda g, c: m(g, n_c - 1 - c)

    def body(*refs):
        i = 0
        p_refs = refs[i:i + n_p]; i += n_p
        x_refs = refs[i:i + n_x]; i += n_x
        sv_refs = refs[i:i + n_s]; i += n_s
        dy_refs = refs[i:i + n_y]; i += n_y
        dp_refs = refs[i:i + n_p]; i += n_p
        dx_refs = refs[i:i + n_x]; i += n_x
        ds_refs = refs[i:]
        c = pl.program_id(1)
        chunk = n_c - 1 - c

        @pl.when(c == 0)
        def _():
            for s in ds_refs:
                s[...] = jnp.zeros(s.shape, s.dtype)
            for d in dp_refs:
                d[...] = jnp.zeros(d.shape, d.dtype)

        pv = [p[...] for p in p_refs]
        xv = [x[...] for x in x_refs]
        sv = [s[0, 0] for s in sv_refs]
        _, vjp = jax.vjp(lambda p, x, s: fn(chunk, p, x, s), pv, xv, sv)
        dp, dx, ds = vjp(([d[...] for d in dy_refs], [s[...] for s in ds_refs]))
        for r, v in zip(dp_refs, dp):
            r[...] += v
        for r, v in zip(dx_refs, dx):
            r[...] = v
        for r, v in zip(ds_refs, ds):
            r[...] = v

    in_specs = [pl.BlockSpec(b, rev(m)) for _, b, m in params] + [pl.BlockSpec(b, rev(m)) for _, b, m in xs]
    for shp in state_shapes:
        in_specs.append(pl.BlockSpec((1, 1) + shp, lambda g, c, n=len(shp): (g, n_c - 1 - c) + (0,) * n))
    in_specs += [pl.BlockSpec(b, rev(m)) for _, b, m in dys]
    out_specs = [pl.BlockSpec(b, rev(m)) for _, b, m in params] + [pl.BlockSpec(b, rev(m)) for _, b, m in xs]
    out_shape = [jax.ShapeDtypeStruct(a.shape, F32) for a, _, _ in params]
    out_shape += [jax.ShapeDtypeStruct(a.shape, F32) for a, _, _ in xs]
    res = pl.pallas_call(
        body, name=name, grid=grid, in_specs=in_specs, out_specs=out_specs, out_shape=out_shape,
        scratch_shapes=[pltpu.VMEM(shp, F32) for shp in state_shapes],
        compiler_params=_cparams(("arbitrary", "arbitrary")),
    )(*[a for a, _, _ in params], *[a for a, _, _ in xs], *saved, *[a for a, _, _ in dys])
    return list(res[:n_p]), list(res[n_p:])


class _Op:
    def __init__(self, name, fn, grid, params, xs, outs, state_shapes=()):
        self.name, self.fn, self.grid = name, fn, grid
        self.params, self.xs, self.outs, self.state_shapes = params, xs, outs, list(state_shapes)
        self.ys, self.saved = _chunk_fwd(name + "_fwd", fn, grid, params, xs, outs, self.state_shapes)

    def bwd(self, dys):
        dy_defs = [(d, b, m) for d, (_, b, m) in zip(dys, self.outs)]
        return _chunk_bwd(self.name + "_bwd", self.fn, self.grid, self.params, self.xs, self.saved, dy_defs,
                          self.state_shapes)


def _whole(a):
    nd = a.ndim
    return (a, a.shape, lambda g, c: (0,) * nd)


def _pick(n, prefs):
    for p in prefs:
        if n % p == 0:
            return p
    return n


def _matmul(name, a, b, mode, add=None):
    if mode == "nn":
        (m, k), n = a.shape, b.shape[1]
    elif mode == "nt":
        (m, k), n = a.shape, b.shape[0]
    else:
        (k, m), n = a.shape, b.shape[1]
    bm = _pick(m, (512, 256, 128))
    bn = _pick(n, (1408, 768, 512, 896, 640, 384, 256, 128))
    bk = _pick(k, (1024, 1408, 512, 256, 128))
    n_k = k // bk
    dims = {"nn": (((1,), (0,)), ((), ())), "nt": (((1,), (1,)), ((), ())), "tn": (((0,), (0,)), ((), ()))}[mode]

    def body(*refs):
        if add is None:
            a_ref, b_ref, o_ref, acc = refs
        else:
            a_ref, b_ref, c_ref, o_ref, acc = refs
        kk = pl.program_id(2)

        @pl.when(kk == 0)
        def _():
            acc[...] = jnp.zeros(acc.shape, F32)

        acc[...] += lax.dot_general(a_ref[...].astype(MM_DTYPE), b_ref[...].astype(MM_DTYPE), dims,
                                    preferred_element_type=F32)

        @pl.when(kk == n_k - 1)
        def _():
            r = acc[...]
            if add is not None:
                r = r + c_ref[...]
            o_ref[...] = r

    if mode == "nn":
        a_spec = pl.BlockSpec((bm, bk), lambda i, j, kk: (i, kk))
        b_spec = pl.BlockSpec((bk, bn), lambda i, j, kk: (kk, j))
    elif mode == "nt":
        a_spec = pl.BlockSpec((bm, bk), lambda i, j, kk: (i, kk))
        b_spec = pl.BlockSpec((bn, bk), lambda i, j, kk: (j, kk))
    else:
        a_spec = pl.BlockSpec((bk, bm), lambda i, j, kk: (kk, i))
        b_spec = pl.BlockSpec((bk, bn), lambda i, j, kk: (kk, j))
    in_specs = [a_spec, b_spec]
    args = [a, b]
    if add is not None:
        in_specs.append(pl.BlockSpec((bm, bn), lambda i, j, kk: (i, j)))
        args.append(add)
    return pl.pallas_call(
        body, name=name, grid=(m // bm, n // bn, n_k), in_specs=in_specs,
        out_specs=pl.BlockSpec((bm, bn), lambda i, j, kk: (i, j)),
        out_shape=jax.ShapeDtypeStruct((m, n), F32),
        scratch_shapes=[pltpu.VMEM((bm, bn), F32)],
        compiler_params=_cparams(("arbitrary", "arbitrary", "arbitrary")),
    )(*args)


def _ln_res_fn(_, p, x, st):
    g, b = p
    xin, m = x
    pre = ALPHA * xin + m
    mu = jnp.mean(pre, -1, keepdims=True)
    d = pre - mu
    var = jnp.mean(d * d, -1, keepdims=True)
    return [d * lax.rsqrt(var + LN_EPS) * g + b], []


def _make_conv_fn(taps, act):
    def fn(_, p, x, st):
        ws, b = p[:taps], p[taps]
        (xin,), (prev,) = x, st
        n = xin.shape[0]
        ext = jnp.concatenate([prev, xin], axis=0)
        y = b
        for k in range(taps):
            y = y + ws[k] * _rows(_roll(ext, taps - 1 - k, 0), 8, 8 + n)
        if act:
            y = _silu(y)
        return [y], [_rows(xin, n - 8, n)]

    return fn


def _ffn_act_fn(_, p, x, st):
    wg, wu, bg, bu = p[0:3], p[3:6], p[6], p[7]
    (hg, hu), (pg, pu) = x, st
    n = hg.shape[0]
    eg = jnp.concatenate([pg, hg], axis=0)
    eu = jnp.concatenate([pu, hu], axis=0)
    g, u = bg, bu
    for k in range(3):
        g = g + wg[k] * _rows(_roll(eg, 2 - k, 0), 8, 8 + n)
        u = u + wu[k] * _rows(_roll(eu, 2 - k, 0), 8, 8 + n)
    return [_silu(g) * u], [_rows(hg, n - 8, n), _rows(hu, n - 8, n)]


def _ssd_fn(_, p, x, st):
    dtb, alog, dsk = p
    xbc, dtr = x
    L = SSD_CHUNK
    tril = _iota((L, L), 0) >= _iota((L, L), 1)
    trilf = tril.astype(F32)
    xs, bm, cm = _cols(xbc, 0, 512), _cols(xbc, 512, 640), _cols(xbc, 640, 768)
    dt = _softplus(dtr + dtb)
    da = dt * (-jnp.exp(alog))
    cs = _dot(trilf, da, HI)
    tot = jnp.sum(da, axis=0, keepdims=True)
    xc = xs * dt
    xdec = xc * jnp.exp(tot - cs)
    ecs = jnp.exp(cs)
    etot = jnp.exp(tot)
    ys, new_st = [], []
    for pr in range(4):
        lo, hi = LANES * pr, LANES * (pr + 1)
        grp = pr // 2
        c_g = cm * _lane_mask(64 * grp, 64 * grp + 64)
        gmat = _dot_nt(c_g, bm)
        cs_p, xc_p = _cols(cs, lo, hi), _cols(xc, lo, hi)
        yd = jnp.zeros((L, LANES), F32)
        for half in range(2):
            sel = (_iota((LANES, L), 0) == 64 * half).astype(F32)
            col = _dot(cs_p, sel, HI)
            diff = col - col.T
            dec = jnp.where(tril, jnp.exp(jnp.where(tril, diff, 0.0)), 0.0)
            yd = yd + _dot(gmat * dec, xc_p) * _lane_mask(64 * half, 64 * half + 64)
        s_in = st[pr]
        y_off = _dot(c_g, s_in) * _cols(ecs, lo, hi)
        ys.append(yd + y_off + _cols(dsk, lo, hi) * _cols(xs, lo, hi))
        new_st.append(s_in * _cols(etot, lo, hi) + _dot_tn(bm, _cols(xdec, lo, hi)))
    return [jnp.concatenate(ys, axis=1)], new_st


def _hg_fn(_, p, x, st):
    (lb,) = p
    hq, hf, hi = x
    (s_t,) = st
    L = HG_CHUNK
    q = _silu(hq)
    logf = jnp.log(lb + (1.0 - lb) * _sigmoid(hf))
    k = (1.0 - lb) * _sigmoid(-hf)
    ti, si = _iota((L, L), 0), _iota((L, L), 1)
    bc = _dot((ti >= si).astype(F32), logf, HI)
    tot = jnp.sum(logf, axis=0, keepdims=True)
    attn = jnp.where(ti == si, _dot_nt(q, k), 0.0)
    t1 = _iota((L, 1), 0)
    for lvl in range(6):
        blk = L >> lvl
        half = blk // 2
        pos = t1 & (blk - 1)
        upper = pos >= half
        piv = t1 - pos + half
        bcp = _dot((si == piv).astype(F32), bc, HI)
        qq = jnp.where(upper, q * jnp.exp(jnp.where(upper, bc - bcp, 0.0)), 0.0)
        kk = jnp.where(upper, 0.0, k * jnp.exp(jnp.where(upper, 0.0, bcp - bc)))
        same = (ti >> (6 - lvl)) == (si >> (6 - lvl))
        attn = attn + jnp.where(same, _dot_nt(qq, kk), 0.0)
    out = _dot(attn, hi) + _dot_nt(q * jnp.exp(bc), s_t)
    new_s = s_t * jnp.exp(tot) + _dot_tn(hi, k * jnp.exp(tot - bc))
    return [out], [new_s]


def _swa_fn(chunk, p, x, st):
    (sinks,) = p
    q, k, v = x
    kp, vp = st
    T = SWA_BLOCK
    kc = jnp.concatenate([kp, k], axis=0)
    vc = jnp.concatenate([vp, v], axis=0)
    qi, kj = _iota((T, 2 * T), 0), _iota((T, 2 * T), 1)
    rel = qi + T - kj
    mask = (rel >= 0) & (rel < T) & ((kj >= T) | (chunk > 0))
    srow = _iota((8, LANES), 0)
    outs = []
    for pr in range(4):
        grp = pr // 2
        gm = _lane_mask(64 * grp, 64 * grp + 64)
        km, vm = kc * gm, vc * gm
        q2 = _cols(q, LANES * pr, LANES * (pr + 1))
        o2 = jnp.zeros((T, LANES), F32)
        for half in range(2):
            hm = _lane_mask(64 * half, 64 * half + 64)
            qh = q2 * hm
            if half != grp:
                qh = _roll(qh, 64, 1)
            s = _dot_nt(qh, km) * 0.125
            s = jnp.where(mask, s, MASK_VALUE)
            sink = jnp.mean(jnp.sum(jnp.where(srow == 2 * pr + half, sinks, 0.0), axis=0, keepdims=True),
                            axis=-1, keepdims=True)
            mx = lax.stop_gradient(jnp.maximum(jnp.max(s, axis=-1, keepdims=True), sink))
            e = jnp.exp(s - mx)
            den = jnp.sum(e, axis=-1, keepdims=True) + jnp.exp(sink - mx)
            o = _dot(e / den, vm)
            if half != grp:
                o = _roll(o, 64, 1)
            o2 = o2 + o * hm
        outs.append(o2)
    return [jnp.concatenate(outs, axis=1)], [k, v]


def _rg_gate_fn(_, p, x, st):
    wa, ba, wx, bx, lam = p
    (xc,) = x
    r = _sigmoid(_dot(xc, wa) + ba)
    i = _sigmoid(_dot(xc, wx) + bx)
    log_a = -RG_C * r * _softplus(-lam)
    a = jnp.exp(log_a)
    t = jnp.tanh(log_a)
    one_minus_a2 = -2.0 * t / (1.0 - t)
    u = jnp.sqrt(jnp.maximum(one_minus_a2, 0.0)) * (i * xc)
    return [a, u], []


def _rg_scan_fn(_, p, x, st):
    a, u = x
    (prev,) = st
    n = a.shape[0]
    row = _iota((n, 1), 0)
    s = 1
    while s < n:
        keep = row >= s
        a_s, u_s = _roll(a, s, 0), _roll(u, s, 0)
        u = jnp.where(keep, a * u_s + u, u)
        a = jnp.where(keep, a * a_s, a)
        s *= 2
    h_in = jnp.sum(jnp.where(_iota((8, 1), 0) == 7, prev, 0.0), axis=0, keepdims=True)
    h = u + a * h_in
    return [h], [_rows(h, n - 8, n)]


def _ab_post_fn(_, p, x, st):
    nw_ssd, nw_hg = p
    y, z, o, hg = x
    lane = _iota((1, 512), 1)
    ya = y * _silu(z)
    sq = ya * ya
    inv = jnp.zeros_like(ya)
    for g in range(2):
        mk = (lane >= 256 * g) & (lane < 256 * (g + 1))
        ms = jnp.sum(jnp.where(mk, sq, 0.0), axis=-1, keepdims=True) / 256.0
        inv = jnp.where(mk, lax.rsqrt(ms + RMS_EPS), inv)
    ya = ya * inv * nw_ssd
    so = o * o
    inv = jnp.zeros_like(o)
    for h in range(4):
        mk = (lane >= 128 * h) & (lane < 128 * (h + 1))
        ms = jnp.sum(jnp.where(mk, so, 0.0), axis=-1, keepdims=True) / 128.0
        inv = jnp.where(mk, lax.rsqrt(ms + RMS_EPS), inv)
    yb = o * inv * nw_hg * _silu(hg)
    return [jnp.concatenate([ya, yb], axis=1)], []


def _cd_post_fn(_, p, x, st):
    yc, h, gate = x
    return [jnp.concatenate([yc, h * _gelu_tanh(gate)], axis=1)], []


def _lb_fn(_, p, x, st):
    l0, l1 = x
    mx = lax.stop_gradient(jnp.maximum(l0, l1))
    e0, e1 = jnp.exp(l0 - mx), jnp.exp(l1 - mx)
    s0, s1 = e0 / (e0 + e1), e1 / (e0 + e1)
    return [jnp.clip(s0 - s0, 0.0, 1.0), jnp.clip((s0 + s1) - s0, 0.0, 1.0)], []


def _loss_kernel(y, target):
    t, d = y.shape
    bt = _pick(t, (512, 256, 128))

    def body(y_ref, t_ref, dy_ref, l_ref):
        @pl.when(pl.program_id(0) == 0)
        def _():
            l_ref[...] = jnp.zeros(l_ref.shape, F32)

        e = y_ref[...] - t_ref[...]
        dy_ref[...] = e * (1.0 / d)
        l_ref[...] += jnp.sum(e * e, axis=0, keepdims=True) * (0.5 / d)

    dy, part = pl.pallas_call(
        body, name="loss", grid=(t // bt,),
        in_specs=[pl.BlockSpec((bt, d), lambda i: (i, 0)), pl.BlockSpec((bt, d), lambda i: (i, 0))],
        out_specs=[pl.BlockSpec((bt, d), lambda i: (i, 0)), pl.BlockSpec((1, d), lambda i: (0, 0))],
        out_shape=[jax.ShapeDtypeStruct((t, d), F32), jax.ShapeDtypeStruct((1, d), F32)],
        compiler_params=_cparams(("arbitrary",)),
    )(y, target)
    return dy, jnp.sum(part)


def _adamw_kernel(name, parts, w, m, v):
    r, c = w.shape
    br = _pick(r, (256, 128, 64, 32, 16, 8))
    c1 = 1.0 / (1.0 - ADAM_B1 ** ADAM_STEP)
    c2 = 1.0 / (1.0 - ADAM_B2 ** ADAM_STEP)

    def body(p_ref, w_ref, m_ref, v_ref, g_ref, d_ref, nm_ref, nv_ref):
        g = p_ref[0]
        for s in range(1, N_DEV):
            g = g + p_ref[s]
        w_, m_, v_ = w_ref[...], m_ref[...], v_ref[...]
        nm = ADAM_B1 * m_ + (1.0 - ADAM_B1) * g
        nv = ADAM_B2 * v_ + (1.0 - ADAM_B2) * (g * g)
        g_ref[...] = g
        nm_ref[...] = nm
        nv_ref[...] = nv
        d_ref[...] = -ADAM_LR * ((nm * c1) / (jnp.sqrt(nv * c2) + ADAM_EPS) + ADAM_WD * w_)

    row = pl.BlockSpec((br, c), lambda i: (i, 0))
    return pl.pallas_call(
        body, name=name, grid=(r // br,),
        in_specs=[pl.BlockSpec((N_DEV, br, c), lambda i: (0, i, 0)), row, row, row],
        out_specs=[row, row, row, row],
        out_shape=[jax.ShapeDtypeStruct((r, c), F32)] * 4,
        compiler_params=_cparams(("arbitrary",)),
    )(parts, w, m, v)


def _mesh_pos():
    return lax.axis_index("x"), lax.axis_index("y"), lax.axis_index("c")


def _all_gather(name, shard):
    r, c = shard.shape

    def body(x_ref, out_ref, send_sems, recv_sems, local_sem):
        x, y, cc = _mesh_pos()
        me, sibling = (x, y, cc), (x, y, 1 - cc)
        chips = [(1 - x, y), (x, 1 - y), (1 - x, 1 - y)]

        def slot(px, py, pc):
            return out_ref.at[4 * px + 2 * py + pc]

        def copy(k, block, to, src=None):
            return pltpu.make_async_remote_copy(
                src_ref=slot(*block) if src is None else src, dst_ref=slot(*block),
                send_sem=send_sems.at[k], recv_sem=recv_sems.at[k],
                device_id=to, device_id_type=pl.DeviceIdType.MESH)

        mine = pltpu.make_async_copy(x_ref, slot(*me), local_sem)
        mine.start()
        first = [copy(0, me, sibling, src=x_ref)]
        first += [copy(1 + j, me, (*chip, cc), src=x_ref) for j, chip in enumerate(chips)]
        for cp in first:
            cp.start()
        passed = [copy(4 + j, (*chip, cc), sibling) for j, chip in enumerate(chips)]
        for j, chip in enumerate(chips):
            copy(1 + j, (*chip, cc), me).wait_recv()
            passed[j].start()
        copy(0, sibling, me).wait_recv()
        for j, chip in enumerate(chips):
            copy(4 + j, (*chip, 1 - cc), me).wait_recv()
        for cp in first + passed:
            cp.wait_send()
        mine.wait()

    return pl.pallas_call(
        body, name=name,
        out_shape=jax.ShapeDtypeStruct((N_DEV, r, c), shard.dtype),
        in_specs=[pl.BlockSpec(memory_space=pl.ANY)],
        out_specs=pl.BlockSpec(memory_space=pl.ANY),
        scratch_shapes=[pltpu.SemaphoreType.DMA((7,)), pltpu.SemaphoreType.DMA((7,)), pltpu.SemaphoreType.DMA],
    )(shard)


def _all_to_all(name, parts):
    n, r, c = parts.shape

    def body(x_ref, out_ref, send_sems, recv_sems, local_sem):
        x, y, cc = _mesh_pos()
        me = 4 * x + 2 * y + cc
        mine = pltpu.make_async_copy(x_ref.at[me], out_ref.at[me], local_sem)
        mine.start()
        copies = []
        for k in range(1, N_DEV):
            px, py, pc = x ^ ((k >> 2) & 1), y ^ ((k >> 1) & 1), cc ^ (k & 1)
            peer = 4 * px + 2 * py + pc
            copies.append(pltpu.make_async_remote_copy(
                src_ref=x_ref.at[peer], dst_ref=out_ref.at[me],
                send_sem=send_sems.at[k - 1], recv_sem=recv_sems.at[k - 1],
                device_id=(px, py, pc), device_id_type=pl.DeviceIdType.MESH))
        for cp in copies:
            cp.start()
        for cp in copies:
            cp.wait_recv()
        for cp in copies:
            cp.wait_send()
        mine.wait()

    return pl.pallas_call(
        body, name=name,
        out_shape=jax.ShapeDtypeStruct((n, r, c), parts.dtype),
        in_specs=[pl.BlockSpec(memory_space=pl.ANY)],
        out_specs=pl.BlockSpec(memory_space=pl.ANY),
        scratch_shapes=[pltpu.SemaphoreType.DMA((7,)), pltpu.SemaphoreType.DMA((7,)), pltpu.SemaphoreType.DMA],
    )(parts)


PACK_W = 1024
ROW_ALIGN = 16

SHARDED = [("ab_w_in", 2), ("ab_w_out", 1), ("cd_w_in", 2), ("cd_w_out", 1), ("ffn_w_up", 2), ("ffn_w_down", 1),
           ("ssd_conv_w", 2), ("rg_conv_w", 2), ("rg_conv_b", 1), ("rg_ba", 1), ("rg_bx", 1), ("rg_lambda", 1),
           ("ffn_conv_w", 2), ("ln_g", 2), ("ln_b", 2)]
MATMUL_W = ("ab_w_in", "ab_w_out", "cd_w_in", "cd_w_out", "ffn_w_up", "ffn_w_down")
REPLICATED = ["ssd_conv_b", "ssd_dt_bias", "ssd_a_log", "ssd_d", "ssd_norm_w", "hg_lower", "hg_norm_w", "swa_sinks",
              "rg_wa", "rg_wx", "ffn_conv_b"]
WEIGHTS = ["ab_w_in", "ssd_conv_w", "ssd_conv_b", "ssd_dt_bias", "ssd_a_log", "ssd_d", "ssd_norm_w", "hg_lower",
           "hg_norm_w", "ab_w_out", "cd_w_in", "swa_sinks", "rg_conv_w", "rg_conv_b", "rg_wa", "rg_ba", "rg_wx",
           "rg_bx", "rg_lambda", "cd_w_out", "ffn_w_up", "ffn_conv_w", "ffn_conv_b", "ffn_w_down", "ln_g", "ln_b"]


def _rows_of(n_elems):
    r = -(-n_elems // PACK_W)
    return -(-r // ROW_ALIGN) * ROW_ALIGN


def _pack(arrs, dtype):
    out = []
    for a in arrs:
        flat = a.reshape(-1).astype(dtype)
        rows = _rows_of(flat.shape[0])
        out.append(jnp.pad(flat, (0, rows * PACK_W - flat.shape[0])).reshape(rows, PACK_W))
    return jnp.concatenate(out, axis=0)


def _unpack(buf, shapes):
    lead = buf.shape[:-2]
    out, r0 = [], 0
    for shp in shapes:
        n = int(np.prod(shp))
        rows = _rows_of(n)
        piece = buf[..., r0:r0 + rows, :].reshape(lead + (rows * PACK_W,))[..., :n]
        out.append(piece.reshape(lead + tuple(shp)))
        r0 += rows
    return out


def _merge_shards(g, axis):
    g = jnp.moveaxis(g, 0, axis)
    shp = g.shape
    return g.reshape(shp[:axis] + (shp[axis] * shp[axis + 1],) + shp[axis + 2:])


def _split_shards(full, axis):
    shp = full.shape
    g = full.reshape(shp[:axis] + (N_DEV, shp[axis] // N_DEV) + shp[axis + 1:])
    return jnp.moveaxis(g, axis, 0)


def _row_vec(v):
    return v.reshape(1, -1)


def _heads64(v):
    return jnp.repeat(v, 64).reshape(1, 512)


def _block_diag(w):
    eye = jnp.eye(8, dtype=w.dtype)
    return jnp.einsum("gij,gh->gihj", w, eye).reshape(512, 512)


def _block_diag_grad(d):
    return jnp.stack([d[64 * g:64 * g + 64, 64 * g:64 * g + 64] for g in range(8)])


def _ab_pad_weight(w):
    z, xbc, dt = w[:, 0:512], w[:, 512:1280], w[:, 1280:1288]
    hq, hf, hi, hg = w[:, 1288:1800], w[:, 1800:2312], w[:, 2312:2824], w[:, 2824:3336]
    return jnp.concatenate([z, jnp.repeat(dt, 64, axis=1), hq, hf, hi, hg, xbc], axis=1)


def _ab_unpad_grad(d):
    z, dt, hq, hf = d[:, 0:512], d[:, 512:1024], d[:, 1024:1536], d[:, 1536:2048]
    hi, hg, xbc = d[:, 2048:2560], d[:, 2560:3072], d[:, 3072:3840]
    return jnp.concatenate([z, xbc, dt.reshape(-1, 8, 64).sum(-1), hq, hf, hi, hg], axis=1)


def _cd_perm_weight(w):
    return jnp.concatenate([w[:, 0:512], w[:, 768:1792], w[:, 512:768]], axis=1)


def _cd_unperm_grad(d):
    return jnp.concatenate([d[:, 0:512], d[:, 1536:1792], d[:, 512:1536]], axis=1)


def _colblk(arr, width, first):
    return lambda rows: (arr, (rows, width), lambda g, c: (c, first + g))


def _local_step(x, target, w):
    t = x.shape[0]
    bt = _pick(t, (512, 256, 128))
    nb = t // bt
    bs = _pick(t, (256, 128))

    def rowop(name, fn, params, xs, widths_out):
        outs = [((t, wd), (bt, wd), lambda g, c: (c, 0)) for wd in widths_out]
        return _Op(name, fn, (1, nb), params, xs, outs)

    def rowblk(arr, width, first=0):
        return (arr, (bt, width), lambda g, c: (c, first))

    lb_op = _Op("hg_lb", _lb_fn, (1, 1), [],
                [(w["hg_lower"][0:1], (1, 512), lambda g, c: (0, 0)), (w["hg_lower"][1:2], (1, 512), lambda g, c: (0, 0))],
                [((1, 512), (1, 512), lambda g, c: (0, 0))] * 2)
    lb_all = lb_op.ys

    tape = []
    grads = {}

    def add_grad(name, idx, val):
        grads.setdefault(name, {})[idx] = val

    for layer in range(DEPTH):
        j = layer // 2
        rec = {"x_in": x}
        if layer % 2 == 0:
            w_in = _ab_pad_weight(w["ab_w_in"][j])
            h = _matmul(f"ab_in{j}", x, w_in, "nn")
            conv_p = [_row_vec(w["ssd_conv_w"][j, k]) for k in range(4)] + [_row_vec(w["ssd_conv_b"][j])]
            conv = _Op(f"ssd_conv{j}", _make_conv_fn(4, True), (3, nb),
                       [(a, (1, 256), lambda g, c: (0, g)) for a in conv_p],
                       [(h, (bt, 256), lambda g, c: (c, AB_XBC // 256 + g))],
                       [((t, 768), (bt, 256), lambda g, c: (c, g))], [(8, 256)])
            xbc = conv.ys[0]
            ssd = _Op(f"ssd{j}", _ssd_fn, (1, t // SSD_CHUNK),
                      [_whole(_heads64(w["ssd_dt_bias"][j])), _whole(_heads64(w["ssd_a_log"][j])),
                       _whole(_heads64(w["ssd_d"][j]))],
                      [(xbc, (SSD_CHUNK, 768), lambda g, c: (c, 0)),
                       (h, (SSD_CHUNK, 512), lambda g, c: (c, AB_DT // 512))],
                      [((t, 512), (SSD_CHUNK, 512), lambda g, c: (c, 0))], [(LANES, LANES)] * 4)
            hg = _Op(f"hg{j}", _hg_fn, (4, t // HG_CHUNK),
                     [(lb_all[j], (1, LANES), lambda g, c: (0, g))],
                     [(h, (HG_CHUNK, LANES), lambda g, c, o=o: (c, o // LANES + g)) for o in (AB_HQ, AB_HF, AB_HI)],
                     [((t, 512), (HG_CHUNK, LANES), lambda g, c: (c, g))], [(LANES, LANES)])
            post = rowop(f"ab_post{j}", _ab_post_fn,
                         [_whole(_row_vec(w["ssd_norm_w"][j])), _whole(jnp.tile(_row_vec(w["hg_norm_w"][j]), (1, 4)))],
                         [rowblk(ssd.ys[0], 512), rowblk(h, 512, AB_Z // 512), rowblk(hg.ys[0], 512),
                          rowblk(h, 512, AB_HG // 512)], [1024])
            rec.update(kind="ab", w_in=w_in, conv=conv, ssd=ssd, hg=hg, post=post)
            w_out = w["ab_w_out"][j]
        else:
            w_in = _cd_perm_weight(w["cd_w_in"][j])
            h = _matmul(f"cd_in{j}", x, w_in, "nn")
            swa = _Op(f"swa{j}", _swa_fn, (1, t // SWA_BLOCK),
                      [_whole(jnp.tile(w["swa_sinks"][j].reshape(8, 1), (1, LANES)))],
                      [(h, (SWA_BLOCK, 512), lambda g, c: (c, 0)),
                       (h, (SWA_BLOCK, LANES), lambda g, c: (c, CD_K // LANES)),
                       (h, (SWA_BLOCK, LANES), lambda g, c: (c, CD_V // LANES))],
                      [((t, 512), (SWA_BLOCK, 512), lambda g, c: (c, 0))], [(SWA_BLOCK, LANES)] * 2)
            conv_p = [_row_vec(w["rg_conv_w"][j, k]) for k in range(4)] + [_row_vec(w["rg_conv_b"][j])]
            conv = _Op(f"rg_conv{j}", _make_conv_fn(4, False), (2, nb),
                       [(a, (1, 256), lambda g, c: (0, g)) for a in conv_p],
                       [(h, (bt, 256), lambda g, c: (c, CD_XR // 256 + g))],
                       [((t, 512), (bt, 256), lambda g, c: (c, g))], [(8, 256)])
            gate = rowop(f"rg_gate{j}", _rg_gate_fn,
                         [_whole(_block_diag(w["rg_wa"][j])), _whole(_row_vec(w["rg_ba"][j])),
                          _whole(_block_diag(w["rg_wx"][j])), _whole(_row_vec(w["rg_bx"][j])),
                          _whole(_row_vec(w["rg_lambda"][j]))],
                         [rowblk(conv.ys[0], 512)], [512, 512])
            scan = _Op(f"rg_scan{j}", _rg_scan_fn, (2, t // bs), [],
                       [(gate.ys[0], (bs, 256), lambda g, c: (c, g)), (gate.ys[1], (bs, 256), lambda g, c: (c, g))],
                       [((t, 512), (bs, 256), lambda g, c: (c, g))], [(8, 256)])
            post = rowop(f"cd_post{j}", _cd_post_fn, [],
                         [rowblk(swa.ys[0], 512), rowblk(scan.ys[0], 512), rowblk(h, 512, CD_GATE // 512)], [1024])
            rec.update(kind="cd", w_in=w_in, swa=swa, conv=conv, gate=gate, scan=scan, post=post)
            w_out = w["cd_w_out"][j]
        ycat = post.ys[0]
        m = _matmul(f"mix_out{layer}", ycat, w_out, "nn")
        ln1 = rowop(f"ln_a{layer}", _ln_res_fn,
                    [_whole(_row_vec(w["ln_g"][layer, 0])), _whole(_row_vec(w["ln_b"][layer, 0]))],
                    [rowblk(x, 1024), rowblk(m, 1024)], [1024])
        x1 = ln1.ys[0]
        hu = _matmul(f"ffn_up{layer}", x1, w["ffn_w_up"][layer], "nn")
        cw, cb = w["ffn_conv_w"][layer], w["ffn_conv_b"][layer]
        n_fb = FFN_DIM // 256
        act_p = ([(_row_vec(cw[k, :FFN_DIM]), (1, 256), lambda g, c: (0, g)) for k in range(3)]
                 + [(_row_vec(cw[k, FFN_DIM:]), (1, 256), lambda g, c: (0, g)) for k in range(3)]
                 + [(_row_vec(cb[:FFN_DIM]), (1, 256), lambda g, c: (0, g)),
                    (_row_vec(cb[FFN_DIM:]), (1, 256), lambda g, c: (0, g))])
        act = _Op(f"ffn_act{layer}", _ffn_act_fn, (n_fb, nb), act_p,
                  [(hu, (bt, 256), lambda g, c: (c, g)), (hu, (bt, 256), lambda g, c: (c, n_fb + g))],
                  [((t, FFN_DIM), (bt, 256), lambda g, c: (c, g))], [(8, 256)] * 2)
        a = act.ys[0]
        f = _matmul(f"ffn_down{layer}", a, w["ffn_w_down"][layer], "nn")
        ln2 = rowop(f"ln_f{layer}", _ln_res_fn,
                    [_whole(_row_vec(w["ln_g"][layer, 1])), _whole(_row_vec(w["ln_b"][layer, 1]))],
                    [rowblk(x1, 1024), rowblk(f, 1024)], [1024])
        rec.update(ycat=ycat, w_out=w_out, ln1=ln1, x1=x1, act=act, a=a, ln2=ln2)
        tape.append(rec)
        x = ln2.ys[0]

    dx, loss = _loss_kernel(x, target)

    d_lb = [jnp.zeros((1, 512), F32), jnp.zeros((1, 512), F32)]
    for layer in reversed(range(DEPTH)):
        j = layer // 2
        rec = tape[layer]
        (dg, db), (dx1_res, df) = rec["ln2"].bwd([dx])
        add_grad("ln_g", (layer, 1), dg[0]); add_grad("ln_b", (layer, 1), db[0])
        add_grad("ffn_w_down", layer, _matmul(f"ffn_down_dw{layer}", rec["a"], df, "tn"))
        da = _matmul(f"ffn_down_dx{layer}", df, w["ffn_w_down"][layer], "nt")
        dpa, (dhu_g, dhu_u) = rec["act"].bwd([da])
        n_fb = FFN_DIM // 256
        dhu = jnp.concatenate([dhu_g[:, :FFN_DIM], dhu_u[:, FFN_DIM:]], axis=1)
        add_grad("ffn_conv_w", layer, jnp.stack([jnp.concatenate([dpa[k][0], dpa[3 + k][0]]) for k in range(3)]))
        add_grad("ffn_conv_b", layer, jnp.concatenate([dpa[6][0], dpa[7][0]]))
        add_grad("ffn_w_up", layer, _matmul(f"ffn_up_dw{layer}", rec["x1"], dhu, "tn"))
        dx1 = _matmul(f"ffn_up_dx{layer}", dhu, w["ffn_w_up"][layer], "nt", add=dx1_res)
        (dg, db), (dx_res, dm) = rec["ln1"].bwd([dx1])
        add_grad("ln_g", (layer, 0), dg[0]); add_grad("ln_b", (layer, 0), db[0])
        dw_out = _matmul(f"mix_out_dw{layer}", rec["ycat"], dm, "tn")
        dycat = _matmul(f"mix_out_dx{layer}", dm, rec["w_out"], "nt")
        if rec["kind"] == "ab":
            add_grad("ab_w_out", j, dw_out)
            (dnw_s, dnw_h), (dy_ssd, dz_h, do_hg, dhg_h) = rec["post"].bwd([dycat])
            add_grad("ssd_norm_w", j, dnw_s[0]); add_grad("hg_norm_w", j, dnw_h[0].reshape(4, LANES).sum(0))
            (dlb,), (dq_h, df_h, di_h) = rec["hg"].bwd([do_hg])
            d_lb[j] = dlb
            (ddtb, dalog, ddsk), (dxbc_c, ddt_h) = rec["ssd"].bwd([dy_ssd])
            add_grad("ssd_dt_bias", j, ddtb[0].reshape(8, 64).sum(-1))
            add_grad("ssd_a_log", j, dalog[0].reshape(8, 64).sum(-1))
            add_grad("ssd_d", j, ddsk[0].reshape(8, 64).sum(-1))
            dcp, (dxbc_h,) = rec["conv"].bwd([dxbc_c])
            add_grad("ssd_conv_w", j, jnp.stack([dcp[k][0] for k in range(4)]))
            add_grad("ssd_conv_b", j, dcp[4][0])
            dh = jnp.concatenate([dz_h[:, AB_Z:AB_DT], ddt_h[:, AB_DT:AB_HQ], dq_h[:, AB_HQ:AB_HF], df_h[:, AB_HF:AB_HI],
                                  di_h[:, AB_HI:AB_HG], dhg_h[:, AB_HG:AB_XBC], dxbc_h[:, AB_XBC:AB_PAD]], axis=1)
            add_grad("ab_w_in", j, _ab_unpad_grad(_matmul(f"ab_in_dw{j}", rec["x_in"], dh, "tn")))
            dx = _matmul(f"ab_in_dx{j}", dh, rec["w_in"], "nt", add=dx_res)
        else:
            add_grad("cd_w_out", j, dw_out)
            _, (dyc, dhs, dgate_h) = rec["post"].bwd([dycat])
            _, (da_s, du_s) = rec["scan"].bwd([dhs])
            (dwa, dba, dwx, dbx, dlam), (dxc,) = rec["gate"].bwd([da_s, du_s])
            add_grad("rg_wa", j, _block_diag_grad(dwa)); add_grad("rg_wx", j, _block_diag_grad(dwx))
            add_grad("rg_ba", j, dba[0]); add_grad("rg_bx", j, dbx[0]); add_grad("rg_lambda", j, dlam[0])
            dcp, (dxr_h,) = rec["conv"].bwd([dxc])
            add_grad("rg_conv_w", j, jnp.stack([dcp[k][0] for k in range(4)]))
            add_grad("rg_conv_b", j, dcp[4][0])
            (dsink,), (dq_h, dk_h, dv_h) = rec["swa"].bwd([dyc])
            add_grad("swa_sinks", j, dsink.sum(-1))
            dh = jnp.concatenate([dq_h[:, CD_Q:CD_GATE], dgate_h[:, CD_GATE:CD_XR], dxr_h[:, CD_XR:CD_K],
                                  dk_h[:, CD_K:CD_V], dv_h[:, CD_V:CD_IN]], axis=1)
            add_grad("cd_w_in", j, _cd_unperm_grad(_matmul(f"cd_in_dw{j}", rec["x_in"], dh, "tn")))
            dx = _matmul(f"cd_in_dx{j}", dh, rec["w_in"], "nt", add=dx_res)

    _, (dl0, dl1) = lb_op.bwd(d_lb)
    out = {"hg_lower": jnp.concatenate([dl0, dl1], axis=0)}
    for name, parts in grads.items():
        keys = sorted(parts)
        if isinstance(keys[0], tuple):
            out[name] = jnp.stack([jnp.stack([parts[(l, s)] for s in range(2)]) for l in range(DEPTH)])
        else:
            out[name] = jnp.stack([parts[k] for k in keys])
    return loss, dx, out


def kernel(x, ab_w_in, ssd_conv_w, ssd_conv_b, ssd_dt_bias, ssd_a_log, ssd_d, ssd_norm_w, hg_lower, hg_norm_w, ab_w_out, cd_w_in, swa_sinks, rg_conv_w, rg_conv_b, rg_wa, rg_ba, rg_wx, rg_bx, rg_lambda, cd_w_out, ffn_w_up, ffn_conv_w, ffn_conv_b, ffn_w_down, ln_g, ln_b, loss_target, m_ab_w_in, m_ssd_conv_w, m_ssd_conv_b, m_ssd_dt_bias, m_ssd_a_log, m_ssd_d, m_ssd_norm_w, m_hg_lower, m_hg_norm_w, m_ab_w_out, m_cd_w_in, m_swa_sinks, m_rg_conv_w, m_rg_conv_b, m_rg_wa, m_rg_ba, m_rg_wx, m_rg_bx, m_rg_lambda, m_cd_w_out, m_ffn_w_up, m_ffn_conv_w, m_ffn_conv_b, m_ffn_w_down, m_ln_g, m_ln_b, v_ab_w_in, v_ssd_conv_w, v_ssd_conv_b, v_ssd_dt_bias, v_ssd_a_log, v_ssd_d, v_ssd_norm_w, v_hg_lower, v_hg_norm_w, v_ab_w_out, v_cd_w_in, v_swa_sinks, v_rg_conv_w, v_rg_conv_b, v_rg_wa, v_rg_ba, v_rg_wx, v_rg_bx, v_rg_lambda, v_cd_w_out, v_ffn_w_up, v_ffn_conv_w, v_ffn_conv_b, v_ffn_w_down, v_ln_g, v_ln_b):
    args = dict(locals())
    wts = {n: args[n] for n in WEIGHTS}
    mom = {n: args["m_" + n] for n in WEIGHTS}
    var = {n: args["v_" + n] for n in WEIGHTS}
    axis = dict(SHARDED)
    small = [n for n, _ in SHARDED if n not in MATMUL_W]
    sharded = [n for n, _ in SHARDED]

    got_mm = _all_gather("gather_matmul_w", _pack([wts[n] for n in MATMUL_W], MM_DTYPE))
    got_sm = _all_gather("gather_small_w", _pack([wts[n] for n in small], F32))
    full = {n: wts[n] for n in REPLICATED}
    for n, g in zip(MATMUL_W, _unpack(got_mm, [wts[n].shape for n in MATMUL_W])):
        full[n] = _merge_shards(g, axis[n])
    for n, g in zip(small, _unpack(got_sm, [wts[n].shape for n in small])):
        full[n] = _merge_shards(g, axis[n])

    loss, grad_x, grads = _local_step(x[0], loss_target[0], full)
    loss = lax.psum(loss, ("x", "y", "c"))

    shard_parts = [_split_shards(grads[n], axis[n]) for n in sharded]
    packed = jnp.concatenate([jax.vmap(lambda a: _pack([a], F32))(p) for p in shard_parts], axis=1)
    recv = _all_to_all("exchange_grads", packed)
    res = _adamw_kernel("adamw_sharded", recv, _pack([wts[n] for n in sharded], F32),
                        _pack([mom[n] for n in sharded], F32), _pack([var[n] for n in sharded], F32))
    shapes = [wts[n].shape for n in sharded]
    new = {}
    for kind, buf in zip(("grad", "delta", "new_m", "new_v"), res):
        for n, a in zip(sharded, _unpack(buf, shapes)):
            new[kind, n] = a

    rep_parts = _all_gather("gather_replicated_grads", _pack([grads[n] for n in REPLICATED], F32))
    res = _adamw_kernel("adamw_replicated", rep_parts, _pack([wts[n] for n in REPLICATED], F32),
                        _pack([mom[n] for n in REPLICATED], F32), _pack([var[n] for n in REPLICATED], F32))
    shapes = [wts[n].shape for n in REPLICATED]
    for kind, buf in zip(("grad", "delta", "new_m", "new_v"), res):
        for n, a in zip(REPLICATED, _unpack(buf, shapes)):
            new[kind, n] = a

    outs = [loss, grad_x[None]]
    for kind in ("grad", "delta", "new_m", "new_v"):
        outs += [new[kind, n] for n in WEIGHTS]
    return tuple(outs)
```

```python
import math

import numpy as np
import jax
import jax.numpy as jnp
from jax import lax
from jax.experimental import pallas as pl
from jax.experimental.pallas import tpu as pltpu

F32 = jnp.float32
BF16 = jnp.bfloat16
HI = lax.Precision.HIGHEST
MM_DTYPE = BF16

DEPTH = 4
N_DEV = 8
LN_EPS = 1e-5
RMS_EPS = 1e-6
MASK_VALUE = -1e9
ALPHA = (2 * DEPTH) ** 0.25
RG_C = 8.0
FFN_DIM = 2816
SSD_CHUNK = 128
HG_CHUNK = 64
HG_STEP = 128
SWA_BLOCK = 128
LANES = 128
VMEM_LIMIT = 56 * 1024 * 1024

ADAM_LR, ADAM_B1, ADAM_B2, ADAM_EPS, ADAM_WD, ADAM_STEP = 0.001, 0.9, 0.999, 1e-08, 0.01, 10

AB_HEADS, AB_DT, AB_ZG, AB_XBC, AB_PAD = 0, 1536, 2048, 3072, 3840
CD_QKV, CD_GATE, CD_XR, CD_PAD = 0, 1024, 1536, 2048
FFN_BLK = 256


def _cols(x, lo, hi):
    n = x.shape[1]

    @jax.custom_vjp
    def f(x):
        return x[:, lo:hi]

    def bwd(_, g):
        parts = []
        if lo > 0:
            parts.append(jnp.zeros((g.shape[0], lo), g.dtype))
        parts.append(g)
        if hi < n:
            parts.append(jnp.zeros((g.shape[0], n - hi), g.dtype))
        return (jnp.concatenate(parts, axis=1) if len(parts) > 1 else g,)

    f.defvjp(lambda x: (f(x), None), bwd)
    return f(x)


def _rows(x, lo, hi):
    n = x.shape[0]

    @jax.custom_vjp
    def f(x):
        return x[lo:hi, :]

    def bwd(_, g):
        parts = []
        if lo > 0:
            parts.append(jnp.zeros((lo, g.shape[1]), g.dtype))
        parts.append(g)
        if hi < n:
            parts.append(jnp.zeros((n - hi, g.shape[1]), g.dtype))
        return (jnp.concatenate(parts, axis=0) if len(parts) > 1 else g,)

    f.defvjp(lambda x: (f(x), None), bwd)
    return f(x)


def _roll(x, shift, axis):
    n = x.shape[axis]
    shift = shift % n
    if shift == 0:
        return x

    @jax.custom_vjp
    def f(x):
        return pltpu.roll(x, shift, axis)

    f.defvjp(lambda x: (f(x), None), lambda _, g: (pltpu.roll(g, n - shift, axis),))
    return f(x)


def _dot(a, b, precision=None):
    return lax.dot_general(a, b, (((1,), (0,)), ((), ())), precision=precision, preferred_element_type=F32)


def _dot_nt(a, b, precision=None):
    return lax.dot_general(a, b, (((1,), (1,)), ((), ())), precision=precision, preferred_element_type=F32)


def _dot_tn(a, b, precision=None):
    return lax.dot_general(a, b, (((0,), (0,)), ((), ())), precision=precision, preferred_element_type=F32)


def _sigmoid(x):
    return 0.5 * jnp.tanh(0.5 * x) + 0.5


def _silu(x):
    return x * _sigmoid(x)


def _softplus(x):
    return jnp.maximum(x, 0.0) + jnp.log(1.0 + jnp.exp(-jnp.abs(x)))


def _gelu_tanh(x):
    c = math.sqrt(2.0 / math.pi)
    return 0.5 * x * (1.0 + jnp.tanh(c * (x + 0.044715 * (x * x * x))))


def _iota(shape, axis):
    return lax.broadcasted_iota(jnp.int32, shape, axis)


def _lane_mask(lo, hi, width=LANES):
    lane = _iota((1, width), 1)
    return ((lane >= lo) & (lane < hi)).astype(F32)


def _cparams(sem):
    return pltpu.CompilerParams(dimension_semantics=sem, vmem_limit_bytes=VMEM_LIMIT)


def _chunk_fwd(name, fn, grid, params, xs, outs, state_shapes):
    n_g, n_c = grid
    n_p, n_x, n_o, n_s = len(params), len(xs), len(outs), len(state_shapes)

    def body(*refs):
        p_refs, x_refs = refs[:n_p], refs[n_p:n_p + n_x]
        o_refs = refs[n_p + n_x:n_p + n_x + n_o]
        sv_refs = refs[n_p + n_x + n_o:n_p + n_x + n_o + n_s]
        st_refs = refs[n_p + n_x + n_o + n_s:]
        c = pl.program_id(1)

        @pl.when(c == 0)
        def _():
            for s in st_refs:
                s[...] = jnp.zeros(s.shape, s.dtype)

        st = [s[...] for s in st_refs]
        ys, new_st = fn(c, [p[...] for p in p_refs], [x[...].astype(F32) for x in x_refs], st)
        for o, y in zip(o_refs, ys):
            o[...] = y.astype(o.dtype)
        for sv, s in zip(sv_refs, st):
            sv[0, 0] = s
        for s_ref, s in zip(st_refs, new_st):
            s_ref[...] = s

    in_specs = [pl.BlockSpec(b, m) for _, b, m in params] + [pl.BlockSpec(b, m) for _, b, m in xs]
    out_specs = [pl.BlockSpec(b, m) for _, b, m, _ in outs]
    out_shape = [jax.ShapeDtypeStruct(s, d) for s, _, _, d in outs]
    for shp in state_shapes:
        out_specs.append(pl.BlockSpec((1, 1) + shp, lambda g, c, n=len(shp): (g, c) + (0,) * n))
        out_shape.append(jax.ShapeDtypeStruct((n_g, n_c) + shp, F32))
    res = pl.pallas_call(
        body, name=name, grid=grid, in_specs=in_specs, out_specs=out_specs, out_shape=out_shape,
        scratch_shapes=[pltpu.VMEM(shp, F32) for shp in state_shapes],
        compiler_params=_cparams(("arbitrary", "arbitrary")),
    )(*[a for a, _, _ in params], *[a for a, _, _ in xs])
    return list(res[:n_o]), list(res[n_o:])


def _chunk_bwd(name, fn, grid, params, xs, saved, dys, state_shapes, dx_dtypes, dx_into):
    n_g, n_c = grid
    n_p, n_x, n_s, n_y = len(params), len(xs), len(state_shapes), len(dys)
    into = sorted(dx_into)
    n_a = len(into)

    def rev(m):
        return lambda g, c: m(g, n_c - 1 - c)

    def body(*refs):
        i = 0
        p_refs = refs[i:i + n_p]; i += n_p
        x_refs = refs[i:i + n_x]; i += n_x
        sv_refs = refs[i:i + n_s]; i += n_s
        dy_refs = refs[i:i + n_y]; i += n_y
        i += n_a
        dp_refs = refs[i:i + n_p]; i += n_p
        dx_refs = refs[i:i + n_x]; i += n_x
        ds_refs = refs[i:]
        c = pl.program_id(1)
        chunk = n_c - 1 - c

        @pl.when(c == 0)
        def _():
            for s in ds_refs:
                s[...] = jnp.zeros(s.shape, s.dtype)
            for d in dp_refs:
                d[...] = jnp.zeros(d.shape, d.dtype)

        pv = [p[...] for p in p_refs]
        xv = [x[...].astype(F32) for x in x_refs]
        sv = [s[0, 0] for s in sv_refs]
        _, vjp = jax.vjp(lambda p, x, s: fn(chunk, p, x, s), pv, xv, sv)
        dp, dx, ds = vjp(([d[...].astype(F32) for d in dy_refs], [s[...] for s in ds_refs]))
        for r, v in zip(dp_refs, dp):
            r[...] += v
        for r, v in zip(dx_refs, dx):
            r[...] = v.astype(r.dtype)
        for r, v in zip(ds_refs, ds):
            r[...] = v

    in_specs = [pl.BlockSpec(b, rev(m)) for _, b, m in params] + [pl.BlockSpec(b, rev(m)) for _, b, m in xs]
    for shp in state_shapes:
        in_specs.append(pl.BlockSpec((1, 1) + shp, lambda g, c, n=len(shp): (g, n_c - 1 - c) + (0,) * n))
    in_specs += [pl.BlockSpec(b, rev(m)) for _, b, m in dys]
    in_specs += [pl.BlockSpec(memory_space=pl.ANY)] * n_a
    out_specs = [pl.BlockSpec(b, rev(m)) for _, b, m in params] + [pl.BlockSpec(b, rev(m)) for _, b, m in xs]
    out_shape = [jax.ShapeDtypeStruct(a.shape, F32) for a, _, _ in params]
    out_shape += [jax.ShapeDtypeStruct(a.shape, d) for (a, _, _), d in zip(xs, dx_dtypes)]
    first_alias = n_p + n_x + n_s + n_y
    aliases = {first_alias + k: n_p + xi for k, xi in enumerate(into)}
    res = pl.pallas_call(
        body, name=name, grid=grid, in_specs=in_specs, out_specs=out_specs, out_shape=out_shape,
        scratch_shapes=[pltpu.VMEM(shp, F32) for shp in state_shapes],
        input_output_aliases=aliases,
        compiler_params=_cparams(("arbitrary", "arbitrary")),
    )(*[a for a, _, _ in params], *[a for a, _, _ in xs], *saved, *[a for a, _, _ in dys],
      *[dx_into[xi] for xi in into])
    return list(res[:n_p]), list(res[n_p:])


class _Op:
    def __init__(self, name, fn, grid, params, xs, outs, state_shapes=(), dx_dtypes=None):
        self.name, self.fn, self.grid = name, fn, grid
        self.params, self.xs, self.outs, self.state_shapes = params, xs, outs, list(state_shapes)
        self.dx_dtypes = dx_dtypes or [F32] * len(xs)
        self.ys, self.saved = _chunk_fwd(name + "_fwd", fn, grid, params, xs, outs, self.state_shapes)

    def bwd(self, dys, dx_into=None):
        dy_defs = [(d, b, m) for d, (_, b, m, _) in zip(dys, self.outs)]
        return _chunk_bwd(self.name + "_bwd", self.fn, self.grid, self.params, self.xs, self.saved, dy_defs,
                          self.state_shapes, self.dx_dtypes, dx_into or {})


def _whole(a):
    nd = a.ndim
    return (a, a.shape, lambda g, c: (0,) * nd)


def _pick(n, prefs):
    for p in prefs:
        if n % p == 0:
            return p
    return n


def _mm_blocks(mode, m, n, k):
    bn = _pick(n, (1408, 1280, 1024, 768, 512, 256, 128))
    if mode == "tn":
        return _pick(m, (1408, 1024, 768, 512, 256, 128)), bn, _pick(k, (512, 256, 128))
    bk = k if k <= 2816 else _pick(k, (1920, 1408, 1024, 512, 256, 128))
    return _pick(m, (1024, 512, 256, 128)), bn, bk


def _matmul(name, a, b, mode, *, a_lead=None, b_lead=None, add=None, out_dtype=F32, out_stack=None):
    a2, b2 = a.shape[-2:], b.shape[-2:]
    if mode == "nn":
        (m, k), n = a2, b2[1]
    elif mode == "nt":
        (m, k), n = a2, b2[0]
    else:
        (k, m), n = a2, b2[1]
    bm, bn, bk = _mm_blocks(mode, m, n, k)
    n_k = k // bk
    dims = {"nn": (((1,), (0,)), ((), ())), "nt": (((1,), (1,)), ((), ())), "tn": (((0,), (0,)), ((), ()))}[mode]
    has_add, has_buf = add is not None, out_stack is not None and out_stack[2] is not None

    def body(*refs):
        a_ref, b_ref = refs[0], refs[1]
        c_ref = refs[2] if has_add else None
        o_ref, acc = refs[2 + has_add + has_buf], refs[3 + has_add + has_buf]
        kk = pl.program_id(2)
        part = lax.dot_general(a_ref[...].astype(MM_DTYPE), b_ref[...].astype(MM_DTYPE), dims,
                               preferred_element_type=F32)

        def finish(r):
            if has_add:
                r = r + c_ref[...]
            o_ref[...] = r.astype(o_ref.dtype)

        if n_k == 1:
            finish(part)
        else:
            @pl.when(kk == 0)
            def _():
                acc[...] = part

            @pl.when((kk > 0) & (kk < n_k - 1))
            def _():
                acc[...] += part

            @pl.when(kk == n_k - 1)
            def _():
                finish(acc[...] + part)

    def spec(block, imap, lead):
        if lead is None:
            return pl.BlockSpec(block, imap)
        return pl.BlockSpec((None,) + block, lambda i, j, kk: (lead,) + imap(i, j, kk))

    if mode == "nn":
        a_spec = spec((bm, bk), lambda i, j, kk: (i, kk), a_lead)
        b_spec = spec((bk, bn), lambda i, j, kk: (kk, j), b_lead)
    elif mode == "nt":
        a_spec = spec((bm, bk), lambda i, j, kk: (i, kk), a_lead)
        b_spec = spec((bn, bk), lambda i, j, kk: (j, kk), b_lead)
    else:
        a_spec = spec((bk, bm), lambda i, j, kk: (kk, i), a_lead)
        b_spec = spec((bk, bn), lambda i, j, kk: (kk, j), b_lead)
    in_specs, args, aliases = [a_spec, b_spec], [a, b], {}
    if has_add:
        in_specs.append(pl.BlockSpec((bm, bn), lambda i, j, kk: (i, j)))
        args.append(add)
    if out_stack is None:
        out_spec = pl.BlockSpec((bm, bn), lambda i, j, kk: (i, j))
        out_shape = jax.ShapeDtypeStruct((m, n), out_dtype)
    else:
        n_l, lead, buf = out_stack
        out_spec = pl.BlockSpec((None, bm, bn), lambda i, j, kk: (lead, i, j))
        out_shape = jax.ShapeDtypeStruct((n_l, m, n), out_dtype)
        if buf is not None:
            in_specs.append(pl.BlockSpec(memory_space=pl.ANY))
            args.append(buf)
            aliases = {len(args) - 1: 0}
    return pl.pallas_call(
        body, name=name, grid=(m // bm, n // bn, n_k), in_specs=in_specs, out_specs=out_spec, out_shape=out_shape,
        scratch_shapes=[pltpu.VMEM((bm, bn) if n_k > 1 else (8, LANES), F32)],
        input_output_aliases=aliases,
        compiler_params=_cparams(("arbitrary", "arbitrary", "arbitrary")),
    )(*args)


def _ln_res_fn(_, p, x, st):
    g, b = p
    xin, m = x
    pre = ALPHA * xin + m
    mu = jnp.mean(pre, -1, keepdims=True)
    d = pre - mu
    var = jnp.mean(d * d, -1, keepdims=True)
    return [d * lax.rsqrt(var + LN_EPS) * g + b], []


def _make_conv_fn(taps, act):
    def fn(_, p, x, st):
        ws, b = p[:taps], p[taps]
        (xin,), (prev,) = x, st
        n = xin.shape[0]
        ext = jnp.concatenate([prev, xin], axis=0)
        y = b
        for k in range(taps):
            y = y + ws[k] * _rows(_roll(ext, taps - 1 - k, 0), 8, 8 + n)
        if act:
            y = _silu(y)
        return [y], [_rows(xin, n - 8, n)]

    return fn


def _ffn_act_fn(_, p, x, st):
    ws, b = p[:3], p[3]
    (xin,), (prev,) = x, st
    n = xin.shape[0]
    ext = jnp.concatenate([prev, xin], axis=0)
    y = b
    for k in range(3):
        y = y + ws[k] * _rows(_roll(ext, 2 - k, 0), 8, 8 + n)
    return [_silu(_cols(y, 0, FFN_BLK)) * _cols(y, FFN_BLK, 2 * FFN_BLK)], [_rows(xin, n - 8, n)]


def _ssd_fn(_, p, x, st):
    dtb, alog, dsk = p
    xbc, dtr = x
    L = SSD_CHUNK
    tril = _iota((L, L), 0) >= _iota((L, L), 1)
    trilf = tril.astype(F32)
    xs, bm, cm = _cols(xbc, 0, 512), _cols(xbc, 512, 640), _cols(xbc, 640, 768)
    dt = _softplus(dtr + dtb)
    da = dt * (-jnp.exp(alog))
    cs = _dot(trilf, da, HI)
    tot = jnp.sum(da, axis=0, keepdims=True)
    xc = xs * dt
    xdec = xc * jnp.exp(tot - cs)
    ecs = jnp.exp(cs)
    etot = jnp.exp(tot)
    ys, new_st = [], []
    for pr in range(4):
        lo, hi = LANES * pr, LANES * (pr + 1)
        grp = pr // 2
        c_g = cm * _lane_mask(64 * grp, 64 * grp + 64)
        gmat = _dot_nt(c_g, bm)
        cs_p, xc_p = _cols(cs, lo, hi), _cols(xc, lo, hi)
        yd = jnp.zeros((L, LANES), F32)
        for half in range(2):
            sel = (_iota((LANES, L), 0) == 64 * half).astype(F32)
            col = _dot(cs_p, sel, HI)
            diff = col - col.T
            dec = jnp.where(tril, jnp.exp(jnp.where(tril, diff, 0.0)), 0.0)
            yd = yd + _dot(gmat * dec, xc_p) * _lane_mask(64 * half, 64 * half + 64)
        s_in = st[pr]
        y_off = _dot(c_g, s_in) * _cols(ecs, lo, hi)
        ys.append(yd + y_off + _cols(dsk, lo, hi) * _cols(xs, lo, hi))
        new_st.append(s_in * _cols(etot, lo, hi) + _dot_tn(bm, _cols(xdec, lo, hi)))
    return [jnp.concatenate(ys, axis=1)], new_st


def _hg_chunk(hq, hf, hi, lb, s_t):
    L = HG_CHUNK
    q = _silu(hq)
    logf = jnp.log(lb + (1.0 - lb) * _sigmoid(hf))
    k = (1.0 - lb) * _sigmoid(-hf)
    ti, si = _iota((L, L), 0), _iota((L, L), 1)
    bc = _dot((ti >= si).astype(F32), logf, HI)
    tot = jnp.sum(logf, axis=0, keepdims=True)
    attn = jnp.where(ti == si, _dot_nt(q, k), 0.0)
    t6, s6 = _iota((6 * L, 1), 0), _iota((6 * L, L), 1)
    lvl6 = t6 >> 6
    row6 = t6 & (L - 1)
    blk6 = L >> lvl6
    pos6 = row6 & (blk6 - 1)
    piv6 = row6 - pos6 + (blk6 >> 1)
    bcp6 = _dot((s6 == piv6).astype(F32), bc, HI)
    t1 = _iota((L, 1), 0)
    for lvl in range(6):
        blk = L >> lvl
        upper = (t1 & (blk - 1)) >= blk // 2
        bcp = _rows(bcp6, L * lvl, L * (lvl + 1))
        qq = jnp.where(upper, q * jnp.exp(jnp.where(upper, bc - bcp, 0.0)), 0.0)
        kk = jnp.where(upper, 0.0, k * jnp.exp(jnp.where(upper, 0.0, bcp - bc)))
        same = (ti >> (6 - lvl)) == (si >> (6 - lvl))
        attn = attn + jnp.where(same, _dot_nt(qq, kk), 0.0)
    out = _dot(attn, hi) + _dot_nt(q * jnp.exp(bc), s_t)
    new_s = s_t * jnp.exp(tot) + _dot_tn(hi, k * jnp.exp(tot - bc))
    return out, new_s


def _hg_fn(_, p, x, st):
    (lb,) = p
    (xin,) = x
    st = list(st)
    outs = []
    for sub in range(HG_STEP // HG_CHUNK):
        rows = _rows(xin, HG_CHUNK * sub, HG_CHUNK * (sub + 1))
        heads = []
        for h in range(4):
            o = 3 * LANES * h
            out, st[h] = _hg_chunk(_cols(rows, o, o + LANES), _cols(rows, o + LANES, o + 2 * LANES),
                                   _cols(rows, o + 2 * LANES, o + 3 * LANES), _cols(lb, LANES * h, LANES * (h + 1)),
                                   st[h])
            heads.append(out)
        outs.append(jnp.concatenate(heads, axis=1))
    return [jnp.concatenate(outs, axis=0)], st


def _swa_fn(chunk, p, x, st):
    (sinks,) = p
    (xin,) = x
    q, k, v = _cols(xin, 0, 512), _cols(xin, 512, 640), _cols(xin, 640, 768)
    kp, vp = st
    T = SWA_BLOCK
    kc = jnp.concatenate([kp, k], axis=0)
    vc = jnp.concatenate([vp, v], axis=0)
    qi, kj = _iota((T, 2 * T), 0), _iota((T, 2 * T), 1)
    rel = qi + T - kj
    mask = (rel >= 0) & (rel < T) & ((kj >= T) | (chunk > 0))
    srow = _iota((8, LANES), 0)
    outs = []
    for pr in range(4):
        grp = pr // 2
        gm = _lane_mask(64 * grp, 64 * grp + 64)
        km, vm = kc * gm, vc * gm
        q2 = _cols(q, LANES * pr, LANES * (pr + 1))
        o2 = jnp.zeros((T, LANES), F32)
        for half in range(2):
            hm = _lane_mask(64 * half, 64 * half + 64)
            qh = q2 * hm
            if half != grp:
                qh = _roll(qh, 64, 1)
            s = _dot_nt(qh, km) * 0.125
            s = jnp.where(mask, s, MASK_VALUE)
            sink = jnp.mean(jnp.sum(jnp.where(srow == 2 * pr + half, sinks, 0.0), axis=0, keepdims=True),
                            axis=-1, keepdims=True)
            mx = lax.stop_gradient(jnp.maximum(jnp.max(s, axis=-1, keepdims=True), sink))
            e = jnp.exp(s - mx)
            den = jnp.sum(e, axis=-1, keepdims=True) + jnp.exp(sink - mx)
            o = _dot(e / den, vm)
            if half != grp:
                o = _roll(o, 64, 1)
            o2 = o2 + o * hm
        outs.append(o2)
    return [jnp.concatenate(outs, axis=1)], [k, v]


def _rg_gate_fn(_, p, x, st):
    wa, ba, wx, bx, lam = p
    (xc,) = x
    r = _sigmoid(_dot(xc, wa) + ba)
    i = _sigmoid(_dot(xc, wx) + bx)
    log_a = -RG_C * r * _softplus(-lam)
    a = jnp.exp(log_a)
    t = jnp.tanh(log_a)
    one_minus_a2 = -2.0 * t / (1.0 - t)
    u = jnp.sqrt(jnp.maximum(one_minus_a2, 0.0)) * (i * xc)
    return [a, u], []


def _rg_scan_fn(_, p, x, st):
    a, u = x
    (prev,) = st
    n = a.shape[0]
    row = _iota((n, 1), 0)
    s = 1
    while s < n:
        keep = row >= s
        a_s, u_s = _roll(a, s, 0), _roll(u, s, 0)
        u = jnp.where(keep, a * u_s + u, u)
        a = jnp.where(keep, a * a_s, a)
        s *= 2
    h_in = jnp.sum(jnp.where(_iota((8, 1), 0) == 7, prev, 0.0), axis=0, keepdims=True)
    h = u + a * h_in
    return [h], [_rows(h, n - 8, n)]


def _ab_post_fn(_, p, x, st):
    nw_ssd, nw_hg = p
    y, o, zg = x
    z, hgate = _cols(zg, 0, 512), _cols(zg, 512, 1024)
    lane = _iota((1, 512), 1)
    ya = y * _silu(z)
    sq = ya * ya
    inv = jnp.zeros_like(ya)
    for g in range(2):
        mk = (lane >= 256 * g) & (lane < 256 * (g + 1))
        ms = jnp.sum(jnp.where(mk, sq, 0.0), axis=-1, keepdims=True) / 256.0
        inv = jnp.where(mk, lax.rsqrt(ms + RMS_EPS), inv)
    ya = ya * inv * nw_ssd
    so = o * o
    inv = jnp.zeros_like(o)
    for h in range(4):
        mk = (lane >= 128 * h) & (lane < 128 * (h + 1))
        ms = jnp.sum(jnp.where(mk, so, 0.0), axis=-1, keepdims=True) / 128.0
        inv = jnp.where(mk, lax.rsqrt(ms + RMS_EPS), inv)
    yb = o * inv * nw_hg * _silu(hgate)
    return [jnp.concatenate([ya, yb], axis=1)], []


def _cd_post_fn(_, p, x, st):
    yc, h, gate = x
    return [jnp.concatenate([yc, h * _gelu_tanh(gate)], axis=1)], []


def _lb_fn(_, p, x, st):
    l0, l1 = x
    mx = lax.stop_gradient(jnp.maximum(l0, l1))
    e0, e1 = jnp.exp(l0 - mx), jnp.exp(l1 - mx)
    s0, s1 = e0 / (e0 + e1), e1 / (e0 + e1)
    return [jnp.clip(s0 - s0, 0.0, 1.0), jnp.clip((s0 + s1) - s0, 0.0, 1.0)], []


def _loss_kernel(y, target):
    t, d = y.shape
    bt = _pick(t, (512, 256, 128))

    def body(y_ref, t_ref, dy_ref, l_ref):
        @pl.when(pl.program_id(0) == 0)
        def _():
            l_ref[...] = jnp.zeros(l_ref.shape, F32)

        e = y_ref[...] - t_ref[...]
        dy_ref[...] = e * (1.0 / d)
        l_ref[...] += jnp.sum(e * e, axis=0, keepdims=True) * (0.5 / d)

    dy, part = pl.pallas_call(
        body, name="loss", grid=(t // bt,),
        in_specs=[pl.BlockSpec((bt, d), lambda i: (i, 0)), pl.BlockSpec((bt, d), lambda i: (i, 0))],
        out_specs=[pl.BlockSpec((bt, d), lambda i: (i, 0)), pl.BlockSpec((1, d), lambda i: (0, 0))],
        out_shape=[jax.ShapeDtypeStruct((t, d), F32), jax.ShapeDtypeStruct((1, d), F32)],
        compiler_params=_cparams(("arbitrary",)),
    )(y, target)
    return dy, jnp.sum(part)


def _adamw_math(parts, w_, m_, v_):
    c1 = 1.0 / (1.0 - ADAM_B1 ** ADAM_STEP)
    c2 = 1.0 / (1.0 - ADAM_B2 ** ADAM_STEP)
    g = parts[0].astype(F32)
    for s in range(1, N_DEV):
        g = g + parts[s].astype(F32)
    nm = ADAM_B1 * m_ + (1.0 - ADAM_B1) * g
    nv = ADAM_B2 * v_ + (1.0 - ADAM_B2) * (g * g)
    return g, -ADAM_LR * ((nm * c1) / (jnp.sqrt(nv * c2) + ADAM_EPS) + ADAM_WD * w_), nm, nv


def _adamw_big(name, parts, w, m, v):
    n_l, r, c = w.shape
    br = _pick(r, (256, 176, 128, 64, 32, 16, 8))

    def body(p_ref, w_ref, m_ref, v_ref, g_ref, d_ref, nm_ref, nv_ref):
        res = _adamw_math([p_ref[s] for s in range(N_DEV)], w_ref[...], m_ref[...], v_ref[...])
        for ref, val in zip((g_ref, d_ref, nm_ref, nv_ref), res):
            ref[...] = val

    blk = pl.BlockSpec((None, br, c), lambda l, i: (l, i, 0))
    return pl.pallas_call(
        body, name=name, grid=(n_l, r // br),
        in_specs=[pl.BlockSpec((N_DEV, None, br, c), lambda l, i: (0, l, i, 0)), blk, blk, blk],
        out_specs=[blk] * 4, out_shape=[jax.ShapeDtypeStruct(w.shape, F32)] * 4,
        compiler_params=_cparams(("arbitrary", "arbitrary")),
    )(parts, w, m, v)


def _adamw_small(name, items):
    n = len(items)

    def body(*refs):
        ins, outs = refs[:4 * n], refs[4 * n:]
        for i in range(n):
            p_ref, w_ref, m_ref, v_ref = ins[4 * i:4 * i + 4]
            res = _adamw_math([p_ref[s] for s in range(N_DEV)], w_ref[...], m_ref[...], v_ref[...])
            for ref, val in zip(outs[4 * i:4 * i + 4], res):
                ref[...] = val

    flat = [a for it in items for a in it]
    out_shape = [jax.ShapeDtypeStruct(it[1].shape, F32) for it in items for _ in range(4)]
    res = pl.pallas_call(
        body, name=name, out_shape=out_shape,
        in_specs=[pl.BlockSpec(memory_space=pltpu.VMEM)] * len(flat),
        out_specs=[pl.BlockSpec(memory_space=pltpu.VMEM)] * len(out_shape),
        compiler_params=pltpu.CompilerParams(vmem_limit_bytes=VMEM_LIMIT),
    )(*flat)
    return [res[4 * i:4 * i + 4] for i in range(n)]


def _mesh_pos():
    return lax.axis_index("x"), lax.axis_index("y"), lax.axis_index("c")


def _any_specs(n):
    return [pl.BlockSpec(memory_space=pl.ANY)] * n


def _all_gather(name, shards):
    n = len(shards)

    def body(*refs):
        x_refs, out_refs = refs[:n], refs[n:2 * n]
        send_sems, recv_sems, local_sems = refs[2 * n:]
        x, y, cc = _mesh_pos()
        me, sibling = (x, y, cc), (x, y, 1 - cc)
        chips = [(1 - x, y), (x, 1 - y), (1 - x, 1 - y)]

        def copy(i, k, block, to, src=None):
            px, py, pc = block
            dst = out_refs[i].at[4 * px + 2 * py + pc]
            return pltpu.make_async_remote_copy(
                src_ref=dst if src is None else src, dst_ref=dst,
                send_sem=send_sems.at[i, k], recv_sem=recv_sems.at[i, k],
                device_id=to, device_id_type=pl.DeviceIdType.MESH)

        locals_, sends = [], []
        for i in range(n):
            mine = pltpu.make_async_copy(x_refs[i], out_refs[i].at[4 * x + 2 * y + cc], local_sems.at[i])
            mine.start()
            locals_.append(mine)
            first = [copy(i, 0, me, sibling, src=x_refs[i])]
            first += [copy(i, 1 + j, me, (*chip, cc), src=x_refs[i]) for j, chip in enumerate(chips)]
            for cp in first:
                cp.start()
            sends += first
        for i in range(n):
            for j, chip in enumerate(chips):
                copy(i, 1 + j, (*chip, cc), me).wait_recv()
                fwd = copy(i, 4 + j, (*chip, cc), sibling)
                fwd.start()
                sends.append(fwd)
        for i in range(n):
            copy(i, 0, sibling, me).wait_recv()
            for j, chip in enumerate(chips):
                copy(i, 4 + j, (*chip, 1 - cc), me).wait_recv()
        for cp in sends:
            cp.wait_send()
        for cp in locals_:
            cp.wait()

    return pl.pallas_call(
        body, name=name,
        out_shape=[jax.ShapeDtypeStruct((N_DEV,) + s.shape, s.dtype) for s in shards],
        in_specs=_any_specs(n), out_specs=_any_specs(n),
        scratch_shapes=[pltpu.SemaphoreType.DMA((n, 7)), pltpu.SemaphoreType.DMA((n, 7)),
                        pltpu.SemaphoreType.DMA((n,))],
    )(*shards)


def _all_to_all(name, parts):
    n = len(parts)

    def body(*refs):
        x_refs, out_refs = refs[:n], refs[n:2 * n]
        send_sems, recv_sems, local_sems = refs[2 * n:]
        x, y, cc = _mesh_pos()
        me = 4 * x + 2 * y + cc
        locals_, copies = [], []
        for i in range(n):
            mine = pltpu.make_async_copy(x_refs[i].at[me], out_refs[i].at[me], local_sems.at[i])
            mine.start()
            locals_.append(mine)
            for k in range(1, N_DEV):
                px, py, pc = x ^ ((k >> 2) & 1), y ^ ((k >> 1) & 1), cc ^ (k & 1)
                cp = pltpu.make_async_remote_copy(
                    src_ref=x_refs[i].at[4 * px + 2 * py + pc], dst_ref=out_refs[i].at[me],
                    send_sem=send_sems.at[i, k - 1], recv_sem=recv_sems.at[i, k - 1],
                    device_id=(px, py, pc), device_id_type=pl.DeviceIdType.MESH)
                cp.start()
                copies.append(cp)
        for cp in copies:
            cp.wait_recv()
        for cp in copies:
            cp.wait_send()
        for cp in locals_:
            cp.wait()

    return pl.pallas_call(
        body, name=name,
        out_shape=[jax.ShapeDtypeStruct(p.shape, p.dtype) for p in parts],
        in_specs=_any_specs(n), out_specs=_any_specs(n),
        scratch_shapes=[pltpu.SemaphoreType.DMA((n, 7)), pltpu.SemaphoreType.DMA((n, 7)),
                        pltpu.SemaphoreType.DMA((n,))],
    )(*parts)


SHARDED = [("ab_w_in", 2), ("ab_w_out", 1), ("cd_w_in", 2), ("cd_w_out", 1), ("ffn_w_up", 2), ("ffn_w_down", 1),
           ("ssd_conv_w", 2), ("rg_conv_w", 2), ("rg_conv_b", 1), ("rg_ba", 1), ("rg_bx", 1), ("rg_lambda", 1),
           ("ffn_conv_w", 2), ("ln_g", 2), ("ln_b", 2)]
MATMUL_W = ("ab_w_in", "ab_w_out", "cd_w_in", "cd_w_out", "ffn_w_up", "ffn_w_down")
REPLICATED = ["ssd_conv_b", "ssd_dt_bias", "ssd_a_log", "ssd_d", "ssd_norm_w", "hg_lower", "hg_norm_w", "swa_sinks",
              "rg_wa", "rg_wx", "ffn_conv_b"]
WEIGHTS = ["ab_w_in", "ssd_conv_w", "ssd_conv_b", "ssd_dt_bias", "ssd_a_log", "ssd_d", "ssd_norm_w", "hg_lower",
           "hg_norm_w", "ab_w_out", "cd_w_in", "swa_sinks", "rg_conv_w", "rg_conv_b", "rg_wa", "rg_ba", "rg_wx",
           "rg_bx", "rg_lambda", "cd_w_out", "ffn_w_up", "ffn_conv_w", "ffn_conv_b", "ffn_w_down", "ln_g", "ln_b"]


def _as2d(a):
    return a.reshape(-1, a.shape[-1])


def _merge_shards(g, axis):
    g = jnp.moveaxis(g, 0, axis)
    shp = g.shape
    return g.reshape(shp[:axis] + (shp[axis] * shp[axis + 1],) + shp[axis + 2:])


def _split_shards(full, axis):
    shp = full.shape
    g = full.reshape(shp[:axis] + (N_DEV, shp[axis] // N_DEV) + shp[axis + 1:])
    return jnp.moveaxis(g, axis, 0)


def _ab_pad(w):
    z, xbc, dt = w[..., 0:512], w[..., 512:1280], w[..., 1280:1288]
    hq, hf, hi, hg = w[..., 1288:1800], w[..., 1800:2312], w[..., 2312:2824], w[..., 2824:3336]
    heads = [p[..., LANES * h:LANES * (h + 1)] for h in range(4) for p in (hq, hf, hi)]
    return jnp.concatenate(heads + [jnp.repeat(dt, 64, axis=-1), z, hg, xbc], axis=-1)


def _ab_unpad(d):
    lead = d.shape[:-1]
    heads = d[..., :AB_DT].reshape(lead + (4, 3, LANES))
    hq, hf, hi = (heads[..., i, :].reshape(lead + (512,)) for i in range(3))
    dt = d[..., AB_DT:AB_ZG].reshape(lead + (8, 64)).sum(-1)
    z, hg, xbc = d[..., AB_ZG:AB_ZG + 512], d[..., AB_ZG + 512:AB_XBC], d[..., AB_XBC:AB_PAD]
    return jnp.concatenate([z, xbc, dt, hq, hf, hi, hg], axis=-1)


def _cd_pad(w):
    return jnp.concatenate([w[..., :768], jnp.zeros(w.shape[:-1] + (256,), w.dtype), w[..., 768:]], axis=-1)


def _cd_unpad(d):
    return jnp.concatenate([d[..., :768], d[..., CD_GATE:CD_PAD]], axis=-1)


def _ffn_perm(w):
    lead = w.shape[:-1]
    return jnp.swapaxes(w.reshape(lead + (2, FFN_DIM // FFN_BLK, FFN_BLK)), -3, -2).reshape(lead + (2 * FFN_DIM,))


def _ffn_unperm(d):
    lead = d.shape[:-1]
    return jnp.swapaxes(d.reshape(lead + (FFN_DIM // FFN_BLK, 2, FFN_BLK)), -3, -2).reshape(lead + (2 * FFN_DIM,))


def _block_diag(w):
    eye = jnp.eye(8, dtype=w.dtype)
    return jnp.einsum("gij,gh->gihj", w, eye).reshape(512, 512)


def _block_diag_grad(d):
    return jnp.stack([d[64 * g:64 * g + 64, 64 * g:64 * g + 64] for g in range(8)])


def _prep_weights(w):
    k = dict(w)
    k["ab_w_in"] = _ab_pad(w["ab_w_in"])
    k["cd_w_in"] = _cd_pad(w["cd_w_in"])
    k["ffn_w_up"] = _ffn_perm(w["ffn_w_up"])
    k["ffn_conv_w"] = _ffn_perm(w["ffn_conv_w"])
    k["ffn_conv_b"] = _ffn_perm(w["ffn_conv_b"])
    return k


def _unprep_grads(g):
    out = dict(g)
    out["ab_w_in"] = _ab_unpad(g["ab_w_in"])
    out["cd_w_in"] = _cd_unpad(g["cd_w_in"])
    out["ffn_w_up"] = _ffn_unperm(g["ffn_w_up"])
    out["ffn_conv_w"] = _ffn_unperm(g["ffn_conv_w"])
    out["ffn_conv_b"] = _ffn_unperm(g["ffn_conv_b"])
    return out


def _row_vec(v):
    return v.reshape(1, -1)


def _heads64(v):
    return jnp.repeat(v, 64).reshape(1, 512)


def _local_step(x, target, w):
    t = x.shape[0]
    bt = _pick(t, (512, 256, 128))
    nb = t // bt
    bs = _pick(t, (256, 128))

    def rowop(name, fn, params, xs, widths_out, out_dtype=F32, dx_dtypes=None):
        outs = [((t, wd), (bt, wd), lambda g, c: (c, 0), out_dtype) for wd in widths_out]
        return _Op(name, fn, (1, nb), params, xs, outs, dx_dtypes=dx_dtypes)

    def rowblk(arr, width, first=0):
        return (arr, (bt, width), lambda g, c: (c, first))

    one_row = lambda g, c: (0, 0)
    lb_op = _Op("hg_lb", _lb_fn, (1, 1), [],
                [(w["hg_lower"][0:1], (1, 512), one_row), (w["hg_lower"][1:2], (1, 512), one_row)],
                [((1, 512), (1, 512), one_row, F32)] * 2)
    lb_all = lb_op.ys

    tape = []
    grads = {}
    stacks = {}

    def add_grad(name, idx, val):
        grads.setdefault(name, {})[idx] = val

    def dw_matmul(name, a, b, wname, n_l, idx):
        stacks[wname] = _matmul(name, a, b, "tn", out_stack=(n_l, idx, stacks.get(wname)))

    for layer in range(DEPTH):
        j = layer // 2
        rec = {"x_in": x}
        if layer % 2 == 0:
            h = _matmul(f"ab_in{j}", x, w["ab_w_in"], "nn", b_lead=j)
            conv_p = [_row_vec(w["ssd_conv_w"][j, k]) for k in range(4)] + [_row_vec(w["ssd_conv_b"][j])]
            conv = _Op(f"ssd_conv{j}", _make_conv_fn(4, True), (3, nb),
                       [(a, (1, 256), lambda g, c: (0, g)) for a in conv_p],
                       [(h, (bt, 256), lambda g, c: (c, AB_XBC // 256 + g))],
                       [((t, 768), (bt, 256), lambda g, c: (c, g), F32)], [(8, 256)], dx_dtypes=[MM_DTYPE])
            ssd = _Op(f"ssd{j}", _ssd_fn, (1, t // SSD_CHUNK),
                      [_whole(_heads64(w["ssd_dt_bias"][j])), _whole(_heads64(w["ssd_a_log"][j])),
                       _whole(_heads64(w["ssd_d"][j]))],
                      [(conv.ys[0], (SSD_CHUNK, 768), lambda g, c: (c, 0)),
                       (h, (SSD_CHUNK, 512), lambda g, c: (c, AB_DT // 512))],
                      [((t, 512), (SSD_CHUNK, 512), lambda g, c: (c, 0), F32)], [(LANES, LANES)] * 4,
                      dx_dtypes=[F32, MM_DTYPE])
            hg = _Op(f"hg{j}", _hg_fn, (1, t // HG_STEP), [_whole(lb_all[j])],
                     [(h, (HG_STEP, AB_DT), lambda g, c: (c, 0))],
                     [((t, 512), (HG_STEP, 512), lambda g, c: (c, 0), F32)], [(LANES, LANES)] * 4,
                     dx_dtypes=[MM_DTYPE])
            post = rowop(f"ab_post{j}", _ab_post_fn,
                         [_whole(_row_vec(w["ssd_norm_w"][j])), _whole(jnp.tile(_row_vec(w["hg_norm_w"][j]), (1, 4)))],
                         [rowblk(ssd.ys[0], 512), rowblk(hg.ys[0], 512), rowblk(h, 1024, AB_ZG // 1024)], [1024],
                         out_dtype=MM_DTYPE, dx_dtypes=[F32, F32, MM_DTYPE])
            rec.update(kind="ab", conv=conv, ssd=ssd, hg=hg, post=post)
            w_out = w["ab_w_out"]
        else:
            h = _matmul(f"cd_in{j}", x, w["cd_w_in"], "nn", b_lead=j)
            swa = _Op(f"swa{j}", _swa_fn, (1, t // SWA_BLOCK),
                      [_whole(jnp.tile(w["swa_sinks"][j].reshape(8, 1), (1, LANES)))],
                      [(h, (SWA_BLOCK, 1024), lambda g, c: (c, 0))],
                      [((t, 512), (SWA_BLOCK, 512), lambda g, c: (c, 0), F32)], [(SWA_BLOCK, LANES)] * 2,
                      dx_dtypes=[MM_DTYPE])
            conv_p = [_row_vec(w["rg_conv_w"][j, k]) for k in range(4)] + [_row_vec(w["rg_conv_b"][j])]
            conv = _Op(f"rg_conv{j}", _make_conv_fn(4, False), (2, nb),
                       [(a, (1, 256), lambda g, c: (0, g)) for a in conv_p],
                       [(h, (bt, 256), lambda g, c: (c, CD_XR // 256 + g))],
                       [((t, 512), (bt, 256), lambda g, c: (c, g), F32)], [(8, 256)], dx_dtypes=[MM_DTYPE])
            gate = rowop(f"rg_gate{j}", _rg_gate_fn,
                         [_whole(_block_diag(w["rg_wa"][j])), _whole(_row_vec(w["rg_ba"][j])),
                          _whole(_block_diag(w["rg_wx"][j])), _whole(_row_vec(w["rg_bx"][j])),
                          _whole(_row_vec(w["rg_lambda"][j]))],
                         [rowblk(conv.ys[0], 512)], [512, 512])
            scan = _Op(f"rg_scan{j}", _rg_scan_fn, (2, t // bs), [],
                       [(gate.ys[0], (bs, 256), lambda g, c: (c, g)), (gate.ys[1], (bs, 256), lambda g, c: (c, g))],
                       [((t, 512), (bs, 256), lambda g, c: (c, g), F32)], [(8, 256)])
            post = rowop(f"cd_post{j}", _cd_post_fn, [],
                         [rowblk(swa.ys[0], 512), rowblk(scan.ys[0], 512), rowblk(h, 512, CD_GATE // 512)], [1024],
                         out_dtype=MM_DTYPE, dx_dtypes=[F32, F32, MM_DTYPE])
            rec.update(kind="cd", swa=swa, conv=conv, gate=gate, scan=scan, post=post)
            w_out = w["cd_w_out"]
        ycat = post.ys[0]
        m = _matmul(f"mix_out{layer}", ycat, w_out, "nn", b_lead=j)
        ln1 = rowop(f"ln_a{layer}", _ln_res_fn,
                    [_whole(_row_vec(w["ln_g"][layer, 0])), _whole(_row_vec(w["ln_b"][layer, 0]))],
                    [rowblk(x, 1024), rowblk(m, 1024)], [1024], dx_dtypes=[F32, MM_DTYPE])
        x1 = ln1.ys[0]
        hu = _matmul(f"ffn_up{layer}", x1, w["ffn_w_up"], "nn", b_lead=layer)
        n_fb = FFN_DIM // FFN_BLK
        act_p = [_row_vec(w["ffn_conv_w"][layer, k]) for k in range(3)] + [_row_vec(w["ffn_conv_b"][layer])]
        act = _Op(f"ffn_act{layer}", _ffn_act_fn, (n_fb, nb),
                  [(a, (1, 2 * FFN_BLK), lambda g, c: (0, g)) for a in act_p],
                  [(hu, (bt, 2 * FFN_BLK), lambda g, c: (c, g))],
                  [((t, FFN_DIM), (bt, FFN_BLK), lambda g, c: (c, g), MM_DTYPE)], [(8, 2 * FFN_BLK)],
                  dx_dtypes=[MM_DTYPE])
        a = act.ys[0]
        f = _matmul(f"ffn_down{layer}", a, w["ffn_w_down"], "nn", b_lead=layer)
        ln2 = rowop(f"ln_f{layer}", _ln_res_fn,
                    [_whole(_row_vec(w["ln_g"][layer, 1])), _whole(_row_vec(w["ln_b"][layer, 1]))],
                    [rowblk(x1, 1024), rowblk(f, 1024)], [1024], dx_dtypes=[F32, MM_DTYPE])
        rec.update(ycat=ycat, w_out=w_out, ln1=ln1, x1=x1, act=act, a=a, ln2=ln2)
        tape.append(rec)
        x = ln2.ys[0]

    dx, loss = _loss_kernel(x, target)

    d_lb = [jnp.zeros((1, 512), F32), jnp.zeros((1, 512), F32)]
    for layer in reversed(range(DEPTH)):
        j = layer // 2
        rec = tape[layer]
        (dg, db), (dx1_res, df) = rec["ln2"].bwd([dx])
        add_grad("ln_g", (layer, 1), dg[0]); add_grad("ln_b", (layer, 1), db[0])
        dw_matmul(f"ffn_down_dw{layer}", rec["a"], df, "ffn_w_down", DEPTH, layer)
        da = _matmul(f"ffn_down_dx{layer}", df, w["ffn_w_down"], "nt", b_lead=layer)
        dpa, (dhu,) = rec["act"].bwd([da])
        add_grad("ffn_conv_w", layer, jnp.stack([dpa[k][0] for k in range(3)]))
        add_grad("ffn_conv_b", layer, dpa[3][0])
        dw_matmul(f"ffn_up_dw{layer}", rec["x1"], dhu, "ffn_w_up", DEPTH, layer)
        dx1 = _matmul(f"ffn_up_dx{layer}", dhu, w["ffn_w_up"], "nt", b_lead=layer, add=dx1_res)
        (dg, db), (dx_res, dm) = rec["ln1"].bwd([dx1])
        add_grad("ln_g", (layer, 0), dg[0]); add_grad("ln_b", (layer, 0), db[0])
        kind = rec["kind"]
        dw_matmul(f"mix_out_dw{layer}", rec["ycat"], dm, kind + "_w_out", 2, j)
        dycat = _matmul(f"mix_out_dx{layer}", dm, rec["w_out"], "nt", b_lead=j)
        if kind == "ab":
            (dnw_s, dnw_h), (dy_ssd, do_hg, dh) = rec["post"].bwd([dycat])
            add_grad("ssd_norm_w", j, dnw_s[0]); add_grad("hg_norm_w", j, dnw_h[0].reshape(4, LANES).sum(0))
            (dlb,), (dh,) = rec["hg"].bwd([do_hg], dx_into={0: dh})
            d_lb[j] = dlb
            (ddtb, dalog, ddsk), (dxbc_c, dh) = rec["ssd"].bwd([dy_ssd], dx_into={1: dh})
            add_grad("ssd_dt_bias", j, ddtb[0].reshape(8, 64).sum(-1))
            add_grad("ssd_a_log", j, dalog[0].reshape(8, 64).sum(-1))
            add_grad("ssd_d", j, ddsk[0].reshape(8, 64).sum(-1))
            dcp, (dh,) = rec["conv"].bwd([dxbc_c], dx_into={0: dh})
            add_grad("ssd_conv_w", j, jnp.stack([dcp[k][0] for k in range(4)]))
            add_grad("ssd_conv_b", j, dcp[4][0])
        else:
            _, (dyc, dhs, dh) = rec["post"].bwd([dycat])
            _, (da_s, du_s) = rec["scan"].bwd([dhs])
            (dwa, dba, dwx, dbx, dlam), (dxc,) = rec["gate"].bwd([da_s, du_s])
            add_grad("rg_wa", j, _block_diag_grad(dwa)); add_grad("rg_wx", j, _block_diag_grad(dwx))
            add_grad("rg_ba", j, dba[0]); add_grad("rg_bx", j, dbx[0]); add_grad("rg_lambda", j, dlam[0])
            dcp, (dh,) = rec["conv"].bwd([dxc], dx_into={0: dh})
            add_grad("rg_conv_w", j, jnp.stack([dcp[k][0] for k in range(4)]))
            add_grad("rg_conv_b", j, dcp[4][0])
            (dsink,), (dh,) = rec["swa"].bwd([dyc], dx_into={0: dh})
            add_grad("swa_sinks", j, dsink.sum(-1))
        dw_matmul(f"{kind}_in_dw{j}", rec["x_in"], dh, kind + "_w_in", 2, j)
        dx = _matmul(f"{kind}_in_dx{j}", dh, w[kind + "_w_in"], "nt", b_lead=j, add=dx_res)

    _, (dl0, dl1) = lb_op.bwd(d_lb)
    out = dict(stacks)
    out["hg_lower"] = jnp.concatenate([dl0, dl1], axis=0)
    for name, parts in grads.items():
        keys = sorted(parts)
        if isinstance(keys[0], tuple):
            out[name] = jnp.stack([jnp.stack([parts[(l, s)] for s in range(2)]) for l in range(DEPTH)])
        else:
            out[name] = jnp.stack([parts[k] for k in keys])
    return loss, dx, out


def _local_step_full(x, target, full):
    loss, dx, kg = _local_step(x, target, _prep_weights(full))
    return loss, dx, _unprep_grads(kg)


def kernel(x, ab_w_in, ssd_conv_w, ssd_conv_b, ssd_dt_bias, ssd_a_log, ssd_d, ssd_norm_w, hg_lower, hg_norm_w, ab_w_out, cd_w_in, swa_sinks, rg_conv_w, rg_conv_b, rg_wa, rg_ba, rg_wx, rg_bx, rg_lambda, cd_w_out, ffn_w_up, ffn_conv_w, ffn_conv_b, ffn_w_down, ln_g, ln_b, loss_target, m_ab_w_in, m_ssd_conv_w, m_ssd_conv_b, m_ssd_dt_bias, m_ssd_a_log, m_ssd_d, m_ssd_norm_w, m_hg_lower, m_hg_norm_w, m_ab_w_out, m_cd_w_in, m_swa_sinks, m_rg_conv_w, m_rg_conv_b, m_rg_wa, m_rg_ba, m_rg_wx, m_rg_bx, m_rg_lambda, m_cd_w_out, m_ffn_w_up, m_ffn_conv_w, m_ffn_conv_b, m_ffn_w_down, m_ln_g, m_ln_b, v_ab_w_in, v_ssd_conv_w, v_ssd_conv_b, v_ssd_dt_bias, v_ssd_a_log, v_ssd_d, v_ssd_norm_w, v_hg_lower, v_hg_norm_w, v_ab_w_out, v_cd_w_in, v_swa_sinks, v_rg_conv_w, v_rg_conv_b, v_rg_wa, v_rg_ba, v_rg_wx, v_rg_bx, v_rg_lambda, v_cd_w_out, v_ffn_w_up, v_ffn_conv_w, v_ffn_conv_b, v_ffn_w_down, v_ln_g, v_ln_b):
    args = dict(locals())
    wts = {n: args[n] for n in WEIGHTS}
    mom = {n: args["m_" + n] for n in WEIGHTS}
    var = {n: args["v_" + n] for n in WEIGHTS}
    axis = dict(SHARDED)
    big = list(MATMUL_W)
    small = [n for n, _ in SHARDED if n not in MATMUL_W]

    got = _all_gather("gather_w", [wts[n].astype(MM_DTYPE) for n in big] + [_as2d(wts[n]) for n in small])
    full = {n: wts[n] for n in REPLICATED}
    for n, g in zip(big + small, got):
        full[n] = _merge_shards(g.reshape((N_DEV,) + wts[n].shape), axis[n])

    loss, grad_x, grads = _local_step_full(x[0], loss_target[0], full)
    loss = lax.psum(loss, ("x", "y", "c"))

    send = [_split_shards(grads[n], axis[n]).astype(MM_DTYPE) for n in big]
    send += [_split_shards(grads[n], axis[n]).reshape((N_DEV,) + _as2d(wts[n]).shape) for n in small]
    recv = _all_to_all("exchange_grads", send)
    rep = _all_gather("gather_replicated_grads", [_as2d(grads[n]) for n in REPLICATED])

    new = {}
    for n, parts in zip(big, recv[:len(big)]):
        new[n] = _adamw_big("adamw_" + n, parts, wts[n], mom[n], var[n])
    names = small + REPLICATED
    res = _adamw_small("adamw_small", [(p, _as2d(wts[n]), _as2d(mom[n]), _as2d(var[n]))
                                       for n, p in zip(names, list(recv[len(big):]) + list(rep))])
    for n, r in zip(names, res):
        new[n] = [a.reshape(wts[n].shape) for a in r]

    outs = [loss, grad_x[None]]
    for kind in range(4):
        outs += [new[n][kind] for n in WEIGHTS]
    return tuple(outs)
```

```python
import math

import numpy as np
import jax
import jax.numpy as jnp
from jax import lax
from jax.experimental import pallas as pl
from jax.experimental.pallas import tpu as pltpu

F32 = jnp.float32
BF16 = jnp.bfloat16
HI = lax.Precision.HIGHEST
MM_DTYPE = BF16

DEPTH = 4
N_DEV = 8
LN_EPS = 1e-5
RMS_EPS = 1e-6
MASK_VALUE = -1e9
ALPHA = (2 * DEPTH) ** 0.25
RG_C = 8.0
FFN_DIM = 2816
SSD_CHUNK = 128
HG_CHUNK = 64
HG_STEP = 128
SWA_BLOCK = 128
LANES = 128
VMEM_LIMIT = 56 * 1024 * 1024

ADAM_LR, ADAM_B1, ADAM_B2, ADAM_EPS, ADAM_WD, ADAM_STEP = 0.001, 0.9, 0.999, 1e-08, 0.01, 10

AB_HEADS, AB_DT, AB_ZG, AB_XBC, AB_PAD = 0, 1536, 2048, 3072, 3840
CD_QKV, CD_GATE, CD_XR, CD_PAD = 0, 1024, 1536, 2048
FFN_BLK = 256


def _cols(x, lo, hi):
    n = x.shape[1]

    @jax.custom_vjp
    def f(x):
        return x[:, lo:hi]

    def bwd(_, g):
        parts = []
        if lo > 0:
            parts.append(jnp.zeros((g.shape[0], lo), g.dtype))
        parts.append(g)
        if hi < n:
            parts.append(jnp.zeros((g.shape[0], n - hi), g.dtype))
        return (jnp.concatenate(parts, axis=1) if len(parts) > 1 else g,)

    f.defvjp(lambda x: (f(x), None), bwd)
    return f(x)


def _rows(x, lo, hi):
    n = x.shape[0]

    @jax.custom_vjp
    def f(x):
        return x[lo:hi, :]

    def bwd(_, g):
        parts = []
        if lo > 0:
            parts.append(jnp.zeros((lo, g.shape[1]), g.dtype))
        parts.append(g)
        if hi < n:
            parts.append(jnp.zeros((n - hi, g.shape[1]), g.dtype))
        return (jnp.concatenate(parts, axis=0) if len(parts) > 1 else g,)

    f.defvjp(lambda x: (f(x), None), bwd)
    return f(x)


def _roll(x, shift, axis):
    n = x.shape[axis]
    shift = shift % n
    if shift == 0:
        return x

    @jax.custom_vjp
    def f(x):
        return pltpu.roll(x, shift, axis)

    f.defvjp(lambda x: (f(x), None), lambda _, g: (pltpu.roll(g, n - shift, axis),))
    return f(x)


def _dot(a, b, precision=None):
    return lax.dot_general(a, b, (((1,), (0,)), ((), ())), precision=precision, preferred_element_type=F32)


def _dot_nt(a, b, precision=None):
    return lax.dot_general(a, b, (((1,), (1,)), ((), ())), precision=precision, preferred_element_type=F32)


def _dot_tn(a, b, precision=None):
    return lax.dot_general(a, b, (((0,), (0,)), ((), ())), precision=precision, preferred_element_type=F32)


def _sigmoid(x):
    return 0.5 * jnp.tanh(0.5 * x) + 0.5


def _silu(x):
    return x * _sigmoid(x)


def _softplus(x):
    return jnp.maximum(x, 0.0) + jnp.log(1.0 + jnp.exp(-jnp.abs(x)))


def _gelu_tanh(x):
    c = math.sqrt(2.0 / math.pi)
    return 0.5 * x * (1.0 + jnp.tanh(c * (x + 0.044715 * (x * x * x))))


def _iota(shape, axis):
    return lax.broadcasted_iota(jnp.int32, shape, axis)


def _lane_mask(lo, hi, width=LANES):
    lane = _iota((1, width), 1)
    return ((lane >= lo) & (lane < hi)).astype(F32)


def _mesh_pos():
    return lax.axis_index("x"), lax.axis_index("y"), lax.axis_index("c")


def _carry_shapes(carry):
    return [jax.ShapeDtypeStruct((N_DEV,) + a.shape if kind == "gather" else a.shape, a.dtype) for kind, a in carry]


def _carry_scratch(carry):
    n = len(carry)
    if n == 0:
        return []
    return [pltpu.SemaphoreType.DMA((n, N_DEV - 1)), pltpu.SemaphoreType.DMA((n, N_DEV - 1)),
            pltpu.SemaphoreType.DMA((n,))]


def _carry_run(start, kinds, in_refs, out_refs, send_sems, recv_sems, local_sems):
    x, y, cc = _mesh_pos()
    me = 4 * x + 2 * y + cc
    for i, kind in enumerate(kinds):
        mine = in_refs[i] if kind == "gather" else in_refs[i].at[me]
        local = pltpu.make_async_copy(mine, out_refs[i].at[me], local_sems.at[i])
        remote = []
        for k in range(1, N_DEV):
            px, py, pc = x ^ ((k >> 2) & 1), y ^ ((k >> 1) & 1), cc ^ (k & 1)
            src = in_refs[i] if kind == "gather" else in_refs[i].at[4 * px + 2 * py + pc]
            remote.append(pltpu.make_async_remote_copy(
                src_ref=src, dst_ref=out_refs[i].at[me],
                send_sem=send_sems.at[i, k - 1], recv_sem=recv_sems.at[i, k - 1],
                device_id=(px, py, pc), device_id_type=pl.DeviceIdType.MESH))
        if start:
            local.start()
            for cp in remote:
                cp.start()
        else:
            for cp in remote:
                cp.wait_recv()
            for cp in remote:
                cp.wait_send()
            local.wait()


def _remote_copies(name, carry):
    n = len(carry)
    kinds = [k for k, _ in carry]

    def body(*refs):
        sems = refs[2 * n:]
        _carry_run(True, kinds, refs[:n], refs[n:2 * n], *sems)
        _carry_run(False, kinds, refs[:n], refs[n:2 * n], *sems)

    return pl.pallas_call(
        body, name=name, out_shape=_carry_shapes(carry),
        in_specs=[pl.BlockSpec(memory_space=pl.ANY)] * n, out_specs=[pl.BlockSpec(memory_space=pl.ANY)] * n,
        scratch_shapes=_carry_scratch(carry),
    )(*[a for _, a in carry])


def _cparams(sem):
    return pltpu.CompilerParams(dimension_semantics=sem, vmem_limit_bytes=VMEM_LIMIT)


def _chunk_fwd(name, fn, grid, params, xs, outs, state_shapes, carry=()):
    n_g, n_c = grid
    n_p, n_x, n_o, n_s, n_r = len(params), len(xs), len(outs), len(state_shapes), len(carry)
    kinds = [k for k, _ in carry]

    def body(*refs):
        i = 0
        p_refs = refs[i:i + n_p]; i += n_p
        x_refs = refs[i:i + n_x]; i += n_x
        ci_refs = refs[i:i + n_r]; i += n_r
        o_refs = refs[i:i + n_o]; i += n_o
        sv_refs = refs[i:i + n_s]; i += n_s
        co_refs = refs[i:i + n_r]; i += n_r
        st_refs = refs[i:i + n_s]; i += n_s
        sems = refs[i:]
        g, c = pl.program_id(0), pl.program_id(1)

        if n_r:
            @pl.when((g == 0) & (c == 0))
            def _():
                _carry_run(True, kinds, ci_refs, co_refs, *sems)

        @pl.when(c == 0)
        def _():
            for s in st_refs:
                s[...] = jnp.zeros(s.shape, s.dtype)

        st = [s[...] for s in st_refs]
        ys, new_st = fn(c, [p[...] for p in p_refs], [x[...].astype(F32) for x in x_refs], st)
        for o, y in zip(o_refs, ys):
            o[...] = y.astype(o.dtype)
        for sv, s in zip(sv_refs, st):
            sv[0, 0] = s
        for s_ref, s in zip(st_refs, new_st):
            s_ref[...] = s

        if n_r:
            @pl.when((g == n_g - 1) & (c == n_c - 1))
            def _():
                _carry_run(False, kinds, ci_refs, co_refs, *sems)

    any_spec = pl.BlockSpec(memory_space=pl.ANY)
    in_specs = [pl.BlockSpec(b, m) for _, b, m in params] + [pl.BlockSpec(b, m) for _, b, m in xs] + [any_spec] * n_r
    out_specs = [pl.BlockSpec(b, m) for _, b, m, _ in outs]
    out_shape = [jax.ShapeDtypeStruct(s, d) for s, _, _, d in outs]
    for shp in state_shapes:
        out_specs.append(pl.BlockSpec((1, 1) + shp, lambda g, c, n=len(shp): (g, c) + (0,) * n))
        out_shape.append(jax.ShapeDtypeStruct((n_g, n_c) + shp, F32))
    out_specs += [any_spec] * n_r
    out_shape += _carry_shapes(carry)
    res = pl.pallas_call(
        body, name=name, grid=grid, in_specs=in_specs, out_specs=out_specs, out_shape=out_shape,
        scratch_shapes=[pltpu.VMEM(shp, F32) for shp in state_shapes] + _carry_scratch(carry),
        compiler_params=_cparams(("arbitrary", "arbitrary")),
    )(*[a for a, _, _ in params], *[a for a, _, _ in xs], *[a for _, a in carry])
    return list(res[:n_o]), list(res[n_o:n_o + n_s]), list(res[n_o + n_s:])


def _chunk_bwd(name, fn, grid, params, xs, saved, dys, state_shapes, dx_dtypes, dx_into):
    n_g, n_c = grid
    n_p, n_x, n_s, n_y = len(params), len(xs), len(state_shapes), len(dys)
    into = sorted(dx_into)
    n_a = len(into)

    def rev(m):
        return lambda g, c: m(g, n_c - 1 - c)

    def body(*refs):
        i = 0
        p_refs = refs[i:i + n_p]; i += n_p
        x_refs = refs[i:i + n_x]; i += n_x
        sv_refs = refs[i:i + n_s]; i += n_s
        dy_refs = refs[i:i + n_y]; i += n_y
        i += n_a
        dp_refs = refs[i:i + n_p]; i += n_p
        dx_refs = refs[i:i + n_x]; i += n_x
        ds_refs = refs[i:]
        c = pl.program_id(1)
        chunk = n_c - 1 - c

        @pl.when(c == 0)
        def _():
            for s in ds_refs:
                s[...] = jnp.zeros(s.shape, s.dtype)
            for d in dp_refs:
                d[...] = jnp.zeros(d.shape, d.dtype)

        pv = [p[...] for p in p_refs]
        xv = [x[...].astype(F32) for x in x_refs]
        sv = [s[0, 0] for s in sv_refs]
        _, vjp = jax.vjp(lambda p, x, s: fn(chunk, p, x, s), pv, xv, sv)
        dp, dx, ds = vjp(([d[...].astype(F32) for d in dy_refs], [s[...] for s in ds_refs]))
        for r, v in zip(dp_refs, dp):
            r[...] += v
        for r, v in zip(dx_refs, dx):
            r[...] = v.astype(r.dtype)
        for r, v in zip(ds_refs, ds):
            r[...] = v

    in_specs = [pl.BlockSpec(b, rev(m)) for _, b, m in params] + [pl.BlockSpec(b, rev(m)) for _, b, m in xs]
    for shp in state_shapes:
        in_specs.append(pl.BlockSpec((1, 1) + shp, lambda g, c, n=len(shp): (g, n_c - 1 - c) + (0,) * n))
    in_specs += [pl.BlockSpec(b, rev(m)) for _, b, m in dys]
    in_specs += [pl.BlockSpec(memory_space=pl.ANY)] * n_a
    out_specs = [pl.BlockSpec(b, rev(m)) for _, b, m in params] + [pl.BlockSpec(b, rev(m)) for _, b, m in xs]
    out_shape = [jax.ShapeDtypeStruct(a.shape, F32) for a, _, _ in params]
    out_shape += [jax.ShapeDtypeStruct(a.shape, d) for (a, _, _), d in zip(xs, dx_dtypes)]
    first_alias = n_p + n_x + n_s + n_y
    aliases = {first_alias + k: n_p + xi for k, xi in enumerate(into)}
    res = pl.pallas_call(
        body, name=name, grid=grid, in_specs=in_specs, out_specs=out_specs, out_shape=out_shape,
        scratch_shapes=[pltpu.VMEM(shp, F32) for shp in state_shapes],
        input_output_aliases=aliases,
        compiler_params=_cparams(("arbitrary", "arbitrary")),
    )(*[a for a, _, _ in params], *[a for a, _, _ in xs], *saved, *[a for a, _, _ in dys],
      *[dx_into[xi] for xi in into])
    return list(res[:n_p]), list(res[n_p:])


class _Op:
    def __init__(self, name, fn, grid, params, xs, outs, state_shapes=(), dx_dtypes=None, comm=None):
        self.name, self.fn, self.grid = name, fn, grid
        self.params, self.xs, self.outs, self.state_shapes = params, xs, outs, list(state_shapes)
        self.dx_dtypes = dx_dtypes or [F32] * len(xs)
        reqs = comm.take(name + "_fwd") if comm is not None else []
        self.ys, self.saved, got = _chunk_fwd(name + "_fwd", fn, grid, params, xs, outs, self.state_shapes,
                                              carry=[(k, a) for k, a, _ in reqs])
        for (_, _, done), g in zip(reqs, got):
            done(g)

    def bwd(self, dys, dx_into=None):
        dy_defs = [(d, b, m) for d, (_, b, m, _) in zip(dys, self.outs)]
        return _chunk_bwd(self.name + "_bwd", self.fn, self.grid, self.params, self.xs, self.saved, dy_defs,
                          self.state_shapes, self.dx_dtypes, dx_into or {})


def _whole(a):
    nd = a.ndim
    return (a, a.shape, lambda g, c: (0,) * nd)


def _pick(n, prefs):
    for p in prefs:
        if n % p == 0:
            return p
    return n


def _mm_blocks(mode, m, n, k):
    bn = _pick(n, (1408, 1280, 1024, 768, 512, 256, 128))
    if mode == "tn":
        return _pick(m, (1408, 1024, 768, 512, 256, 128)), bn, _pick(k, (512, 256, 128))
    bk = k if k <= 2816 else _pick(k, (1920, 1408, 1024, 512, 256, 128))
    return _pick(m, (1024, 512, 256, 128)), bn, bk


def _matmul(name, a, b, mode, *, add=None, out_dtype=F32, comm=None):
    if mode == "nn":
        (m, k), n = a.shape, b.shape[1]
    elif mode == "nt":
        (m, k), n = a.shape, b.shape[0]
    else:
        (k, m), n = a.shape, b.shape[1]
    bm, bn, bk = _mm_blocks(mode, m, n, k)
    n_i, n_j, n_k = m // bm, n // bn, k // bk
    dims = {"nn": (((1,), (0,)), ((), ())), "nt": (((1,), (1,)), ((), ())), "tn": (((0,), (0,)), ((), ()))}[mode]
    has_add = add is not None
    reqs = comm.take(name) if comm is not None else []
    carry = [(kind, arr) for kind, arr, _ in reqs]
    kinds = [kind for kind, _ in carry]
    n_r = len(carry)

    def body(*refs):
        i = 2
        a_ref, b_ref = refs[0], refs[1]
        c_ref = refs[i] if has_add else None
        i += has_add
        ci_refs = refs[i:i + n_r]; i += n_r
        o_ref = refs[i]; i += 1
        co_refs = refs[i:i + n_r]; i += n_r
        acc = refs[i]; i += 1
        sems = refs[i:]
        ii, jj, kk = pl.program_id(0), pl.program_id(1), pl.program_id(2)

        if n_r:
            @pl.when((ii == 0) & (jj == 0) & (kk == 0))
            def _():
                _carry_run(True, kinds, ci_refs, co_refs, *sems)

        part = lax.dot_general(a_ref[...].astype(MM_DTYPE), b_ref[...].astype(MM_DTYPE), dims,
                               preferred_element_type=F32)

        def finish(r):
            if has_add:
                r = r + c_ref[...]
            o_ref[...] = r.astype(o_ref.dtype)

        if n_k == 1:
            finish(part)
        else:
            @pl.when(kk == 0)
            def _():
                acc[...] = part

            @pl.when((kk > 0) & (kk < n_k - 1))
            def _():
                acc[...] += part

            @pl.when(kk == n_k - 1)
            def _():
                finish(acc[...] + part)

        if n_r:
            @pl.when((ii == n_i - 1) & (jj == n_j - 1) & (kk == n_k - 1))
            def _():
                _carry_run(False, kinds, ci_refs, co_refs, *sems)

    if mode == "nn":
        a_spec = pl.BlockSpec((bm, bk), lambda i, j, kk: (i, kk))
        b_spec = pl.BlockSpec((bk, bn), lambda i, j, kk: (kk, j))
    elif mode == "nt":
        a_spec = pl.BlockSpec((bm, bk), lambda i, j, kk: (i, kk))
        b_spec = pl.BlockSpec((bn, bk), lambda i, j, kk: (j, kk))
    else:
        a_spec = pl.BlockSpec((bk, bm), lambda i, j, kk: (kk, i))
        b_spec = pl.BlockSpec((bk, bn), lambda i, j, kk: (kk, j))
    any_spec = pl.BlockSpec(memory_space=pl.ANY)
    in_specs, args = [a_spec, b_spec], [a, b]
    if has_add:
        in_specs.append(pl.BlockSpec((bm, bn), lambda i, j, kk: (i, j)))
        args.append(add)
    res = pl.pallas_call(
        body, name=name, grid=(n_i, n_j, n_k), in_specs=in_specs + [any_spec] * n_r,
        out_specs=[pl.BlockSpec((bm, bn), lambda i, j, kk: (i, j))] + [any_spec] * n_r,
        out_shape=[jax.ShapeDtypeStruct((m, n), out_dtype)] + _carry_shapes(carry),
        scratch_shapes=[pltpu.VMEM((bm, bn) if n_k > 1 else (8, LANES), F32)] + _carry_scratch(carry),
        compiler_params=_cparams(("arbitrary", "arbitrary", "arbitrary")),
    )(*args, *[arr for _, arr in carry])
    for (_, _, done), g in zip(reqs, res[1:]):
        done(g)
    return res[0]


def _ln_res_fn(_, p, x, st):
    g, b = p
    xin, m = x
    pre = ALPHA * xin + m
    mu = jnp.mean(pre, -1, keepdims=True)
    d = pre - mu
    var = jnp.mean(d * d, -1, keepdims=True)
    return [d * lax.rsqrt(var + LN_EPS) * g + b], []


def _make_conv_fn(taps, act):
    def fn(_, p, x, st):
        ws, b = p[:taps], p[taps]
        (xin,), (prev,) = x, st
        n = xin.shape[0]
        ext = jnp.concatenate([prev, xin], axis=0)
        y = b
        for k in range(taps):
            y = y + ws[k] * _rows(_roll(ext, taps - 1 - k, 0), 8, 8 + n)
        if act:
            y = _silu(y)
        return [y], [_rows(xin, n - 8, n)]

    return fn


def _ffn_act_fn(_, p, x, st):
    ws, b = p[:3], p[3]
    (xin,), (prev,) = x, st
    n = xin.shape[0]
    ext = jnp.concatenate([prev, xin], axis=0)
    y = b
    for k in range(3):
        y = y + ws[k] * _rows(_roll(ext, 2 - k, 0), 8, 8 + n)
    return [_silu(_cols(y, 0, FFN_BLK)) * _cols(y, FFN_BLK, 2 * FFN_BLK)], [_rows(xin, n - 8, n)]


def _ssd_fn(_, p, x, st):
    dtb, alog, dsk = p
    xbc, dtr = x
    L = SSD_CHUNK
    tril = _iota((L, L), 0) >= _iota((L, L), 1)
    trilf = tril.astype(F32)
    xs, bm, cm = _cols(xbc, 0, 512), _cols(xbc, 512, 640), _cols(xbc, 640, 768)
    dt = _softplus(dtr + dtb)
    da = dt * (-jnp.exp(alog))
    cs = _dot(trilf, da, HI)
    tot = jnp.sum(da, axis=0, keepdims=True)
    xc = xs * dt
    xdec = xc * jnp.exp(tot - cs)
    ecs = jnp.exp(cs)
    etot = jnp.exp(tot)
    ys, new_st = [], []
    for pr in range(4):
        lo, hi = LANES * pr, LANES * (pr + 1)
        grp = pr // 2
        c_g = cm * _lane_mask(64 * grp, 64 * grp + 64)
        gmat = _dot_nt(c_g, bm)
        cs_p, xc_p = _cols(cs, lo, hi), _cols(xc, lo, hi)
        yd = jnp.zeros((L, LANES), F32)
        for half in range(2):
            sel = (_iota((LANES, L), 0) == 64 * half).astype(F32)
            col = _dot(cs_p, sel, HI)
            diff = col - col.T
            dec = jnp.where(tril, jnp.exp(jnp.where(tril, diff, 0.0)), 0.0)
            yd = yd + _dot(gmat * dec, xc_p) * _lane_mask(64 * half, 64 * half + 64)
        s_in = st[pr]
        y_off = _dot(c_g, s_in) * _cols(ecs, lo, hi)
        ys.append(yd + y_off + _cols(dsk, lo, hi) * _cols(xs, lo, hi))
        new_st.append(s_in * _cols(etot, lo, hi) + _dot_tn(bm, _cols(xdec, lo, hi)))
    return [jnp.concatenate(ys, axis=1)], new_st


def _hg_chunk(hq, hf, hi, lb, s_t):
    L = HG_CHUNK
    q = _silu(hq)
    logf = jnp.log(lb + (1.0 - lb) * _sigmoid(hf))
    k = (1.0 - lb) * _sigmoid(-hf)
    ti, si = _iota((L, L), 0), _iota((L, L), 1)
    bc = _dot((ti >= si).astype(F32), logf, HI)
    tot = jnp.sum(logf, axis=0, keepdims=True)
    attn = jnp.where(ti == si, _dot_nt(q, k), 0.0)
    t6, s6 = _iota((6 * L, 1), 0), _iota((6 * L, L), 1)
    lvl6 = t6 >> 6
    row6 = t6 & (L - 1)
    blk6 = L >> lvl6
    pos6 = row6 & (blk6 - 1)
    piv6 = row6 - pos6 + (blk6 >> 1)
    bcp6 = _dot((s6 == piv6).astype(F32), bc, HI)
    t1 = _iota((L, 1), 0)
    for lvl in range(6):
        blk = L >> lvl
        upper = (t1 & (blk - 1)) >= blk // 2
        bcp = _rows(bcp6, L * lvl, L * (lvl + 1))
        qq = jnp.where(upper, q * jnp.exp(jnp.where(upper, bc - bcp, 0.0)), 0.0)
        kk = jnp.where(upper, 0.0, k * jnp.exp(jnp.where(upper, 0.0, bcp - bc)))
        same = (ti >> (6 - lvl)) == (si >> (6 - lvl))
        attn = attn + jnp.where(same, _dot_nt(qq, kk), 0.0)
    out = _dot(attn, hi) + _dot_nt(q * jnp.exp(bc), s_t)
    new_s = s_t * jnp.exp(tot) + _dot_tn(hi, k * jnp.exp(tot - bc))
    return out, new_s


def _hg_fn(_, p, x, st):
    (lb,) = p
    (xin,) = x
    st = list(st)
    outs = []
    for sub in range(HG_STEP // HG_CHUNK):
        rows = _rows(xin, HG_CHUNK * sub, HG_CHUNK * (sub + 1))
        heads = []
        for h in range(4):
            o = 3 * LANES * h
            out, st[h] = _hg_chunk(_cols(rows, o, o + LANES), _cols(rows, o + LANES, o + 2 * LANES),
                                   _cols(rows, o + 2 * LANES, o + 3 * LANES), _cols(lb, LANES * h, LANES * (h + 1)),
                                   st[h])
            heads.append(out)
        outs.append(jnp.concatenate(heads, axis=1))
    return [jnp.concatenate(outs, axis=0)], st


def _swa_fn(chunk, p, x, st):
    (sinks,) = p
    (xin,) = x
    q, k, v = _cols(xin, 0, 512), _cols(xin, 512, 640), _cols(xin, 640, 768)
    kp, vp = st
    T = SWA_BLOCK
    kc = jnp.concatenate([kp, k], axis=0)
    vc = jnp.concatenate([vp, v], axis=0)
    qi, kj = _iota((T, 2 * T), 0), _iota((T, 2 * T), 1)
    rel = qi + T - kj
    mask = (rel >= 0) & (rel < T) & ((kj >= T) | (chunk > 0))
    srow = _iota((8, LANES), 0)
    outs = []
    for pr in range(4):
        grp = pr // 2
        gm = _lane_mask(64 * grp, 64 * grp + 64)
        km, vm = kc * gm, vc * gm
        q2 = _cols(q, LANES * pr, LANES * (pr + 1))
        o2 = jnp.zeros((T, LANES), F32)
        for half in range(2):
            hm = _lane_mask(64 * half, 64 * half + 64)
            qh = q2 * hm
            if half != grp:
                qh = _roll(qh, 64, 1)
            s = _dot_nt(qh, km) * 0.125
            s = jnp.where(mask, s, MASK_VALUE)
            sink = jnp.mean(jnp.sum(jnp.where(srow == 2 * pr + half, sinks, 0.0), axis=0, keepdims=True),
                            axis=-1, keepdims=True)
            mx = lax.stop_gradient(jnp.maximum(jnp.max(s, axis=-1, keepdims=True), sink))
            e = jnp.exp(s - mx)
            den = jnp.sum(e, axis=-1, keepdims=True) + jnp.exp(sink - mx)
            o = _dot(e / den, vm)
            if half != grp:
                o = _roll(o, 64, 1)
            o2 = o2 + o * hm
        outs.append(o2)
    return [jnp.concatenate(outs, axis=1)], [k, v]


def _rg_gate_fn(_, p, x, st):
    wa, ba, wx, bx, lam = p
    (xc,) = x
    r = _sigmoid(_dot(xc, wa) + ba)
    i = _sigmoid(_dot(xc, wx) + bx)
    log_a = -RG_C * r * _softplus(-lam)
    a = jnp.exp(log_a)
    t = jnp.tanh(log_a)
    one_minus_a2 = -2.0 * t / (1.0 - t)
    u = jnp.sqrt(jnp.maximum(one_minus_a2, 0.0)) * (i * xc)
    return [a, u], []


def _rg_scan_fn(_, p, x, st):
    a, u = x
    (prev,) = st
    n = a.shape[0]
    row = _iota((n, 1), 0)
    s = 1
    while s < n:
        keep = row >= s
        a_s, u_s = _roll(a, s, 0), _roll(u, s, 0)
        u = jnp.where(keep, a * u_s + u, u)
        a = jnp.where(keep, a * a_s, a)
        s *= 2
    h_in = jnp.sum(jnp.where(_iota((8, 1), 0) == 7, prev, 0.0), axis=0, keepdims=True)
    h = u + a * h_in
    return [h], [_rows(h, n - 8, n)]


def _ab_post_fn(_, p, x, st):
    nw_ssd, nw_hg = p
    y, o, zg = x
    z, hgate = _cols(zg, 0, 512), _cols(zg, 512, 1024)
    lane = _iota((1, 512), 1)
    ya = y * _silu(z)
    sq = ya * ya
    inv = jnp.zeros_like(ya)
    for g in range(2):
        mk = (lane >= 256 * g) & (lane < 256 * (g + 1))
        ms = jnp.sum(jnp.where(mk, sq, 0.0), axis=-1, keepdims=True) / 256.0
        inv = jnp.where(mk, lax.rsqrt(ms + RMS_EPS), inv)
    ya = ya * inv * nw_ssd
    so = o * o
    inv = jnp.zeros_like(o)
    for h in range(4):
        mk = (lane >= 128 * h) & (lane < 128 * (h + 1))
        ms = jnp.sum(jnp.where(mk, so, 0.0), axis=-1, keepdims=True) / 128.0
        inv = jnp.where(mk, lax.rsqrt(ms + RMS_EPS), inv)
    yb = o * inv * nw_hg * _silu(hgate)
    return [jnp.concatenate([ya, yb], axis=1)], []


def _cd_post_fn(_, p, x, st):
    yc, h, gate = x
    return [jnp.concatenate([yc, h * _gelu_tanh(gate)], axis=1)], []


def _lb_fn(_, p, x, st):
    l0, l1 = x
    mx = lax.stop_gradient(jnp.maximum(l0, l1))
    e0, e1 = jnp.exp(l0 - mx), jnp.exp(l1 - mx)
    s0, s1 = e0 / (e0 + e1), e1 / (e0 + e1)
    return [jnp.clip(s0 - s0, 0.0, 1.0), jnp.clip((s0 + s1) - s0, 0.0, 1.0)], []


def _loss_kernel(y, target):
    t, d = y.shape
    bt = _pick(t, (512, 256, 128))

    def body(y_ref, t_ref, dy_ref, l_ref):
        @pl.when(pl.program_id(0) == 0)
        def _():
            l_ref[...] = jnp.zeros(l_ref.shape, F32)

        e = y_ref[...] - t_ref[...]
        dy_ref[...] = e * (1.0 / d)
        l_ref[...] += jnp.sum(e * e, axis=0, keepdims=True) * (0.5 / d)

    dy, part = pl.pallas_call(
        body, name="loss", grid=(t // bt,),
        in_specs=[pl.BlockSpec((bt, d), lambda i: (i, 0)), pl.BlockSpec((bt, d), lambda i: (i, 0))],
        out_specs=[pl.BlockSpec((bt, d), lambda i: (i, 0)), pl.BlockSpec((1, d), lambda i: (0, 0))],
        out_shape=[jax.ShapeDtypeStruct((t, d), F32), jax.ShapeDtypeStruct((1, d), F32)],
        compiler_params=_cparams(("arbitrary",)),
    )(y, target)
    return dy, jnp.sum(part)


def _adamw_math(parts, w_, m_, v_):
    c1 = 1.0 / (1.0 - ADAM_B1 ** ADAM_STEP)
    c2 = 1.0 / (1.0 - ADAM_B2 ** ADAM_STEP)
    g = parts[0].astype(F32)
    for s in range(1, N_DEV):
        g = g + parts[s].astype(F32)
    nm = ADAM_B1 * m_ + (1.0 - ADAM_B1) * g
    nv = ADAM_B2 * v_ + (1.0 - ADAM_B2) * (g * g)
    return g, -ADAM_LR * ((nm * c1) / (jnp.sqrt(nv * c2) + ADAM_EPS) + ADAM_WD * w_), nm, nv


def _adamw_big(name, parts, w, m, v):
    n_l, r, c = w.shape
    br = _pick(r, (256, 176, 128, 64, 32, 16, 8))

    def body(*refs):
        p_refs, (w_ref, m_ref, v_ref), outs = refs[:n_l], refs[n_l:n_l + 3], refs[n_l + 3:]
        for l in range(n_l):
            @pl.when(pl.program_id(0) == l)
            def _(p_ref=p_refs[l]):
                res = _adamw_math([p_ref[s] for s in range(N_DEV)], w_ref[...], m_ref[...], v_ref[...])
                for ref, val in zip(outs, res):
                    ref[...] = val

    blk = pl.BlockSpec((None, br, c), lambda l, i: (l, i, 0))
    p_specs = [pl.BlockSpec((N_DEV, br, c), lambda l, i, k=k: (0, jnp.where(l == k, i, 0), 0)) for k in range(n_l)]
    return pl.pallas_call(
        body, name=name, grid=(n_l, r // br), in_specs=p_specs + [blk, blk, blk],
        out_specs=[blk] * 4, out_shape=[jax.ShapeDtypeStruct(w.shape, F32)] * 4,
        compiler_params=_cparams(("arbitrary", "arbitrary")),
    )(*parts, w, m, v)


def _adamw_small(name, items):
    n = len(items)

    def body(*refs):
        ins, outs = refs[:4 * n], refs[4 * n:]
        for i in range(n):
            p_ref, w_ref, m_ref, v_ref = ins[4 * i:4 * i + 4]
            res = _adamw_math([p_ref[s] for s in range(N_DEV)], w_ref[...], m_ref[...], v_ref[...])
            for ref, val in zip(outs[4 * i:4 * i + 4], res):
                ref[...] = val

    flat = [a for it in items for a in it]
    out_shape = [jax.ShapeDtypeStruct(it[1].shape, F32) for it in items for _ in range(4)]
    res = pl.pallas_call(
        body, name=name, out_shape=out_shape,
        in_specs=[pl.BlockSpec(memory_space=pltpu.VMEM)] * len(flat),
        out_specs=[pl.BlockSpec(memory_space=pltpu.VMEM)] * len(out_shape),
        compiler_params=pltpu.CompilerParams(vmem_limit_bytes=VMEM_LIMIT),
    )(*flat)
    return [res[4 * i:4 * i + 4] for i in range(n)]


SHARDED = [("ab_w_in", 2), ("ab_w_out", 1), ("cd_w_in", 2), ("cd_w_out", 1), ("ffn_w_up", 2), ("ffn_w_down", 1),
           ("ssd_conv_w", 2), ("rg_conv_w", 2), ("rg_conv_b", 1), ("rg_ba", 1), ("rg_bx", 1), ("rg_lambda", 1),
           ("ffn_conv_w", 2), ("ln_g", 2), ("ln_b", 2)]
MATMUL_W = ("ab_w_in", "ab_w_out", "cd_w_in", "cd_w_out", "ffn_w_up", "ffn_w_down")
REPLICATED = ["ssd_conv_b", "ssd_dt_bias", "ssd_a_log", "ssd_d", "ssd_norm_w", "hg_lower", "hg_norm_w", "swa_sinks",
              "rg_wa", "rg_wx", "ffn_conv_b"]
WEIGHTS = ["ab_w_in", "ssd_conv_w", "ssd_conv_b", "ssd_dt_bias", "ssd_a_log", "ssd_d", "ssd_norm_w", "hg_lower",
           "hg_norm_w", "ab_w_out", "cd_w_in", "swa_sinks", "rg_conv_w", "rg_conv_b", "rg_wa", "rg_ba", "rg_wx",
           "rg_bx", "rg_lambda", "cd_w_out", "ffn_w_up", "ffn_conv_w", "ffn_conv_b", "ffn_w_down", "ln_g", "ln_b"]


def _as2d(a):
    return a.reshape(-1, a.shape[-1])


def _merge_shards(g, axis):
    g = jnp.moveaxis(g, 0, axis)
    shp = g.shape
    return g.reshape(shp[:axis] + (shp[axis] * shp[axis + 1],) + shp[axis + 2:])


def _split_shards(full, axis):
    shp = full.shape
    g = full.reshape(shp[:axis] + (N_DEV, shp[axis] // N_DEV) + shp[axis + 1:])
    return jnp.moveaxis(g, axis, 0)


def _ab_pad(w):
    z, xbc, dt = w[..., 0:512], w[..., 512:1280], w[..., 1280:1288]
    hq, hf, hi, hg = w[..., 1288:1800], w[..., 1800:2312], w[..., 2312:2824], w[..., 2824:3336]
    heads = [p[..., LANES * h:LANES * (h + 1)] for h in range(4) for p in (hq, hf, hi)]
    return jnp.concatenate(heads + [jnp.repeat(dt, 64, axis=-1), z, hg, xbc], axis=-1)


def _ab_unpad(d):
    lead = d.shape[:-1]
    heads = d[..., :AB_DT].reshape(lead + (4, 3, LANES))
    hq, hf, hi = (heads[..., i, :].reshape(lead + (512,)) for i in range(3))
    dt = d[..., AB_DT:AB_ZG].reshape(lead + (8, 64)).sum(-1)
    z, hg, xbc = d[..., AB_ZG:AB_ZG + 512], d[..., AB_ZG + 512:AB_XBC], d[..., AB_XBC:AB_PAD]
    return jnp.concatenate([z, xbc, dt, hq, hf, hi, hg], axis=-1)


def _cd_pad(w):
    return jnp.concatenate([w[..., :768], jnp.zeros(w.shape[:-1] + (256,), w.dtype), w[..., 768:]], axis=-1)


def _cd_unpad(d):
    return jnp.concatenate([d[..., :768], d[..., CD_GATE:CD_PAD]], axis=-1)


def _ffn_perm(w):
    parts = []
    for i in range(FFN_DIM // FFN_BLK):
        parts += [w[..., FFN_BLK * i:FFN_BLK * (i + 1)], w[..., FFN_DIM + FFN_BLK * i:FFN_DIM + FFN_BLK * (i + 1)]]
    return jnp.concatenate(parts, axis=-1)


def _ffn_unperm(d):
    n = FFN_DIM // FFN_BLK
    gate = [d[..., 2 * FFN_BLK * i:2 * FFN_BLK * i + FFN_BLK] for i in range(n)]
    up = [d[..., 2 * FFN_BLK * i + FFN_BLK:2 * FFN_BLK * (i + 1)] for i in range(n)]
    return jnp.concatenate(gate + up, axis=-1)


def _block_diag(w):
    eye = jnp.eye(8, dtype=w.dtype)
    return jnp.einsum("gij,gh->gihj", w, eye).reshape(512, 512)


def _block_diag_grad(d):
    return jnp.stack([d[64 * g:64 * g + 64, 64 * g:64 * g + 64] for g in range(8)])


def _same(a):
    return a


BIG = {"ab_w_in": (1, _ab_pad, _ab_unpad), "ab_w_out": (0, _same, _same), "cd_w_in": (1, _cd_pad, _cd_unpad),
       "cd_w_out": (0, _same, _same), "ffn_w_up": (1, _ffn_perm, _ffn_unperm), "ffn_w_down": (0, _same, _same)}


class _MeshComm:
    def __init__(self, shards):
        self.shards, self.full, self.recv, self.posted = shards, {}, {}, {}

    def post(self, carrier, req):
        self.posted.setdefault(carrier, []).append(req)

    def take(self, carrier):
        return self.posted.pop(carrier, [])

    def take_all(self):
        reqs = [r for name in list(self.posted) for r in self.posted.pop(name)]
        return reqs

    def gather_req(self, name, layer):
        axis, prep, _ = BIG[name]

        def done(got):
            self.full[name, layer] = prep(_merge_shards(got, axis))

        return ("gather", self.shards[name][layer].astype(MM_DTYPE), done)

    def weight(self, name, layer):
        return self.full[name, layer]

    def grad_req(self, name, layer, d):
        axis, _, unprep = BIG[name]

        def done(got):
            self.recv[name, layer] = got

        return ("exchange", _split_shards(unprep(d), axis).astype(MM_DTYPE), done)


class _LocalComm:
    def __init__(self, full):
        self.full_w, self.grads = full, {}

    def post(self, carrier, req):
        pass

    def take(self, carrier):
        return []

    def gather_req(self, name, layer):
        return None

    def weight(self, name, layer):
        return BIG[name][1](self.full_w[name][layer].astype(MM_DTYPE))

    def grad_req(self, name, layer, d):
        self.grads[name, layer] = BIG[name][2](d)
        return None


def _row_vec(v):
    return v.reshape(1, -1)


def _heads64(v):
    return jnp.repeat(v, 64).reshape(1, 512)


def _local_step(x, target, w, comm):
    kinds = ["ab" if layer % 2 == 0 else "cd" for layer in range(DEPTH)]
    in_name = [f"{kinds[layer]}_in{layer // 2}" for layer in range(DEPTH)]
    core_name = [("hg" if layer % 2 == 0 else "swa") + f"{layer // 2}_fwd" for layer in range(DEPTH)]
    comm.post(in_name[0], comm.gather_req("ffn_w_up", 0))
    comm.post(core_name[0], comm.gather_req("ffn_w_down", 0))
    for layer in range(DEPTH - 1):
        nxt, nj = kinds[layer + 1], (layer + 1) // 2
        comm.post(in_name[layer], comm.gather_req(nxt + "_w_in", nj))
        comm.post(core_name[layer], comm.gather_req(nxt + "_w_out", nj))
        comm.post(core_name[layer], comm.gather_req("ffn_w_down", layer + 1))
        comm.post(f"ffn_up{layer}", comm.gather_req("ffn_w_up", layer + 1))

    t = x.shape[0]
    bt = _pick(t, (512, 256, 128))
    nb = t // bt
    bs = _pick(t, (256, 128))

    def rowop(name, fn, params, xs, widths_out, out_dtype=F32, dx_dtypes=None):
        outs = [((t, wd), (bt, wd), lambda g, c: (c, 0), out_dtype) for wd in widths_out]
        return _Op(name, fn, (1, nb), params, xs, outs, dx_dtypes=dx_dtypes)

    def rowblk(arr, width, first=0):
        return (arr, (bt, width), lambda g, c: (c, first))

    one_row = lambda g, c: (0, 0)
    lb_op = _Op("hg_lb", _lb_fn, (1, 1), [],
                [(w["hg_lower"][0:1], (1, 512), one_row), (w["hg_lower"][1:2], (1, 512), one_row)],
                [((1, 512), (1, 512), one_row, F32)] * 2)
    lb_all = lb_op.ys

    tape = []
    grads = {}

    def add_grad(name, idx, val):
        grads.setdefault(name, {})[idx] = val

    def dw_matmul(name, a, b, wname, idx, carrier):
        d = _matmul(name, a, b, "tn", comm=comm)
        comm.post(carrier, comm.grad_req(wname, idx, d))

    for layer in range(DEPTH):
        j = layer // 2
        rec = {"x_in": x}
        if layer % 2 == 0:
            h = _matmul(f"ab_in{j}", x, comm.weight("ab_w_in", j), "nn", comm=comm)
            conv_p = [_row_vec(w["ssd_conv_w"][j, k]) for k in range(4)] + [_row_vec(w["ssd_conv_b"][j])]
            conv = _Op(f"ssd_conv{j}", _make_conv_fn(4, True), (3, nb),
                       [(a, (1, 256), lambda g, c: (0, g)) for a in conv_p],
                       [(h, (bt, 256), lambda g, c: (c, AB_XBC // 256 + g))],
                       [((t, 768), (bt, 256), lambda g, c: (c, g), F32)], [(8, 256)], dx_dtypes=[MM_DTYPE])
            ssd = _Op(f"ssd{j}", _ssd_fn, (1, t // SSD_CHUNK),
                      [_whole(_heads64(w["ssd_dt_bias"][j])), _whole(_heads64(w["ssd_a_log"][j])),
                       _whole(_heads64(w["ssd_d"][j]))],
                      [(conv.ys[0], (SSD_CHUNK, 768), lambda g, c: (c, 0)),
                       (h, (SSD_CHUNK, 512), lambda g, c: (c, AB_DT // 512))],
                      [((t, 512), (SSD_CHUNK, 512), lambda g, c: (c, 0), F32)], [(LANES, LANES)] * 4,
                      dx_dtypes=[F32, MM_DTYPE])
            hg = _Op(f"hg{j}", _hg_fn, (1, t // HG_STEP), [_whole(lb_all[j])],
                     [(h, (HG_STEP, AB_DT), lambda g, c: (c, 0))],
                     [((t, 512), (HG_STEP, 512), lambda g, c: (c, 0), F32)], [(LANES, LANES)] * 4,
                     dx_dtypes=[MM_DTYPE], comm=comm)
            post = rowop(f"ab_post{j}", _ab_post_fn,
                         [_whole(_row_vec(w["ssd_norm_w"][j])), _whole(jnp.tile(_row_vec(w["hg_norm_w"][j]), (1, 4)))],
                         [rowblk(ssd.ys[0], 512), rowblk(hg.ys[0], 512), rowblk(h, 1024, AB_ZG // 1024)], [1024],
                         out_dtype=MM_DTYPE, dx_dtypes=[F32, F32, MM_DTYPE])
            rec.update(kind="ab", conv=conv, ssd=ssd, hg=hg, post=post)
        else:
            h = _matmul(f"cd_in{j}", x, comm.weight("cd_w_in", j), "nn", comm=comm)
            swa = _Op(f"swa{j}", _swa_fn, (1, t // SWA_BLOCK),
                      [_whole(jnp.tile(w["swa_sinks"][j].reshape(8, 1), (1, LANES)))],
                      [(h, (SWA_BLOCK, 1024), lambda g, c: (c, 0))],
                      [((t, 512), (SWA_BLOCK, 512), lambda g, c: (c, 0), F32)], [(SWA_BLOCK, LANES)] * 2,
                      dx_dtypes=[MM_DTYPE], comm=comm)
            conv_p = [_row_vec(w["rg_conv_w"][j, k]) for k in range(4)] + [_row_vec(w["rg_conv_b"][j])]
            conv = _Op(f"rg_conv{j}", _make_conv_fn(4, False), (2, nb),
                       [(a, (1, 256), lambda g, c: (0, g)) for a in conv_p],
                       [(h, (bt, 256), lambda g, c: (c, CD_XR // 256 + g))],
                       [((t, 512), (bt, 256), lambda g, c: (c, g), F32)], [(8, 256)], dx_dtypes=[MM_DTYPE])
            gate = rowop(f"rg_gate{j}", _rg_gate_fn,
                         [_whole(_block_diag(w["rg_wa"][j])), _whole(_row_vec(w["rg_ba"][j])),
                          _whole(_block_diag(w["rg_wx"][j])), _whole(_row_vec(w["rg_bx"][j])),
                          _whole(_row_vec(w["rg_lambda"][j]))],
                         [rowblk(conv.ys[0], 512)], [512, 512])
            scan = _Op(f"rg_scan{j}", _rg_scan_fn, (2, t // bs), [],
                       [(gate.ys[0], (bs, 256), lambda g, c: (c, g)), (gate.ys[1], (bs, 256), lambda g, c: (c, g))],
                       [((t, 512), (bs, 256), lambda g, c: (c, g), F32)], [(8, 256)])
            post = rowop(f"cd_post{j}", _cd_post_fn, [],
                         [rowblk(swa.ys[0], 512), rowblk(scan.ys[0], 512), rowblk(h, 512, CD_GATE // 512)], [1024],
                         out_dtype=MM_DTYPE, dx_dtypes=[F32, F32, MM_DTYPE])
            rec.update(kind="cd", swa=swa, conv=conv, gate=gate, scan=scan, post=post)
        kind = rec["kind"]
        ycat = post.ys[0]
        m = _matmul(f"mix_out{layer}", ycat, comm.weight(kind + "_w_out", j), "nn", comm=comm)
        ln1 = rowop(f"ln_a{layer}", _ln_res_fn,
                    [_whole(_row_vec(w["ln_g"][layer, 0])), _whole(_row_vec(w["ln_b"][layer, 0]))],
                    [rowblk(x, 1024), rowblk(m, 1024)], [1024], dx_dtypes=[F32, MM_DTYPE])
        x1 = ln1.ys[0]
        hu = _matmul(f"ffn_up{layer}", x1, comm.weight("ffn_w_up", layer), "nn", comm=comm)
        n_fb = FFN_DIM // FFN_BLK
        act_p = [_row_vec(w["ffn_conv_w"][layer, k]) for k in range(3)] + [_row_vec(w["ffn_conv_b"][layer])]
        act = _Op(f"ffn_act{layer}", _ffn_act_fn, (n_fb, nb),
                  [(a, (1, 2 * FFN_BLK), lambda g, c: (0, g)) for a in act_p],
                  [(hu, (bt, 2 * FFN_BLK), lambda g, c: (c, g))],
                  [((t, FFN_DIM), (bt, FFN_BLK), lambda g, c: (c, g), MM_DTYPE)], [(8, 2 * FFN_BLK)],
                  dx_dtypes=[MM_DTYPE])
        a = act.ys[0]
        f = _matmul(f"ffn_down{layer}", a, comm.weight("ffn_w_down", layer), "nn", comm=comm)
        ln2 = rowop(f"ln_f{layer}", _ln_res_fn,
                    [_whole(_row_vec(w["ln_g"][layer, 1])), _whole(_row_vec(w["ln_b"][layer, 1]))],
                    [rowblk(x1, 1024), rowblk(f, 1024)], [1024], dx_dtypes=[F32, MM_DTYPE])
        rec.update(ycat=ycat, ln1=ln1, x1=x1, act=act, a=a, ln2=ln2)
        tape.append(rec)
        x = ln2.ys[0]

    dx, loss = _loss_kernel(x, target)

    d_lb = [jnp.zeros((1, 512), F32), jnp.zeros((1, 512), F32)]
    for layer in reversed(range(DEPTH)):
        j = layer // 2
        rec = tape[layer]
        (dg, db), (dx1_res, df) = rec["ln2"].bwd([dx])
        add_grad("ln_g", (layer, 1), dg[0]); add_grad("ln_b", (layer, 1), db[0])
        dw_matmul(f"ffn_down_dw{layer}", rec["a"], df, "ffn_w_down", layer, f"ffn_up_dw{layer}")
        da = _matmul(f"ffn_down_dx{layer}", df, comm.weight("ffn_w_down", layer), "nt", comm=comm)
        dpa, (dhu,) = rec["act"].bwd([da])
        add_grad("ffn_conv_w", layer, jnp.stack([dpa[k][0] for k in range(3)]))
        add_grad("ffn_conv_b", layer, dpa[3][0])
        dw_matmul(f"ffn_up_dw{layer}", rec["x1"], dhu, "ffn_w_up", layer, f"ffn_up_dx{layer}")
        dx1 = _matmul(f"ffn_up_dx{layer}", dhu, comm.weight("ffn_w_up", layer), "nt", add=dx1_res, comm=comm)
        (dg, db), (dx_res, dm) = rec["ln1"].bwd([dx1])
        add_grad("ln_g", (layer, 0), dg[0]); add_grad("ln_b", (layer, 0), db[0])
        kind = rec["kind"]
        dw_matmul(f"mix_out_dw{layer}", rec["ycat"], dm, kind + "_w_out", j, f"{kind}_in_dw{j}")
        dycat = _matmul(f"mix_out_dx{layer}", dm, comm.weight(kind + "_w_out", j), "nt", comm=comm)
        if kind == "ab":
            (dnw_s, dnw_h), (dy_ssd, do_hg, dh) = rec["post"].bwd([dycat])
            add_grad("ssd_norm_w", j, dnw_s[0]); add_grad("hg_norm_w", j, dnw_h[0].reshape(4, LANES).sum(0))
            (dlb,), (dh,) = rec["hg"].bwd([do_hg], dx_into={0: dh})
            d_lb[j] = dlb
            (ddtb, dalog, ddsk), (dxbc_c, dh) = rec["ssd"].bwd([dy_ssd], dx_into={1: dh})
            add_grad("ssd_dt_bias", j, ddtb[0].reshape(8, 64).sum(-1))
            add_grad("ssd_a_log", j, dalog[0].reshape(8, 64).sum(-1))
            add_grad("ssd_d", j, ddsk[0].reshape(8, 64).sum(-1))
            dcp, (dh,) = rec["conv"].bwd([dxbc_c], dx_into={0: dh})
            add_grad("ssd_conv_w", j, jnp.stack([dcp[k][0] for k in range(4)]))
            add_grad("ssd_conv_b", j, dcp[4][0])
        else:
            _, (dyc, dhs, dh) = rec["post"].bwd([dycat])
            _, (da_s, du_s) = rec["scan"].bwd([dhs])
            (dwa, dba, dwx, dbx, dlam), (dxc,) = rec["gate"].bwd([da_s, du_s])
            add_grad("rg_wa", j, _block_diag_grad(dwa)); add_grad("rg_wx", j, _block_diag_grad(dwx))
            add_grad("rg_ba", j, dba[0]); add_grad("rg_bx", j, dbx[0]); add_grad("rg_lambda", j, dlam[0])
            dcp, (dh,) = rec["conv"].bwd([dxc], dx_into={0: dh})
            add_grad("rg_conv_w", j, jnp.stack([dcp[k][0] for k in range(4)]))
            add_grad("rg_conv_b", j, dcp[4][0])
            (dsink,), (dh,) = rec["swa"].bwd([dyc], dx_into={0: dh})
            add_grad("swa_sinks", j, dsink.sum(-1))
        dw_matmul(f"{kind}_in_dw{j}", rec["x_in"], dh, kind + "_w_in", j, f"ffn_down_dx{layer - 1}")
        dx = _matmul(f"{kind}_in_dx{j}", dh, comm.weight(kind + "_w_in", j), "nt", add=dx_res, comm=comm)

    _, (dl0, dl1) = lb_op.bwd(d_lb)
    out = {"hg_lower": jnp.concatenate([dl0, dl1], axis=0)}
    for name, parts in grads.items():
        keys = sorted(parts)
        if isinstance(keys[0], tuple):
            out[name] = jnp.stack([jnp.stack([parts[(l, s)] for s in range(2)]) for l in range(DEPTH)])
        else:
            out[name] = jnp.stack([parts[k] for k in keys])
    return loss, dx, out


def _small_prep(w):
    k = {n: a for n, a in w.items() if n not in BIG}
    k["ffn_conv_w"], k["ffn_conv_b"] = _ffn_perm(w["ffn_conv_w"]), _ffn_perm(w["ffn_conv_b"])
    return k


def _small_unprep(g):
    out = dict(g)
    out["ffn_conv_w"], out["ffn_conv_b"] = _ffn_unperm(g["ffn_conv_w"]), _ffn_unperm(g["ffn_conv_b"])
    return out


def _local_step_full(x, target, full):
    comm = _LocalComm(full)
    loss, dx, sg = _local_step(x, target, _small_prep(full), comm)
    grads = _small_unprep(sg)
    for name in BIG:
        grads[name] = jnp.stack([comm.grads[name, l] for l in range(full[name].shape[0])])
    return loss, dx, grads


def kernel(x, ab_w_in, ssd_conv_w, ssd_conv_b, ssd_dt_bias, ssd_a_log, ssd_d, ssd_norm_w, hg_lower, hg_norm_w, ab_w_out, cd_w_in, swa_sinks, rg_conv_w, rg_conv_b, rg_wa, rg_ba, rg_wx, rg_bx, rg_lambda, cd_w_out, ffn_w_up, ffn_conv_w, ffn_conv_b, ffn_w_down, ln_g, ln_b, loss_target, m_ab_w_in, m_ssd_conv_w, m_ssd_conv_b, m_ssd_dt_bias, m_ssd_a_log, m_ssd_d, m_ssd_norm_w, m_hg_lower, m_hg_norm_w, m_ab_w_out, m_cd_w_in, m_swa_sinks, m_rg_conv_w, m_rg_conv_b, m_rg_wa, m_rg_ba, m_rg_wx, m_rg_bx, m_rg_lambda, m_cd_w_out, m_ffn_w_up, m_ffn_conv_w, m_ffn_conv_b, m_ffn_w_down, m_ln_g, m_ln_b, v_ab_w_in, v_ssd_conv_w, v_ssd_conv_b, v_ssd_dt_bias, v_ssd_a_log, v_ssd_d, v_ssd_norm_w, v_hg_lower, v_hg_norm_w, v_ab_w_out, v_cd_w_in, v_swa_sinks, v_rg_conv_w, v_rg_conv_b, v_rg_wa, v_rg_ba, v_rg_wx, v_rg_bx, v_rg_lambda, v_cd_w_out, v_ffn_w_up, v_ffn_conv_w, v_ffn_conv_b, v_ffn_w_down, v_ln_g, v_ln_b):
    args = dict(locals())
    wts = {n: args[n] for n in WEIGHTS}
    mom = {n: args["m_" + n] for n in WEIGHTS}
    var = {n: args["v_" + n] for n in WEIGHTS}
    axis = dict(SHARDED)
    small = [n for n, _ in SHARDED if n not in BIG]
    comm = _MeshComm(wts)

    def run(name, reqs):
        for (_, _, done), got in zip(reqs, _remote_copies(name, [(k, a) for k, a, _ in reqs])):
            done(got)

    full = {n: wts[n] for n in REPLICATED}

    def keep_small(n):
        def done(got):
            full[n] = _merge_shards(got.reshape((N_DEV,) + wts[n].shape), axis[n])
        return ("gather", _as2d(wts[n]), done)

    run("gather_first", [comm.gather_req("ab_w_in", 0), comm.gather_req("ab_w_out", 0)] + [keep_small(n) for n in small])

    loss, grad_x, sg = _local_step(x[0], loss_target[0], _small_prep(full), comm)
    grads = _small_unprep(sg)
    loss = lax.psum(loss, ("x", "y", "c"))

    parts = {}

    def keep_parts(n, kind, arr):
        return (kind, arr, lambda got: parts.__setitem__(n, got))

    last = comm.take_all()
    last += [keep_parts(n, "exchange", _split_shards(grads[n], axis[n]).reshape((N_DEV,) + _as2d(wts[n]).shape))
             for n in small]
    last += [keep_parts(n, "gather", _as2d(grads[n])) for n in REPLICATED]
    run("exchange_last", last)

    new = {}
    for n in BIG:
        new[n] = _adamw_big("adamw_" + n, [comm.recv[n, l] for l in range(wts[n].shape[0])], wts[n], mom[n], var[n])
    names = small + REPLICATED
    res = _adamw_small("adamw_small", [(parts[n], _as2d(wts[n]), _as2d(mom[n]), _as2d(var[n])) for n in names])
    for n, r in zip(names, res):
        new[n] = [a.reshape(wts[n].shape) for a in r]

    outs = [loss, grad_x[None]]
    for kind in range(4):
        outs += [new[n][kind] for n in WEIGHTS]
    return tuple(outs)
```

```python
import math

import numpy as np
import jax
import jax.numpy as jnp
from jax import lax
from jax.experimental import pallas as pl
from jax.experimental.pallas import tpu as pltpu

F32 = jnp.float32
BF16 = jnp.bfloat16
MM_DTYPE = BF16

DEPTH = 4
N_DEV = 8
LN_EPS = 1e-5
RMS_EPS = 1e-6
MASK_VALUE = -1e9
ALPHA = (2 * DEPTH) ** 0.25
RG_C = 8.0
FFN_DIM = 2816
SSD_CHUNK = 128
HG_STEP = 128
SWA_BLOCK = 128
LANES = 128
VMEM_LIMIT = 56 * 1024 * 1024

ADAM_LR, ADAM_B1, ADAM_B2, ADAM_EPS, ADAM_WD, ADAM_STEP = 0.001, 0.9, 0.999, 1e-08, 0.01, 10

AB_HEADS, AB_DT, AB_ZG, AB_XBC, AB_PAD = 0, 1536, 2048, 3072, 3840
CD_QKV, CD_GATE, CD_XR, CD_PAD = 0, 1024, 1536, 2048
FFN_BLK = 256


def _cols(x, lo, hi):
    n = x.shape[1]

    @jax.custom_vjp
    def f(x):
        return x[:, lo:hi]

    def bwd(_, g):
        parts = []
        if lo > 0:
            parts.append(jnp.zeros((g.shape[0], lo), g.dtype))
        parts.append(g)
        if hi < n:
            parts.append(jnp.zeros((g.shape[0], n - hi), g.dtype))
        return (jnp.concatenate(parts, axis=1) if len(parts) > 1 else g,)

    f.defvjp(lambda x: (f(x), None), bwd)
    return f(x)


def _rows(x, lo, hi):
    n = x.shape[0]

    @jax.custom_vjp
    def f(x):
        return x[lo:hi, :]

    def bwd(_, g):
        parts = []
        if lo > 0:
            parts.append(jnp.zeros((lo, g.shape[1]), g.dtype))
        parts.append(g)
        if hi < n:
            parts.append(jnp.zeros((n - hi, g.shape[1]), g.dtype))
        return (jnp.concatenate(parts, axis=0) if len(parts) > 1 else g,)

    f.defvjp(lambda x: (f(x), None), bwd)
    return f(x)


def _roll(x, shift, axis):
    n = x.shape[axis]
    shift = shift % n
    if shift == 0:
        return x

    @jax.custom_vjp
    def f(x):
        return pltpu.roll(x, shift, axis)

    f.defvjp(lambda x: (f(x), None), lambda _, g: (pltpu.roll(g, n - shift, axis),))
    return f(x)


def _dot(a, b, precision=None):
    return lax.dot_general(a, b, (((1,), (0,)), ((), ())), precision=precision, preferred_element_type=F32)


def _dot_nt(a, b, precision=None):
    return lax.dot_general(a, b, (((1,), (1,)), ((), ())), precision=precision, preferred_element_type=F32)


def _dot_tn(a, b, precision=None):
    return lax.dot_general(a, b, (((0,), (0,)), ((), ())), precision=precision, preferred_element_type=F32)


def _split3(x):
    hi = x.astype(BF16)
    r = x - hi.astype(F32)
    mid = r.astype(BF16)
    return hi, mid, (r - mid.astype(F32)).astype(BF16)


def _sel_dot(sel, x):
    def run(mat, v, dims):
        n = v.shape[1]
        y = lax.dot_general(mat, jnp.concatenate(_split3(v), axis=1), dims, preferred_element_type=F32)
        return y[:, :n] + y[:, n:2 * n] + y[:, 2 * n:]

    @jax.custom_vjp
    def f(sel, x):
        return run(sel, x, (((1,), (0,)), ((), ())))

    def bwd(sel, g):
        return jnp.zeros_like(sel), run(sel, g, (((0,), (0,)), ((), ())))

    f.defvjp(lambda sel, x: (f(sel, x), sel), bwd)
    return f(sel, x)


def _dot_sel(x, sel):
    def run(v, mat, dims):
        m = v.shape[0]
        y = lax.dot_general(jnp.concatenate(_split3(v), axis=0), mat, dims, preferred_element_type=F32)
        return y[:m] + y[m:2 * m] + y[2 * m:]

    @jax.custom_vjp
    def f(x, sel):
        return run(x, sel, (((1,), (0,)), ((), ())))

    def bwd(sel, g):
        return run(g, sel, (((1,), (1,)), ((), ()))), jnp.zeros_like(sel)

    f.defvjp(lambda x, sel: (f(x, sel), sel), bwd)
    return f(x, sel)


def _sigmoid(x):
    return 0.5 * jnp.tanh(0.5 * x) + 0.5


def _silu(x):
    return x * _sigmoid(x)


def _softplus(x):
    return jnp.maximum(x, 0.0) + jnp.log(1.0 + jnp.exp(-jnp.abs(x)))


def _gelu_tanh(x):
    c = math.sqrt(2.0 / math.pi)
    return 0.5 * x * (1.0 + jnp.tanh(c * (x + 0.044715 * (x * x * x))))


def _iota(shape, axis):
    return lax.broadcasted_iota(jnp.int32, shape, axis)


def _lane_mask(lo, hi, width=LANES):
    lane = _iota((1, width), 1)
    return ((lane >= lo) & (lane < hi)).astype(F32)


def _mesh_pos():
    return lax.axis_index("x"), lax.axis_index("y"), lax.axis_index("c")


def _carry_shapes(carry):
    return [jax.ShapeDtypeStruct((N_DEV,) + a.shape if kind == "gather" else a.shape, a.dtype) for kind, a in carry]


def _carry_scratch(carry):
    n = len(carry)
    if n == 0:
        return []
    return [pltpu.SemaphoreType.DMA((n, N_DEV - 1)), pltpu.SemaphoreType.DMA((n, N_DEV - 1)),
            pltpu.SemaphoreType.DMA((n,))]


def _carry_run(start, kinds, in_refs, out_refs, send_sems, recv_sems, local_sems):
    x, y, cc = _mesh_pos()
    me = 4 * x + 2 * y + cc
    for i, kind in enumerate(kinds):
        mine = in_refs[i] if kind == "gather" else in_refs[i].at[me]
        local = pltpu.make_async_copy(mine, out_refs[i].at[me], local_sems.at[i])
        remote = []
        for k in range(1, N_DEV):
            px, py, pc = x ^ ((k >> 2) & 1), y ^ ((k >> 1) & 1), cc ^ (k & 1)
            src = in_refs[i] if kind == "gather" else in_refs[i].at[4 * px + 2 * py + pc]
            remote.append(pltpu.make_async_remote_copy(
                src_ref=src, dst_ref=out_refs[i].at[me],
                send_sem=send_sems.at[i, k - 1], recv_sem=recv_sems.at[i, k - 1],
                device_id=(px, py, pc), device_id_type=pl.DeviceIdType.MESH))
        if start:
            local.start()
            for cp in remote:
                cp.start()
        else:
            for cp in remote:
                cp.wait_recv()
            for cp in remote:
                cp.wait_send()
            local.wait()


def _remote_copies(name, carry):
    n = len(carry)
    kinds = [k for k, _ in carry]

    def body(*refs):
        sems = refs[2 * n:]
        _carry_run(True, kinds, refs[:n], refs[n:2 * n], *sems)
        _carry_run(False, kinds, refs[:n], refs[n:2 * n], *sems)

    return pl.pallas_call(
        body, name=name, out_shape=_carry_shapes(carry),
        in_specs=[pl.BlockSpec(memory_space=pl.ANY)] * n, out_specs=[pl.BlockSpec(memory_space=pl.ANY)] * n,
        scratch_shapes=_carry_scratch(carry),
    )(*[a for _, a in carry])


def _cparams(sem):
    return pltpu.CompilerParams(dimension_semantics=sem, vmem_limit_bytes=VMEM_LIMIT)


def _chunk_fwd(name, fn, grid, params, xs, outs, state_shapes, carry=()):
    n_g, n_c = grid
    n_p, n_x, n_o, n_s, n_r = len(params), len(xs), len(outs), len(state_shapes), len(carry)
    kinds = [k for k, _ in carry]

    def body(*refs):
        i = 0
        p_refs = refs[i:i + n_p]; i += n_p
        x_refs = refs[i:i + n_x]; i += n_x
        ci_refs = refs[i:i + n_r]; i += n_r
        o_refs = refs[i:i + n_o]; i += n_o
        sv_refs = refs[i:i + n_s]; i += n_s
        co_refs = refs[i:i + n_r]; i += n_r
        st_refs = refs[i:i + n_s]; i += n_s
        sems = refs[i:]
        g, c = pl.program_id(0), pl.program_id(1)

        if n_r:
            @pl.when((g == 0) & (c == 0))
            def _():
                _carry_run(True, kinds, ci_refs, co_refs, *sems)

        @pl.when(c == 0)
        def _():
            for s in st_refs:
                s[...] = jnp.zeros(s.shape, s.dtype)

        st = [s[...] for s in st_refs]
        ys, new_st = fn(c, [p[...] for p in p_refs], [x[...].astype(F32) for x in x_refs], st)
        for o, y in zip(o_refs, ys):
            o[...] = y.astype(o.dtype)
        for sv, s in zip(sv_refs, st):
            sv[0, 0] = s
        for s_ref, s in zip(st_refs, new_st):
            s_ref[...] = s

        if n_r:
            @pl.when((g == n_g - 1) & (c == n_c - 1))
            def _():
                _carry_run(False, kinds, ci_refs, co_refs, *sems)

    any_spec = pl.BlockSpec(memory_space=pl.ANY)
    in_specs = [pl.BlockSpec(b, m) for _, b, m in params] + [pl.BlockSpec(b, m) for _, b, m in xs] + [any_spec] * n_r
    out_specs = [pl.BlockSpec(b, m) for _, b, m, _ in outs]
    out_shape = [jax.ShapeDtypeStruct(s, d) for s, _, _, d in outs]
    for shp in state_shapes:
        out_specs.append(pl.BlockSpec((1, 1) + shp, lambda g, c, n=len(shp): (g, c) + (0,) * n))
        out_shape.append(jax.ShapeDtypeStruct((n_g, n_c) + shp, F32))
    out_specs += [any_spec] * n_r
    out_shape += _carry_shapes(carry)
    res = pl.pallas_call(
        body, name=name, grid=grid, in_specs=in_specs, out_specs=out_specs, out_shape=out_shape,
        scratch_shapes=[pltpu.VMEM(shp, F32) for shp in state_shapes] + _carry_scratch(carry),
        compiler_params=_cparams(("arbitrary", "arbitrary")),
    )(*[a for a, _, _ in params], *[a for a, _, _ in xs], *[a for _, a in carry])
    return list(res[:n_o]), list(res[n_o:n_o + n_s]), list(res[n_o + n_s:])


def _chunk_bwd(name, fn, grid, params, xs, saved, dys, state_shapes, dx_dtypes, dx_into, carry=()):
    n_g, n_c = grid
    n_p, n_x, n_s, n_y, n_r = len(params), len(xs), len(state_shapes), len(dys), len(carry)
    kinds = [k for k, _ in carry]
    into = sorted(dx_into)
    n_a = len(into)

    def rev(m):
        return lambda g, c: m(g, n_c - 1 - c)

    def body(*refs):
        i = 0
        p_refs = refs[i:i + n_p]; i += n_p
        x_refs = refs[i:i + n_x]; i += n_x
        sv_refs = refs[i:i + n_s]; i += n_s
        dy_refs = refs[i:i + n_y]; i += n_y
        i += n_a
        ci_refs = refs[i:i + n_r]; i += n_r
        dp_refs = refs[i:i + n_p]; i += n_p
        dx_refs = refs[i:i + n_x]; i += n_x
        co_refs = refs[i:i + n_r]; i += n_r
        ds_refs = refs[i:i + n_s]; i += n_s
        sems = refs[i:]
        g, c = pl.program_id(0), pl.program_id(1)
        chunk = n_c - 1 - c

        if n_r:
            @pl.when((g == 0) & (c == 0))
            def _():
                _carry_run(True, kinds, ci_refs, co_refs, *sems)

        @pl.when(c == 0)
        def _():
            for s in ds_refs:
                s[...] = jnp.zeros(s.shape, s.dtype)
            for d in dp_refs:
                d[...] = jnp.zeros(d.shape, d.dtype)

        pv = [p[...] for p in p_refs]
        xv = [x[...].astype(F32) for x in x_refs]
        sv = [s[0, 0] for s in sv_refs]
        _, vjp = jax.vjp(lambda p, x, s: fn(chunk, p, x, s), pv, xv, sv)
        dp, dx, ds = vjp(([d[...].astype(F32) for d in dy_refs], [s[...] for s in ds_refs]))
        for r, v in zip(dp_refs, dp):
            r[...] += v
        for r, v in zip(dx_refs, dx):
            r[...] = v.astype(r.dtype)
        for r, v in zip(ds_refs, ds):
            r[...] = v

        if n_r:
            @pl.when((g == n_g - 1) & (c == n_c - 1))
            def _():
                _carry_run(False, kinds, ci_refs, co_refs, *sems)

    any_spec = pl.BlockSpec(memory_space=pl.ANY)
    in_specs = [pl.BlockSpec(b, rev(m)) for _, b, m in params] + [pl.BlockSpec(b, rev(m)) for _, b, m in xs]
    for shp in state_shapes:
        in_specs.append(pl.BlockSpec((1, 1) + shp, lambda g, c, n=len(shp): (g, n_c - 1 - c) + (0,) * n))
    in_specs += [pl.BlockSpec(b, rev(m)) for _, b, m in dys]
    in_specs += [any_spec] * (n_a + n_r)
    out_specs = [pl.BlockSpec(b, rev(m)) for _, b, m in params] + [pl.BlockSpec(b, rev(m)) for _, b, m in xs]
    out_specs += [any_spec] * n_r
    out_shape = [jax.ShapeDtypeStruct(a.shape, F32) for a, _, _ in params]
    out_shape += [jax.ShapeDtypeStruct(a.shape, d) for (a, _, _), d in zip(xs, dx_dtypes)]
    out_shape += _carry_shapes(carry)
    first_alias = n_p + n_x + n_s + n_y
    aliases = {first_alias + k: n_p + xi for k, xi in enumerate(into)}
    res = pl.pallas_call(
        body, name=name, grid=grid, in_specs=in_specs, out_specs=out_specs, out_shape=out_shape,
        scratch_shapes=[pltpu.VMEM(shp, F32) for shp in state_shapes] + _carry_scratch(carry),
        input_output_aliases=aliases,
        compiler_params=_cparams(("arbitrary", "arbitrary")),
    )(*[a for a, _, _ in params], *[a for a, _, _ in xs], *saved, *[a for a, _, _ in dys],
      *[dx_into[xi] for xi in into], *[a for _, a in carry])
    return list(res[:n_p]), list(res[n_p:n_p + n_x]), list(res[n_p + n_x:])


class _Op:
    def __init__(self, name, fn, grid, params, xs, outs, state_shapes=(), dx_dtypes=None, comm=None):
        self.name, self.fn, self.grid, self.comm = name, fn, grid, comm
        self.params, self.xs, self.outs, self.state_shapes = params, xs, outs, list(state_shapes)
        self.dx_dtypes = dx_dtypes or [F32] * len(xs)
        reqs = comm.take(name + "_fwd") if comm is not None else []
        self.ys, self.saved, got = _chunk_fwd(name + "_fwd", fn, grid, params, xs, outs, self.state_shapes,
                                              carry=[(k, a) for k, a, _ in reqs])
        for (_, _, done), g in zip(reqs, got):
            done(g)

    def bwd(self, dys, dx_into=None):
        dy_defs = [(d, b, m) for d, (_, b, m, _) in zip(dys, self.outs)]
        reqs = self.comm.take(self.name + "_bwd") if self.comm is not None else []
        dps, dxs, got = _chunk_bwd(self.name + "_bwd", self.fn, self.grid, self.params, self.xs, self.saved, dy_defs,
                                   self.state_shapes, self.dx_dtypes, dx_into or {},
                                   carry=[(k, a) for k, a, _ in reqs])
        for (_, _, done), g in zip(reqs, got):
            done(g)
        return dps, dxs


def _whole(a):
    nd = a.ndim
    return (a, a.shape, lambda g, c: (0,) * nd)


def _pick(n, prefs):
    for p in prefs:
        if n % p == 0:
            return p
    return n


def _mm_blocks(mode, m, n, k):
    bn = _pick(n, (1408, 1280, 1024, 768, 512, 256, 128))
    if mode == "tn":
        return _pick(m, (1408, 1024, 768, 512, 256, 128)), bn, _pick(k, (1024, 512, 256, 128))
    bk = k if k <= 3840 else _pick(k, (2816, 1920, 1408, 1024, 512, 256, 128))
    return _pick(m, (1024, 512, 256, 128)), bn, bk


def _matmul(name, a, b, mode, *, add=None, out_dtype=F32, comm=None):
    if mode == "nn":
        (m, k), n = a.shape, b.shape[1]
    elif mode == "nt":
        (m, k), n = a.shape, b.shape[0]
    else:
        (k, m), n = a.shape, b.shape[1]
    bm, bn, bk = _mm_blocks(mode, m, n, k)
    n_i, n_j, n_k = m // bm, n // bn, k // bk
    dims = {"nn": (((1,), (0,)), ((), ())), "nt": (((1,), (1,)), ((), ())), "tn": (((0,), (0,)), ((), ()))}[mode]
    has_add = add is not None
    reqs = comm.take(name) if comm is not None else []
    carry = [(kind, arr) for kind, arr, _ in reqs]
    kinds = [kind for kind, _ in carry]
    n_r = len(carry)

    def body(*refs):
        i = 2
        a_ref, b_ref = refs[0], refs[1]
        c_ref = refs[i] if has_add else None
        i += has_add
        ci_refs = refs[i:i + n_r]; i += n_r
        o_ref = refs[i]; i += 1
        co_refs = refs[i:i + n_r]; i += n_r
        acc = refs[i]; i += 1
        sems = refs[i:]
        ii, jj, kk = pl.program_id(0), pl.program_id(1), pl.program_id(2)

        if n_r:
            @pl.when((ii == 0) & (jj == 0) & (kk == 0))
            def _():
                _carry_run(True, kinds, ci_refs, co_refs, *sems)

        part = lax.dot_general(a_ref[...].astype(MM_DTYPE), b_ref[...].astype(MM_DTYPE), dims,
                               preferred_element_type=F32)

        def finish(r):
            if has_add:
                r = r + c_ref[...]
            o_ref[...] = r.astype(o_ref.dtype)

        if n_k == 1:
            finish(part)
        else:
            @pl.when(kk == 0)
            def _():
                acc[...] = part

            @pl.when((kk > 0) & (kk < n_k - 1))
            def _():
                acc[...] += part

            @pl.when(kk == n_k - 1)
            def _():
                finish(acc[...] + part)

        if n_r:
            @pl.when((ii == n_i - 1) & (jj == n_j - 1) & (kk == n_k - 1))
            def _():
                _carry_run(False, kinds, ci_refs, co_refs, *sems)

    if mode == "nn":
        a_spec = pl.BlockSpec((bm, bk), lambda i, j, kk: (i, kk))
        b_spec = pl.BlockSpec((bk, bn), lambda i, j, kk: (kk, j))
    elif mode == "nt":
        a_spec = pl.BlockSpec((bm, bk), lambda i, j, kk: (i, kk))
        b_spec = pl.BlockSpec((bn, bk), lambda i, j, kk: (j, kk))
    else:
        a_spec = pl.BlockSpec((bk, bm), lambda i, j, kk: (kk, i))
        b_spec = pl.BlockSpec((bk, bn), lambda i, j, kk: (kk, j))
    any_spec = pl.BlockSpec(memory_space=pl.ANY)
    in_specs, args = [a_spec, b_spec], [a, b]
    if has_add:
        in_specs.append(pl.BlockSpec((bm, bn), lambda i, j, kk: (i, j)))
        args.append(add)
    res = pl.pallas_call(
        body, name=name, grid=(n_i, n_j, n_k), in_specs=in_specs + [any_spec] * n_r,
        out_specs=[pl.BlockSpec((bm, bn), lambda i, j, kk: (i, j))] + [any_spec] * n_r,
        out_shape=[jax.ShapeDtypeStruct((m, n), out_dtype)] + _carry_shapes(carry),
        scratch_shapes=[pltpu.VMEM((bm, bn) if n_k > 1 else (8, LANES), F32)] + _carry_scratch(carry),
        compiler_params=_cparams(("arbitrary", "arbitrary", "arbitrary")),
    )(*args, *[arr for _, arr in carry])
    for (_, _, done), g in zip(reqs, res[1:]):
        done(g)
    return res[0]


def _ln_res_fn(_, p, x, st):
    g, b = p
    xin, m = x
    pre = ALPHA * xin + m
    mu = jnp.mean(pre, -1, keepdims=True)
    d = pre - mu
    var = jnp.mean(d * d, -1, keepdims=True)
    return [d * lax.rsqrt(var + LN_EPS) * g + b], []


def _make_conv_fn(taps, act):
    def fn(_, p, x, st):
        ws, b = p[:taps], p[taps]
        (xin,), (prev,) = x, st
        n = xin.shape[0]
        ext = jnp.concatenate([prev, xin], axis=0)
        y = b
        for k in range(taps):
            y = y + ws[k] * _rows(_roll(ext, taps - 1 - k, 0), 8, 8 + n)
        if act:
            y = _silu(y)
        return [y], [_rows(xin, n - 8, n)]

    return fn


def _ffn_act_fn(_, p, x, st):
    ws, b = p[:3], p[3]
    (xin,), (prev,) = x, st
    n = xin.shape[0]
    ext = jnp.concatenate([prev, xin], axis=0)
    y = b
    for k in range(3):
        y = y + ws[k] * _rows(_roll(ext, 2 - k, 0), 8, 8 + n)
    return [_silu(_cols(y, 0, FFN_BLK)) * _cols(y, FFN_BLK, 2 * FFN_BLK)], [_rows(xin, n - 8, n)]


def _ssd_fn(_, p, x, st):
    dtb, alog, dsk = p
    xbc, dtr = x
    L = SSD_CHUNK
    tril = _iota((L, L), 0) >= _iota((L, L), 1)
    xs, bm, cm = _cols(xbc, 0, 512), _cols(xbc, 512, 640), _cols(xbc, 640, 768)
    dt = _softplus(dtr + dtb)
    da = dt * (-jnp.exp(alog))
    cs = _sel_dot(tril.astype(BF16), da)
    pick = ((_iota((LANES, 2 * LANES), 0) == 0) & (_iota((LANES, 2 * LANES), 1) < LANES)) | (
        (_iota((LANES, 2 * LANES), 0) == 64) & (_iota((LANES, 2 * LANES), 1) >= LANES))
    pick = pick.astype(BF16)
    tot = jnp.sum(da, axis=0, keepdims=True)
    xc = xs * dt
    xdec = xc * jnp.exp(tot - cs)
    ecs = jnp.exp(cs)
    etot = jnp.exp(tot)
    ys, new_st = [], []
    for pr in range(4):
        lo, hi = LANES * pr, LANES * (pr + 1)
        grp = pr // 2
        c_g = cm * _lane_mask(64 * grp, 64 * grp + 64)
        gmat = _dot_nt(c_g, bm)
        cs_p, xc_p = _cols(cs, lo, hi), _cols(xc, lo, hi)
        cols2 = _dot_sel(cs_p, pick)
        yd = jnp.zeros((L, LANES), F32)
        for half in range(2):
            col = _cols(cols2, LANES * half, LANES * (half + 1))
            diff = col - col.T
            dec = jnp.where(tril, jnp.exp(jnp.where(tril, diff, 0.0)), 0.0)
            yd = yd + _dot(gmat * dec, xc_p) * _lane_mask(64 * half, 64 * half + 64)
        s_in = st[pr]
        y_off = _dot(c_g, s_in) * _cols(ecs, lo, hi)
        ys.append(yd + y_off + _cols(dsk, lo, hi) * _cols(xs, lo, hi))
        new_st.append(s_in * _cols(etot, lo, hi) + _dot_tn(bm, _cols(xdec, lo, hi)))
    return [jnp.concatenate(ys, axis=1)], new_st


def _hg_fn(_, p, x, st):
    (lb,) = p
    (xin,) = x
    L = HG_STEP
    n_lvl = L.bit_length() - 1
    hq, hf, hi = _cols(xin, 0, 512), _cols(xin, 512, 1024), _cols(xin, 1024, 1536)
    q = _silu(hq)
    logf = jnp.log(lb + (1.0 - lb) * _sigmoid(hf))
    k = (1.0 - lb) * _sigmoid(-hf)
    ti, si = _iota((L, L), 0), _iota((L, L), 1)
    bc = _sel_dot((ti >= si).astype(BF16), logf)
    tot = jnp.sum(logf, axis=0, keepdims=True)
    tn, sn = _iota((n_lvl * L, 1), 0), _iota((n_lvl * L, L), 1)
    row = tn & (L - 1)
    blk = L >> (tn >> n_lvl)
    piv = row - (row & (blk - 1)) + (blk >> 1)
    bcp_all = _sel_dot((sn == piv).astype(BF16), bc)
    t1 = _iota((L, 1), 0)
    qqs, kks, sames = [], [], []
    for lvl in range(n_lvl):
        size = L >> lvl
        upper = (t1 & (size - 1)) >= size // 2
        bcp = _rows(bcp_all, L * lvl, L * (lvl + 1))
        qqs.append(jnp.where(upper, q * jnp.exp(jnp.where(upper, bc - bcp, 0.0)), 0.0))
        kks.append(jnp.where(upper, 0.0, k * jnp.exp(jnp.where(upper, 0.0, bcp - bc))))
        sames.append((ti >> (n_lvl - lvl)) == (si >> (n_lvl - lvl)))
    q_in = q * jnp.exp(bc)
    k_out = k * jnp.exp(tot - bc)
    diag = q * k
    etot = jnp.exp(tot)
    outs, new_st = [], []
    for h in range(4):
        lo, up = LANES * h, LANES * (h + 1)
        attn = jnp.zeros((L, L), F32)
        for lvl in range(n_lvl):
            attn = attn + jnp.where(sames[lvl], _dot_nt(_cols(qqs[lvl], lo, up), _cols(kks[lvl], lo, up)), 0.0)
        v = _cols(hi, lo, up)
        out = _dot(attn, v) + jnp.sum(_cols(diag, lo, up), axis=-1, keepdims=True) * v
        outs.append(out + _dot_nt(_cols(q_in, lo, up), st[h]))
        new_st.append(st[h] * _cols(etot, lo, up) + _dot_tn(v, _cols(k_out, lo, up)))
    return [jnp.concatenate(outs, axis=1)], new_st


def _swa_fn(chunk, p, x, st):
    (sinks,) = p
    (xin,) = x
    q, k, v = _cols(xin, 0, 512), _cols(xin, 512, 640), _cols(xin, 640, 768)
    kp, vp = st
    T = SWA_BLOCK
    kc = jnp.concatenate([kp, k], axis=0)
    vc = jnp.concatenate([vp, v], axis=0)
    qi, kj = _iota((T, 2 * T), 0), _iota((T, 2 * T), 1)
    rel = qi + T - kj
    mask = (rel >= 0) & (rel < T) & ((kj >= T) | (chunk > 0))
    srow = _iota((8, LANES), 0)
    outs = []
    for pr in range(4):
        grp = pr // 2
        gm = _lane_mask(64 * grp, 64 * grp + 64)
        km, vm = kc * gm, vc * gm
        q2 = _cols(q, LANES * pr, LANES * (pr + 1))
        o2 = jnp.zeros((T, LANES), F32)
        for half in range(2):
            hm = _lane_mask(64 * half, 64 * half + 64)
            qh = q2 * hm
            if half != grp:
                qh = _roll(qh, 64, 1)
            s = _dot_nt(qh, km) * 0.125
            s = jnp.where(mask, s, MASK_VALUE)
            sink = jnp.mean(jnp.sum(jnp.where(srow == 2 * pr + half, sinks, 0.0), axis=0, keepdims=True),
                            axis=-1, keepdims=True)
            mx = lax.stop_gradient(jnp.maximum(jnp.max(s, axis=-1, keepdims=True), sink))
            e = jnp.exp(s - mx)
            den = jnp.sum(e, axis=-1, keepdims=True) + jnp.exp(sink - mx)
            o = _dot(e / den, vm)
            if half != grp:
                o = _roll(o, 64, 1)
            o2 = o2 + o * hm
        outs.append(o2)
    return [jnp.concatenate(outs, axis=1)], [k, v]


def _rg_gate_fn(_, p, x, st):
    wa, ba, wx, bx, lam = p
    (xc,) = x
    r = _sigmoid(_dot(xc, wa) + ba)
    i = _sigmoid(_dot(xc, wx) + bx)
    log_a = -RG_C * r * _softplus(-lam)
    a = jnp.exp(log_a)
    t = jnp.tanh(log_a)
    one_minus_a2 = -2.0 * t / (1.0 - t)
    u = jnp.sqrt(jnp.maximum(one_minus_a2, 0.0)) * (i * xc)
    return [a, u], []


def _rg_scan_fn(_, p, x, st):
    a, u = x
    (prev,) = st
    n = a.shape[0]
    row = _iota((n, 1), 0)
    s = 1
    while s < n:
        keep = row >= s
        a_s, u_s = _roll(a, s, 0), _roll(u, s, 0)
        u = jnp.where(keep, a * u_s + u, u)
        a = jnp.where(keep, a * a_s, a)
        s *= 2
    h_in = jnp.sum(jnp.where(_iota((8, 1), 0) == 7, prev, 0.0), axis=0, keepdims=True)
    h = u + a * h_in
    return [h], [_rows(h, n - 8, n)]


def _ab_post_fn(_, p, x, st):
    nw_ssd, nw_hg = p
    y, o, zg = x
    z, hgate = _cols(zg, 0, 512), _cols(zg, 512, 1024)
    lane = _iota((1, 512), 1)
    ya = y * _silu(z)
    sq = ya * ya
    inv = jnp.zeros_like(ya)
    for g in range(2):
        mk = (lane >= 256 * g) & (lane < 256 * (g + 1))
        ms = jnp.sum(jnp.where(mk, sq, 0.0), axis=-1, keepdims=True) / 256.0
        inv = jnp.where(mk, lax.rsqrt(ms + RMS_EPS), inv)
    ya = ya * inv * nw_ssd
    so = o * o
    inv = jnp.zeros_like(o)
    for h in range(4):
        mk = (lane >= 128 * h) & (lane < 128 * (h + 1))
        ms = jnp.sum(jnp.where(mk, so, 0.0), axis=-1, keepdims=True) / 128.0
        inv = jnp.where(mk, lax.rsqrt(ms + RMS_EPS), inv)
    yb = o * inv * nw_hg * _silu(hgate)
    return [jnp.concatenate([ya, yb], axis=1)], []


def _cd_post_fn(_, p, x, st):
    yc, h, gate = x
    return [jnp.concatenate([yc, h * _gelu_tanh(gate)], axis=1)], []


def _lb_fn(_, p, x, st):
    l0, l1 = x
    mx = lax.stop_gradient(jnp.maximum(l0, l1))
    e0, e1 = jnp.exp(l0 - mx), jnp.exp(l1 - mx)
    s0, s1 = e0 / (e0 + e1), e1 / (e0 + e1)
    return [jnp.clip(s0 - s0, 0.0, 1.0), jnp.clip((s0 + s1) - s0, 0.0, 1.0)], []


def _loss_kernel(y, target):
    t, d = y.shape
    bt = _pick(t, (512, 256, 128))

    def body(y_ref, t_ref, dy_ref, l_ref):
        @pl.when(pl.program_id(0) == 0)
        def _():
            l_ref[...] = jnp.zeros(l_ref.shape, F32)

        e = y_ref[...] - t_ref[...]
        dy_ref[...] = e * (1.0 / d)
        l_ref[...] += jnp.sum(e * e, axis=0, keepdims=True) * (0.5 / d)

    dy, part = pl.pallas_call(
        body, name="loss", grid=(t // bt,),
        in_specs=[pl.BlockSpec((bt, d), lambda i: (i, 0)), pl.BlockSpec((bt, d), lambda i: (i, 0))],
        out_specs=[pl.BlockSpec((bt, d), lambda i: (i, 0)), pl.BlockSpec((1, d), lambda i: (0, 0))],
        out_shape=[jax.ShapeDtypeStruct((t, d), F32), jax.ShapeDtypeStruct((1, d), F32)],
        compiler_params=_cparams(("arbitrary",)),
    )(y, target)
    return dy, jnp.sum(part)


def _adamw_math(parts, w_, m_, v_):
    c1 = 1.0 / (1.0 - ADAM_B1 ** ADAM_STEP)
    c2 = 1.0 / (1.0 - ADAM_B2 ** ADAM_STEP)
    g = parts[0].astype(F32)
    for s in range(1, N_DEV):
        g = g + parts[s].astype(F32)
    nm = ADAM_B1 * m_ + (1.0 - ADAM_B1) * g
    nv = ADAM_B2 * v_ + (1.0 - ADAM_B2) * (g * g)
    return g, -ADAM_LR * ((nm * c1) / (jnp.sqrt(nv * c2) + ADAM_EPS) + ADAM_WD * w_), nm, nv


def _adamw_big(name, parts, w, m, v):
    n_l, r, c = w.shape
    br = _pick(r, (256, 176, 128, 64, 32, 16, 8))

    def body(*refs):
        p_refs, (w_ref, m_ref, v_ref), outs = refs[:n_l], refs[n_l:n_l + 3], refs[n_l + 3:]
        for l in range(n_l):
            @pl.when(pl.program_id(0) == l)
            def _(p_ref=p_refs[l]):
                res = _adamw_math([p_ref[s] for s in range(N_DEV)], w_ref[...], m_ref[...], v_ref[...])
                for ref, val in zip(outs, res):
                    ref[...] = val

    blk = pl.BlockSpec((None, br, c), lambda l, i: (l, i, 0))
    p_specs = [pl.BlockSpec((N_DEV, br, c), lambda l, i, k=k: (0, jnp.where(l == k, i, 0), 0)) for k in range(n_l)]
    return pl.pallas_call(
        body, name=name, grid=(n_l, r // br), in_specs=p_specs + [blk, blk, blk],
        out_specs=[blk] * 4, out_shape=[jax.ShapeDtypeStruct(w.shape, F32)] * 4,
        compiler_params=_cparams(("arbitrary", "arbitrary")),
    )(*parts, w, m, v)


def _adamw_small(name, items):
    n = len(items)

    def body(*refs):
        ins, outs = refs[:4 * n], refs[4 * n:]
        for i in range(n):
            p_ref, w_ref, m_ref, v_ref = ins[4 * i:4 * i + 4]
            res = _adamw_math([p_ref[s] for s in range(N_DEV)], w_ref[...], m_ref[...], v_ref[...])
            for ref, val in zip(outs[4 * i:4 * i + 4], res):
                ref[...] = val

    flat = [a for it in items for a in it]
    out_shape = [jax.ShapeDtypeStruct(it[1].shape, F32) for it in items for _ in range(4)]
    res = pl.pallas_call(
        body, name=name, out_shape=out_shape,
        in_specs=[pl.BlockSpec(memory_space=pltpu.VMEM)] * len(flat),
        out_specs=[pl.BlockSpec(memory_space=pltpu.VMEM)] * len(out_shape),
        compiler_params=pltpu.CompilerParams(vmem_limit_bytes=VMEM_LIMIT),
    )(*flat)
    return [res[4 * i:4 * i + 4] for i in range(n)]


SHARDED = [("ab_w_in", 2), ("ab_w_out", 1), ("cd_w_in", 2), ("cd_w_out", 1), ("ffn_w_up", 2), ("ffn_w_down", 1),
           ("ssd_conv_w", 2), ("rg_conv_w", 2), ("rg_conv_b", 1), ("rg_ba", 1), ("rg_bx", 1), ("rg_lambda", 1),
           ("ffn_conv_w", 2), ("ln_g", 2), ("ln_b", 2)]
MATMUL_W = ("ab_w_in", "ab_w_out", "cd_w_in", "cd_w_out", "ffn_w_up", "ffn_w_down")
REPLICATED = ["ssd_conv_b", "ssd_dt_bias", "ssd_a_log", "ssd_d", "ssd_norm_w", "hg_lower", "hg_norm_w", "swa_sinks",
              "rg_wa", "rg_wx", "ffn_conv_b"]
WEIGHTS = ["ab_w_in", "ssd_conv_w", "ssd_conv_b", "ssd_dt_bias", "ssd_a_log", "ssd_d", "ssd_norm_w", "hg_lower",
           "hg_norm_w", "ab_w_out", "cd_w_in", "swa_sinks", "rg_conv_w", "rg_conv_b", "rg_wa", "rg_ba", "rg_wx",
           "rg_bx", "rg_lambda", "cd_w_out", "ffn_w_up", "ffn_conv_w", "ffn_conv_b", "ffn_w_down", "ln_g", "ln_b"]


def _as2d(a):
    return a.reshape(-1, a.shape[-1])


def _merge_shards(g, axis):
    g = jnp.moveaxis(g, 0, axis)
    shp = g.shape
    return g.reshape(shp[:axis] + (shp[axis] * shp[axis + 1],) + shp[axis + 2:])


def _split_shards(full, axis):
    shp = full.shape
    g = full.reshape(shp[:axis] + (N_DEV, shp[axis] // N_DEV) + shp[axis + 1:])
    return jnp.moveaxis(g, axis, 0)


def _ab_pad(w):
    z, xbc, dt = w[..., 0:512], w[..., 512:1280], w[..., 1280:1288]
    hqfi, hg = w[..., 1288:2824], w[..., 2824:3336]
    return jnp.concatenate([hqfi, jnp.repeat(dt, 64, axis=-1), z, hg, xbc], axis=-1)


def _ab_unpad(d):
    lead = d.shape[:-1]
    dt = d[..., AB_DT:AB_ZG].reshape(lead + (8, 64)).sum(-1)
    z, hg, xbc = d[..., AB_ZG:AB_ZG + 512], d[..., AB_ZG + 512:AB_XBC], d[..., AB_XBC:AB_PAD]
    return jnp.concatenate([z, xbc, dt, d[..., :AB_DT], hg], axis=-1)


def _cd_pad(w):
    return jnp.concatenate([w[..., :768], jnp.zeros(w.shape[:-1] + (256,), w.dtype), w[..., 768:]], axis=-1)


def _cd_unpad(d):
    return jnp.concatenate([d[..., :768], d[..., CD_GATE:CD_PAD]], axis=-1)


def _ffn_perm(w):
    parts = []
    for i in range(FFN_DIM // FFN_BLK):
        parts += [w[..., FFN_BLK * i:FFN_BLK * (i + 1)], w[..., FFN_DIM + FFN_BLK * i:FFN_DIM + FFN_BLK * (i + 1)]]
    return jnp.concatenate(parts, axis=-1)


def _ffn_unperm(d):
    n = FFN_DIM // FFN_BLK
    gate = [d[..., 2 * FFN_BLK * i:2 * FFN_BLK * i + FFN_BLK] for i in range(n)]
    up = [d[..., 2 * FFN_BLK * i + FFN_BLK:2 * FFN_BLK * (i + 1)] for i in range(n)]
    return jnp.concatenate(gate + up, axis=-1)


def _block_diag(w):
    eye = jnp.eye(8, dtype=w.dtype)
    return jnp.einsum("gij,gh->gihj", w, eye).reshape(512, 512)


def _block_diag_grad(d):
    return jnp.stack([d[64 * g:64 * g + 64, 64 * g:64 * g + 64] for g in range(8)])


def _same(a):
    return a


BIG = {"ab_w_in": (1, _ab_pad, _ab_unpad), "ab_w_out": (0, _same, _same), "cd_w_in": (1, _cd_pad, _cd_unpad),
       "cd_w_out": (0, _same, _same), "ffn_w_up": (1, _ffn_perm, _ffn_unperm), "ffn_w_down": (0, _same, _same)}


class _MeshComm:
    def __init__(self, shards):
        self.shards, self.full, self.recv, self.posted = shards, {}, {}, {}

    def post(self, carrier, req):
        self.posted.setdefault(carrier, []).append(req)

    def take(self, carrier):
        return self.posted.pop(carrier, [])

    def take_all(self):
        reqs = [r for name in list(self.posted) for r in self.posted.pop(name)]
        return reqs

    def gather_req(self, name, layer):
        axis, prep, _ = BIG[name]

        def done(got):
            self.full[name, layer] = prep(_merge_shards(got, axis))

        return ("gather", self.shards[name][layer].astype(MM_DTYPE), done)

    def weight(self, name, layer):
        return self.full[name, layer]

    def grad_req(self, name, layer, d):
        axis, _, unprep = BIG[name]

        def done(got):
            self.recv[name, layer] = got

        return ("exchange", _split_shards(unprep(d), axis).astype(MM_DTYPE), done)


class _LocalComm:
    def __init__(self, full):
        self.full_w, self.grads = full, {}

    def post(self, carrier, req):
        pass

    def take(self, carrier):
        return []

    def gather_req(self, name, layer):
        return None

    def weight(self, name, layer):
        return BIG[name][1](self.full_w[name][layer].astype(MM_DTYPE))

    def grad_req(self, name, layer, d):
        self.grads[name, layer] = BIG[name][2](d)
        return None


def _row_vec(v):
    return v.reshape(1, -1)


def _heads64(v):
    return jnp.repeat(v, 64).reshape(1, 512)


def _local_step(x, target, w, comm):
    kinds = ["ab" if layer % 2 == 0 else "cd" for layer in range(DEPTH)]
    in_name = [f"{kinds[layer]}_in{layer // 2}" for layer in range(DEPTH)]
    core_name = [("hg" if layer % 2 == 0 else "swa") + f"{layer // 2}_fwd" for layer in range(DEPTH)]
    comm.post("ssd0_fwd", comm.gather_req("ffn_w_down", 0))
    comm.post(core_name[0], comm.gather_req("ffn_w_up", 0))
    for layer in range(DEPTH - 1):
        nxt, nj = kinds[layer + 1], (layer + 1) // 2
        if kinds[layer] == "ab":
            comm.post(in_name[layer], comm.gather_req(nxt + "_w_in", nj))
            comm.post(core_name[layer], comm.gather_req(nxt + "_w_out", nj))
            comm.post(core_name[layer], comm.gather_req("ffn_w_down", layer + 1))
        else:
            comm.post(in_name[layer], comm.gather_req(nxt + "_w_out", nj))
            comm.post(core_name[layer], comm.gather_req(nxt + "_w_in", nj))
            comm.post(f"ffn_up{layer}", comm.gather_req("ffn_w_down", layer + 1))
        comm.post(f"ffn_act{layer}_fwd", comm.gather_req("ffn_w_up", layer + 1))

    t = x.shape[0]
    bt = _pick(t, (512, 256, 128))
    nb = t // bt
    bs = _pick(t, (256, 128))

    def rowop(name, fn, params, xs, widths_out, out_dtype=F32, dx_dtypes=None):
        outs = [((t, wd), (bt, wd), lambda g, c: (c, 0), out_dtype) for wd in widths_out]
        return _Op(name, fn, (1, nb), params, xs, outs, dx_dtypes=dx_dtypes)

    def rowblk(arr, width, first=0):
        return (arr, (bt, width), lambda g, c: (c, first))

    one_row = lambda g, c: (0, 0)
    lb_op = _Op("hg_lb", _lb_fn, (1, 1), [],
                [(w["hg_lower"][0:1], (1, 512), one_row), (w["hg_lower"][1:2], (1, 512), one_row)],
                [((1, 512), (1, 512), one_row, F32)] * 2)
    lb_all = lb_op.ys

    tape = []
    grads = {}

    def add_grad(name, idx, val):
        grads.setdefault(name, {})[idx] = val

    def dw_matmul(name, a, b, wname, idx, carrier):
        d = _matmul(name, a, b, "tn", comm=comm)
        comm.post(carrier, comm.grad_req(wname, idx, d))

    for layer in range(DEPTH):
        j = layer // 2
        rec = {"x_in": x}
        if layer % 2 == 0:
            h = _matmul(f"ab_in{j}", x, comm.weight("ab_w_in", j), "nn", comm=comm)
            conv_p = [_row_vec(w["ssd_conv_w"][j, k]) for k in range(4)] + [_row_vec(w["ssd_conv_b"][j])]
            conv = _Op(f"ssd_conv{j}", _make_conv_fn(4, True), (3, nb),
                       [(a, (1, 256), lambda g, c: (0, g)) for a in conv_p],
                       [(h, (bt, 256), lambda g, c: (c, AB_XBC // 256 + g))],
                       [((t, 768), (bt, 256), lambda g, c: (c, g), F32)], [(8, 256)], dx_dtypes=[MM_DTYPE])
            ssd = _Op(f"ssd{j}", _ssd_fn, (1, t // SSD_CHUNK),
                      [_whole(_heads64(w["ssd_dt_bias"][j])), _whole(_heads64(w["ssd_a_log"][j])),
                       _whole(_heads64(w["ssd_d"][j]))],
                      [(conv.ys[0], (SSD_CHUNK, 768), lambda g, c: (c, 0)),
                       (h, (SSD_CHUNK, 512), lambda g, c: (c, AB_DT // 512))],
                      [((t, 512), (SSD_CHUNK, 512), lambda g, c: (c, 0), F32)], [(LANES, LANES)] * 4,
                      dx_dtypes=[F32, MM_DTYPE], comm=comm)
            hg = _Op(f"hg{j}", _hg_fn, (1, t // HG_STEP), [_whole(lb_all[j])],
                     [(h, (HG_STEP, AB_DT), lambda g, c: (c, 0))],
                     [((t, 512), (HG_STEP, 512), lambda g, c: (c, 0), F32)], [(LANES, LANES)] * 4,
                     dx_dtypes=[MM_DTYPE], comm=comm)
            post = rowop(f"ab_post{j}", _ab_post_fn,
                         [_whole(_row_vec(w["ssd_norm_w"][j])), _whole(jnp.tile(_row_vec(w["hg_norm_w"][j]), (1, 4)))],
                         [rowblk(ssd.ys[0], 512), rowblk(hg.ys[0], 512), rowblk(h, 1024, AB_ZG // 1024)], [1024],
                         out_dtype=MM_DTYPE, dx_dtypes=[F32, F32, MM_DTYPE])
            rec.update(kind="ab", conv=conv, ssd=ssd, hg=hg, post=post)
        else:
            h = _matmul(f"cd_in{j}", x, comm.weight("cd_w_in", j), "nn", comm=comm)
            swa = _Op(f"swa{j}", _swa_fn, (1, t // SWA_BLOCK),
                      [_whole(jnp.tile(w["swa_sinks"][j].reshape(8, 1), (1, LANES)))],
                      [(h, (SWA_BLOCK, 1024), lambda g, c: (c, 0))],
                      [((t, 512), (SWA_BLOCK, 512), lambda g, c: (c, 0), F32)], [(SWA_BLOCK, LANES)] * 2,
                      dx_dtypes=[MM_DTYPE], comm=comm)
            conv_p = [_row_vec(w["rg_conv_w"][j, k]) for k in range(4)] + [_row_vec(w["rg_conv_b"][j])]
            conv = _Op(f"rg_conv{j}", _make_conv_fn(4, False), (2, nb),
                       [(a, (1, 256), lambda g, c: (0, g)) for a in conv_p],
                       [(h, (bt, 256), lambda g, c: (c, CD_XR // 256 + g))],
                       [((t, 512), (bt, 256), lambda g, c: (c, g), F32)], [(8, 256)], dx_dtypes=[MM_DTYPE])
            gate = rowop(f"rg_gate{j}", _rg_gate_fn,
                         [_whole(_block_diag(w["rg_wa"][j])), _whole(_row_vec(w["rg_ba"][j])),
                          _whole(_block_diag(w["rg_wx"][j])), _whole(_row_vec(w["rg_bx"][j])),
                          _whole(_row_vec(w["rg_lambda"][j]))],
                         [rowblk(conv.ys[0], 512)], [512, 512])
            scan = _Op(f"rg_scan{j}", _rg_scan_fn, (2, t // bs), [],
                       [(gate.ys[0], (bs, 256), lambda g, c: (c, g)), (gate.ys[1], (bs, 256), lambda g, c: (c, g))],
                       [((t, 512), (bs, 256), lambda g, c: (c, g), F32)], [(8, 256)])
            post = rowop(f"cd_post{j}", _cd_post_fn, [],
                         [rowblk(swa.ys[0], 512), rowblk(scan.ys[0], 512), rowblk(h, 512, CD_GATE // 512)], [1024],
                         out_dtype=MM_DTYPE, dx_dtypes=[F32, F32, MM_DTYPE])
            rec.update(kind="cd", swa=swa, conv=conv, gate=gate, scan=scan, post=post)
        kind = rec["kind"]
        ycat = post.ys[0]
        m = _matmul(f"mix_out{layer}", ycat, comm.weight(kind + "_w_out", j), "nn", comm=comm)
        ln1 = rowop(f"ln_a{layer}", _ln_res_fn,
                    [_whole(_row_vec(w["ln_g"][layer, 0])), _whole(_row_vec(w["ln_b"][layer, 0]))],
                    [rowblk(x, 1024), rowblk(m, 1024)], [1024], dx_dtypes=[F32, MM_DTYPE])
        x1 = ln1.ys[0]
        hu = _matmul(f"ffn_up{layer}", x1, comm.weight("ffn_w_up", layer), "nn", comm=comm)
        n_fb = FFN_DIM // FFN_BLK
        act_p = [_row_vec(w["ffn_conv_w"][layer, k]) for k in range(3)] + [_row_vec(w["ffn_conv_b"][layer])]
        act = _Op(f"ffn_act{layer}", _ffn_act_fn, (n_fb, nb),
                  [(a, (1, 2 * FFN_BLK), lambda g, c: (0, g)) for a in act_p],
                  [(hu, (bt, 2 * FFN_BLK), lambda g, c: (c, g))],
                  [((t, FFN_DIM), (bt, FFN_BLK), lambda g, c: (c, g), MM_DTYPE)], [(8, 2 * FFN_BLK)],
                  dx_dtypes=[MM_DTYPE], comm=comm)
        a = act.ys[0]
        f = _matmul(f"ffn_down{layer}", a, comm.weight("ffn_w_down", layer), "nn", comm=comm)
        ln2 = rowop(f"ln_f{layer}", _ln_res_fn,
                    [_whole(_row_vec(w["ln_g"][layer, 1])), _whole(_row_vec(w["ln_b"][layer, 1]))],
                    [rowblk(x1, 1024), rowblk(f, 1024)], [1024], dx_dtypes=[F32, MM_DTYPE])
        rec.update(ycat=ycat, ln1=ln1, x1=x1, act=act, a=a, ln2=ln2)
        tape.append(rec)
        x = ln2.ys[0]

    dx, loss = _loss_kernel(x, target)

    d_lb = [jnp.zeros((1, 512), F32), jnp.zeros((1, 512), F32)]
    for layer in reversed(range(DEPTH)):
        j = layer // 2
        rec = tape[layer]
        (dg, db), (dx1_res, df) = rec["ln2"].bwd([dx])
        add_grad("ln_g", (layer, 1), dg[0]); add_grad("ln_b", (layer, 1), db[0])
        dw_matmul(f"ffn_down_dw{layer}", rec["a"], df, "ffn_w_down", layer, f"ffn_act{layer}_bwd")
        da = _matmul(f"ffn_down_dx{layer}", df, comm.weight("ffn_w_down", layer), "nt", comm=comm)
        dpa, (dhu,) = rec["act"].bwd([da])
        add_grad("ffn_conv_w", layer, jnp.stack([dpa[k][0] for k in range(3)]))
        add_grad("ffn_conv_b", layer, dpa[3][0])
        core_bwd = ("hg" if rec["kind"] == "ab" else "swa") + f"{j}_bwd"
        dw_matmul(f"ffn_up_dw{layer}", rec["x1"], dhu, "ffn_w_up", layer, core_bwd)
        dx1 = _matmul(f"ffn_up_dx{layer}", dhu, comm.weight("ffn_w_up", layer), "nt", add=dx1_res, comm=comm)
        (dg, db), (dx_res, dm) = rec["ln1"].bwd([dx1])
        add_grad("ln_g", (layer, 0), dg[0]); add_grad("ln_b", (layer, 0), db[0])
        kind = rec["kind"]
        dw_matmul(f"mix_out_dw{layer}", rec["ycat"], dm, kind + "_w_out", j, f"{kind}_in_dw{j}")
        dycat = _matmul(f"mix_out_dx{layer}", dm, comm.weight(kind + "_w_out", j), "nt", comm=comm)
        if kind == "ab":
            (dnw_s, dnw_h), (dy_ssd, do_hg, dh) = rec["post"].bwd([dycat])
            add_grad("ssd_norm_w", j, dnw_s[0]); add_grad("hg_norm_w", j, dnw_h[0].reshape(4, LANES).sum(0))
            (dlb,), (dh,) = rec["hg"].bwd([do_hg], dx_into={0: dh})
            d_lb[j] = dlb
            (ddtb, dalog, ddsk), (dxbc_c, dh) = rec["ssd"].bwd([dy_ssd], dx_into={1: dh})
            add_grad("ssd_dt_bias", j, ddtb[0].reshape(8, 64).sum(-1))
            add_grad("ssd_a_log", j, dalog[0].reshape(8, 64).sum(-1))
            add_grad("ssd_d", j, ddsk[0].reshape(8, 64).sum(-1))
            dcp, (dh,) = rec["conv"].bwd([dxbc_c], dx_into={0: dh})
            add_grad("ssd_conv_w", j, jnp.stack([dcp[k][0] for k in range(4)]))
            add_grad("ssd_conv_b", j, dcp[4][0])
        else:
            _, (dyc, dhs, dh) = rec["post"].bwd([dycat])
            _, (da_s, du_s) = rec["scan"].bwd([dhs])
            (dwa, dba, dwx, dbx, dlam), (dxc,) = rec["gate"].bwd([da_s, du_s])
            add_grad("rg_wa", j, _block_diag_grad(dwa)); add_grad("rg_wx", j, _block_diag_grad(dwx))
            add_grad("rg_ba", j, dba[0]); add_grad("rg_bx", j, dbx[0]); add_grad("rg_lambda", j, dlam[0])
            dcp, (dh,) = rec["conv"].bwd([dxc], dx_into={0: dh})
            add_grad("rg_conv_w", j, jnp.stack([dcp[k][0] for k in range(4)]))
            add_grad("rg_conv_b", j, dcp[4][0])
            (dsink,), (dh,) = rec["swa"].bwd([dyc], dx_into={0: dh})
            add_grad("swa_sinks", j, dsink.sum(-1))
        dw_matmul(f"{kind}_in_dw{j}", rec["x_in"], dh, kind + "_w_in", j, f"ffn_act{layer - 1}_bwd")
        dx = _matmul(f"{kind}_in_dx{j}", dh, comm.weight(kind + "_w_in", j), "nt", add=dx_res, comm=comm)

    _, (dl0, dl1) = lb_op.bwd(d_lb)
    out = {"hg_lower": jnp.concatenate([dl0, dl1], axis=0)}
    for name, parts in grads.items():
        keys = sorted(parts)
        if isinstance(keys[0], tuple):
            out[name] = jnp.stack([jnp.stack([parts[(l, s)] for s in range(2)]) for l in range(DEPTH)])
        else:
            out[name] = jnp.stack([parts[k] for k in keys])
    return loss, dx, out


def _small_prep(w):
    k = {n: a for n, a in w.items() if n not in BIG}
    k["ffn_conv_w"], k["ffn_conv_b"] = _ffn_perm(w["ffn_conv_w"]), _ffn_perm(w["ffn_conv_b"])
    return k


def _small_unprep(g):
    out = dict(g)
    out["ffn_conv_w"], out["ffn_conv_b"] = _ffn_unperm(g["ffn_conv_w"]), _ffn_unperm(g["ffn_conv_b"])
    return out


def _local_step_full(x, target, full):
    comm = _LocalComm(full)
    loss, dx, sg = _local_step(x, target, _small_prep(full), comm)
    grads = _small_unprep(sg)
    for name in BIG:
        grads[name] = jnp.stack([comm.grads[name, l] for l in range(full[name].shape[0])])
    return loss, dx, grads


def kernel(x, ab_w_in, ssd_conv_w, ssd_conv_b, ssd_dt_bias, ssd_a_log, ssd_d, ssd_norm_w, hg_lower, hg_norm_w, ab_w_out, cd_w_in, swa_sinks, rg_conv_w, rg_conv_b, rg_wa, rg_ba, rg_wx, rg_bx, rg_lambda, cd_w_out, ffn_w_up, ffn_conv_w, ffn_conv_b, ffn_w_down, ln_g, ln_b, loss_target, m_ab_w_in, m_ssd_conv_w, m_ssd_conv_b, m_ssd_dt_bias, m_ssd_a_log, m_ssd_d, m_ssd_norm_w, m_hg_lower, m_hg_norm_w, m_ab_w_out, m_cd_w_in, m_swa_sinks, m_rg_conv_w, m_rg_conv_b, m_rg_wa, m_rg_ba, m_rg_wx, m_rg_bx, m_rg_lambda, m_cd_w_out, m_ffn_w_up, m_ffn_conv_w, m_ffn_conv_b, m_ffn_w_down, m_ln_g, m_ln_b, v_ab_w_in, v_ssd_conv_w, v_ssd_conv_b, v_ssd_dt_bias, v_ssd_a_log, v_ssd_d, v_ssd_norm_w, v_hg_lower, v_hg_norm_w, v_ab_w_out, v_cd_w_in, v_swa_sinks, v_rg_conv_w, v_rg_conv_b, v_rg_wa, v_rg_ba, v_rg_wx, v_rg_bx, v_rg_lambda, v_cd_w_out, v_ffn_w_up, v_ffn_conv_w, v_ffn_conv_b, v_ffn_w_down, v_ln_g, v_ln_b):
    args = dict(locals())
    wts = {n: args[n] for n in WEIGHTS}
    mom = {n: args["m_" + n] for n in WEIGHTS}
    var = {n: args["v_" + n] for n in WEIGHTS}
    axis = dict(SHARDED)
    small = [n for n, _ in SHARDED if n not in BIG]
    comm = _MeshComm(wts)

    def run(name, reqs):
        for (_, _, done), got in zip(reqs, _remote_copies(name, [(k, a) for k, a, _ in reqs])):
            done(got)

    full = {n: wts[n] for n in REPLICATED}

    def keep_small(n):
        def done(got):
            full[n] = _merge_shards(got.reshape((N_DEV,) + wts[n].shape), axis[n])
        return ("gather", _as2d(wts[n]), done)

    run("gather_first", [comm.gather_req("ab_w_in", 0), comm.gather_req("ab_w_out", 0)] + [keep_small(n) for n in small])

    loss, grad_x, sg = _local_step(x[0], loss_target[0], _small_prep(full), comm)
    grads = _small_unprep(sg)
    loss = lax.psum(loss, ("x", "y", "c"))

    parts = {}

    def keep_parts(n, kind, arr):
        return (kind, arr, lambda got: parts.__setitem__(n, got))

    last = comm.take_all()
    last += [keep_parts(n, "exchange", _split_shards(grads[n], axis[n]).reshape((N_DEV,) + _as2d(wts[n]).shape))
             for n in small]
    last += [keep_parts(n, "gather", _as2d(grads[n])) for n in REPLICATED]
    run("exchange_last", last)

    new = {}
    for n in BIG:
        new[n] = _adamw_big("adamw_" + n, [comm.recv[n, l] for l in range(wts[n].shape[0])], wts[n], mom[n], var[n])
    names = small + REPLICATED
    res = _adamw_small("adamw_small", [(parts[n], _as2d(wts[n]), _as2d(mom[n]), _as2d(var[n])) for n in names])
    for n, r in zip(names, res):
        new[n] = [a.reshape(wts[n].shape) for a in r]

    outs = [loss, grad_x[None]]
    for kind in range(4):
        outs += [new[n][kind] for n in WEIGHTS]
    return tuple(outs)
```

```python
import math

import numpy as np
import jax
import jax.numpy as jnp
from jax import lax
from jax.experimental import pallas as pl
from jax.experimental.pallas import tpu as pltpu

F32 = jnp.float32
BF16 = jnp.bfloat16
MM_DTYPE = BF16

DEPTH = 4
N_DEV = 8
LN_EPS = 1e-5
RMS_EPS = 1e-6
MASK_VALUE = -1e9
ALPHA = (2 * DEPTH) ** 0.25
RG_C = 8.0
FFN_DIM = 2816
SSD_CHUNK = 128
HG_STEP = 128
SWA_BLOCK = 128
LANES = 128
VMEM_LIMIT = 56 * 1024 * 1024

ADAM_LR, ADAM_B1, ADAM_B2, ADAM_EPS, ADAM_WD, ADAM_STEP = 0.001, 0.9, 0.999, 1e-08, 0.01, 10

AB_HEADS, AB_DT, AB_ZG, AB_XBC, AB_PAD = 0, 1536, 2048, 3072, 3840
CD_QKV, CD_GATE, CD_XR, CD_PAD = 0, 1024, 1536, 2048
FFN_BLK = 256
FFN_STRIP = 32


def _cols(x, lo, hi):
    n = x.shape[1]

    @jax.custom_vjp
    def f(x):
        return x[:, lo:hi]

    def bwd(_, g):
        parts = []
        if lo > 0:
            parts.append(jnp.zeros((g.shape[0], lo), g.dtype))
        parts.append(g)
        if hi < n:
            parts.append(jnp.zeros((g.shape[0], n - hi), g.dtype))
        return (jnp.concatenate(parts, axis=1) if len(parts) > 1 else g,)

    f.defvjp(lambda x: (f(x), None), bwd)
    return f(x)


def _rows(x, lo, hi):
    n = x.shape[0]

    @jax.custom_vjp
    def f(x):
        return x[lo:hi, :]

    def bwd(_, g):
        parts = []
        if lo > 0:
            parts.append(jnp.zeros((lo, g.shape[1]), g.dtype))
        parts.append(g)
        if hi < n:
            parts.append(jnp.zeros((n - hi, g.shape[1]), g.dtype))
        return (jnp.concatenate(parts, axis=0) if len(parts) > 1 else g,)

    f.defvjp(lambda x: (f(x), None), bwd)
    return f(x)


def _roll(x, shift, axis):
    n = x.shape[axis]
    shift = shift % n
    if shift == 0:
        return x

    @jax.custom_vjp
    def f(x):
        return pltpu.roll(x, shift, axis)

    f.defvjp(lambda x: (f(x), None), lambda _, g: (pltpu.roll(g, n - shift, axis),))
    return f(x)


def _dot(a, b, precision=None):
    return lax.dot_general(a, b, (((1,), (0,)), ((), ())), precision=precision, preferred_element_type=F32)


def _dot_nt(a, b, precision=None):
    return lax.dot_general(a, b, (((1,), (1,)), ((), ())), precision=precision, preferred_element_type=F32)


def _dot_tn(a, b, precision=None):
    return lax.dot_general(a, b, (((0,), (0,)), ((), ())), precision=precision, preferred_element_type=F32)


def _split3(x):
    hi = x.astype(BF16)
    r = x - hi.astype(F32)
    mid = r.astype(BF16)
    return hi, mid, (r - mid.astype(F32)).astype(BF16)


def _sel_dot(sel, x):
    def run(mat, v, dims):
        n = v.shape[1]
        y = lax.dot_general(mat, jnp.concatenate(_split3(v), axis=1), dims, preferred_element_type=F32)
        return y[:, :n] + y[:, n:2 * n] + y[:, 2 * n:]

    @jax.custom_vjp
    def f(sel, x):
        return run(sel, x, (((1,), (0,)), ((), ())))

    def bwd(sel, g):
        return jnp.zeros_like(sel), run(sel, g, (((0,), (0,)), ((), ())))

    f.defvjp(lambda sel, x: (f(sel, x), sel), bwd)
    return f(sel, x)


def _dot_sel(x, sel):
    def run(v, mat, dims):
        m = v.shape[0]
        y = lax.dot_general(jnp.concatenate(_split3(v), axis=0), mat, dims, preferred_element_type=F32)
        return y[:m] + y[m:2 * m] + y[2 * m:]

    @jax.custom_vjp
    def f(x, sel):
        return run(x, sel, (((1,), (0,)), ((), ())))

    def bwd(sel, g):
        return run(g, sel, (((1,), (1,)), ((), ()))), jnp.zeros_like(sel)

    f.defvjp(lambda x, sel: (f(x, sel), sel), bwd)
    return f(x, sel)


def _sigmoid(x):
    return 0.5 * jnp.tanh(0.5 * x) + 0.5


def _silu(x):
    return x * _sigmoid(x)


def _softplus(x):
    return jnp.maximum(x, 0.0) + jnp.log(1.0 + jnp.exp(-jnp.abs(x)))


def _gelu_tanh(x):
    c = math.sqrt(2.0 / math.pi)
    return 0.5 * x * (1.0 + jnp.tanh(c * (x + 0.044715 * (x * x * x))))


def _iota(shape, axis):
    return lax.broadcasted_iota(jnp.int32, shape, axis)


def _lane_mask(lo, hi, width=LANES):
    lane = _iota((1, width), 1)
    return ((lane >= lo) & (lane < hi)).astype(F32)


def _mesh_pos():
    return lax.axis_index("x"), lax.axis_index("y"), lax.axis_index("c")


def _carry_shapes(carry):
    return [jax.ShapeDtypeStruct((N_DEV,) + a.shape if kind == "gather" else a.shape, a.dtype) for kind, a in carry]


def _carry_scratch(carry):
    n = len(carry)
    if n == 0:
        return []
    return [pltpu.SemaphoreType.DMA((n, N_DEV - 1)), pltpu.SemaphoreType.DMA((n, N_DEV - 1)),
            pltpu.SemaphoreType.DMA((n,))]


def _carry_run(start, kinds, in_refs, out_refs, send_sems, recv_sems, local_sems):
    x, y, cc = _mesh_pos()
    me = 4 * x + 2 * y + cc
    for i, kind in enumerate(kinds):
        mine = in_refs[i] if kind == "gather" else in_refs[i].at[me]
        local = pltpu.make_async_copy(mine, out_refs[i].at[me], local_sems.at[i])
        remote = []
        for k in range(1, N_DEV):
            px, py, pc = x ^ ((k >> 2) & 1), y ^ ((k >> 1) & 1), cc ^ (k & 1)
            src = in_refs[i] if kind == "gather" else in_refs[i].at[4 * px + 2 * py + pc]
            remote.append(pltpu.make_async_remote_copy(
                src_ref=src, dst_ref=out_refs[i].at[me],
                send_sem=send_sems.at[i, k - 1], recv_sem=recv_sems.at[i, k - 1],
                device_id=(px, py, pc), device_id_type=pl.DeviceIdType.MESH))
        if start:
            local.start()
            for cp in remote:
                cp.start()
        else:
            for cp in remote:
                cp.wait_recv()
            for cp in remote:
                cp.wait_send()
            local.wait()


def _remote_copies(name, carry):
    n = len(carry)
    kinds = [k for k, _ in carry]

    def body(*refs):
        sems = refs[2 * n:]
        _carry_run(True, kinds, refs[:n], refs[n:2 * n], *sems)
        _carry_run(False, kinds, refs[:n], refs[n:2 * n], *sems)

    return pl.pallas_call(
        body, name=name, out_shape=_carry_shapes(carry),
        in_specs=[pl.BlockSpec(memory_space=pl.ANY)] * n, out_specs=[pl.BlockSpec(memory_space=pl.ANY)] * n,
        scratch_shapes=_carry_scratch(carry),
    )(*[a for _, a in carry])


def _cparams(sem):
    return pltpu.CompilerParams(dimension_semantics=sem, vmem_limit_bytes=VMEM_LIMIT)


def _chunk_fwd(name, fn, grid, params, xs, outs, state_shapes, carry=()):
    n_g, n_c = grid
    n_p, n_x, n_o, n_s, n_r = len(params), len(xs), len(outs), len(state_shapes), len(carry)
    kinds = [k for k, _ in carry]

    def body(*refs):
        i = 0
        p_refs = refs[i:i + n_p]; i += n_p
        x_refs = refs[i:i + n_x]; i += n_x
        ci_refs = refs[i:i + n_r]; i += n_r
        o_refs = refs[i:i + n_o]; i += n_o
        sv_refs = refs[i:i + n_s]; i += n_s
        co_refs = refs[i:i + n_r]; i += n_r
        st_refs = refs[i:i + n_s]; i += n_s
        sems = refs[i:]
        g, c = pl.program_id(0), pl.program_id(1)

        if n_r:
            @pl.when((g == 0) & (c == 0))
            def _():
                _carry_run(True, kinds, ci_refs, co_refs, *sems)

        @pl.when(c == 0)
        def _():
            for s in st_refs:
                s[...] = jnp.zeros(s.shape, s.dtype)

        st = [s[...] for s in st_refs]
        ys, new_st = fn(c, [p[...] for p in p_refs], [x[...].astype(F32) for x in x_refs], st)
        for o, y in zip(o_refs, ys):
            o[...] = y.astype(o.dtype)
        for sv, s in zip(sv_refs, st):
            sv[0, 0] = s
        for s_ref, s in zip(st_refs, new_st):
            s_ref[...] = s

        if n_r:
            @pl.when((g == n_g - 1) & (c == n_c - 1))
            def _():
                _carry_run(False, kinds, ci_refs, co_refs, *sems)

    any_spec = pl.BlockSpec(memory_space=pl.ANY)
    in_specs = [pl.BlockSpec(b, m) for _, b, m in params] + [pl.BlockSpec(b, m) for _, b, m in xs] + [any_spec] * n_r
    out_specs = [pl.BlockSpec(b, m) for _, b, m, _ in outs]
    out_shape = [jax.ShapeDtypeStruct(s, d) for s, _, _, d in outs]
    for shp in state_shapes:
        out_specs.append(pl.BlockSpec((1, 1) + shp, lambda g, c, n=len(shp): (g, c) + (0,) * n))
        out_shape.append(jax.ShapeDtypeStruct((n_g, n_c) + shp, F32))
    out_specs += [any_spec] * n_r
    out_shape += _carry_shapes(carry)
    res = pl.pallas_call(
        body, name=name, grid=grid, in_specs=in_specs, out_specs=out_specs, out_shape=out_shape,
        scratch_shapes=[pltpu.VMEM(shp, F32) for shp in state_shapes] + _carry_scratch(carry),
        compiler_params=_cparams(("arbitrary", "arbitrary")),
    )(*[a for a, _, _ in params], *[a for a, _, _ in xs], *[a for _, a in carry])
    return list(res[:n_o]), list(res[n_o:n_o + n_s]), list(res[n_o + n_s:])


def _chunk_bwd(name, fn, grid, params, xs, saved, dys, state_shapes, dx_dtypes, dx_into, carry=(), bwd_fn=None):
    n_g, n_c = grid
    n_p, n_x, n_s, n_y, n_r = len(params), len(xs), len(state_shapes), len(dys), len(carry)
    kinds = [k for k, _ in carry]
    into = sorted(dx_into)
    n_a = len(into)

    def rev(m):
        return lambda g, c: m(g, n_c - 1 - c)

    def body(*refs):
        i = 0
        p_refs = refs[i:i + n_p]; i += n_p
        x_refs = refs[i:i + n_x]; i += n_x
        sv_refs = refs[i:i + n_s]; i += n_s
        dy_refs = refs[i:i + n_y]; i += n_y
        i += n_a
        ci_refs = refs[i:i + n_r]; i += n_r
        dp_refs = refs[i:i + n_p]; i += n_p
        dx_refs = refs[i:i + n_x]; i += n_x
        co_refs = refs[i:i + n_r]; i += n_r
        ds_refs = refs[i:i + n_s]; i += n_s
        sems = refs[i:]
        g, c = pl.program_id(0), pl.program_id(1)
        chunk = n_c - 1 - c

        if n_r:
            @pl.when((g == 0) & (c == 0))
            def _():
                _carry_run(True, kinds, ci_refs, co_refs, *sems)

        @pl.when(c == 0)
        def _():
            for s in ds_refs:
                s[...] = jnp.zeros(s.shape, s.dtype)
            for d in dp_refs:
                d[...] = jnp.zeros(d.shape, d.dtype)

        pv = [p[...] for p in p_refs]
        xv = [x[...].astype(F32) for x in x_refs]
        sv = [s[0, 0] for s in sv_refs]
        dyv, dsv = [d[...].astype(F32) for d in dy_refs], [s[...] for s in ds_refs]
        if bwd_fn is None:
            _, vjp = jax.vjp(lambda p, x, s: fn(chunk, p, x, s), pv, xv, sv)
            dp, dx, ds = vjp((dyv, dsv))
        else:
            dp, dx, ds = bwd_fn(chunk, pv, xv, sv, dyv, dsv)
        for r, v in zip(dp_refs, dp):
            r[...] += v
        for r, v in zip(dx_refs, dx):
            r[...] = v.astype(r.dtype)
        for r, v in zip(ds_refs, ds):
            r[...] = v

        if n_r:
            @pl.when((g == n_g - 1) & (c == n_c - 1))
            def _():
                _carry_run(False, kinds, ci_refs, co_refs, *sems)

    any_spec = pl.BlockSpec(memory_space=pl.ANY)
    in_specs = [pl.BlockSpec(b, rev(m)) for _, b, m in params] + [pl.BlockSpec(b, rev(m)) for _, b, m in xs]
    for shp in state_shapes:
        in_specs.append(pl.BlockSpec((1, 1) + shp, lambda g, c, n=len(shp): (g, n_c - 1 - c) + (0,) * n))
    in_specs += [pl.BlockSpec(b, rev(m)) for _, b, m in dys]
    in_specs += [any_spec] * (n_a + n_r)
    out_specs = [pl.BlockSpec(b, rev(m)) for _, b, m in params] + [pl.BlockSpec(b, rev(m)) for _, b, m in xs]
    out_specs += [any_spec] * n_r
    out_shape = [jax.ShapeDtypeStruct(a.shape, F32) for a, _, _ in params]
    out_shape += [jax.ShapeDtypeStruct(a.shape, d) for (a, _, _), d in zip(xs, dx_dtypes)]
    out_shape += _carry_shapes(carry)
    first_alias = n_p + n_x + n_s + n_y
    aliases = {first_alias + k: n_p + xi for k, xi in enumerate(into)}
    res = pl.pallas_call(
        body, name=name, grid=grid, in_specs=in_specs, out_specs=out_specs, out_shape=out_shape,
        scratch_shapes=[pltpu.VMEM(shp, F32) for shp in state_shapes] + _carry_scratch(carry),
        input_output_aliases=aliases,
        compiler_params=_cparams(("arbitrary", "arbitrary")),
    )(*[a for a, _, _ in params], *[a for a, _, _ in xs], *saved, *[a for a, _, _ in dys],
      *[dx_into[xi] for xi in into], *[a for _, a in carry])
    return list(res[:n_p]), list(res[n_p:n_p + n_x]), list(res[n_p + n_x:])


class _Op:
    def __init__(self, name, fn, grid, params, xs, outs, state_shapes=(), dx_dtypes=None, comm=None, bwd_fn=None):
        self.name, self.fn, self.grid, self.comm, self.bwd_fn = name, fn, grid, comm, bwd_fn
        self.params, self.xs, self.outs, self.state_shapes = params, xs, outs, list(state_shapes)
        self.dx_dtypes = dx_dtypes or [F32] * len(xs)
        reqs = comm.take(name + "_fwd") if comm is not None else []
        self.ys, self.saved, got = _chunk_fwd(name + "_fwd", fn, grid, params, xs, outs, self.state_shapes,
                                              carry=[(k, a) for k, a, _ in reqs])
        for (_, _, done), g in zip(reqs, got):
            done(g)

    def bwd(self, dys, dx_into=None):
        dy_defs = [(d, b, m) for d, (_, b, m, _) in zip(dys, self.outs)]
        reqs = self.comm.take(self.name + "_bwd") if self.comm is not None else []
        dps, dxs, got = _chunk_bwd(self.name + "_bwd", self.fn, self.grid, self.params, self.xs, self.saved, dy_defs,
                                   self.state_shapes, self.dx_dtypes, dx_into or {},
                                   carry=[(k, a) for k, a, _ in reqs], bwd_fn=self.bwd_fn)
        for (_, _, done), g in zip(reqs, got):
            done(g)
        return dps, dxs


def _whole(a):
    nd = a.ndim
    return (a, a.shape, lambda g, c: (0,) * nd)


def _pick(n, prefs):
    for p in prefs:
        if n % p == 0:
            return p
    return n


def _mm_blocks(mode, m, n, k):
    bn = _pick(n, (1408, 1280, 1024, 768, 512, 256, 128))
    if mode == "tn":
        return _pick(m, (1408, 1024, 768, 512, 256, 128)), bn, _pick(k, (1024, 512, 256, 128))
    bk = k if k <= 3840 else _pick(k, (2816, 1920, 1408, 1024, 512, 256, 128))
    return _pick(m, (1024, 512, 256, 128)), bn, bk


def _matmul(name, a, b, mode, *, add=None, out_dtype=F32, comm=None):
    if mode == "nn":
        (m, k), n = a.shape, b.shape[1]
    elif mode == "nt":
        (m, k), n = a.shape, b.shape[0]
    else:
        (k, m), n = a.shape, b.shape[1]
    bm, bn, bk = _mm_blocks(mode, m, n, k)
    n_i, n_j, n_k = m // bm, n // bn, k // bk
    dims = {"nn": (((1,), (0,)), ((), ())), "nt": (((1,), (1,)), ((), ())), "tn": (((0,), (0,)), ((), ()))}[mode]
    has_add = add is not None
    reqs = comm.take(name) if comm is not None else []
    carry = [(kind, arr) for kind, arr, _ in reqs]
    kinds = [kind for kind, _ in carry]
    n_r = len(carry)

    def body(*refs):
        i = 2
        a_ref, b_ref = refs[0], refs[1]
        c_ref = refs[i] if has_add else None
        i += has_add
        ci_refs = refs[i:i + n_r]; i += n_r
        o_ref = refs[i]; i += 1
        co_refs = refs[i:i + n_r]; i += n_r
        acc = refs[i]; i += 1
        sems = refs[i:]
        ii, jj, kk = pl.program_id(0), pl.program_id(1), pl.program_id(2)

        if n_r:
            @pl.when((ii == 0) & (jj == 0) & (kk == 0))
            def _():
                _carry_run(True, kinds, ci_refs, co_refs, *sems)

        part = lax.dot_general(a_ref[...].astype(MM_DTYPE), b_ref[...].astype(MM_DTYPE), dims,
                               preferred_element_type=F32)

        def finish(r):
            if has_add:
                r = r + c_ref[...]
            o_ref[...] = r.astype(o_ref.dtype)

        if n_k == 1:
            finish(part)
        else:
            @pl.when(kk == 0)
            def _():
                acc[...] = part

            @pl.when((kk > 0) & (kk < n_k - 1))
            def _():
                acc[...] += part

            @pl.when(kk == n_k - 1)
            def _():
                finish(acc[...] + part)

        if n_r:
            @pl.when((ii == n_i - 1) & (jj == n_j - 1) & (kk == n_k - 1))
            def _():
                _carry_run(False, kinds, ci_refs, co_refs, *sems)

    if mode == "nn":
        a_spec = pl.BlockSpec((bm, bk), lambda i, j, kk: (i, kk))
        b_spec = pl.BlockSpec((bk, bn), lambda i, j, kk: (kk, j))
    elif mode == "nt":
        a_spec = pl.BlockSpec((bm, bk), lambda i, j, kk: (i, kk))
        b_spec = pl.BlockSpec((bn, bk), lambda i, j, kk: (j, kk))
    else:
        a_spec = pl.BlockSpec((bk, bm), lambda i, j, kk: (kk, i))
        b_spec = pl.BlockSpec((bk, bn), lambda i, j, kk: (kk, j))
    any_spec = pl.BlockSpec(memory_space=pl.ANY)
    in_specs, args = [a_spec, b_spec], [a, b]
    if has_add:
        in_specs.append(pl.BlockSpec((bm, bn), lambda i, j, kk: (i, j)))
        args.append(add)
    res = pl.pallas_call(
        body, name=name, grid=(n_i, n_j, n_k), in_specs=in_specs + [any_spec] * n_r,
        out_specs=[pl.BlockSpec((bm, bn), lambda i, j, kk: (i, j))] + [any_spec] * n_r,
        out_shape=[jax.ShapeDtypeStruct((m, n), out_dtype)] + _carry_shapes(carry),
        scratch_shapes=[pltpu.VMEM((bm, bn) if n_k > 1 else (8, LANES), F32)] + _carry_scratch(carry),
        compiler_params=_cparams(("arbitrary", "arbitrary", "arbitrary")),
    )(*args, *[arr for _, arr in carry])
    for (_, _, done), g in zip(reqs, res[1:]):
        done(g)
    return res[0]


def _ln_res_fn(_, p, x, st):
    g, b = p
    xin, m = x
    pre = ALPHA * xin + m
    mu = jnp.mean(pre, -1, keepdims=True)
    d = pre - mu
    var = jnp.mean(d * d, -1, keepdims=True)
    return [d * lax.rsqrt(var + LN_EPS) * g + b], []


def _make_conv_fn(taps, act):
    def fn(_, p, x, st):
        ws, b = p[:taps], p[taps]
        (xin,), (prev,) = x, st
        n = xin.shape[0]
        ext = jnp.concatenate([prev, xin], axis=0)
        y = b
        for k in range(taps):
            y = y + ws[k] * _rows(_roll(ext, taps - 1 - k, 0), 8, 8 + n)
        if act:
            y = _silu(y)
        return [y], [_rows(xin, n - 8, n)]

    return fn


def _ffn_act_fn(_, p, x, st):
    ws, b = p[:3], p[3]
    (xin,), (prev,) = x, st
    n = xin.shape[0]
    ext = jnp.concatenate([prev, xin], axis=0)
    y = b
    for k in range(3):
        y = y + ws[k] * _rows(_roll(ext, 2 - k, 0), 8, 8 + n)
    return [_silu(_cols(y, 0, FFN_BLK)) * _cols(y, FFN_BLK, 2 * FFN_BLK)], [_rows(xin, n - 8, n)]


def _ffn_act_bwd(_, p, x, st, dy, dst):
    (w0, w1, w2), b = p[:3], p[3]
    (xin,), (prev,), (da,), (dlast,) = x, st, dy, dst
    n, wd = xin.shape
    rs = FFN_STRIP
    zero8 = jnp.zeros((8, wd), F32)
    acc = [zero8] * 4
    after = zero8
    strips = [None] * (n // rs)
    for i in reversed(range(n // rs)):
        r0 = rs * i
        xs = jnp.concatenate([prev if i == 0 else xin[r0 - 8:r0], xin[r0:r0 + rs]], axis=0)
        x0, x1, x2 = xs[8:], pltpu.roll(xs, 1, 0)[8:], pltpu.roll(xs, 2, 0)[8:]
        y = b + w2 * x0 + w1 * x1 + w0 * x2
        g, u = y[:, :FFN_BLK], y[:, FFN_BLK:]
        s = _sigmoid(g)
        d = da[r0:r0 + rs]
        dyy = jnp.concatenate([d * u * (s * (1.0 + g * (1.0 - s))), d * (g * s)], axis=1)
        for k, v in enumerate((dyy * x2, dyy * x1, dyy * x0, dyy)):
            for r in range(0, rs, 8):
                acc[k] = acc[k] + v[r:r + 8]
        dyp = jnp.concatenate([dyy, after], axis=0)
        dxs = w2 * dyy + w1 * pltpu.roll(dyp, rs + 8 - 1, 0)[:rs] + w0 * pltpu.roll(dyp, rs + 8 - 2, 0)[:rs]
        if i == n // rs - 1:
            dxs = jnp.concatenate([dxs[:rs - 8], dxs[rs - 8:] + dlast], axis=0)
        strips[i] = dxs
        after = dyy[:8]
    head = jnp.concatenate([zero8, after], axis=0)
    dprev = (w1 * pltpu.roll(head, 16 - 1, 0) + w0 * pltpu.roll(head, 16 - 2, 0))[:8]
    return [jnp.sum(a, axis=0, keepdims=True) for a in acc], [jnp.concatenate(strips, axis=0)], [dprev]


def _ssd_fn(_, p, x, st):
    dtb, alog, dsk = p
    xbc, dtr = x
    L = SSD_CHUNK
    tril = _iota((L, L), 0) >= _iota((L, L), 1)
    xs, bm, cm = _cols(xbc, 0, 512), _cols(xbc, 512, 640), _cols(xbc, 640, 768)
    dt = _softplus(dtr + dtb)
    da = dt * (-jnp.exp(alog))
    cs = _sel_dot(tril.astype(BF16), da)
    pick = ((_iota((LANES, 2 * LANES), 0) == 0) & (_iota((LANES, 2 * LANES), 1) < LANES)) | (
        (_iota((LANES, 2 * LANES), 0) == 64) & (_iota((LANES, 2 * LANES), 1) >= LANES))
    pick = pick.astype(BF16)
    tot = jnp.sum(da, axis=0, keepdims=True)
    xc = xs * dt
    xdec = xc * jnp.exp(tot - cs)
    ecs = jnp.exp(cs)
    etot = jnp.exp(tot)
    ys, new_st = [], []
    for pr in range(4):
        lo, hi = LANES * pr, LANES * (pr + 1)
        grp = pr // 2
        c_g = cm * _lane_mask(64 * grp, 64 * grp + 64)
        gmat = _dot_nt(c_g, bm)
        cs_p, xc_p = _cols(cs, lo, hi), _cols(xc, lo, hi)
        cols2 = _dot_sel(cs_p, pick)
        yd = jnp.zeros((L, LANES), F32)
        for half in range(2):
            col = _cols(cols2, LANES * half, LANES * (half + 1))
            diff = col - col.T
            dec = jnp.where(tril, jnp.exp(jnp.where(tril, diff, 0.0)), 0.0)
            yd = yd + _dot(gmat * dec, xc_p) * _lane_mask(64 * half, 64 * half + 64)
        s_in = st[pr]
        y_off = _dot(c_g, s_in) * _cols(ecs, lo, hi)
        ys.append(yd + y_off + _cols(dsk, lo, hi) * _cols(xs, lo, hi))
        new_st.append(s_in * _cols(etot, lo, hi) + _dot_tn(bm, _cols(xdec, lo, hi)))
    return [jnp.concatenate(ys, axis=1)], new_st


def _hg_fn(_, p, x, st):
    (lb,) = p
    (xin,) = x
    L = HG_STEP
    n_lvl = L.bit_length() - 1
    hq, hf, hi = _cols(xin, 0, 512), _cols(xin, 512, 1024), _cols(xin, 1024, 1536)
    q = _silu(hq)
    logf = jnp.log(lb + (1.0 - lb) * _sigmoid(hf))
    k = (1.0 - lb) * _sigmoid(-hf)
    ti, si = _iota((L, L), 0), _iota((L, L), 1)
    bc = _sel_dot((ti >= si).astype(BF16), logf)
    tot = jnp.sum(logf, axis=0, keepdims=True)
    tn, sn = _iota((n_lvl * L, 1), 0), _iota((n_lvl * L, L), 1)
    row = tn & (L - 1)
    blk = L >> (tn >> n_lvl)
    piv = row - (row & (blk - 1)) + (blk >> 1)
    bcp_all = _sel_dot((sn == piv).astype(BF16), bc)
    t1 = _iota((L, 1), 0)
    qqs, kks, sames = [], [], []
    for lvl in range(n_lvl):
        size = L >> lvl
        upper = (t1 & (size - 1)) >= size // 2
        bcp = _rows(bcp_all, L * lvl, L * (lvl + 1))
        qqs.append(jnp.where(upper, q * jnp.exp(jnp.where(upper, bc - bcp, 0.0)), 0.0))
        kks.append(jnp.where(upper, 0.0, k * jnp.exp(jnp.where(upper, 0.0, bcp - bc))))
        sames.append((ti >> (n_lvl - lvl)) == (si >> (n_lvl - lvl)))
    q_in = q * jnp.exp(bc)
    k_out = k * jnp.exp(tot - bc)
    diag = q * k
    etot = jnp.exp(tot)
    outs, new_st = [], []
    for h in range(4):
        lo, up = LANES * h, LANES * (h + 1)
        attn = jnp.zeros((L, L), F32)
        for lvl in range(n_lvl):
            attn = attn + jnp.where(sames[lvl], _dot_nt(_cols(qqs[lvl], lo, up), _cols(kks[lvl], lo, up)), 0.0)
        v = _cols(hi, lo, up)
        out = _dot(attn, v) + jnp.sum(_cols(diag, lo, up), axis=-1, keepdims=True) * v
        outs.append(out + _dot_nt(_cols(q_in, lo, up), st[h]))
        new_st.append(st[h] * _cols(etot, lo, up) + _dot_tn(v, _cols(k_out, lo, up)))
    return [jnp.concatenate(outs, axis=1)], new_st


def _swa_fn(chunk, p, x, st):
    (sinks,) = p
    (xin,) = x
    q, k, v = _cols(xin, 0, 512), _cols(xin, 512, 640), _cols(xin, 640, 768)
    kp, vp = st
    T = SWA_BLOCK
    kc = jnp.concatenate([kp, k], axis=0)
    vc = jnp.concatenate([vp, v], axis=0)
    qi, kj = _iota((T, 2 * T), 0), _iota((T, 2 * T), 1)
    rel = qi + T - kj
    mask = (rel >= 0) & (rel < T) & ((kj >= T) | (chunk > 0))
    srow = _iota((8, LANES), 0)
    outs = []
    for pr in range(4):
        grp = pr // 2
        gm = _lane_mask(64 * grp, 64 * grp + 64)
        km, vm = kc * gm, vc * gm
        q2 = _cols(q, LANES * pr, LANES * (pr + 1))
        o2 = jnp.zeros((T, LANES), F32)
        for half in range(2):
            hm = _lane_mask(64 * half, 64 * half + 64)
            qh = q2 * hm
            if half != grp:
                qh = _roll(qh, 64, 1)
            s = _dot_nt(qh, km) * 0.125
            s = jnp.where(mask, s, MASK_VALUE)
            sink = jnp.mean(jnp.sum(jnp.where(srow == 2 * pr + half, sinks, 0.0), axis=0, keepdims=True),
                            axis=-1, keepdims=True)
            mx = lax.stop_gradient(jnp.maximum(jnp.max(s, axis=-1, keepdims=True), sink))
            e = jnp.exp(s - mx)
            den = jnp.sum(e, axis=-1, keepdims=True) + jnp.exp(sink - mx)
            o = _dot(e / den, vm)
            if half != grp:
                o = _roll(o, 64, 1)
            o2 = o2 + o * hm
        outs.append(o2)
    return [jnp.concatenate(outs, axis=1)], [k, v]


def _rg_gate_fn(_, p, x, st):
    wa, ba, wx, bx, lam = p
    (xc,) = x
    r = _sigmoid(_dot(xc, wa) + ba)
    i = _sigmoid(_dot(xc, wx) + bx)
    log_a = -RG_C * r * _softplus(-lam)
    a = jnp.exp(log_a)
    t = jnp.tanh(log_a)
    one_minus_a2 = -2.0 * t / (1.0 - t)
    u = jnp.sqrt(jnp.maximum(one_minus_a2, 0.0)) * (i * xc)
    return [a, u], []


def _rg_scan_fn(_, p, x, st):
    a, u = x
    (prev,) = st
    n = a.shape[0]
    row = _iota((n, 1), 0)
    s = 1
    while s < n:
        keep = row >= s
        a_s, u_s = _roll(a, s, 0), _roll(u, s, 0)
        u = jnp.where(keep, a * u_s + u, u)
        a = jnp.where(keep, a * a_s, a)
        s *= 2
    h_in = jnp.sum(jnp.where(_iota((8, 1), 0) == 7, prev, 0.0), axis=0, keepdims=True)
    h = u + a * h_in
    return [h], [_rows(h, n - 8, n)]


def _ab_post_fn(_, p, x, st):
    nw_ssd, nw_hg = p
    y, o, zg = x
    z, hgate = _cols(zg, 0, 512), _cols(zg, 512, 1024)
    lane = _iota((1, 512), 1)
    ya = y * _silu(z)
    sq = ya * ya
    inv = jnp.zeros_like(ya)
    for g in range(2):
        mk = (lane >= 256 * g) & (lane < 256 * (g + 1))
        ms = jnp.sum(jnp.where(mk, sq, 0.0), axis=-1, keepdims=True) / 256.0
        inv = jnp.where(mk, lax.rsqrt(ms + RMS_EPS), inv)
    ya = ya * inv * nw_ssd
    so = o * o
    inv = jnp.zeros_like(o)
    for h in range(4):
        mk = (lane >= 128 * h) & (lane < 128 * (h + 1))
        ms = jnp.sum(jnp.where(mk, so, 0.0), axis=-1, keepdims=True) / 128.0
        inv = jnp.where(mk, lax.rsqrt(ms + RMS_EPS), inv)
    yb = o * inv * nw_hg * _silu(hgate)
    return [jnp.concatenate([ya, yb], axis=1)], []


def _cd_post_fn(_, p, x, st):
    yc, h, gate = x
    return [jnp.concatenate([yc, h * _gelu_tanh(gate)], axis=1)], []


def _lb_fn(_, p, x, st):
    l0, l1 = x
    mx = lax.stop_gradient(jnp.maximum(l0, l1))
    e0, e1 = jnp.exp(l0 - mx), jnp.exp(l1 - mx)
    s0, s1 = e0 / (e0 + e1), e1 / (e0 + e1)
    return [jnp.clip(s0 - s0, 0.0, 1.0), jnp.clip((s0 + s1) - s0, 0.0, 1.0)], []


def _loss_kernel(y, target):
    t, d = y.shape
    bt = _pick(t, (512, 256, 128))

    def body(y_ref, t_ref, dy_ref, l_ref):
        @pl.when(pl.program_id(0) == 0)
        def _():
            l_ref[...] = jnp.zeros(l_ref.shape, F32)

        e = y_ref[...] - t_ref[...]
        dy_ref[...] = e * (1.0 / d)
        l_ref[...] += jnp.sum(e * e, axis=0, keepdims=True) * (0.5 / d)

    dy, part = pl.pallas_call(
        body, name="loss", grid=(t // bt,),
        in_specs=[pl.BlockSpec((bt, d), lambda i: (i, 0)), pl.BlockSpec((bt, d), lambda i: (i, 0))],
        out_specs=[pl.BlockSpec((bt, d), lambda i: (i, 0)), pl.BlockSpec((1, d), lambda i: (0, 0))],
        out_shape=[jax.ShapeDtypeStruct((t, d), F32), jax.ShapeDtypeStruct((1, d), F32)],
        compiler_params=_cparams(("arbitrary",)),
    )(y, target)
    return dy, jnp.sum(part)


def _adamw_math(parts, w_, m_, v_):
    c1 = 1.0 / (1.0 - ADAM_B1 ** ADAM_STEP)
    c2 = 1.0 / (1.0 - ADAM_B2 ** ADAM_STEP)
    g = parts[0].astype(F32)
    for s in range(1, N_DEV):
        g = g + parts[s].astype(F32)
    nm = ADAM_B1 * m_ + (1.0 - ADAM_B1) * g
    nv = ADAM_B2 * v_ + (1.0 - ADAM_B2) * (g * g)
    return g, -ADAM_LR * ((nm * c1) / (jnp.sqrt(nv * c2) + ADAM_EPS) + ADAM_WD * w_), nm, nv


def _adamw_big(name, parts, w, m, v):
    n_l, r, c = w.shape
    br = _pick(r, (256, 176, 128, 64, 32, 16, 8))

    def body(*refs):
        p_refs, (w_ref, m_ref, v_ref), outs = refs[:n_l], refs[n_l:n_l + 3], refs[n_l + 3:]
        for l in range(n_l):
            @pl.when(pl.program_id(0) == l)
            def _(p_ref=p_refs[l]):
                res = _adamw_math([p_ref[s] for s in range(N_DEV)], w_ref[...], m_ref[...], v_ref[...])
                for ref, val in zip(outs, res):
                    ref[...] = val

    blk = pl.BlockSpec((None, br, c), lambda l, i: (l, i, 0))
    p_specs = [pl.BlockSpec((N_DEV, br, c), lambda l, i, k=k: (0, jnp.where(l == k, i, 0), 0)) for k in range(n_l)]
    return pl.pallas_call(
        body, name=name, grid=(n_l, r // br), in_specs=p_specs + [blk, blk, blk],
        out_specs=[blk] * 4, out_shape=[jax.ShapeDtypeStruct(w.shape, F32)] * 4,
        compiler_params=_cparams(("arbitrary", "arbitrary")),
    )(*parts, w, m, v)


def _adamw_small(name, items):
    n = len(items)

    def body(*refs):
        ins, outs = refs[:4 * n], refs[4 * n:]
        for i in range(n):
            p_ref, w_ref, m_ref, v_ref = ins[4 * i:4 * i + 4]
            res = _adamw_math([p_ref[s] for s in range(N_DEV)], w_ref[...], m_ref[...], v_ref[...])
            for ref, val in zip(outs[4 * i:4 * i + 4], res):
                ref[...] = val

    flat = [a for it in items for a in it]
    out_shape = [jax.ShapeDtypeStruct(it[1].shape, F32) for it in items for _ in range(4)]
    res = pl.pallas_call(
        body, name=name, out_shape=out_shape,
        in_specs=[pl.BlockSpec(memory_space=pltpu.VMEM)] * len(flat),
        out_specs=[pl.BlockSpec(memory_space=pltpu.VMEM)] * len(out_shape),
        compiler_params=pltpu.CompilerParams(vmem_limit_bytes=VMEM_LIMIT),
    )(*flat)
    return [res[4 * i:4 * i + 4] for i in range(n)]


SHARDED = [("ab_w_in", 2), ("ab_w_out", 1), ("cd_w_in", 2), ("cd_w_out", 1), ("ffn_w_up", 2), ("ffn_w_down", 1),
           ("ssd_conv_w", 2), ("rg_conv_w", 2), ("rg_conv_b", 1), ("rg_ba", 1), ("rg_bx", 1), ("rg_lambda", 1),
           ("ffn_conv_w", 2), ("ln_g", 2), ("ln_b", 2)]
MATMUL_W = ("ab_w_in", "ab_w_out", "cd_w_in", "cd_w_out", "ffn_w_up", "ffn_w_down")
REPLICATED = ["ssd_conv_b", "ssd_dt_bias", "ssd_a_log", "ssd_d", "ssd_norm_w", "hg_lower", "hg_norm_w", "swa_sinks",
              "rg_wa", "rg_wx", "ffn_conv_b"]
WEIGHTS = ["ab_w_in", "ssd_conv_w", "ssd_conv_b", "ssd_dt_bias", "ssd_a_log", "ssd_d", "ssd_norm_w", "hg_lower",
           "hg_norm_w", "ab_w_out", "cd_w_in", "swa_sinks", "rg_conv_w", "rg_conv_b", "rg_wa", "rg_ba", "rg_wx",
           "rg_bx", "rg_lambda", "cd_w_out", "ffn_w_up", "ffn_conv_w", "ffn_conv_b", "ffn_w_down", "ln_g", "ln_b"]


def _as2d(a):
    return a.reshape(-1, a.shape[-1])


def _merge_shards(g, axis):
    g = jnp.moveaxis(g, 0, axis)
    shp = g.shape
    return g.reshape(shp[:axis] + (shp[axis] * shp[axis + 1],) + shp[axis + 2:])


def _split_shards(full, axis):
    shp = full.shape
    g = full.reshape(shp[:axis] + (N_DEV, shp[axis] // N_DEV) + shp[axis + 1:])
    return jnp.moveaxis(g, axis, 0)


def _ab_pad(w):
    z, xbc, dt = w[..., 0:512], w[..., 512:1280], w[..., 1280:1288]
    hqfi, hg = w[..., 1288:2824], w[..., 2824:3336]
    return jnp.concatenate([hqfi, jnp.repeat(dt, 64, axis=-1), z, hg, xbc], axis=-1)


def _ab_unpad(d):
    lead = d.shape[:-1]
    dt = d[..., AB_DT:AB_ZG].reshape(lead + (8, 64)).sum(-1)
    z, hg, xbc = d[..., AB_ZG:AB_ZG + 512], d[..., AB_ZG + 512:AB_XBC], d[..., AB_XBC:AB_PAD]
    return jnp.concatenate([z, xbc, dt, d[..., :AB_DT], hg], axis=-1)


def _cd_pad(w):
    return jnp.concatenate([w[..., :768], jnp.zeros(w.shape[:-1] + (256,), w.dtype), w[..., 768:]], axis=-1)


def _cd_unpad(d):
    return jnp.concatenate([d[..., :768], d[..., CD_GATE:CD_PAD]], axis=-1)


def _ffn_perm(w):
    parts = []
    for i in range(FFN_DIM // FFN_BLK):
        parts += [w[..., FFN_BLK * i:FFN_BLK * (i + 1)], w[..., FFN_DIM + FFN_BLK * i:FFN_DIM + FFN_BLK * (i + 1)]]
    return jnp.concatenate(parts, axis=-1)


def _ffn_unperm(d):
    n = FFN_DIM // FFN_BLK
    gate = [d[..., 2 * FFN_BLK * i:2 * FFN_BLK * i + FFN_BLK] for i in range(n)]
    up = [d[..., 2 * FFN_BLK * i + FFN_BLK:2 * FFN_BLK * (i + 1)] for i in range(n)]
    return jnp.concatenate(gate + up, axis=-1)


def _block_diag(w):
    eye = jnp.eye(8, dtype=w.dtype)
    return jnp.einsum("gij,gh->gihj", w, eye).reshape(512, 512)


def _block_diag_grad(d):
    return jnp.stack([d[64 * g:64 * g + 64, 64 * g:64 * g + 64] for g in range(8)])


def _same(a):
    return a


BIG = {"ab_w_in": (1, _ab_pad, _ab_unpad), "ab_w_out": (0, _same, _same), "cd_w_in": (1, _cd_pad, _cd_unpad),
       "cd_w_out": (0, _same, _same), "ffn_w_up": (1, _ffn_perm, _ffn_unperm), "ffn_w_down": (0, _same, _same)}


class _MeshComm:
    def __init__(self, shards):
        self.shards, self.full, self.recv, self.posted = shards, {}, {}, {}

    def post(self, carrier, req):
        self.posted.setdefault(carrier, []).append(req)

    def take(self, carrier):
        return self.posted.pop(carrier, [])

    def take_all(self):
        reqs = [r for name in list(self.posted) for r in self.posted.pop(name)]
        return reqs

    def gather_req(self, name, layer):
        axis, prep, _ = BIG[name]

        def done(got):
            self.full[name, layer] = prep(_merge_shards(got, axis))

        return ("gather", self.shards[name][layer].astype(MM_DTYPE), done)

    def weight(self, name, layer):
        return self.full[name, layer]

    def grad_req(self, name, layer, d):
        axis, _, unprep = BIG[name]

        def done(got):
            self.recv[name, layer] = got

        return ("exchange", _split_shards(unprep(d), axis).astype(MM_DTYPE), done)


class _LocalComm:
    def __init__(self, full):
        self.full_w, self.grads = full, {}

    def post(self, carrier, req):
        pass

    def take(self, carrier):
        return []

    def gather_req(self, name, layer):
        return None

    def weight(self, name, layer):
        return BIG[name][1](self.full_w[name][layer].astype(MM_DTYPE))

    def grad_req(self, name, layer, d):
        self.grads[name, layer] = BIG[name][2](d)
        return None


def _row_vec(v):
    return v.reshape(1, -1)


def _heads64(v):
    return jnp.repeat(v, 64).reshape(1, 512)


def _local_step(x, target, w, comm):
    kinds = ["ab" if layer % 2 == 0 else "cd" for layer in range(DEPTH)]
    in_name = [f"{kinds[layer]}_in{layer // 2}" for layer in range(DEPTH)]
    core_name = [("hg" if layer % 2 == 0 else "swa") + f"{layer // 2}_fwd" for layer in range(DEPTH)]
    comm.post("ssd0_fwd", comm.gather_req("ffn_w_down", 0))
    comm.post(core_name[0], comm.gather_req("ffn_w_up", 0))
    for layer in range(DEPTH - 1):
        nxt, nj = kinds[layer + 1], (layer + 1) // 2
        if kinds[layer] == "ab":
            comm.post(in_name[layer], comm.gather_req(nxt + "_w_in", nj))
            comm.post(core_name[layer], comm.gather_req(nxt + "_w_out", nj))
            comm.post(core_name[layer], comm.gather_req("ffn_w_down", layer + 1))
        else:
            comm.post(in_name[layer], comm.gather_req(nxt + "_w_out", nj))
            comm.post(core_name[layer], comm.gather_req(nxt + "_w_in", nj))
            comm.post(f"ffn_up{layer}", comm.gather_req("ffn_w_down", layer + 1))
        comm.post(f"ffn_act{layer}_fwd", comm.gather_req("ffn_w_up", layer + 1))

    t = x.shape[0]
    bt = _pick(t, (512, 256, 128))
    nb = t // bt
    bs = _pick(t, (256, 128))

    def rowop(name, fn, params, xs, widths_out, out_dtype=F32, dx_dtypes=None):
        outs = [((t, wd), (bt, wd), lambda g, c: (c, 0), out_dtype) for wd in widths_out]
        return _Op(name, fn, (1, nb), params, xs, outs, dx_dtypes=dx_dtypes)

    def rowblk(arr, width, first=0):
        return (arr, (bt, width), lambda g, c: (c, first))

    one_row = lambda g, c: (0, 0)
    lb_op = _Op("hg_lb", _lb_fn, (1, 1), [],
                [(w["hg_lower"][0:1], (1, 512), one_row), (w["hg_lower"][1:2], (1, 512), one_row)],
                [((1, 512), (1, 512), one_row, F32)] * 2)
    lb_all = lb_op.ys

    tape = []
    grads = {}

    def add_grad(name, idx, val):
        grads.setdefault(name, {})[idx] = val

    def dw_matmul(name, a, b, wname, idx, carrier):
        d = _matmul(name, a, b, "tn", out_dtype=MM_DTYPE, comm=comm)
        comm.post(carrier, comm.grad_req(wname, idx, d))

    for layer in range(DEPTH):
        j = layer // 2
        rec = {"x_in": x}
        if layer % 2 == 0:
            h = _matmul(f"ab_in{j}", x, comm.weight("ab_w_in", j), "nn", comm=comm)
            conv_p = [_row_vec(w["ssd_conv_w"][j, k]) for k in range(4)] + [_row_vec(w["ssd_conv_b"][j])]
            conv = _Op(f"ssd_conv{j}", _make_conv_fn(4, True), (3, nb),
                       [(a, (1, 256), lambda g, c: (0, g)) for a in conv_p],
                       [(h, (bt, 256), lambda g, c: (c, AB_XBC // 256 + g))],
                       [((t, 768), (bt, 256), lambda g, c: (c, g), F32)], [(8, 256)], dx_dtypes=[MM_DTYPE])
            ssd = _Op(f"ssd{j}", _ssd_fn, (1, t // SSD_CHUNK),
                      [_whole(_heads64(w["ssd_dt_bias"][j])), _whole(_heads64(w["ssd_a_log"][j])),
                       _whole(_heads64(w["ssd_d"][j]))],
                      [(conv.ys[0], (SSD_CHUNK, 768), lambda g, c: (c, 0)),
                       (h, (SSD_CHUNK, 512), lambda g, c: (c, AB_DT // 512))],
                      [((t, 512), (SSD_CHUNK, 512), lambda g, c: (c, 0), F32)], [(LANES, LANES)] * 4,
                      dx_dtypes=[F32, MM_DTYPE], comm=comm)
            hg = _Op(f"hg{j}", _hg_fn, (1, t // HG_STEP), [_whole(lb_all[j])],
                     [(h, (HG_STEP, AB_DT), lambda g, c: (c, 0))],
                     [((t, 512), (HG_STEP, 512), lambda g, c: (c, 0), F32)], [(LANES, LANES)] * 4,
                     dx_dtypes=[MM_DTYPE], comm=comm)
            post = rowop(f"ab_post{j}", _ab_post_fn,
                         [_whole(_row_vec(w["ssd_norm_w"][j])), _whole(jnp.tile(_row_vec(w["hg_norm_w"][j]), (1, 4)))],
                         [rowblk(ssd.ys[0], 512), rowblk(hg.ys[0], 512), rowblk(h, 1024, AB_ZG // 1024)], [1024],
                         out_dtype=MM_DTYPE, dx_dtypes=[F32, F32, MM_DTYPE])
            rec.update(kind="ab", conv=conv, ssd=ssd, hg=hg, post=post)
        else:
            h = _matmul(f"cd_in{j}", x, comm.weight("cd_w_in", j), "nn", comm=comm)
            swa = _Op(f"swa{j}", _swa_fn, (1, t // SWA_BLOCK),
                      [_whole(jnp.tile(w["swa_sinks"][j].reshape(8, 1), (1, LANES)))],
                      [(h, (SWA_BLOCK, 1024), lambda g, c: (c, 0))],
                      [((t, 512), (SWA_BLOCK, 512), lambda g, c: (c, 0), F32)], [(SWA_BLOCK, LANES)] * 2,
                      dx_dtypes=[MM_DTYPE], comm=comm)
            conv_p = [_row_vec(w["rg_conv_w"][j, k]) for k in range(4)] + [_row_vec(w["rg_conv_b"][j])]
            conv = _Op(f"rg_conv{j}", _make_conv_fn(4, False), (2, nb),
                       [(a, (1, 256), lambda g, c: (0, g)) for a in conv_p],
                       [(h, (bt, 256), lambda g, c: (c, CD_XR // 256 + g))],
                       [((t, 512), (bt, 256), lambda g, c: (c, g), F32)], [(8, 256)], dx_dtypes=[MM_DTYPE])
            gate = rowop(f"rg_gate{j}", _rg_gate_fn,
                         [_whole(_block_diag(w["rg_wa"][j])), _whole(_row_vec(w["rg_ba"][j])),
                          _whole(_block_diag(w["rg_wx"][j])), _whole(_row_vec(w["rg_bx"][j])),
                          _whole(_row_vec(w["rg_lambda"][j]))],
                         [rowblk(conv.ys[0], 512)], [512, 512])
            scan = _Op(f"rg_scan{j}", _rg_scan_fn, (2, t // bs), [],
                       [(gate.ys[0], (bs, 256), lambda g, c: (c, g)), (gate.ys[1], (bs, 256), lambda g, c: (c, g))],
                       [((t, 512), (bs, 256), lambda g, c: (c, g), F32)], [(8, 256)])
            post = rowop(f"cd_post{j}", _cd_post_fn, [],
                         [rowblk(swa.ys[0], 512), rowblk(scan.ys[0], 512), rowblk(h, 512, CD_GATE // 512)], [1024],
                         out_dtype=MM_DTYPE, dx_dtypes=[F32, F32, MM_DTYPE])
            rec.update(kind="cd", swa=swa, conv=conv, gate=gate, scan=scan, post=post)
        kind = rec["kind"]
        ycat = post.ys[0]
        m = _matmul(f"mix_out{layer}", ycat, comm.weight(kind + "_w_out", j), "nn", comm=comm)
        ln1 = rowop(f"ln_a{layer}", _ln_res_fn,
                    [_whole(_row_vec(w["ln_g"][layer, 0])), _whole(_row_vec(w["ln_b"][layer, 0]))],
                    [rowblk(x, 1024), rowblk(m, 1024)], [1024], dx_dtypes=[F32, MM_DTYPE])
        x1 = ln1.ys[0]
        hu = _matmul(f"ffn_up{layer}", x1, comm.weight("ffn_w_up", layer), "nn", comm=comm)
        n_fb = FFN_DIM // FFN_BLK
        act_p = [_row_vec(w["ffn_conv_w"][layer, k]) for k in range(3)] + [_row_vec(w["ffn_conv_b"][layer])]
        act = _Op(f"ffn_act{layer}", _ffn_act_fn, (n_fb, nb),
                  [(a, (1, 2 * FFN_BLK), lambda g, c: (0, g)) for a in act_p],
                  [(hu, (bt, 2 * FFN_BLK), lambda g, c: (c, g))],
                  [((t, FFN_DIM), (bt, FFN_BLK), lambda g, c: (c, g), MM_DTYPE)], [(8, 2 * FFN_BLK)],
                  dx_dtypes=[MM_DTYPE], comm=comm, bwd_fn=_ffn_act_bwd)
        a = act.ys[0]
        f = _matmul(f"ffn_down{layer}", a, comm.weight("ffn_w_down", layer), "nn", comm=comm)
        ln2 = rowop(f"ln_f{layer}", _ln_res_fn,
                    [_whole(_row_vec(w["ln_g"][layer, 1])), _whole(_row_vec(w["ln_b"][layer, 1]))],
                    [rowblk(x1, 1024), rowblk(f, 1024)], [1024], dx_dtypes=[F32, MM_DTYPE])
        rec.update(ycat=ycat, ln1=ln1, x1=x1, act=act, a=a, ln2=ln2)
        tape.append(rec)
        x = ln2.ys[0]

    dx, loss = _loss_kernel(x, target)

    d_lb = [jnp.zeros((1, 512), F32), jnp.zeros((1, 512), F32)]
    for layer in reversed(range(DEPTH)):
        j = layer // 2
        rec = tape[layer]
        (dg, db), (dx1_res, df) = rec["ln2"].bwd([dx])
        add_grad("ln_g", (layer, 1), dg[0]); add_grad("ln_b", (layer, 1), db[0])
        dw_matmul(f"ffn_down_dw{layer}", rec["a"], df, "ffn_w_down", layer, f"ffn_act{layer}_bwd")
        da = _matmul(f"ffn_down_dx{layer}", df, comm.weight("ffn_w_down", layer), "nt", comm=comm)
        dpa, (dhu,) = rec["act"].bwd([da])
        add_grad("ffn_conv_w", layer, jnp.stack([dpa[k][0] for k in range(3)]))
        add_grad("ffn_conv_b", layer, dpa[3][0])
        core_bwd = ("hg" if rec["kind"] == "ab" else "swa") + f"{j}_bwd"
        dw_matmul(f"ffn_up_dw{layer}", rec["x1"], dhu, "ffn_w_up", layer, core_bwd)
        dx1 = _matmul(f"ffn_up_dx{layer}", dhu, comm.weight("ffn_w_up", layer), "nt", add=dx1_res, comm=comm)
        (dg, db), (dx_res, dm) = rec["ln1"].bwd([dx1])
        add_grad("ln_g", (layer, 0), dg[0]); add_grad("ln_b", (layer, 0), db[0])
        kind = rec["kind"]
        dw_matmul(f"mix_out_dw{layer}", rec["ycat"], dm, kind + "_w_out", j, f"{kind}_in_dw{j}")
        dycat = _matmul(f"mix_out_dx{layer}", dm, comm.weight(kind + "_w_out", j), "nt", comm=comm)
        if kind == "ab":
            (dnw_s, dnw_h), (dy_ssd, do_hg, dh) = rec["post"].bwd([dycat])
            add_grad("ssd_norm_w", j, dnw_s[0]); add_grad("hg_norm_w", j, dnw_h[0].reshape(4, LANES).sum(0))
            (dlb,), (dh,) = rec["hg"].bwd([do_hg], dx_into={0: dh})
            d_lb[j] = dlb
            (ddtb, dalog, ddsk), (dxbc_c, dh) = rec["ssd"].bwd([dy_ssd], dx_into={1: dh})
            add_grad("ssd_dt_bias", j, ddtb[0].reshape(8, 64).sum(-1))
            add_grad("ssd_a_log", j, dalog[0].reshape(8, 64).sum(-1))
            add_grad("ssd_d", j, ddsk[0].reshape(8, 64).sum(-1))
            dcp, (dh,) = rec["conv"].bwd([dxbc_c], dx_into={0: dh})
            add_grad("ssd_conv_w", j, jnp.stack([dcp[k][0] for k in range(4)]))
            add_grad("ssd_conv_b", j, dcp[4][0])
        else:
            _, (dyc, dhs, dh) = rec["post"].bwd([dycat])
            _, (da_s, du_s) = rec["scan"].bwd([dhs])
            (dwa, dba, dwx, dbx, dlam), (dxc,) = rec["gate"].bwd([da_s, du_s])
            add_grad("rg_wa", j, _block_diag_grad(dwa)); add_grad("rg_wx", j, _block_diag_grad(dwx))
            add_grad("rg_ba", j, dba[0]); add_grad("rg_bx", j, dbx[0]); add_grad("rg_lambda", j, dlam[0])
            dcp, (dh,) = rec["conv"].bwd([dxc], dx_into={0: dh})
            add_grad("rg_conv_w", j, jnp.stack([dcp[k][0] for k in range(4)]))
            add_grad("rg_conv_b", j, dcp[4][0])
            (dsink,), (dh,) = rec["swa"].bwd([dyc], dx_into={0: dh})
            add_grad("swa_sinks", j, dsink.sum(-1))
        dw_matmul(f"{kind}_in_dw{j}", rec["x_in"], dh, kind + "_w_in", j,
                  f"ffn_act{layer - 1}_bwd" if layer > 0 else f"{kind}_in_dx{j}")
        dx = _matmul(f"{kind}_in_dx{j}", dh, comm.weight(kind + "_w_in", j), "nt", add=dx_res, comm=comm)

    _, (dl0, dl1) = lb_op.bwd(d_lb)
    out = {"hg_lower": jnp.concatenate([dl0, dl1], axis=0)}
    for name, parts in grads.items():
        keys = sorted(parts)
        if isinstance(keys[0], tuple):
            out[name] = jnp.stack([jnp.stack([parts[(l, s)] for s in range(2)]) for l in range(DEPTH)])
        else:
            out[name] = jnp.stack([parts[k] for k in keys])
    return loss, dx, out


def _small_prep(w):
    k = {n: a for n, a in w.items() if n not in BIG}
    k["ffn_conv_w"], k["ffn_conv_b"] = _ffn_perm(w["ffn_conv_w"]), _ffn_perm(w["ffn_conv_b"])
    return k


def _small_unprep(g):
    out = dict(g)
    out["ffn_conv_w"], out["ffn_conv_b"] = _ffn_unperm(g["ffn_conv_w"]), _ffn_unperm(g["ffn_conv_b"])
    return out


def _local_step_full(x, target, full):
    comm = _LocalComm(full)
    loss, dx, sg = _local_step(x, target, _small_prep(full), comm)
    grads = _small_unprep(sg)
    for name in BIG:
        grads[name] = jnp.stack([comm.grads[name, l] for l in range(full[name].shape[0])])
    return loss, dx, grads


def kernel(x, ab_w_in, ssd_conv_w, ssd_conv_b, ssd_dt_bias, ssd_a_log, ssd_d, ssd_norm_w, hg_lower, hg_norm_w, ab_w_out, cd_w_in, swa_sinks, rg_conv_w, rg_conv_b, rg_wa, rg_ba, rg_wx, rg_bx, rg_lambda, cd_w_out, ffn_w_up, ffn_conv_w, ffn_conv_b, ffn_w_down, ln_g, ln_b, loss_target, m_ab_w_in, m_ssd_conv_w, m_ssd_conv_b, m_ssd_dt_bias, m_ssd_a_log, m_ssd_d, m_ssd_norm_w, m_hg_lower, m_hg_norm_w, m_ab_w_out, m_cd_w_in, m_swa_sinks, m_rg_conv_w, m_rg_conv_b, m_rg_wa, m_rg_ba, m_rg_wx, m_rg_bx, m_rg_lambda, m_cd_w_out, m_ffn_w_up, m_ffn_conv_w, m_ffn_conv_b, m_ffn_w_down, m_ln_g, m_ln_b, v_ab_w_in, v_ssd_conv_w, v_ssd_conv_b, v_ssd_dt_bias, v_ssd_a_log, v_ssd_d, v_ssd_norm_w, v_hg_lower, v_hg_norm_w, v_ab_w_out, v_cd_w_in, v_swa_sinks, v_rg_conv_w, v_rg_conv_b, v_rg_wa, v_rg_ba, v_rg_wx, v_rg_bx, v_rg_lambda, v_cd_w_out, v_ffn_w_up, v_ffn_conv_w, v_ffn_conv_b, v_ffn_w_down, v_ln_g, v_ln_b):
    args = dict(locals())
    wts = {n: args[n] for n in WEIGHTS}
    mom = {n: args["m_" + n] for n in WEIGHTS}
    var = {n: args["v_" + n] for n in WEIGHTS}
    axis = dict(SHARDED)
    small = [n for n, _ in SHARDED if n not in BIG]
    comm = _MeshComm(wts)

    def run(name, reqs):
        for (_, _, done), got in zip(reqs, _remote_copies(name, [(k, a) for k, a, _ in reqs])):
            done(got)

    full = {n: wts[n] for n in REPLICATED}

    def keep_small(n):
        def done(got):
            full[n] = _merge_shards(got.reshape((N_DEV,) + wts[n].shape), axis[n])
        return ("gather", _as2d(wts[n]), done)

    run("gather_first", [comm.gather_req("ab_w_in", 0), comm.gather_req("ab_w_out", 0)] + [keep_small(n) for n in small])

    loss, grad_x, sg = _local_step(x[0], loss_target[0], _small_prep(full), comm)
    grads = _small_unprep(sg)
    loss = lax.psum(loss, ("x", "y", "c"))

    parts = {}

    def keep_parts(n, kind, arr):
        return (kind, arr, lambda got: parts.__setitem__(n, got))

    last = comm.take_all()
    last += [keep_parts(n, "exchange", _split_shards(grads[n], axis[n]).reshape((N_DEV,) + _as2d(wts[n]).shape))
             for n in small]
    last += [keep_parts(n, "gather", _as2d(grads[n])) for n in REPLICATED]
    run("exchange_last", last)

    new = {}
    for n in BIG:
        new[n] = _adamw_big("adamw_" + n, [comm.recv[n, l] for l in range(wts[n].shape[0])], wts[n], mom[n], var[n])
    names = small + REPLICATED
    res = _adamw_small("adamw_small", [(parts[n], _as2d(wts[n]), _as2d(mom[n]), _as2d(var[n])) for n in names])
    for n, r in zip(names, res):
        new[n] = [a.reshape(wts[n].shape) for a in r]

    outs = [loss, grad_x[None]]
    for kind in range(4):
        outs += [new[n][kind] for n in WEIGHTS]
    return tuple(outs)
```

```python
import math

import numpy as np
import jax
import jax.numpy as jnp
from jax import lax
from jax.experimental import pallas as pl
from jax.experimental.pallas import tpu as pltpu

F32 = jnp.float32
BF16 = jnp.bfloat16
MM_DTYPE = BF16

DEPTH = 4
N_DEV = 8
LN_EPS = 1e-5
RMS_EPS = 1e-6
MASK_VALUE = -1e9
ALPHA = (2 * DEPTH) ** 0.25
RG_C = 8.0
FFN_DIM = 2816
SSD_CHUNK = 128
HG_STEP = 128
SWA_BLOCK = 128
LANES = 128
VMEM_LIMIT = 56 * 1024 * 1024

ADAM_LR, ADAM_B1, ADAM_B2, ADAM_EPS, ADAM_WD, ADAM_STEP = 0.001, 0.9, 0.999, 1e-08, 0.01, 10

AB_HEADS, AB_DT, AB_ZG, AB_XBC, AB_PAD = 0, 1536, 2048, 3072, 3840
CD_QKV, CD_GATE, CD_XR, CD_PAD = 0, 1024, 1536, 2048
FFN_BLK = 256
FFN_STRIP = 32


def _cols(x, lo, hi):
    n = x.shape[1]

    @jax.custom_vjp
    def f(x):
        return x[:, lo:hi]

    def bwd(_, g):
        parts = []
        if lo > 0:
            parts.append(jnp.zeros((g.shape[0], lo), g.dtype))
        parts.append(g)
        if hi < n:
            parts.append(jnp.zeros((g.shape[0], n - hi), g.dtype))
        return (jnp.concatenate(parts, axis=1) if len(parts) > 1 else g,)

    f.defvjp(lambda x: (f(x), None), bwd)
    return f(x)


def _rows(x, lo, hi):
    n = x.shape[0]

    @jax.custom_vjp
    def f(x):
        return x[lo:hi, :]

    def bwd(_, g):
        parts = []
        if lo > 0:
            parts.append(jnp.zeros((lo, g.shape[1]), g.dtype))
        parts.append(g)
        if hi < n:
            parts.append(jnp.zeros((n - hi, g.shape[1]), g.dtype))
        return (jnp.concatenate(parts, axis=0) if len(parts) > 1 else g,)

    f.defvjp(lambda x: (f(x), None), bwd)
    return f(x)


def _roll(x, shift, axis):
    n = x.shape[axis]
    shift = shift % n
    if shift == 0:
        return x

    @jax.custom_vjp
    def f(x):
        return pltpu.roll(x, shift, axis)

    f.defvjp(lambda x: (f(x), None), lambda _, g: (pltpu.roll(g, n - shift, axis),))
    return f(x)


def _dot(a, b, precision=None):
    return lax.dot_general(a, b, (((1,), (0,)), ((), ())), precision=precision, preferred_element_type=F32)


def _dot_nt(a, b, precision=None):
    return lax.dot_general(a, b, (((1,), (1,)), ((), ())), precision=precision, preferred_element_type=F32)


def _dot_tn(a, b, precision=None):
    return lax.dot_general(a, b, (((0,), (0,)), ((), ())), precision=precision, preferred_element_type=F32)


def _split3(x):
    hi = x.astype(BF16)
    r = x - hi.astype(F32)
    mid = r.astype(BF16)
    return hi, mid, (r - mid.astype(F32)).astype(BF16)


def _sel_dot(sel, x):
    def run(mat, v, dims):
        n = v.shape[1]
        y = lax.dot_general(mat, jnp.concatenate(_split3(v), axis=1), dims, preferred_element_type=F32)
        return y[:, :n] + y[:, n:2 * n] + y[:, 2 * n:]

    @jax.custom_vjp
    def f(sel, x):
        return run(sel, x, (((1,), (0,)), ((), ())))

    def bwd(sel, g):
        return jnp.zeros_like(sel), run(sel, g, (((0,), (0,)), ((), ())))

    f.defvjp(lambda sel, x: (f(sel, x), sel), bwd)
    return f(sel, x)


def _dot_sel(x, sel):
    def run(v, mat, dims):
        m = v.shape[0]
        y = lax.dot_general(jnp.concatenate(_split3(v), axis=0), mat, dims, preferred_element_type=F32)
        return y[:m] + y[m:2 * m] + y[2 * m:]

    @jax.custom_vjp
    def f(x, sel):
        return run(x, sel, (((1,), (0,)), ((), ())))

    def bwd(sel, g):
        return run(g, sel, (((1,), (1,)), ((), ()))), jnp.zeros_like(sel)

    f.defvjp(lambda x, sel: (f(x, sel), sel), bwd)
    return f(x, sel)


def _sigmoid(x):
    return 0.5 * jnp.tanh(0.5 * x) + 0.5


def _silu(x):
    return x * _sigmoid(x)


def _softplus(x):
    return jnp.maximum(x, 0.0) + jnp.log(1.0 + jnp.exp(-jnp.abs(x)))


def _gelu_tanh(x):
    c = math.sqrt(2.0 / math.pi)
    return 0.5 * x * (1.0 + jnp.tanh(c * (x + 0.044715 * (x * x * x))))


def _iota(shape, axis):
    return lax.broadcasted_iota(jnp.int32, shape, axis)


def _lane_mask(lo, hi, width=LANES):
    lane = _iota((1, width), 1)
    return ((lane >= lo) & (lane < hi)).astype(F32)


def _mesh_pos():
    return lax.axis_index("x"), lax.axis_index("y"), lax.axis_index("c")


def _carry_shapes(carry):
    return [jax.ShapeDtypeStruct((N_DEV,) + a.shape if kind == "gather" else a.shape, a.dtype) for kind, a in carry]


def _carry_scratch(carry):
    n = len(carry)
    if n == 0:
        return []
    return [pltpu.SemaphoreType.DMA((n, N_DEV - 1)), pltpu.SemaphoreType.DMA((n, N_DEV - 1)),
            pltpu.SemaphoreType.DMA((n,))]


def _carry_run(start, kinds, in_refs, out_refs, send_sems, recv_sems, local_sems):
    x, y, cc = _mesh_pos()
    me = 4 * x + 2 * y + cc
    for i, kind in enumerate(kinds):
        mine = in_refs[i] if kind == "gather" else in_refs[i].at[me]
        local = pltpu.make_async_copy(mine, out_refs[i].at[me], local_sems.at[i])
        remote = []
        for k in range(1, N_DEV):
            px, py, pc = x ^ ((k >> 2) & 1), y ^ ((k >> 1) & 1), cc ^ (k & 1)
            src = in_refs[i] if kind == "gather" else in_refs[i].at[4 * px + 2 * py + pc]
            remote.append(pltpu.make_async_remote_copy(
                src_ref=src, dst_ref=out_refs[i].at[me],
                send_sem=send_sems.at[i, k - 1], recv_sem=recv_sems.at[i, k - 1],
                device_id=(px, py, pc), device_id_type=pl.DeviceIdType.MESH))
        if start:
            local.start()
            for cp in remote:
                cp.start()
        else:
            for cp in remote:
                cp.wait_recv()
            for cp in remote:
                cp.wait_send()
            local.wait()


def _remote_copies(name, carry):
    n = len(carry)
    kinds = [k for k, _ in carry]

    def body(*refs):
        sems = refs[2 * n:]
        _carry_run(True, kinds, refs[:n], refs[n:2 * n], *sems)
        _carry_run(False, kinds, refs[:n], refs[n:2 * n], *sems)

    return pl.pallas_call(
        body, name=name, out_shape=_carry_shapes(carry),
        in_specs=[pl.BlockSpec(memory_space=pl.ANY)] * n, out_specs=[pl.BlockSpec(memory_space=pl.ANY)] * n,
        scratch_shapes=_carry_scratch(carry),
    )(*[a for _, a in carry])


def _cparams(sem):
    return pltpu.CompilerParams(dimension_semantics=sem, vmem_limit_bytes=VMEM_LIMIT)


def _chunk_fwd(name, fn, grid, params, xs, outs, state_shapes, carry=()):
    n_g, n_c = grid
    n_p, n_x, n_o, n_s, n_r = len(params), len(xs), len(outs), len(state_shapes), len(carry)
    kinds = [k for k, _ in carry]

    def body(*refs):
        i = 0
        p_refs = refs[i:i + n_p]; i += n_p
        x_refs = refs[i:i + n_x]; i += n_x
        ci_refs = refs[i:i + n_r]; i += n_r
        o_refs = refs[i:i + n_o]; i += n_o
        sv_refs = refs[i:i + n_s]; i += n_s
        co_refs = refs[i:i + n_r]; i += n_r
        st_refs = refs[i:i + n_s]; i += n_s
        sems = refs[i:]
        g, c = pl.program_id(0), pl.program_id(1)

        if n_r:
            @pl.when((g == 0) & (c == 0))
            def _():
                _carry_run(True, kinds, ci_refs, co_refs, *sems)

        @pl.when(c == 0)
        def _():
            for s in st_refs:
                s[...] = jnp.zeros(s.shape, s.dtype)

        st = [s[...] for s in st_refs]
        ys, new_st = fn(c, [p[...] for p in p_refs], [x[...].astype(F32) for x in x_refs], st)
        for o, y in zip(o_refs, ys):
            o[...] = y.astype(o.dtype)
        for sv, s in zip(sv_refs, st):
            sv[0, 0] = s
        for s_ref, s in zip(st_refs, new_st):
            s_ref[...] = s

        if n_r:
            @pl.when((g == n_g - 1) & (c == n_c - 1))
            def _():
                _carry_run(False, kinds, ci_refs, co_refs, *sems)

    any_spec = pl.BlockSpec(memory_space=pl.ANY)
    in_specs = [pl.BlockSpec(b, m) for _, b, m in params] + [pl.BlockSpec(b, m) for _, b, m in xs] + [any_spec] * n_r
    out_specs = [pl.BlockSpec(b, m) for _, b, m, _ in outs]
    out_shape = [jax.ShapeDtypeStruct(s, d) for s, _, _, d in outs]
    for shp in state_shapes:
        out_specs.append(pl.BlockSpec((1, 1) + shp, lambda g, c, n=len(shp): (g, c) + (0,) * n))
        out_shape.append(jax.ShapeDtypeStruct((n_g, n_c) + shp, F32))
    out_specs += [any_spec] * n_r
    out_shape += _carry_shapes(carry)
    res = pl.pallas_call(
        body, name=name, grid=grid, in_specs=in_specs, out_specs=out_specs, out_shape=out_shape,
        scratch_shapes=[pltpu.VMEM(shp, F32) for shp in state_shapes] + _carry_scratch(carry),
        compiler_params=_cparams(("arbitrary", "arbitrary")),
    )(*[a for a, _, _ in params], *[a for a, _, _ in xs], *[a for _, a in carry])
    return list(res[:n_o]), list(res[n_o:n_o + n_s]), list(res[n_o + n_s:])


def _chunk_bwd(name, fn, grid, params, xs, saved, dys, state_shapes, dx_dtypes, dx_into, carry=(), bwd_fn=None):
    n_g, n_c = grid
    n_p, n_x, n_s, n_y, n_r = len(params), len(xs), len(state_shapes), len(dys), len(carry)
    kinds = [k for k, _ in carry]
    into = sorted(dx_into)
    n_a = len(into)

    def rev(m):
        return lambda g, c: m(g, n_c - 1 - c)

    def body(*refs):
        i = 0
        p_refs = refs[i:i + n_p]; i += n_p
        x_refs = refs[i:i + n_x]; i += n_x
        sv_refs = refs[i:i + n_s]; i += n_s
        dy_refs = refs[i:i + n_y]; i += n_y
        i += n_a
        ci_refs = refs[i:i + n_r]; i += n_r
        dp_refs = refs[i:i + n_p]; i += n_p
        dx_refs = refs[i:i + n_x]; i += n_x
        co_refs = refs[i:i + n_r]; i += n_r
        ds_refs = refs[i:i + n_s]; i += n_s
        sems = refs[i:]
        g, c = pl.program_id(0), pl.program_id(1)
        chunk = n_c - 1 - c

        if n_r:
            @pl.when((g == 0) & (c == 0))
            def _():
                _carry_run(True, kinds, ci_refs, co_refs, *sems)

        @pl.when(c == 0)
        def _():
            for s in ds_refs:
                s[...] = jnp.zeros(s.shape, s.dtype)
            for d in dp_refs:
                d[...] = jnp.zeros(d.shape, d.dtype)

        pv = [p[...] for p in p_refs]
        xv = [x[...].astype(F32) for x in x_refs]
        sv = [s[0, 0] for s in sv_refs]
        dyv, dsv = [d[...].astype(F32) for d in dy_refs], [s[...] for s in ds_refs]
        if bwd_fn is None:
            _, vjp = jax.vjp(lambda p, x, s: fn(chunk, p, x, s), pv, xv, sv)
            dp, dx, ds = vjp((dyv, dsv))
        else:
            dp, dx, ds = bwd_fn(chunk, pv, xv, sv, dyv, dsv)
        for r, v in zip(dp_refs, dp):
            r[...] += v
        for r, v in zip(dx_refs, dx):
            r[...] = v.astype(r.dtype)
        for r, v in zip(ds_refs, ds):
            r[...] = v

        if n_r:
            @pl.when((g == n_g - 1) & (c == n_c - 1))
            def _():
                _carry_run(False, kinds, ci_refs, co_refs, *sems)

    any_spec = pl.BlockSpec(memory_space=pl.ANY)
    in_specs = [pl.BlockSpec(b, rev(m)) for _, b, m in params] + [pl.BlockSpec(b, rev(m)) for _, b, m in xs]
    for shp in state_shapes:
        in_specs.append(pl.BlockSpec((1, 1) + shp, lambda g, c, n=len(shp): (g, n_c - 1 - c) + (0,) * n))
    in_specs += [pl.BlockSpec(b, rev(m)) for _, b, m in dys]
    in_specs += [any_spec] * (n_a + n_r)
    out_specs = [pl.BlockSpec(b, rev(m)) for _, b, m in params] + [pl.BlockSpec(b, rev(m)) for _, b, m in xs]
    out_specs += [any_spec] * n_r
    out_shape = [jax.ShapeDtypeStruct(a.shape, F32) for a, _, _ in params]
    out_shape += [jax.ShapeDtypeStruct(a.shape, d) for (a, _, _), d in zip(xs, dx_dtypes)]
    out_shape += _carry_shapes(carry)
    first_alias = n_p + n_x + n_s + n_y
    aliases = {first_alias + k: n_p + xi for k, xi in enumerate(into)}
    res = pl.pallas_call(
        body, name=name, grid=grid, in_specs=in_specs, out_specs=out_specs, out_shape=out_shape,
        scratch_shapes=[pltpu.VMEM(shp, F32) for shp in state_shapes] + _carry_scratch(carry),
        input_output_aliases=aliases,
        compiler_params=_cparams(("arbitrary", "arbitrary")),
    )(*[a for a, _, _ in params], *[a for a, _, _ in xs], *saved, *[a for a, _, _ in dys],
      *[dx_into[xi] for xi in into], *[a for _, a in carry])
    return list(res[:n_p]), list(res[n_p:n_p + n_x]), list(res[n_p + n_x:])


class _Op:
    def __init__(self, name, fn, grid, params, xs, outs, state_shapes=(), dx_dtypes=None, comm=None, bwd_fn=None):
        self.name, self.fn, self.grid, self.comm, self.bwd_fn = name, fn, grid, comm, bwd_fn
        self.params, self.xs, self.outs, self.state_shapes = params, xs, outs, list(state_shapes)
        self.dx_dtypes = dx_dtypes or [F32] * len(xs)
        reqs = comm.take(name + "_fwd") if comm is not None else []
        self.ys, self.saved, got = _chunk_fwd(name + "_fwd", fn, grid, params, xs, outs, self.state_shapes,
                                              carry=[(k, a) for k, a, _ in reqs])
        for (_, _, done), g in zip(reqs, got):
            done(g)

    def bwd(self, dys, dx_into=None):
        dy_defs = [(d, b, m) for d, (_, b, m, _) in zip(dys, self.outs)]
        reqs = self.comm.take(self.name + "_bwd") if self.comm is not None else []
        dps, dxs, got = _chunk_bwd(self.name + "_bwd", self.fn, self.grid, self.params, self.xs, self.saved, dy_defs,
                                   self.state_shapes, self.dx_dtypes, dx_into or {},
                                   carry=[(k, a) for k, a, _ in reqs], bwd_fn=self.bwd_fn)
        for (_, _, done), g in zip(reqs, got):
            done(g)
        return dps, dxs


def _whole(a):
    nd = a.ndim
    return (a, a.shape, lambda g, c: (0,) * nd)


def _pick(n, prefs):
    for p in prefs:
        if n % p == 0:
            return p
    return n


def _mm_blocks(mode, m, n, k):
    bn = _pick(n, (1408, 1280, 1024, 768, 512, 256, 128))
    if mode == "tn":
        return _pick(m, (1408, 1024, 768, 512, 256, 128)), bn, _pick(k, (1024, 512, 256, 128))
    bk = k if k <= 3840 else _pick(k, (2816, 1920, 1408, 1024, 512, 256, 128))
    return _pick(m, (1024, 512, 256, 128)), bn, bk


def _matmul(name, a, b, mode, *, add=None, out_dtype=F32, comm=None, b_cols=None):
    c0, csize = b_cols if b_cols is not None else (0, b.shape[1])
    if mode == "nn":
        (m, k), n = a.shape, csize
    elif mode == "nt":
        (m, k), n = a.shape, b.shape[0]
        assert k == csize
    else:
        (k, m), n = a.shape, b.shape[1]
        assert b_cols is None
    bm, bn, bk = _mm_blocks(mode, m, n, k)
    assert c0 % (bn if mode == "nn" else bk) == 0
    j0, k0 = (c0 // bn, 0) if mode == "nn" else (0, c0 // bk)
    n_i, n_j, n_k = m // bm, n // bn, k // bk
    dims = {"nn": (((1,), (0,)), ((), ())), "nt": (((1,), (1,)), ((), ())), "tn": (((0,), (0,)), ((), ()))}[mode]
    has_add = add is not None
    reqs = comm.take(name) if comm is not None else []
    carry = [(kind, arr) for kind, arr, _ in reqs]
    kinds = [kind for kind, _ in carry]
    n_r = len(carry)

    def body(*refs):
        i = 2
        a_ref, b_ref = refs[0], refs[1]
        c_ref = refs[i] if has_add else None
        i += has_add
        ci_refs = refs[i:i + n_r]; i += n_r
        o_ref = refs[i]; i += 1
        co_refs = refs[i:i + n_r]; i += n_r
        acc = refs[i]; i += 1
        sems = refs[i:]
        ii, jj, kk = pl.program_id(0), pl.program_id(1), pl.program_id(2)

        if n_r:
            @pl.when((ii == 0) & (jj == 0) & (kk == 0))
            def _():
                _carry_run(True, kinds, ci_refs, co_refs, *sems)

        part = lax.dot_general(a_ref[...].astype(MM_DTYPE), b_ref[...].astype(MM_DTYPE), dims,
                               preferred_element_type=F32)

        def finish(r):
            if has_add:
                r = r + c_ref[...]
            o_ref[...] = r.astype(o_ref.dtype)

        if n_k == 1:
            finish(part)
        else:
            @pl.when(kk == 0)
            def _():
                acc[...] = part

            @pl.when((kk > 0) & (kk < n_k - 1))
            def _():
                acc[...] += part

            @pl.when(kk == n_k - 1)
            def _():
                finish(acc[...] + part)

        if n_r:
            @pl.when((ii == n_i - 1) & (jj == n_j - 1) & (kk == n_k - 1))
            def _():
                _carry_run(False, kinds, ci_refs, co_refs, *sems)

    if mode == "nn":
        a_spec = pl.BlockSpec((bm, bk), lambda i, j, kk: (i, kk))
        b_spec = pl.BlockSpec((bk, bn), lambda i, j, kk: (kk, j0 + j))
    elif mode == "nt":
        a_spec = pl.BlockSpec((bm, bk), lambda i, j, kk: (i, kk))
        b_spec = pl.BlockSpec((bn, bk), lambda i, j, kk: (j, k0 + kk))
    else:
        a_spec = pl.BlockSpec((bk, bm), lambda i, j, kk: (kk, i))
        b_spec = pl.BlockSpec((bk, bn), lambda i, j, kk: (kk, j))
    any_spec = pl.BlockSpec(memory_space=pl.ANY)
    in_specs, args = [a_spec, b_spec], [a, b]
    if has_add:
        in_specs.append(pl.BlockSpec((bm, bn), lambda i, j, kk: (i, j)))
        args.append(add)
    res = pl.pallas_call(
        body, name=name, grid=(n_i, n_j, n_k), in_specs=in_specs + [any_spec] * n_r,
        out_specs=[pl.BlockSpec((bm, bn), lambda i, j, kk: (i, j))] + [any_spec] * n_r,
        out_shape=[jax.ShapeDtypeStruct((m, n), out_dtype)] + _carry_shapes(carry),
        scratch_shapes=[pltpu.VMEM((bm, bn) if n_k > 1 else (8, LANES), F32)] + _carry_scratch(carry),
        compiler_params=_cparams(("arbitrary", "arbitrary", "arbitrary")),
    )(*args, *[arr for _, arr in carry])
    for (_, _, done), g in zip(reqs, res[1:]):
        done(g)
    return res[0]


def _ln_res_fn(_, p, x, st):
    g, b = p
    xin, m = x
    pre = ALPHA * xin + m
    mu = jnp.mean(pre, -1, keepdims=True)
    d = pre - mu
    var = jnp.mean(d * d, -1, keepdims=True)
    return [d * lax.rsqrt(var + LN_EPS) * g + b], []


def _make_conv_fn(taps, act):
    def fn(_, p, x, st):
        ws, b = p[:taps], p[taps]
        (xin,), (prev,) = x, st
        n = xin.shape[0]
        ext = jnp.concatenate([prev, xin], axis=0)
        y = b
        for k in range(taps):
            y = y + ws[k] * _rows(_roll(ext, taps - 1 - k, 0), 8, 8 + n)
        if act:
            y = _silu(y)
        return [y], [_rows(xin, n - 8, n)]

    return fn


def _ffn_act_fn(_, p, x, st):
    n = x[0].shape[0]
    ys = []
    for half in range(2):
        ws, b = p[4 * half:4 * half + 3], p[4 * half + 3]
        ext = jnp.concatenate([st[half], x[half]], axis=0)
        y = b
        for k in range(3):
            y = y + ws[k] * _rows(_roll(ext, 2 - k, 0), 8, 8 + n)
        ys.append(y)
    return [_silu(ys[0]) * ys[1]], [_rows(x[0], n - 8, n), _rows(x[1], n - 8, n)]


def _ffn_act_bwd(_, p, x, st, dy, dst):
    (da,) = dy
    n, wd = x[0].shape
    rs = FFN_STRIP
    last = n // rs - 1
    zero8 = jnp.zeros((8, wd), F32)
    acc = [[zero8] * 4, [zero8] * 4]
    after = [zero8, zero8]
    strips = [[None] * (n // rs), [None] * (n // rs)]
    for i in reversed(range(n // rs)):
        r0 = rs * i
        taps, ys = [], []
        for half in range(2):
            w0, w1, w2, b = p[4 * half:4 * half + 4]
            xs = jnp.concatenate([st[half] if i == 0 else x[half][r0 - 8:r0], x[half][r0:r0 + rs]], axis=0)
            taps.append((pltpu.roll(xs, 2, 0)[8:], pltpu.roll(xs, 1, 0)[8:], xs[8:]))
            ys.append(b + w2 * taps[half][2] + w1 * taps[half][1] + w0 * taps[half][0])
        g, u = ys
        s = _sigmoid(g)
        d = da[r0:r0 + rs]
        dys = (d * u * (s * (1.0 + g * (1.0 - s))), d * (g * s))
        for half in range(2):
            w0, w1, w2, _ = p[4 * half:4 * half + 4]
            dyh = dys[half]
            for k, v in enumerate((dyh * taps[half][0], dyh * taps[half][1], dyh * taps[half][2], dyh)):
                for r in range(0, rs, 8):
                    acc[half][k] = acc[half][k] + v[r:r + 8]
            dyp = jnp.concatenate([dyh, after[half]], axis=0)
            dxs = w2 * dyh + w1 * pltpu.roll(dyp, rs + 8 - 1, 0)[:rs] + w0 * pltpu.roll(dyp, rs + 8 - 2, 0)[:rs]
            if i == last:
                dxs = jnp.concatenate([dxs[:rs - 8], dxs[rs - 8:] + dst[half]], axis=0)
            strips[half][i] = dxs
            after[half] = dyh[:8]
    dprev = []
    for half in range(2):
        w0, w1 = p[4 * half], p[4 * half + 1]
        head = jnp.concatenate([zero8, after[half]], axis=0)
        dprev.append((w1 * pltpu.roll(head, 16 - 1, 0) + w0 * pltpu.roll(head, 16 - 2, 0))[:8])
    dps = [jnp.sum(a, axis=0, keepdims=True) for half in range(2) for a in acc[half]]
    return dps, [jnp.concatenate(s_, axis=0) for s_ in strips], dprev


def _ssd_fn(_, p, x, st):
    dtb, alog, dsk = p
    xbc, dtr = x
    L = SSD_CHUNK
    tril = _iota((L, L), 0) >= _iota((L, L), 1)
    xs, bm, cm = _cols(xbc, 0, 512), _cols(xbc, 512, 640), _cols(xbc, 640, 768)
    dt = _softplus(dtr + dtb)
    da = dt * (-jnp.exp(alog))
    cs = _sel_dot(tril.astype(BF16), da)
    pick = ((_iota((LANES, 2 * LANES), 0) == 0) & (_iota((LANES, 2 * LANES), 1) < LANES)) | (
        (_iota((LANES, 2 * LANES), 0) == 64) & (_iota((LANES, 2 * LANES), 1) >= LANES))
    pick = pick.astype(BF16)
    tot = jnp.sum(da, axis=0, keepdims=True)
    xc = xs * dt
    xdec = xc * jnp.exp(tot - cs)
    ecs = jnp.exp(cs)
    etot = jnp.exp(tot)
    ys, new_st = [], []
    for pr in range(4):
        lo, hi = LANES * pr, LANES * (pr + 1)
        grp = pr // 2
        c_g = cm * _lane_mask(64 * grp, 64 * grp + 64)
        gmat = _dot_nt(c_g, bm)
        cs_p, xc_p = _cols(cs, lo, hi), _cols(xc, lo, hi)
        cols2 = _dot_sel(cs_p, pick)
        yd = jnp.zeros((L, LANES), F32)
        for half in range(2):
            col = _cols(cols2, LANES * half, LANES * (half + 1))
            diff = col - col.T
            dec = jnp.where(tril, jnp.exp(jnp.where(tril, diff, 0.0)), 0.0)
            yd = yd + _dot(gmat * dec, xc_p) * _lane_mask(64 * half, 64 * half + 64)
        s_in = st[pr]
        y_off = _dot(c_g, s_in) * _cols(ecs, lo, hi)
        ys.append(yd + y_off + _cols(dsk, lo, hi) * _cols(xs, lo, hi))
        new_st.append(s_in * _cols(etot, lo, hi) + _dot_tn(bm, _cols(xdec, lo, hi)))
    return [jnp.concatenate(ys, axis=1)], new_st


def _hg_fn(_, p, x, st):
    (lb,) = p
    (xin,) = x
    L = HG_STEP
    n_lvl = L.bit_length() - 1
    hq, hf, hi = _cols(xin, 0, 512), _cols(xin, 512, 1024), _cols(xin, 1024, 1536)
    q = _silu(hq)
    logf = jnp.log(lb + (1.0 - lb) * _sigmoid(hf))
    k = (1.0 - lb) * _sigmoid(-hf)
    ti, si = _iota((L, L), 0), _iota((L, L), 1)
    bc = _sel_dot((ti >= si).astype(BF16), logf)
    tot = jnp.sum(logf, axis=0, keepdims=True)
    tn, sn = _iota((n_lvl * L, 1), 0), _iota((n_lvl * L, L), 1)
    row = tn & (L - 1)
    blk = L >> (tn >> n_lvl)
    piv = row - (row & (blk - 1)) + (blk >> 1)
    bcp_all = _sel_dot((sn == piv).astype(BF16), bc)
    t1 = _iota((L, 1), 0)
    qqs, kks, sames = [], [], []
    for lvl in range(n_lvl):
        size = L >> lvl
        upper = (t1 & (size - 1)) >= size // 2
        bcp = _rows(bcp_all, L * lvl, L * (lvl + 1))
        qqs.append(jnp.where(upper, q * jnp.exp(jnp.where(upper, bc - bcp, 0.0)), 0.0))
        kks.append(jnp.where(upper, 0.0, k * jnp.exp(jnp.where(upper, 0.0, bcp - bc))))
        sames.append((ti >> (n_lvl - lvl)) == (si >> (n_lvl - lvl)))
    q_in = q * jnp.exp(bc)
    k_out = k * jnp.exp(tot - bc)
    diag = q * k
    etot = jnp.exp(tot)
    outs, new_st = [], []
    for h in range(4):
        lo, up = LANES * h, LANES * (h + 1)
        attn = jnp.zeros((L, L), F32)
        for lvl in range(n_lvl):
            attn = attn + jnp.where(sames[lvl], _dot_nt(_cols(qqs[lvl], lo, up), _cols(kks[lvl], lo, up)), 0.0)
        v = _cols(hi, lo, up)
        out = _dot(attn, v) + jnp.sum(_cols(diag, lo, up), axis=-1, keepdims=True) * v
        outs.append(out + _dot_nt(_cols(q_in, lo, up), st[h]))
        new_st.append(st[h] * _cols(etot, lo, up) + _dot_tn(v, _cols(k_out, lo, up)))
    return [jnp.concatenate(outs, axis=1)], new_st


def _swa_fn(chunk, p, x, st):
    (sinks,) = p
    (xin,) = x
    q, k, v = _cols(xin, 0, 512), _cols(xin, 512, 640), _cols(xin, 640, 768)
    kp, vp = st
    T = SWA_BLOCK
    kc = jnp.concatenate([kp, k], axis=0)
    vc = jnp.concatenate([vp, v], axis=0)
    qi, kj = _iota((T, 2 * T), 0), _iota((T, 2 * T), 1)
    rel = qi + T - kj
    mask = (rel >= 0) & (rel < T) & ((kj >= T) | (chunk > 0))
    srow = _iota((8, LANES), 0)
    outs = []
    for pr in range(4):
        grp = pr // 2
        gm = _lane_mask(64 * grp, 64 * grp + 64)
        km, vm = kc * gm, vc * gm
        q2 = _cols(q, LANES * pr, LANES * (pr + 1))
        o2 = jnp.zeros((T, LANES), F32)
        for half in range(2):
            hm = _lane_mask(64 * half, 64 * half + 64)
            qh = q2 * hm
            if half != grp:
                qh = _roll(qh, 64, 1)
            s = _dot_nt(qh, km) * 0.125
            s = jnp.where(mask, s, MASK_VALUE)
            sink = jnp.mean(jnp.sum(jnp.where(srow == 2 * pr + half, sinks, 0.0), axis=0, keepdims=True),
                            axis=-1, keepdims=True)
            mx = lax.stop_gradient(jnp.maximum(jnp.max(s, axis=-1, keepdims=True), sink))
            e = jnp.exp(s - mx)
            den = jnp.sum(e, axis=-1, keepdims=True) + jnp.exp(sink - mx)
            o = _dot(e / den, vm)
            if half != grp:
                o = _roll(o, 64, 1)
            o2 = o2 + o * hm
        outs.append(o2)
    return [jnp.concatenate(outs, axis=1)], [k, v]


def _rg_gate_fn(_, p, x, st):
    wa, ba, wx, bx, lam = p
    (xc,) = x
    r = _sigmoid(_dot(xc, wa) + ba)
    i = _sigmoid(_dot(xc, wx) + bx)
    log_a = -RG_C * r * _softplus(-lam)
    a = jnp.exp(log_a)
    t = jnp.tanh(log_a)
    one_minus_a2 = -2.0 * t / (1.0 - t)
    u = jnp.sqrt(jnp.maximum(one_minus_a2, 0.0)) * (i * xc)
    return [a, u], []


def _rg_scan_fn(_, p, x, st):
    a, u = x
    (prev,) = st
    n = a.shape[0]
    row = _iota((n, 1), 0)
    s = 1
    while s < n:
        keep = row >= s
        a_s, u_s = _roll(a, s, 0), _roll(u, s, 0)
        u = jnp.where(keep, a * u_s + u, u)
        a = jnp.where(keep, a * a_s, a)
        s *= 2
    h_in = jnp.sum(jnp.where(_iota((8, 1), 0) == 7, prev, 0.0), axis=0, keepdims=True)
    h = u + a * h_in
    return [h], [_rows(h, n - 8, n)]


def _ab_post_fn(_, p, x, st):
    nw_ssd, nw_hg = p
    y, o, zg = x
    z, hgate = _cols(zg, 0, 512), _cols(zg, 512, 1024)
    lane = _iota((1, 512), 1)
    ya = y * _silu(z)
    sq = ya * ya
    inv = jnp.zeros_like(ya)
    for g in range(2):
        mk = (lane >= 256 * g) & (lane < 256 * (g + 1))
        ms = jnp.sum(jnp.where(mk, sq, 0.0), axis=-1, keepdims=True) / 256.0
        inv = jnp.where(mk, lax.rsqrt(ms + RMS_EPS), inv)
    ya = ya * inv * nw_ssd
    so = o * o
    inv = jnp.zeros_like(o)
    for h in range(4):
        mk = (lane >= 128 * h) & (lane < 128 * (h + 1))
        ms = jnp.sum(jnp.where(mk, so, 0.0), axis=-1, keepdims=True) / 128.0
        inv = jnp.where(mk, lax.rsqrt(ms + RMS_EPS), inv)
    yb = o * inv * nw_hg * _silu(hgate)
    return [jnp.concatenate([ya, yb], axis=1)], []


def _cd_post_fn(_, p, x, st):
    yc, h, gate = x
    return [jnp.concatenate([yc, h * _gelu_tanh(gate)], axis=1)], []


def _lb_fn(_, p, x, st):
    l0, l1 = x
    mx = lax.stop_gradient(jnp.maximum(l0, l1))
    e0, e1 = jnp.exp(l0 - mx), jnp.exp(l1 - mx)
    s0, s1 = e0 / (e0 + e1), e1 / (e0 + e1)
    return [jnp.clip(s0 - s0, 0.0, 1.0), jnp.clip((s0 + s1) - s0, 0.0, 1.0)], []


def _loss_kernel(y, target):
    t, d = y.shape
    bt = _pick(t, (512, 256, 128))

    def body(y_ref, t_ref, dy_ref, l_ref):
        @pl.when(pl.program_id(0) == 0)
        def _():
            l_ref[...] = jnp.zeros(l_ref.shape, F32)

        e = y_ref[...] - t_ref[...]
        dy_ref[...] = e * (1.0 / d)
        l_ref[...] += jnp.sum(e * e, axis=0, keepdims=True) * (0.5 / d)

    dy, part = pl.pallas_call(
        body, name="loss", grid=(t // bt,),
        in_specs=[pl.BlockSpec((bt, d), lambda i: (i, 0)), pl.BlockSpec((bt, d), lambda i: (i, 0))],
        out_specs=[pl.BlockSpec((bt, d), lambda i: (i, 0)), pl.BlockSpec((1, d), lambda i: (0, 0))],
        out_shape=[jax.ShapeDtypeStruct((t, d), F32), jax.ShapeDtypeStruct((1, d), F32)],
        compiler_params=_cparams(("arbitrary",)),
    )(y, target)
    return dy, jnp.sum(part)


def _adamw_math(parts, w_, m_, v_):
    c1 = 1.0 / (1.0 - ADAM_B1 ** ADAM_STEP)
    c2 = 1.0 / (1.0 - ADAM_B2 ** ADAM_STEP)
    g = parts[0].astype(F32)
    for s in range(1, N_DEV):
        g = g + parts[s].astype(F32)
    nm = ADAM_B1 * m_ + (1.0 - ADAM_B1) * g
    nv = ADAM_B2 * v_ + (1.0 - ADAM_B2) * (g * g)
    return g, -ADAM_LR * ((nm * c1) / (jnp.sqrt(nv * c2) + ADAM_EPS) + ADAM_WD * w_), nm, nv


def _adamw_big(name, parts, w, m, v):
    n_l, r, c = w.shape
    br = _pick(r, (256, 176, 128, 64, 32, 16, 8))

    def body(*refs):
        p_refs, (w_ref, m_ref, v_ref), outs = refs[:n_l], refs[n_l:n_l + 3], refs[n_l + 3:]
        for l in range(n_l):
            @pl.when(pl.program_id(0) == l)
            def _(p_ref=p_refs[l]):
                res = _adamw_math([p_ref[s] for s in range(N_DEV)], w_ref[...], m_ref[...], v_ref[...])
                for ref, val in zip(outs, res):
                    ref[...] = val

    blk = pl.BlockSpec((None, br, c), lambda l, i: (l, i, 0))
    p_specs = [pl.BlockSpec((N_DEV, br, c), lambda l, i, k=k: (0, jnp.where(l == k, i, 0), 0)) for k in range(n_l)]
    return pl.pallas_call(
        body, name=name, grid=(n_l, r // br), in_specs=p_specs + [blk, blk, blk],
        out_specs=[blk] * 4, out_shape=[jax.ShapeDtypeStruct(w.shape, F32)] * 4,
        compiler_params=_cparams(("arbitrary", "arbitrary")),
    )(*parts, w, m, v)


def _adamw_small(name, items):
    n = len(items)

    def body(*refs):
        ins, outs = refs[:4 * n], refs[4 * n:]
        for i in range(n):
            p_ref, w_ref, m_ref, v_ref = ins[4 * i:4 * i + 4]
            res = _adamw_math([p_ref[s] for s in range(N_DEV)], w_ref[...], m_ref[...], v_ref[...])
            for ref, val in zip(outs[4 * i:4 * i + 4], res):
                ref[...] = val

    flat = [a for it in items for a in it]
    out_shape = [jax.ShapeDtypeStruct(it[1].shape, F32) for it in items for _ in range(4)]
    res = pl.pallas_call(
        body, name=name, out_shape=out_shape,
        in_specs=[pl.BlockSpec(memory_space=pltpu.VMEM)] * len(flat),
        out_specs=[pl.BlockSpec(memory_space=pltpu.VMEM)] * len(out_shape),
        compiler_params=pltpu.CompilerParams(vmem_limit_bytes=VMEM_LIMIT),
    )(*flat)
    return [res[4 * i:4 * i + 4] for i in range(n)]


SHARDED = [("ab_w_in", 2), ("ab_w_out", 1), ("cd_w_in", 2), ("cd_w_out", 1), ("ffn_w_up", 2), ("ffn_w_down", 1),
           ("ssd_conv_w", 2), ("rg_conv_w", 2), ("rg_conv_b", 1), ("rg_ba", 1), ("rg_bx", 1), ("rg_lambda", 1),
           ("ffn_conv_w", 2), ("ln_g", 2), ("ln_b", 2)]
MATMUL_W = ("ab_w_in", "ab_w_out", "cd_w_in", "cd_w_out", "ffn_w_up", "ffn_w_down")
REPLICATED = ["ssd_conv_b", "ssd_dt_bias", "ssd_a_log", "ssd_d", "ssd_norm_w", "hg_lower", "hg_norm_w", "swa_sinks",
              "rg_wa", "rg_wx", "ffn_conv_b"]
WEIGHTS = ["ab_w_in", "ssd_conv_w", "ssd_conv_b", "ssd_dt_bias", "ssd_a_log", "ssd_d", "ssd_norm_w", "hg_lower",
           "hg_norm_w", "ab_w_out", "cd_w_in", "swa_sinks", "rg_conv_w", "rg_conv_b", "rg_wa", "rg_ba", "rg_wx",
           "rg_bx", "rg_lambda", "cd_w_out", "ffn_w_up", "ffn_conv_w", "ffn_conv_b", "ffn_w_down", "ln_g", "ln_b"]


def _as2d(a):
    return a.reshape(-1, a.shape[-1])


def _merge_shards(g, axis):
    g = jnp.moveaxis(g, 0, axis)
    shp = g.shape
    return g.reshape(shp[:axis] + (shp[axis] * shp[axis + 1],) + shp[axis + 2:])


def _split_shards(full, axis):
    shp = full.shape
    g = full.reshape(shp[:axis] + (N_DEV, shp[axis] // N_DEV) + shp[axis + 1:])
    return jnp.moveaxis(g, axis, 0)


def _ab_pad(w):
    z, xbc, dt = w[..., 0:512], w[..., 512:1280], w[..., 1280:1288]
    hqfi, hg = w[..., 1288:2824], w[..., 2824:3336]
    return jnp.concatenate([hqfi, jnp.repeat(dt, 64, axis=-1), z, hg, xbc], axis=-1)


def _ab_unpad(d):
    lead = d.shape[:-1]
    dt = d[..., AB_DT:AB_ZG].reshape(lead + (8, 64)).sum(-1)
    z, hg, xbc = d[..., AB_ZG:AB_ZG + 512], d[..., AB_ZG + 512:AB_XBC], d[..., AB_XBC:AB_PAD]
    return jnp.concatenate([z, xbc, dt, d[..., :AB_DT], hg], axis=-1)


def _cd_pad(w):
    return jnp.concatenate([w[..., :768], jnp.zeros(w.shape[:-1] + (256,), w.dtype), w[..., 768:]], axis=-1)


def _cd_unpad(d):
    return jnp.concatenate([d[..., :768], d[..., CD_GATE:CD_PAD]], axis=-1)


def _cat_halves(d):
    return jnp.concatenate(d, axis=1)


def _block_diag(w):
    eye = jnp.eye(8, dtype=w.dtype)
    return jnp.einsum("gij,gh->gihj", w, eye).reshape(512, 512)


def _block_diag_grad(d):
    return jnp.stack([d[64 * g:64 * g + 64, 64 * g:64 * g + 64] for g in range(8)])


def _same(a):
    return a


BIG = {"ab_w_in": (1, _ab_pad, _ab_unpad), "ab_w_out": (0, _same, _same), "cd_w_in": (1, _cd_pad, _cd_unpad),
       "cd_w_out": (0, _same, _same), "ffn_w_up": (1, _same, _cat_halves), "ffn_w_down": (0, _same, _same)}


class _MeshComm:
    def __init__(self, shards):
        self.shards, self.full, self.recv, self.posted = shards, {}, {}, {}

    def post(self, carrier, req):
        self.posted.setdefault(carrier, []).append(req)

    def take(self, carrier):
        return self.posted.pop(carrier, [])

    def take_all(self):
        reqs = [r for name in list(self.posted) for r in self.posted.pop(name)]
        return reqs

    def gather_req(self, name, layer):
        axis, prep, _ = BIG[name]

        def done(got):
            self.full[name, layer] = prep(_merge_shards(got, axis))

        return ("gather", self.shards[name][layer].astype(MM_DTYPE), done)

    def weight(self, name, layer):
        return self.full[name, layer]

    def grad_req(self, name, layer, d):
        axis, _, unprep = BIG[name]

        def done(got):
            self.recv[name, layer] = got

        return ("exchange", _split_shards(unprep(d), axis).astype(MM_DTYPE), done)


class _LocalComm:
    def __init__(self, full):
        self.full_w, self.grads = full, {}

    def post(self, carrier, req):
        pass

    def take(self, carrier):
        return []

    def gather_req(self, name, layer):
        return None

    def weight(self, name, layer):
        return BIG[name][1](self.full_w[name][layer].astype(MM_DTYPE))

    def grad_req(self, name, layer, d):
        self.grads[name, layer] = BIG[name][2](d)
        return None


def _row_vec(v):
    return v.reshape(1, -1)


def _heads64(v):
    return jnp.repeat(v, 64).reshape(1, 512)


def _local_step(x, target, w, comm):
    kinds = ["ab" if layer % 2 == 0 else "cd" for layer in range(DEPTH)]
    in_name = [f"{kinds[layer]}_in{layer // 2}" for layer in range(DEPTH)]
    core_name = [("hg" if layer % 2 == 0 else "swa") + f"{layer // 2}_fwd" for layer in range(DEPTH)]
    comm.post("ssd0_fwd", comm.gather_req("ffn_w_down", 0))
    comm.post(core_name[0], comm.gather_req("ffn_w_up", 0))
    for layer in range(DEPTH - 1):
        nxt, nj = kinds[layer + 1], (layer + 1) // 2
        if kinds[layer] == "ab":
            comm.post(in_name[layer], comm.gather_req(nxt + "_w_in", nj))
            comm.post(core_name[layer], comm.gather_req(nxt + "_w_out", nj))
            comm.post(core_name[layer], comm.gather_req("ffn_w_down", layer + 1))
        else:
            comm.post(in_name[layer], comm.gather_req(nxt + "_w_out", nj))
            comm.post(core_name[layer], comm.gather_req(nxt + "_w_in", nj))
            comm.post(f"ffn_up_g{layer}", comm.gather_req("ffn_w_down", layer + 1))
        comm.post(f"ffn_act{layer}_fwd", comm.gather_req("ffn_w_up", layer + 1))

    t = x.shape[0]
    bt = _pick(t, (512, 256, 128))
    nb = t // bt
    bs = _pick(t, (256, 128))

    def rowop(name, fn, params, xs, widths_out, out_dtype=F32, dx_dtypes=None):
        outs = [((t, wd), (bt, wd), lambda g, c: (c, 0), out_dtype) for wd in widths_out]
        return _Op(name, fn, (1, nb), params, xs, outs, dx_dtypes=dx_dtypes)

    def rowblk(arr, width, first=0):
        return (arr, (bt, width), lambda g, c: (c, first))

    one_row = lambda g, c: (0, 0)
    lb_op = _Op("hg_lb", _lb_fn, (1, 1), [],
                [(w["hg_lower"][0:1], (1, 512), one_row), (w["hg_lower"][1:2], (1, 512), one_row)],
                [((1, 512), (1, 512), one_row, F32)] * 2)
    lb_all = lb_op.ys

    tape = []
    grads = {}

    def add_grad(name, idx, val):
        grads.setdefault(name, {})[idx] = val

    def dw_matmul(name, a, b, wname, idx, carrier):
        d = _matmul(name, a, b, "tn", out_dtype=MM_DTYPE, comm=comm)
        comm.post(carrier, comm.grad_req(wname, idx, d))

    for layer in range(DEPTH):
        j = layer // 2
        rec = {"x_in": x}
        if layer % 2 == 0:
            h = _matmul(f"ab_in{j}", x, comm.weight("ab_w_in", j), "nn", comm=comm)
            conv_p = [_row_vec(w["ssd_conv_w"][j, k]) for k in range(4)] + [_row_vec(w["ssd_conv_b"][j])]
            conv = _Op(f"ssd_conv{j}", _make_conv_fn(4, True), (3, nb),
                       [(a, (1, 256), lambda g, c: (0, g)) for a in conv_p],
                       [(h, (bt, 256), lambda g, c: (c, AB_XBC // 256 + g))],
                       [((t, 768), (bt, 256), lambda g, c: (c, g), F32)], [(8, 256)], dx_dtypes=[MM_DTYPE])
            ssd = _Op(f"ssd{j}", _ssd_fn, (1, t // SSD_CHUNK),
                      [_whole(_heads64(w["ssd_dt_bias"][j])), _whole(_heads64(w["ssd_a_log"][j])),
                       _whole(_heads64(w["ssd_d"][j]))],
                      [(conv.ys[0], (SSD_CHUNK, 768), lambda g, c: (c, 0)),
                       (h, (SSD_CHUNK, 512), lambda g, c: (c, AB_DT // 512))],
                      [((t, 512), (SSD_CHUNK, 512), lambda g, c: (c, 0), F32)], [(LANES, LANES)] * 4,
                      dx_dtypes=[F32, MM_DTYPE], comm=comm)
            hg = _Op(f"hg{j}", _hg_fn, (1, t // HG_STEP), [_whole(lb_all[j])],
                     [(h, (HG_STEP, AB_DT), lambda g, c: (c, 0))],
                     [((t, 512), (HG_STEP, 512), lambda g, c: (c, 0), F32)], [(LANES, LANES)] * 4,
                     dx_dtypes=[MM_DTYPE], comm=comm)
            post = rowop(f"ab_post{j}", _ab_post_fn,
                         [_whole(_row_vec(w["ssd_norm_w"][j])), _whole(jnp.tile(_row_vec(w["hg_norm_w"][j]), (1, 4)))],
                         [rowblk(ssd.ys[0], 512), rowblk(hg.ys[0], 512), rowblk(h, 1024, AB_ZG // 1024)], [1024],
                         out_dtype=MM_DTYPE, dx_dtypes=[F32, F32, MM_DTYPE])
            rec.update(kind="ab", conv=conv, ssd=ssd, hg=hg, post=post)
        else:
            h = _matmul(f"cd_in{j}", x, comm.weight("cd_w_in", j), "nn", comm=comm)
            swa = _Op(f"swa{j}", _swa_fn, (1, t // SWA_BLOCK),
                      [_whole(jnp.tile(w["swa_sinks"][j].reshape(8, 1), (1, LANES)))],
                      [(h, (SWA_BLOCK, 1024), lambda g, c: (c, 0))],
                      [((t, 512), (SWA_BLOCK, 512), lambda g, c: (c, 0), F32)], [(SWA_BLOCK, LANES)] * 2,
                      dx_dtypes=[MM_DTYPE], comm=comm)
            conv_p = [_row_vec(w["rg_conv_w"][j, k]) for k in range(4)] + [_row_vec(w["rg_conv_b"][j])]
            conv = _Op(f"rg_conv{j}", _make_conv_fn(4, False), (2, nb),
                       [(a, (1, 256), lambda g, c: (0, g)) for a in conv_p],
                       [(h, (bt, 256), lambda g, c: (c, CD_XR // 256 + g))],
                       [((t, 512), (bt, 256), lambda g, c: (c, g), F32)], [(8, 256)], dx_dtypes=[MM_DTYPE])
            gate = rowop(f"rg_gate{j}", _rg_gate_fn,
                         [_whole(_block_diag(w["rg_wa"][j])), _whole(_row_vec(w["rg_ba"][j])),
                          _whole(_block_diag(w["rg_wx"][j])), _whole(_row_vec(w["rg_bx"][j])),
                          _whole(_row_vec(w["rg_lambda"][j]))],
                         [rowblk(conv.ys[0], 512)], [512, 512])
            scan = _Op(f"rg_scan{j}", _rg_scan_fn, (2, t // bs), [],
                       [(gate.ys[0], (bs, 256), lambda g, c: (c, g)), (gate.ys[1], (bs, 256), lambda g, c: (c, g))],
                       [((t, 512), (bs, 256), lambda g, c: (c, g), F32)], [(8, 256)])
            post = rowop(f"cd_post{j}", _cd_post_fn, [],
                         [rowblk(swa.ys[0], 512), rowblk(scan.ys[0], 512), rowblk(h, 512, CD_GATE // 512)], [1024],
                         out_dtype=MM_DTYPE, dx_dtypes=[F32, F32, MM_DTYPE])
            rec.update(kind="cd", swa=swa, conv=conv, gate=gate, scan=scan, post=post)
        kind = rec["kind"]
        ycat = post.ys[0]
        m = _matmul(f"mix_out{layer}", ycat, comm.weight(kind + "_w_out", j), "nn", comm=comm)
        ln1 = rowop(f"ln_a{layer}", _ln_res_fn,
                    [_whole(_row_vec(w["ln_g"][layer, 0])), _whole(_row_vec(w["ln_b"][layer, 0]))],
                    [rowblk(x, 1024), rowblk(m, 1024)], [1024], dx_dtypes=[F32, MM_DTYPE])
        x1 = ln1.ys[0]
        w_up = comm.weight("ffn_w_up", layer)
        hg_ = _matmul(f"ffn_up_g{layer}", x1, w_up, "nn", b_cols=(0, FFN_DIM), comm=comm)
        hu_ = _matmul(f"ffn_up_u{layer}", x1, w_up, "nn", b_cols=(FFN_DIM, FFN_DIM), comm=comm)
        n_fb = FFN_DIM // FFN_BLK
        taps = [_row_vec(w["ffn_conv_w"][layer, k]) for k in range(3)] + [_row_vec(w["ffn_conv_b"][layer])]
        act = _Op(f"ffn_act{layer}", _ffn_act_fn, (n_fb, nb),
                  [(a, (1, FFN_BLK), lambda g, c: (0, g)) for a in taps]
                  + [(a, (1, FFN_BLK), lambda g, c: (0, n_fb + g)) for a in taps],
                  [(hg_, (bt, FFN_BLK), lambda g, c: (c, g)), (hu_, (bt, FFN_BLK), lambda g, c: (c, g))],
                  [((t, FFN_DIM), (bt, FFN_BLK), lambda g, c: (c, g), MM_DTYPE)], [(8, FFN_BLK)] * 2,
                  dx_dtypes=[MM_DTYPE, MM_DTYPE], comm=comm, bwd_fn=_ffn_act_bwd)
        a = act.ys[0]
        f = _matmul(f"ffn_down{layer}", a, comm.weight("ffn_w_down", layer), "nn", comm=comm)
        ln2 = rowop(f"ln_f{layer}", _ln_res_fn,
                    [_whole(_row_vec(w["ln_g"][layer, 1])), _whole(_row_vec(w["ln_b"][layer, 1]))],
                    [rowblk(x1, 1024), rowblk(f, 1024)], [1024], dx_dtypes=[F32, MM_DTYPE])
        rec.update(ycat=ycat, ln1=ln1, x1=x1, act=act, a=a, ln2=ln2)
        tape.append(rec)
        x = ln2.ys[0]

    dx, loss = _loss_kernel(x, target)

    d_lb = [jnp.zeros((1, 512), F32), jnp.zeros((1, 512), F32)]
    for layer in reversed(range(DEPTH)):
        j = layer // 2
        rec = tape[layer]
        (dg, db), (dx1_res, df) = rec["ln2"].bwd([dx])
        add_grad("ln_g", (layer, 1), dg[0]); add_grad("ln_b", (layer, 1), db[0])
        dw_matmul(f"ffn_down_dw{layer}", rec["a"], df, "ffn_w_down", layer, f"ffn_act{layer}_bwd")
        da = _matmul(f"ffn_down_dx{layer}", df, comm.weight("ffn_w_down", layer), "nt", comm=comm)
        dpa, (dhg, dhu) = rec["act"].bwd([da])
        halves = [jnp.concatenate([dpa[k][0, :FFN_DIM], dpa[4 + k][0, FFN_DIM:]]) for k in range(4)]
        add_grad("ffn_conv_w", layer, jnp.stack(halves[:3]))
        add_grad("ffn_conv_b", layer, halves[3])
        core_bwd = ("hg" if rec["kind"] == "ab" else "swa") + f"{j}_bwd"
        dw_up = (_matmul(f"ffn_up_dw_g{layer}", rec["x1"], dhg, "tn", out_dtype=MM_DTYPE, comm=comm),
                 _matmul(f"ffn_up_dw_u{layer}", rec["x1"], dhu, "tn", out_dtype=MM_DTYPE, comm=comm))
        comm.post(core_bwd, comm.grad_req("ffn_w_up", layer, dw_up))
        w_up = comm.weight("ffn_w_up", layer)
        dx1 = _matmul(f"ffn_up_dx_g{layer}", dhg, w_up, "nt", b_cols=(0, FFN_DIM), add=dx1_res, comm=comm)
        dx1 = _matmul(f"ffn_up_dx_u{layer}", dhu, w_up, "nt", b_cols=(FFN_DIM, FFN_DIM), add=dx1, comm=comm)
        (dg, db), (dx_res, dm) = rec["ln1"].bwd([dx1])
        add_grad("ln_g", (layer, 0), dg[0]); add_grad("ln_b", (layer, 0), db[0])
        kind = rec["kind"]
        dw_matmul(f"mix_out_dw{layer}", rec["ycat"], dm, kind + "_w_out", j, f"{kind}_in_dw{j}")
        dycat = _matmul(f"mix_out_dx{layer}", dm, comm.weight(kind + "_w_out", j), "nt", comm=comm)
        if kind == "ab":
            (dnw_s, dnw_h), (dy_ssd, do_hg, dh) = rec["post"].bwd([dycat])
            add_grad("ssd_norm_w", j, dnw_s[0]); add_grad("hg_norm_w", j, dnw_h[0].reshape(4, LANES).sum(0))
            (dlb,), (dh,) = rec["hg"].bwd([do_hg], dx_into={0: dh})
            d_lb[j] = dlb
            (ddtb, dalog, ddsk), (dxbc_c, dh) = rec["ssd"].bwd([dy_ssd], dx_into={1: dh})
            add_grad("ssd_dt_bias", j, ddtb[0].reshape(8, 64).sum(-1))
            add_grad("ssd_a_log", j, dalog[0].reshape(8, 64).sum(-1))
            add_grad("ssd_d", j, ddsk[0].reshape(8, 64).sum(-1))
            dcp, (dh,) = rec["conv"].bwd([dxbc_c], dx_into={0: dh})
            add_grad("ssd_conv_w", j, jnp.stack([dcp[k][0] for k in range(4)]))
            add_grad("ssd_conv_b", j, dcp[4][0])
        else:
            _, (dyc, dhs, dh) = rec["post"].bwd([dycat])
            _, (da_s, du_s) = rec["scan"].bwd([dhs])
            (dwa, dba, dwx, dbx, dlam), (dxc,) = rec["gate"].bwd([da_s, du_s])
            add_grad("rg_wa", j, _block_diag_grad(dwa)); add_grad("rg_wx", j, _block_diag_grad(dwx))
            add_grad("rg_ba", j, dba[0]); add_grad("rg_bx", j, dbx[0]); add_grad("rg_lambda", j, dlam[0])
            dcp, (dh,) = rec["conv"].bwd([dxc], dx_into={0: dh})
            add_grad("rg_conv_w", j, jnp.stack([dcp[k][0] for k in range(4)]))
            add_grad("rg_conv_b", j, dcp[4][0])
            (dsink,), (dh,) = rec["swa"].bwd([dyc], dx_into={0: dh})
            add_grad("swa_sinks", j, dsink.sum(-1))
        dw_matmul(f"{kind}_in_dw{j}", rec["x_in"], dh, kind + "_w_in", j,
                  f"ffn_act{layer - 1}_bwd" if layer > 0 else f"{kind}_in_dx{j}")
        dx = _matmul(f"{kind}_in_dx{j}", dh, comm.weight(kind + "_w_in", j), "nt", add=dx_res, comm=comm)

    _, (dl0, dl1) = lb_op.bwd(d_lb)
    out = {"hg_lower": jnp.concatenate([dl0, dl1], axis=0)}
    for name, parts in grads.items():
        keys = sorted(parts)
        if isinstance(keys[0], tuple):
            out[name] = jnp.stack([jnp.stack([parts[(l, s)] for s in range(2)]) for l in range(DEPTH)])
        else:
            out[name] = jnp.stack([parts[k] for k in keys])
    return loss, dx, out


def _local_step_full(x, target, full):
    comm = _LocalComm(full)
    loss, dx, grads = _local_step(x, target, {n: a for n, a in full.items() if n not in BIG}, comm)
    for name in BIG:
        grads[name] = jnp.stack([comm.grads[name, l] for l in range(full[name].shape[0])])
    return loss, dx, grads


def kernel(x, ab_w_in, ssd_conv_w, ssd_conv_b, ssd_dt_bias, ssd_a_log, ssd_d, ssd_norm_w, hg_lower, hg_norm_w, ab_w_out, cd_w_in, swa_sinks, rg_conv_w, rg_conv_b, rg_wa, rg_ba, rg_wx, rg_bx, rg_lambda, cd_w_out, ffn_w_up, ffn_conv_w, ffn_conv_b, ffn_w_down, ln_g, ln_b, loss_target, m_ab_w_in, m_ssd_conv_w, m_ssd_conv_b, m_ssd_dt_bias, m_ssd_a_log, m_ssd_d, m_ssd_norm_w, m_hg_lower, m_hg_norm_w, m_ab_w_out, m_cd_w_in, m_swa_sinks, m_rg_conv_w, m_rg_conv_b, m_rg_wa, m_rg_ba, m_rg_wx, m_rg_bx, m_rg_lambda, m_cd_w_out, m_ffn_w_up, m_ffn_conv_w, m_ffn_conv_b, m_ffn_w_down, m_ln_g, m_ln_b, v_ab_w_in, v_ssd_conv_w, v_ssd_conv_b, v_ssd_dt_bias, v_ssd_a_log, v_ssd_d, v_ssd_norm_w, v_hg_lower, v_hg_norm_w, v_ab_w_out, v_cd_w_in, v_swa_sinks, v_rg_conv_w, v_rg_conv_b, v_rg_wa, v_rg_ba, v_rg_wx, v_rg_bx, v_rg_lambda, v_cd_w_out, v_ffn_w_up, v_ffn_conv_w, v_ffn_conv_b, v_ffn_w_down, v_ln_g, v_ln_b):
    args = dict(locals())
    wts = {n: args[n] for n in WEIGHTS}
    mom = {n: args["m_" + n] for n in WEIGHTS}
    var = {n: args["v_" + n] for n in WEIGHTS}
    axis = dict(SHARDED)
    small = [n for n, _ in SHARDED if n not in BIG]
    comm = _MeshComm(wts)

    def run(name, reqs):
        for (_, _, done), got in zip(reqs, _remote_copies(name, [(k, a) for k, a, _ in reqs])):
            done(got)

    full = {n: wts[n] for n in REPLICATED}

    def keep_small(n):
        def done(got):
            full[n] = _merge_shards(got.reshape((N_DEV,) + wts[n].shape), axis[n])
        return ("gather", _as2d(wts[n]), done)

    run("gather_first", [comm.gather_req("ab_w_in", 0), comm.gather_req("ab_w_out", 0)] + [keep_small(n) for n in small])

    loss, grad_x, grads = _local_step(x[0], loss_target[0], full, comm)
    loss = lax.psum(loss, ("x", "y", "c"))

    parts = {}

    def keep_parts(n, kind, arr):
        return (kind, arr, lambda got: parts.__setitem__(n, got))

    last = comm.take_all()
    last += [keep_parts(n, "exchange", _split_shards(grads[n], axis[n]).reshape((N_DEV,) + _as2d(wts[n]).shape))
             for n in small]
    last += [keep_parts(n, "gather", _as2d(grads[n])) for n in REPLICATED]
    run("exchange_last", last)

    new = {}
    for n in BIG:
        new[n] = _adamw_big("adamw_" + n, [comm.recv[n, l] for l in range(wts[n].shape[0])], wts[n], mom[n], var[n])
    names = small + REPLICATED
    res = _adamw_small("adamw_small", [(parts[n], _as2d(wts[n]), _as2d(mom[n]), _as2d(var[n])) for n in names])
    for n, r in zip(names, res):
        new[n] = [a.reshape(wts[n].shape) for a in r]

    outs = [loss, grad_x[None]]
    for kind in range(4):
        outs += [new[n][kind] for n in WEIGHTS]
    return tuple(outs)
```

```python
import math

import numpy as np
import jax
import jax.numpy as jnp
from jax import lax
from jax.experimental import pallas as pl
from jax.experimental.pallas import tpu as pltpu

F32 = jnp.float32
BF16 = jnp.bfloat16
MM_DTYPE = BF16

DEPTH = 4
N_DEV = 8
LN_EPS = 1e-5
RMS_EPS = 1e-6
MASK_VALUE = -1e9
ALPHA = (2 * DEPTH) ** 0.25
RG_C = 8.0
FFN_DIM = 2816
SSD_CHUNK = 128
HG_STEP = 128
SWA_BLOCK = 128
LANES = 128
VMEM_LIMIT = 56 * 1024 * 1024

ADAM_LR, ADAM_B1, ADAM_B2, ADAM_EPS, ADAM_WD, ADAM_STEP = 0.001, 0.9, 0.999, 1e-08, 0.01, 10

AB_HEADS, AB_DT, AB_ZG, AB_XBC, AB_PAD = 0, 1536, 2048, 3072, 3840
CD_QKV, CD_GATE, CD_XR, CD_PAD = 0, 1024, 1536, 2048
FFN_BLK = 256
FFN_STRIP = 32


def _cols(x, lo, hi):
    n = x.shape[1]

    @jax.custom_vjp
    def f(x):
        return x[:, lo:hi]

    def bwd(_, g):
        parts = []
        if lo > 0:
            parts.append(jnp.zeros((g.shape[0], lo), g.dtype))
        parts.append(g)
        if hi < n:
            parts.append(jnp.zeros((g.shape[0], n - hi), g.dtype))
        return (jnp.concatenate(parts, axis=1) if len(parts) > 1 else g,)

    f.defvjp(lambda x: (f(x), None), bwd)
    return f(x)


def _rows(x, lo, hi):
    n = x.shape[0]

    @jax.custom_vjp
    def f(x):
        return x[lo:hi, :]

    def bwd(_, g):
        parts = []
        if lo > 0:
            parts.append(jnp.zeros((lo, g.shape[1]), g.dtype))
        parts.append(g)
        if hi < n:
            parts.append(jnp.zeros((n - hi, g.shape[1]), g.dtype))
        return (jnp.concatenate(parts, axis=0) if len(parts) > 1 else g,)

    f.defvjp(lambda x: (f(x), None), bwd)
    return f(x)


def _roll(x, shift, axis):
    n = x.shape[axis]
    shift = shift % n
    if shift == 0:
        return x

    @jax.custom_vjp
    def f(x):
        return pltpu.roll(x, shift, axis)

    f.defvjp(lambda x: (f(x), None), lambda _, g: (pltpu.roll(g, n - shift, axis),))
    return f(x)


def _dot(a, b, precision=None):
    return lax.dot_general(a, b, (((1,), (0,)), ((), ())), precision=precision, preferred_element_type=F32)


def _dot_nt(a, b, precision=None):
    return lax.dot_general(a, b, (((1,), (1,)), ((), ())), precision=precision, preferred_element_type=F32)


def _dot_tn(a, b, precision=None):
    return lax.dot_general(a, b, (((0,), (0,)), ((), ())), precision=precision, preferred_element_type=F32)


def _split3(x):
    hi = x.astype(BF16)
    r = x - hi.astype(F32)
    mid = r.astype(BF16)
    return hi, mid, (r - mid.astype(F32)).astype(BF16)


def _sel_dot(sel, x):
    def run(mat, v, dims):
        n = v.shape[1]
        y = lax.dot_general(mat, jnp.concatenate(_split3(v), axis=1), dims, preferred_element_type=F32)
        return y[:, :n] + y[:, n:2 * n] + y[:, 2 * n:]

    @jax.custom_vjp
    def f(sel, x):
        return run(sel, x, (((1,), (0,)), ((), ())))

    def bwd(sel, g):
        return jnp.zeros_like(sel), run(sel, g, (((0,), (0,)), ((), ())))

    f.defvjp(lambda sel, x: (f(sel, x), sel), bwd)
    return f(sel, x)


def _dot_sel(x, sel):
    def run(v, mat, dims):
        m = v.shape[0]
        y = lax.dot_general(jnp.concatenate(_split3(v), axis=0), mat, dims, preferred_element_type=F32)
        return y[:m] + y[m:2 * m] + y[2 * m:]

    @jax.custom_vjp
    def f(x, sel):
        return run(x, sel, (((1,), (0,)), ((), ())))

    def bwd(sel, g):
        return run(g, sel, (((1,), (1,)), ((), ()))), jnp.zeros_like(sel)

    f.defvjp(lambda x, sel: (f(x, sel), sel), bwd)
    return f(x, sel)


def _sigmoid(x):
    return 0.5 * jnp.tanh(0.5 * x) + 0.5


def _silu(x):
    return x * _sigmoid(x)


def _softplus(x):
    return jnp.maximum(x, 0.0) + jnp.log(1.0 + jnp.exp(-jnp.abs(x)))


def _gelu_tanh(x):
    c = math.sqrt(2.0 / math.pi)
    return 0.5 * x * (1.0 + jnp.tanh(c * (x + 0.044715 * (x * x * x))))


def _iota(shape, axis):
    return lax.broadcasted_iota(jnp.int32, shape, axis)


def _lane_mask(lo, hi, width=LANES):
    lane = _iota((1, width), 1)
    return ((lane >= lo) & (lane < hi)).astype(F32)


def _mesh_pos():
    return lax.axis_index("x"), lax.axis_index("y"), lax.axis_index("c")


def _carry_shapes(carry):
    return [jax.ShapeDtypeStruct((N_DEV,) + a.shape if kind == "gather" else a.shape, a.dtype) for kind, a in carry]


def _carry_scratch(carry):
    n = len(carry)
    if n == 0:
        return []
    return [pltpu.SemaphoreType.DMA((n, N_DEV - 1)), pltpu.SemaphoreType.DMA((n, N_DEV - 1)),
            pltpu.SemaphoreType.DMA((n,))]


def _carry_run(start, kinds, in_refs, out_refs, send_sems, recv_sems, local_sems):
    x, y, cc = _mesh_pos()
    me = 4 * x + 2 * y + cc
    for i, kind in enumerate(kinds):
        mine = in_refs[i] if kind == "gather" else in_refs[i].at[me]
        local = pltpu.make_async_copy(mine, out_refs[i].at[me], local_sems.at[i])
        remote = []
        for k in range(1, N_DEV):
            px, py, pc = x ^ ((k >> 2) & 1), y ^ ((k >> 1) & 1), cc ^ (k & 1)
            src = in_refs[i] if kind == "gather" else in_refs[i].at[4 * px + 2 * py + pc]
            remote.append(pltpu.make_async_remote_copy(
                src_ref=src, dst_ref=out_refs[i].at[me],
                send_sem=send_sems.at[i, k - 1], recv_sem=recv_sems.at[i, k - 1],
                device_id=(px, py, pc), device_id_type=pl.DeviceIdType.MESH))
        if start:
            local.start()
            for cp in remote:
                cp.start()
        else:
            for cp in remote:
                cp.wait_recv()
            for cp in remote:
                cp.wait_send()
            local.wait()


def _remote_copies(name, carry):
    n = len(carry)
    kinds = [k for k, _ in carry]

    def body(*refs):
        sems = refs[2 * n:]
        _carry_run(True, kinds, refs[:n], refs[n:2 * n], *sems)
        _carry_run(False, kinds, refs[:n], refs[n:2 * n], *sems)

    return pl.pallas_call(
        body, name=name, out_shape=_carry_shapes(carry),
        in_specs=[pl.BlockSpec(memory_space=pl.ANY)] * n, out_specs=[pl.BlockSpec(memory_space=pl.ANY)] * n,
        scratch_shapes=_carry_scratch(carry),
    )(*[a for _, a in carry])


def _cparams(sem):
    return pltpu.CompilerParams(dimension_semantics=sem, vmem_limit_bytes=VMEM_LIMIT)


def _chunk_fwd(name, fn, grid, params, xs, outs, state_shapes, carry=()):
    n_g, n_c = grid
    n_p, n_x, n_o, n_s, n_r = len(params), len(xs), len(outs), len(state_shapes), len(carry)
    kinds = [k for k, _ in carry]

    def body(*refs):
        i = 0
        p_refs = refs[i:i + n_p]; i += n_p
        x_refs = refs[i:i + n_x]; i += n_x
        ci_refs = refs[i:i + n_r]; i += n_r
        o_refs = refs[i:i + n_o]; i += n_o
        sv_refs = refs[i:i + n_s]; i += n_s
        co_refs = refs[i:i + n_r]; i += n_r
        st_refs = refs[i:i + n_s]; i += n_s
        sems = refs[i:]
        g, c = pl.program_id(0), pl.program_id(1)

        if n_r:
            @pl.when((g == 0) & (c == 0))
            def _():
                _carry_run(True, kinds, ci_refs, co_refs, *sems)

        @pl.when(c == 0)
        def _():
            for s in st_refs:
                s[...] = jnp.zeros(s.shape, s.dtype)

        st = [s[...] for s in st_refs]
        ys, new_st = fn(c, [p[...] for p in p_refs], [x[...].astype(F32) for x in x_refs], st)
        for o, y in zip(o_refs, ys):
            o[...] = y.astype(o.dtype)
        for sv, s in zip(sv_refs, st):
            sv[0, 0] = s
        for s_ref, s in zip(st_refs, new_st):
            s_ref[...] = s

        if n_r:
            @pl.when((g == n_g - 1) & (c == n_c - 1))
            def _():
                _carry_run(False, kinds, ci_refs, co_refs, *sems)

    any_spec = pl.BlockSpec(memory_space=pl.ANY)
    in_specs = [pl.BlockSpec(b, m) for _, b, m in params] + [pl.BlockSpec(b, m) for _, b, m in xs] + [any_spec] * n_r
    out_specs = [pl.BlockSpec(b, m) for _, b, m, _ in outs]
    out_shape = [jax.ShapeDtypeStruct(s, d) for s, _, _, d in outs]
    for shp in state_shapes:
        out_specs.append(pl.BlockSpec((1, 1) + shp, lambda g, c, n=len(shp): (g, c) + (0,) * n))
        out_shape.append(jax.ShapeDtypeStruct((n_g, n_c) + shp, F32))
    out_specs += [any_spec] * n_r
    out_shape += _carry_shapes(carry)
    res = pl.pallas_call(
        body, name=name, grid=grid, in_specs=in_specs, out_specs=out_specs, out_shape=out_shape,
        scratch_shapes=[pltpu.VMEM(shp, F32) for shp in state_shapes] + _carry_scratch(carry),
        compiler_params=_cparams(("arbitrary", "arbitrary")),
    )(*[a for a, _, _ in params], *[a for a, _, _ in xs], *[a for _, a in carry])
    return list(res[:n_o]), list(res[n_o:n_o + n_s]), list(res[n_o + n_s:])


def _chunk_bwd(name, fn, grid, params, xs, saved, dys, state_shapes, dx_dtypes, dx_into, carry=(), bwd_fn=None):
    n_g, n_c = grid
    n_p, n_x, n_s, n_y, n_r = len(params), len(xs), len(state_shapes), len(dys), len(carry)
    kinds = [k for k, _ in carry]
    into = sorted(dx_into)
    n_a = len(into)

    def rev(m):
        return lambda g, c: m(g, n_c - 1 - c)

    def body(*refs):
        i = 0
        p_refs = refs[i:i + n_p]; i += n_p
        x_refs = refs[i:i + n_x]; i += n_x
        sv_refs = refs[i:i + n_s]; i += n_s
        dy_refs = refs[i:i + n_y]; i += n_y
        i += n_a
        ci_refs = refs[i:i + n_r]; i += n_r
        dp_refs = refs[i:i + n_p]; i += n_p
        dx_refs = refs[i:i + n_x]; i += n_x
        co_refs = refs[i:i + n_r]; i += n_r
        ds_refs = refs[i:i + n_s]; i += n_s
        sems = refs[i:]
        g, c = pl.program_id(0), pl.program_id(1)
        chunk = n_c - 1 - c

        if n_r:
            @pl.when((g == 0) & (c == 0))
            def _():
                _carry_run(True, kinds, ci_refs, co_refs, *sems)

        @pl.when(c == 0)
        def _():
            for s in ds_refs:
                s[...] = jnp.zeros(s.shape, s.dtype)
            for d in dp_refs:
                d[...] = jnp.zeros(d.shape, d.dtype)

        pv = [p[...] for p in p_refs]
        xv = [x[...].astype(F32) for x in x_refs]
        sv = [s[0, 0] for s in sv_refs]
        dyv, dsv = [d[...].astype(F32) for d in dy_refs], [s[...] for s in ds_refs]
        if bwd_fn is None:
            _, vjp = jax.vjp(lambda p, x, s: fn(chunk, p, x, s), pv, xv, sv)
            dp, dx, ds = vjp((dyv, dsv))
        else:
            dp, dx, ds = bwd_fn(chunk, pv, xv, sv, dyv, dsv)
        for r, v in zip(dp_refs, dp):
            r[...] += v
        for r, v in zip(dx_refs, dx):
            r[...] = v.astype(r.dtype)
        for r, v in zip(ds_refs, ds):
            r[...] = v

        if n_r:
            @pl.when((g == n_g - 1) & (c == n_c - 1))
            def _():
                _carry_run(False, kinds, ci_refs, co_refs, *sems)

    any_spec = pl.BlockSpec(memory_space=pl.ANY)
    in_specs = [pl.BlockSpec(b, rev(m)) for _, b, m in params] + [pl.BlockSpec(b, rev(m)) for _, b, m in xs]
    for shp in state_shapes:
        in_specs.append(pl.BlockSpec((1, 1) + shp, lambda g, c, n=len(shp): (g, n_c - 1 - c) + (0,) * n))
    in_specs += [pl.BlockSpec(b, rev(m)) for _, b, m in dys]
    in_specs += [any_spec] * (n_a + n_r)
    out_specs = [pl.BlockSpec(b, rev(m)) for _, b, m in params] + [pl.BlockSpec(b, rev(m)) for _, b, m in xs]
    out_specs += [any_spec] * n_r
    out_shape = [jax.ShapeDtypeStruct(a.shape, F32) for a, _, _ in params]
    out_shape += [jax.ShapeDtypeStruct(a.shape, d) for (a, _, _), d in zip(xs, dx_dtypes)]
    out_shape += _carry_shapes(carry)
    first_alias = n_p + n_x + n_s + n_y
    aliases = {first_alias + k: n_p + xi for k, xi in enumerate(into)}
    res = pl.pallas_call(
        body, name=name, grid=grid, in_specs=in_specs, out_specs=out_specs, out_shape=out_shape,
        scratch_shapes=[pltpu.VMEM(shp, F32) for shp in state_shapes] + _carry_scratch(carry),
        input_output_aliases=aliases,
        compiler_params=_cparams(("arbitrary", "arbitrary")),
    )(*[a for a, _, _ in params], *[a for a, _, _ in xs], *saved, *[a for a, _, _ in dys],
      *[dx_into[xi] for xi in into], *[a for _, a in carry])
    return list(res[:n_p]), list(res[n_p:n_p + n_x]), list(res[n_p + n_x:])


class _Op:
    def __init__(self, name, fn, grid, params, xs, outs, state_shapes=(), dx_dtypes=None, comm=None, bwd_fn=None):
        self.name, self.fn, self.grid, self.comm, self.bwd_fn = name, fn, grid, comm, bwd_fn
        self.params, self.xs, self.outs, self.state_shapes = params, xs, outs, list(state_shapes)
        self.dx_dtypes = dx_dtypes or [F32] * len(xs)
        reqs = comm.take(name + "_fwd") if comm is not None else []
        self.ys, self.saved, got = _chunk_fwd(name + "_fwd", fn, grid, params, xs, outs, self.state_shapes,
                                              carry=[(k, a) for k, a, _ in reqs])
        for (_, _, done), g in zip(reqs, got):
            done(g)

    def bwd(self, dys, dx_into=None):
        dy_defs = [(d, b, m) for d, (_, b, m, _) in zip(dys, self.outs)]
        reqs = self.comm.take(self.name + "_bwd") if self.comm is not None else []
        dps, dxs, got = _chunk_bwd(self.name + "_bwd", self.fn, self.grid, self.params, self.xs, self.saved, dy_defs,
                                   self.state_shapes, self.dx_dtypes, dx_into or {},
                                   carry=[(k, a) for k, a, _ in reqs], bwd_fn=self.bwd_fn)
        for (_, _, done), g in zip(reqs, got):
            done(g)
        return dps, dxs


def _whole(a):
    nd = a.ndim
    return (a, a.shape, lambda g, c: (0,) * nd)


def _pick(n, prefs):
    for p in prefs:
        if n % p == 0:
            return p
    return n


def _mm_blocks(mode, m, n, k):
    bn = _pick(n, (1408, 1280, 1024, 768, 512, 256, 128))
    if mode == "tn":
        return _pick(m, (1408, 1024, 768, 512, 256, 128)), bn, _pick(k, (2048, 1024, 512, 256, 128))
    bk = k if k <= 3840 else _pick(k, (2816, 1920, 1408, 1024, 512, 256, 128))
    return _pick(m, (1024, 512, 256, 128)), bn, bk


def _matmul(name, a, b, mode, *, add=None, out_dtype=F32, comm=None, a_cols=None, b_cols=None):
    a0, asize = a_cols if a_cols is not None else (0, a.shape[1])
    c0, csize = b_cols if b_cols is not None else (0, b.shape[1])
    if mode == "nn":
        (m, k), n = (a.shape[0], asize), csize
    elif mode == "nt":
        (m, k), n = (a.shape[0], asize), b.shape[0]
        assert k == csize
    else:
        (k, m), n = (a.shape[0], asize), csize
    bm, bn, bk = _mm_blocks(mode, m, n, k)
    assert c0 % (bk if mode == "nt" else bn) == 0 and a0 % (bm if mode == "tn" else bk) == 0
    j0, k0 = (0, c0 // bk) if mode == "nt" else (c0 // bn, 0)
    ia = a0 // (bm if mode == "tn" else bk)
    n_i, n_j, n_k = m // bm, n // bn, k // bk
    dims = {"nn": (((1,), (0,)), ((), ())), "nt": (((1,), (1,)), ((), ())), "tn": (((0,), (0,)), ((), ()))}[mode]
    has_add = add is not None
    reqs = comm.take(name) if comm is not None else []
    carry = [(kind, arr) for kind, arr, _ in reqs]
    kinds = [kind for kind, _ in carry]
    n_r = len(carry)

    def body(*refs):
        i = 2
        a_ref, b_ref = refs[0], refs[1]
        c_ref = refs[i] if has_add else None
        i += has_add
        ci_refs = refs[i:i + n_r]; i += n_r
        o_ref = refs[i]; i += 1
        co_refs = refs[i:i + n_r]; i += n_r
        acc = refs[i]; i += 1
        sems = refs[i:]
        ii, jj, kk = pl.program_id(0), pl.program_id(1), pl.program_id(2)

        if n_r:
            @pl.when((ii == 0) & (jj == 0) & (kk == 0))
            def _():
                _carry_run(True, kinds, ci_refs, co_refs, *sems)

        part = lax.dot_general(a_ref[...].astype(MM_DTYPE), b_ref[...].astype(MM_DTYPE), dims,
                               preferred_element_type=F32)

        def finish(r):
            if has_add:
                r = r + c_ref[...]
            o_ref[...] = r.astype(o_ref.dtype)

        if n_k == 1:
            finish(part)
        else:
            @pl.when(kk == 0)
            def _():
                acc[...] = part

            @pl.when((kk > 0) & (kk < n_k - 1))
            def _():
                acc[...] += part

            @pl.when(kk == n_k - 1)
            def _():
                finish(acc[...] + part)

        if n_r:
            @pl.when((ii == n_i - 1) & (jj == n_j - 1) & (kk == n_k - 1))
            def _():
                _carry_run(False, kinds, ci_refs, co_refs, *sems)

    if mode == "nn":
        a_spec = pl.BlockSpec((bm, bk), lambda i, j, kk: (i, ia + kk))
        b_spec = pl.BlockSpec((bk, bn), lambda i, j, kk: (kk, j0 + j))
    elif mode == "nt":
        a_spec = pl.BlockSpec((bm, bk), lambda i, j, kk: (i, ia + kk))
        b_spec = pl.BlockSpec((bn, bk), lambda i, j, kk: (j, k0 + kk))
    else:
        a_spec = pl.BlockSpec((bk, bm), lambda i, j, kk: (kk, ia + i))
        b_spec = pl.BlockSpec((bk, bn), lambda i, j, kk: (kk, j0 + j))
    any_spec = pl.BlockSpec(memory_space=pl.ANY)
    in_specs, args = [a_spec, b_spec], [a, b]
    if has_add:
        in_specs.append(pl.BlockSpec((bm, bn), lambda i, j, kk: (i, j)))
        args.append(add)
    res = pl.pallas_call(
        body, name=name, grid=(n_i, n_j, n_k), in_specs=in_specs + [any_spec] * n_r,
        out_specs=[pl.BlockSpec((bm, bn), lambda i, j, kk: (i, j))] + [any_spec] * n_r,
        out_shape=[jax.ShapeDtypeStruct((m, n), out_dtype)] + _carry_shapes(carry),
        scratch_shapes=[pltpu.VMEM((bm, bn) if n_k > 1 else (8, LANES), F32)] + _carry_scratch(carry),
        compiler_params=_cparams(("arbitrary", "arbitrary", "arbitrary")),
    )(*args, *[arr for _, arr in carry])
    for (_, _, done), g in zip(reqs, res[1:]):
        done(g)
    return res[0]


def _ln_res_fn(_, p, x, st):
    g, b = p
    xin, m = x
    pre = ALPHA * xin + m
    mu = jnp.mean(pre, -1, keepdims=True)
    d = pre - mu
    var = jnp.mean(d * d, -1, keepdims=True)
    return [d * lax.rsqrt(var + LN_EPS) * g + b], []


def _make_conv_fn(taps, act):
    def fn(_, p, x, st):
        ws, b = p[:taps], p[taps]
        (xin,), (prev,) = x, st
        n = xin.shape[0]
        ext = jnp.concatenate([prev, xin], axis=0)
        y = b
        for k in range(taps):
            y = y + ws[k] * _rows(_roll(ext, taps - 1 - k, 0), 8, 8 + n)
        if act:
            y = _silu(y)
        return [y], [_rows(xin, n - 8, n)]

    return fn


def _ffn_act_fn(_, p, x, st):
    n = x[0].shape[0]
    ys = []
    for half in range(2):
        ws, b = p[4 * half:4 * half + 3], p[4 * half + 3]
        ext = jnp.concatenate([st[half], x[half]], axis=0)
        y = b
        for k in range(3):
            y = y + ws[k] * _rows(_roll(ext, 2 - k, 0), 8, 8 + n)
        ys.append(y)
    return [_silu(ys[0]) * ys[1]], [_rows(x[0], n - 8, n), _rows(x[1], n - 8, n)]


def _ffn_act_bwd(_, p, x, st, dy, dst):
    (da,) = dy
    n, wd = x[0].shape
    rs = FFN_STRIP
    last = n // rs - 1
    zero8 = jnp.zeros((8, wd), F32)
    acc = [[zero8] * 4, [zero8] * 4]
    after = [zero8, zero8]
    strips = [[None] * (n // rs), [None] * (n // rs)]
    for i in reversed(range(n // rs)):
        r0 = rs * i
        taps, ys = [], []
        for half in range(2):
            w0, w1, w2, b = p[4 * half:4 * half + 4]
            xs = jnp.concatenate([st[half] if i == 0 else x[half][r0 - 8:r0], x[half][r0:r0 + rs]], axis=0)
            taps.append((pltpu.roll(xs, 2, 0)[8:], pltpu.roll(xs, 1, 0)[8:], xs[8:]))
            ys.append(b + w2 * taps[half][2] + w1 * taps[half][1] + w0 * taps[half][0])
        g, u = ys
        s = _sigmoid(g)
        d = da[r0:r0 + rs]
        dys = (d * u * (s * (1.0 + g * (1.0 - s))), d * (g * s))
        for half in range(2):
            w0, w1, w2, _ = p[4 * half:4 * half + 4]
            dyh = dys[half]
            for k, v in enumerate((dyh * taps[half][0], dyh * taps[half][1], dyh * taps[half][2], dyh)):
                for r in range(0, rs, 8):
                    acc[half][k] = acc[half][k] + v[r:r + 8]
            dyp = jnp.concatenate([dyh, after[half]], axis=0)
            dxs = w2 * dyh + w1 * pltpu.roll(dyp, rs + 8 - 1, 0)[:rs] + w0 * pltpu.roll(dyp, rs + 8 - 2, 0)[:rs]
            if i == last:
                dxs = jnp.concatenate([dxs[:rs - 8], dxs[rs - 8:] + dst[half]], axis=0)
            strips[half][i] = dxs
            after[half] = dyh[:8]
    dprev = []
    for half in range(2):
        w0, w1 = p[4 * half], p[4 * half + 1]
        head = jnp.concatenate([zero8, after[half]], axis=0)
        dprev.append((w1 * pltpu.roll(head, 16 - 1, 0) + w0 * pltpu.roll(head, 16 - 2, 0))[:8])
    dps = [jnp.sum(a, axis=0, keepdims=True) for half in range(2) for a in acc[half]]
    return dps, [jnp.concatenate(s_, axis=0) for s_ in strips], dprev


def _ssd_fn(_, p, x, st):
    dtb, alog, dsk = p
    xbc, dtr = x
    L = SSD_CHUNK
    tril = _iota((L, L), 0) >= _iota((L, L), 1)
    xs, bm, cm = _cols(xbc, 0, 512), _cols(xbc, 512, 640), _cols(xbc, 640, 768)
    dt = _softplus(dtr + dtb)
    da = dt * (-jnp.exp(alog))
    cs = _sel_dot(tril.astype(BF16), da)
    pick = ((_iota((LANES, 2 * LANES), 0) == 0) & (_iota((LANES, 2 * LANES), 1) < LANES)) | (
        (_iota((LANES, 2 * LANES), 0) == 64) & (_iota((LANES, 2 * LANES), 1) >= LANES))
    pick = pick.astype(BF16)
    tot = jnp.sum(da, axis=0, keepdims=True)
    xc = xs * dt
    xdec = xc * jnp.exp(tot - cs)
    ecs = jnp.exp(cs)
    etot = jnp.exp(tot)
    ys, new_st = [], []
    for pr in range(4):
        lo, hi = LANES * pr, LANES * (pr + 1)
        grp = pr // 2
        c_g = cm * _lane_mask(64 * grp, 64 * grp + 64)
        gmat = _dot_nt(c_g, bm)
        cs_p, xc_p = _cols(cs, lo, hi), _cols(xc, lo, hi)
        cols2 = _dot_sel(cs_p, pick)
        yd = jnp.zeros((L, LANES), F32)
        for half in range(2):
            col = _cols(cols2, LANES * half, LANES * (half + 1))
            diff = col - col.T
            dec = jnp.where(tril, jnp.exp(jnp.where(tril, diff, 0.0)), 0.0)
            yd = yd + _dot(gmat * dec, xc_p) * _lane_mask(64 * half, 64 * half + 64)
        s_in = st[pr]
        y_off = _dot(c_g, s_in) * _cols(ecs, lo, hi)
        ys.append(yd + y_off + _cols(dsk, lo, hi) * _cols(xs, lo, hi))
        new_st.append(s_in * _cols(etot, lo, hi) + _dot_tn(bm, _cols(xdec, lo, hi)))
    return [jnp.concatenate(ys, axis=1)], new_st


def _hg_fn(_, p, x, st):
    (lb,) = p
    (xin,) = x
    L = HG_STEP
    n_lvl = L.bit_length() - 1
    hq, hf, hi = _cols(xin, 0, 512), _cols(xin, 512, 1024), _cols(xin, 1024, 1536)
    q = _silu(hq)
    logf = jnp.log(lb + (1.0 - lb) * _sigmoid(hf))
    k = (1.0 - lb) * _sigmoid(-hf)
    ti, si = _iota((L, L), 0), _iota((L, L), 1)
    bc = _sel_dot((ti >= si).astype(BF16), logf)
    tot = jnp.sum(logf, axis=0, keepdims=True)
    tn, sn = _iota((n_lvl * L, 1), 0), _iota((n_lvl * L, L), 1)
    row = tn & (L - 1)
    blk = L >> (tn >> n_lvl)
    piv = row - (row & (blk - 1)) + (blk >> 1)
    bcp_all = _sel_dot((sn == piv).astype(BF16), bc)
    t1 = _iota((L, 1), 0)
    qqs, kks, sames = [], [], []
    for lvl in range(n_lvl):
        size = L >> lvl
        upper = (t1 & (size - 1)) >= size // 2
        bcp = _rows(bcp_all, L * lvl, L * (lvl + 1))
        qqs.append(jnp.where(upper, q * jnp.exp(jnp.where(upper, bc - bcp, 0.0)), 0.0))
        kks.append(jnp.where(upper, 0.0, k * jnp.exp(jnp.where(upper, 0.0, bcp - bc))))
        sames.append((ti >> (n_lvl - lvl)) == (si >> (n_lvl - lvl)))
    q_in = q * jnp.exp(bc)
    k_out = k * jnp.exp(tot - bc)
    diag = q * k
    etot = jnp.exp(tot)
    outs, new_st = [], []
    for h in range(4):
        lo, up = LANES * h, LANES * (h + 1)
        attn = jnp.zeros((L, L), F32)
        for lvl in range(n_lvl):
            attn = attn + jnp.where(sames[lvl], _dot_nt(_cols(qqs[lvl], lo, up), _cols(kks[lvl], lo, up)), 0.0)
        v = _cols(hi, lo, up)
        out = _dot(attn, v) + jnp.sum(_cols(diag, lo, up), axis=-1, keepdims=True) * v
        outs.append(out + _dot_nt(_cols(q_in, lo, up), st[h]))
        new_st.append(st[h] * _cols(etot, lo, up) + _dot_tn(v, _cols(k_out, lo, up)))
    return [jnp.concatenate(outs, axis=1)], new_st


def _swa_fn(chunk, p, x, st):
    (sinks,) = p
    (xin,) = x
    q, k, v = _cols(xin, 0, 512), _cols(xin, 512, 640), _cols(xin, 640, 768)
    kp, vp = st
    T = SWA_BLOCK
    kc = jnp.concatenate([kp, k], axis=0)
    vc = jnp.concatenate([vp, v], axis=0)
    qi, kj = _iota((T, 2 * T), 0), _iota((T, 2 * T), 1)
    rel = qi + T - kj
    mask = (rel >= 0) & (rel < T) & ((kj >= T) | (chunk > 0))
    srow = _iota((8, LANES), 0)
    outs = []
    for pr in range(4):
        grp = pr // 2
        gm = _lane_mask(64 * grp, 64 * grp + 64)
        km, vm = kc * gm, vc * gm
        q2 = _cols(q, LANES * pr, LANES * (pr + 1))
        o2 = jnp.zeros((T, LANES), F32)
        for half in range(2):
            hm = _lane_mask(64 * half, 64 * half + 64)
            qh = q2 * hm
            if half != grp:
                qh = _roll(qh, 64, 1)
            s = _dot_nt(qh, km) * 0.125
            s = jnp.where(mask, s, MASK_VALUE)
            sink = jnp.mean(jnp.sum(jnp.where(srow == 2 * pr + half, sinks, 0.0), axis=0, keepdims=True),
                            axis=-1, keepdims=True)
            mx = lax.stop_gradient(jnp.maximum(jnp.max(s, axis=-1, keepdims=True), sink))
            e = jnp.exp(s - mx)
            den = jnp.sum(e, axis=-1, keepdims=True) + jnp.exp(sink - mx)
            o = _dot(e / den, vm)
            if half != grp:
                o = _roll(o, 64, 1)
            o2 = o2 + o * hm
        outs.append(o2)
    return [jnp.concatenate(outs, axis=1)], [k, v]


def _rg_gate_fn(_, p, x, st):
    wa, ba, wx, bx, lam = p
    (xc,) = x
    r = _sigmoid(_dot(xc, wa) + ba)
    i = _sigmoid(_dot(xc, wx) + bx)
    log_a = -RG_C * r * _softplus(-lam)
    a = jnp.exp(log_a)
    t = jnp.tanh(log_a)
    one_minus_a2 = -2.0 * t / (1.0 - t)
    u = jnp.sqrt(jnp.maximum(one_minus_a2, 0.0)) * (i * xc)
    return [a, u], []


def _rg_scan_fn(_, p, x, st):
    a, u = x
    (prev,) = st
    n = a.shape[0]
    row = _iota((n, 1), 0)
    s = 1
    while s < n:
        keep = row >= s
        a_s, u_s = _roll(a, s, 0), _roll(u, s, 0)
        u = jnp.where(keep, a * u_s + u, u)
        a = jnp.where(keep, a * a_s, a)
        s *= 2
    h_in = jnp.sum(jnp.where(_iota((8, 1), 0) == 7, prev, 0.0), axis=0, keepdims=True)
    h = u + a * h_in
    return [h], [_rows(h, n - 8, n)]


def _ab_post_fn(_, p, x, st):
    nw_ssd, nw_hg = p
    y, o, zg = x
    z, hgate = _cols(zg, 0, 512), _cols(zg, 512, 1024)
    lane = _iota((1, 512), 1)
    ya = y * _silu(z)
    sq = ya * ya
    inv = jnp.zeros_like(ya)
    for g in range(2):
        mk = (lane >= 256 * g) & (lane < 256 * (g + 1))
        ms = jnp.sum(jnp.where(mk, sq, 0.0), axis=-1, keepdims=True) / 256.0
        inv = jnp.where(mk, lax.rsqrt(ms + RMS_EPS), inv)
    ya = ya * inv * nw_ssd
    so = o * o
    inv = jnp.zeros_like(o)
    for h in range(4):
        mk = (lane >= 128 * h) & (lane < 128 * (h + 1))
        ms = jnp.sum(jnp.where(mk, so, 0.0), axis=-1, keepdims=True) / 128.0
        inv = jnp.where(mk, lax.rsqrt(ms + RMS_EPS), inv)
    yb = o * inv * nw_hg * _silu(hgate)
    return [jnp.concatenate([ya, yb], axis=1)], []


def _cd_post_fn(_, p, x, st):
    yc, h, gate = x
    return [jnp.concatenate([yc, h * _gelu_tanh(gate)], axis=1)], []


def _lb_fn(_, p, x, st):
    l0, l1 = x
    mx = lax.stop_gradient(jnp.maximum(l0, l1))
    e0, e1 = jnp.exp(l0 - mx), jnp.exp(l1 - mx)
    s0, s1 = e0 / (e0 + e1), e1 / (e0 + e1)
    return [jnp.clip(s0 - s0, 0.0, 1.0), jnp.clip((s0 + s1) - s0, 0.0, 1.0)], []


def _loss_kernel(y, target):
    t, d = y.shape
    bt = _pick(t, (512, 256, 128))

    def body(y_ref, t_ref, dy_ref, l_ref):
        @pl.when(pl.program_id(0) == 0)
        def _():
            l_ref[...] = jnp.zeros(l_ref.shape, F32)

        e = y_ref[...] - t_ref[...]
        dy_ref[...] = e * (1.0 / d)
        l_ref[...] += jnp.sum(e * e, axis=0, keepdims=True) * (0.5 / d)

    dy, part = pl.pallas_call(
        body, name="loss", grid=(t // bt,),
        in_specs=[pl.BlockSpec((bt, d), lambda i: (i, 0)), pl.BlockSpec((bt, d), lambda i: (i, 0))],
        out_specs=[pl.BlockSpec((bt, d), lambda i: (i, 0)), pl.BlockSpec((1, d), lambda i: (0, 0))],
        out_shape=[jax.ShapeDtypeStruct((t, d), F32), jax.ShapeDtypeStruct((1, d), F32)],
        compiler_params=_cparams(("arbitrary",)),
    )(y, target)
    return dy, jnp.sum(part)


def _adamw_math(parts, w_, m_, v_):
    c1 = 1.0 / (1.0 - ADAM_B1 ** ADAM_STEP)
    c2 = 1.0 / (1.0 - ADAM_B2 ** ADAM_STEP)
    g = parts[0].astype(F32)
    for s in range(1, N_DEV):
        g = g + parts[s].astype(F32)
    nm = ADAM_B1 * m_ + (1.0 - ADAM_B1) * g
    nv = ADAM_B2 * v_ + (1.0 - ADAM_B2) * (g * g)
    return g, -ADAM_LR * ((nm * c1) / (jnp.sqrt(nv * c2) + ADAM_EPS) + ADAM_WD * w_), nm, nv


def _adamw_big(name, parts, w, m, v):
    n_l, r, c = w.shape
    br = _pick(r, (256, 176, 128, 64, 32, 16, 8))

    def body(*refs):
        p_refs, (w_ref, m_ref, v_ref), outs = refs[:n_l], refs[n_l:n_l + 3], refs[n_l + 3:]
        for l in range(n_l):
            @pl.when(pl.program_id(0) == l)
            def _(p_ref=p_refs[l]):
                res = _adamw_math([p_ref[s] for s in range(N_DEV)], w_ref[...], m_ref[...], v_ref[...])
                for ref, val in zip(outs, res):
                    ref[...] = val

    blk = pl.BlockSpec((None, br, c), lambda l, i: (l, i, 0))
    p_specs = [pl.BlockSpec((N_DEV, br, c), lambda l, i, k=k: (0, jnp.where(l == k, i, 0), 0)) for k in range(n_l)]
    return pl.pallas_call(
        body, name=name, grid=(n_l, r // br), in_specs=p_specs + [blk, blk, blk],
        out_specs=[blk] * 4, out_shape=[jax.ShapeDtypeStruct(w.shape, F32)] * 4,
        compiler_params=_cparams(("arbitrary", "arbitrary")),
    )(*parts, w, m, v)


def _adamw_small(name, items):
    n = len(items)

    def body(*refs):
        ins, outs = refs[:4 * n], refs[4 * n:]
        for i in range(n):
            p_ref, w_ref, m_ref, v_ref = ins[4 * i:4 * i + 4]
            res = _adamw_math([p_ref[s] for s in range(N_DEV)], w_ref[...], m_ref[...], v_ref[...])
            for ref, val in zip(outs[4 * i:4 * i + 4], res):
                ref[...] = val

    flat = [a for it in items for a in it]
    out_shape = [jax.ShapeDtypeStruct(it[1].shape, F32) for it in items for _ in range(4)]
    res = pl.pallas_call(
        body, name=name, out_shape=out_shape,
        in_specs=[pl.BlockSpec(memory_space=pltpu.VMEM)] * len(flat),
        out_specs=[pl.BlockSpec(memory_space=pltpu.VMEM)] * len(out_shape),
        compiler_params=pltpu.CompilerParams(vmem_limit_bytes=VMEM_LIMIT),
    )(*flat)
    return [res[4 * i:4 * i + 4] for i in range(n)]


SHARDED = [("ab_w_in", 2), ("ab_w_out", 1), ("cd_w_in", 2), ("cd_w_out", 1), ("ffn_w_up", 2), ("ffn_w_down", 1),
           ("ssd_conv_w", 2), ("rg_conv_w", 2), ("rg_conv_b", 1), ("rg_ba", 1), ("rg_bx", 1), ("rg_lambda", 1),
           ("ffn_conv_w", 2), ("ln_g", 2), ("ln_b", 2)]
MATMUL_W = ("ab_w_in", "ab_w_out", "cd_w_in", "cd_w_out", "ffn_w_up", "ffn_w_down")
REPLICATED = ["ssd_conv_b", "ssd_dt_bias", "ssd_a_log", "ssd_d", "ssd_norm_w", "hg_lower", "hg_norm_w", "swa_sinks",
              "rg_wa", "rg_wx", "ffn_conv_b"]
WEIGHTS = ["ab_w_in", "ssd_conv_w", "ssd_conv_b", "ssd_dt_bias", "ssd_a_log", "ssd_d", "ssd_norm_w", "hg_lower",
           "hg_norm_w", "ab_w_out", "cd_w_in", "swa_sinks", "rg_conv_w", "rg_conv_b", "rg_wa", "rg_ba", "rg_wx",
           "rg_bx", "rg_lambda", "cd_w_out", "ffn_w_up", "ffn_conv_w", "ffn_conv_b", "ffn_w_down", "ln_g", "ln_b"]


def _as2d(a):
    return a.reshape(-1, a.shape[-1])


def _merge_shards(g, axis):
    g = jnp.moveaxis(g, 0, axis)
    shp = g.shape
    return g.reshape(shp[:axis] + (shp[axis] * shp[axis + 1],) + shp[axis + 2:])


def _split_shards(full, axis):
    shp = full.shape
    g = full.reshape(shp[:axis] + (N_DEV, shp[axis] // N_DEV) + shp[axis + 1:])
    return jnp.moveaxis(g, axis, 0)


def _ab_pad(w):
    z, xbc, dt = w[..., 0:512], w[..., 512:1280], w[..., 1280:1288]
    hqfi, hg = w[..., 1288:2824], w[..., 2824:3336]
    return jnp.concatenate([hqfi, jnp.repeat(dt, 64, axis=-1), z, hg, xbc], axis=-1)


def _ab_unpad(d):
    lead = d.shape[:-1]
    dt = d[..., AB_DT:AB_ZG].reshape(lead + (8, 64)).sum(-1)
    z, hg, xbc = d[..., AB_ZG:AB_ZG + 512], d[..., AB_ZG + 512:AB_XBC], d[..., AB_XBC:AB_PAD]
    return jnp.concatenate([z, xbc, dt, d[..., :AB_DT], hg], axis=-1)


def _cd_pad(w):
    return jnp.concatenate([w[..., :768], jnp.zeros(w.shape[:-1] + (256,), w.dtype), w[..., 768:]], axis=-1)


def _cd_unpad(d):
    return jnp.concatenate([d[..., :768], d[..., CD_GATE:CD_PAD]], axis=-1)


def _cat_halves(d):
    return jnp.concatenate(d, axis=1)


def _block_diag(w):
    eye = jnp.eye(8, dtype=w.dtype)
    return jnp.einsum("gij,gh->gihj", w, eye).reshape(512, 512)


def _block_diag_grad(d):
    return jnp.stack([d[64 * g:64 * g + 64, 64 * g:64 * g + 64] for g in range(8)])


def _same(a):
    return a


BIG = {"ab_w_in": (1, _ab_pad, _ab_unpad), "ab_w_out": (0, _same, _same), "cd_w_in": (1, _cd_pad, _cd_unpad),
       "cd_w_out": (0, _same, _same), "ffn_w_up": (1, _same, _cat_halves), "ffn_w_down": (0, _same, _same)}


class _MeshComm:
    def __init__(self, shards):
        self.shards, self.full, self.recv, self.posted = shards, {}, {}, {}

    def post(self, carrier, req):
        self.posted.setdefault(carrier, []).append(req)

    def take(self, carrier):
        return self.posted.pop(carrier, [])

    def take_all(self):
        reqs = [r for name in list(self.posted) for r in self.posted.pop(name)]
        return reqs

    def gather_req(self, name, layer):
        axis, prep, _ = BIG[name]

        def done(got):
            self.full[name, layer] = prep(_merge_shards(got, axis))

        return ("gather", self.shards[name][layer].astype(MM_DTYPE), done)

    def weight(self, name, layer):
        return self.full[name, layer]

    def grad_req(self, name, layer, d):
        axis, _, unprep = BIG[name]

        def done(got):
            self.recv[name, layer] = got

        return ("exchange", _split_shards(unprep(d), axis).astype(MM_DTYPE), done)


class _LocalComm:
    def __init__(self, full):
        self.full_w, self.grads = full, {}

    def post(self, carrier, req):
        pass

    def take(self, carrier):
        return []

    def gather_req(self, name, layer):
        return None

    def weight(self, name, layer):
        return BIG[name][1](self.full_w[name][layer].astype(MM_DTYPE))

    def grad_req(self, name, layer, d):
        self.grads[name, layer] = BIG[name][2](d)
        return None


def _row_vec(v):
    return v.reshape(1, -1)


def _heads64(v):
    return jnp.repeat(v, 64).reshape(1, 512)


def _local_step(x, target, w, comm):
    kinds = ["ab" if layer % 2 == 0 else "cd" for layer in range(DEPTH)]
    in_name = [f"{kinds[layer]}_in{layer // 2}" for layer in range(DEPTH)]
    core_name = [("hg" if layer % 2 == 0 else "swa") + f"{layer // 2}_fwd" for layer in range(DEPTH)]
    comm.post("ssd0_fwd", comm.gather_req("ffn_w_down", 0))
    comm.post(core_name[0], comm.gather_req("ffn_w_up", 0))
    for layer in range(DEPTH - 1):
        nxt, nj = kinds[layer + 1], (layer + 1) // 2
        if kinds[layer] == "ab":
            comm.post(in_name[layer], comm.gather_req(nxt + "_w_in", nj))
            comm.post(core_name[layer], comm.gather_req(nxt + "_w_out", nj))
            comm.post(core_name[layer], comm.gather_req("ffn_w_down", layer + 1))
        else:
            comm.post(in_name[layer], comm.gather_req(nxt + "_w_out", nj))
            comm.post(core_name[layer], comm.gather_req(nxt + "_w_in", nj))
            comm.post(f"ffn_up{layer}", comm.gather_req("ffn_w_down", layer + 1))
        comm.post(f"ffn_act{layer}_fwd", comm.gather_req("ffn_w_up", layer + 1))

    t = x.shape[0]
    bt = _pick(t, (512, 256, 128))
    nb = t // bt
    bs = _pick(t, (256, 128))

    def rowop(name, fn, params, xs, widths_out, out_dtype=F32, dx_dtypes=None):
        outs = [((t, wd), (bt, wd), lambda g, c: (c, 0), out_dtype) for wd in widths_out]
        return _Op(name, fn, (1, nb), params, xs, outs, dx_dtypes=dx_dtypes)

    def rowblk(arr, width, first=0):
        return (arr, (bt, width), lambda g, c: (c, first))

    one_row = lambda g, c: (0, 0)
    lb_op = _Op("hg_lb", _lb_fn, (1, 1), [],
                [(w["hg_lower"][0:1], (1, 512), one_row), (w["hg_lower"][1:2], (1, 512), one_row)],
                [((1, 512), (1, 512), one_row, F32)] * 2)
    lb_all = lb_op.ys

    tape = []
    grads = {}

    def add_grad(name, idx, val):
        grads.setdefault(name, {})[idx] = val

    def dw_matmul(name, a, b, wname, idx, carrier):
        d = _matmul(name, a, b, "tn", out_dtype=MM_DTYPE, comm=comm)
        comm.post(carrier, comm.grad_req(wname, idx, d))

    for layer in range(DEPTH):
        j = layer // 2
        rec = {"x_in": x}
        if layer % 2 == 0:
            h = _matmul(f"ab_in{j}", x, comm.weight("ab_w_in", j), "nn", comm=comm)
            conv_p = [_row_vec(w["ssd_conv_w"][j, k]) for k in range(4)] + [_row_vec(w["ssd_conv_b"][j])]
            conv = _Op(f"ssd_conv{j}", _make_conv_fn(4, True), (3, nb),
                       [(a, (1, 256), lambda g, c: (0, g)) for a in conv_p],
                       [(h, (bt, 256), lambda g, c: (c, AB_XBC // 256 + g))],
                       [((t, 768), (bt, 256), lambda g, c: (c, g), F32)], [(8, 256)], dx_dtypes=[MM_DTYPE])
            ssd = _Op(f"ssd{j}", _ssd_fn, (1, t // SSD_CHUNK),
                      [_whole(_heads64(w["ssd_dt_bias"][j])), _whole(_heads64(w["ssd_a_log"][j])),
                       _whole(_heads64(w["ssd_d"][j]))],
                      [(conv.ys[0], (SSD_CHUNK, 768), lambda g, c: (c, 0)),
                       (h, (SSD_CHUNK, 512), lambda g, c: (c, AB_DT // 512))],
                      [((t, 512), (SSD_CHUNK, 512), lambda g, c: (c, 0), F32)], [(LANES, LANES)] * 4,
                      dx_dtypes=[F32, MM_DTYPE], comm=comm)
            hg = _Op(f"hg{j}", _hg_fn, (1, t // HG_STEP), [_whole(lb_all[j])],
                     [(h, (HG_STEP, AB_DT), lambda g, c: (c, 0))],
                     [((t, 512), (HG_STEP, 512), lambda g, c: (c, 0), F32)], [(LANES, LANES)] * 4,
                     dx_dtypes=[MM_DTYPE], comm=comm)
            post = rowop(f"ab_post{j}", _ab_post_fn,
                         [_whole(_row_vec(w["ssd_norm_w"][j])), _whole(jnp.tile(_row_vec(w["hg_norm_w"][j]), (1, 4)))],
                         [rowblk(ssd.ys[0], 512), rowblk(hg.ys[0], 512), rowblk(h, 1024, AB_ZG // 1024)], [1024],
                         out_dtype=MM_DTYPE, dx_dtypes=[F32, F32, MM_DTYPE])
            rec.update(kind="ab", conv=conv, ssd=ssd, hg=hg, post=post)
        else:
            h = _matmul(f"cd_in{j}", x, comm.weight("cd_w_in", j), "nn", comm=comm)
            swa = _Op(f"swa{j}", _swa_fn, (1, t // SWA_BLOCK),
                      [_whole(jnp.tile(w["swa_sinks"][j].reshape(8, 1), (1, LANES)))],
                      [(h, (SWA_BLOCK, 1024), lambda g, c: (c, 0))],
                      [((t, 512), (SWA_BLOCK, 512), lambda g, c: (c, 0), F32)], [(SWA_BLOCK, LANES)] * 2,
                      dx_dtypes=[MM_DTYPE], comm=comm)
            conv_p = [_row_vec(w["rg_conv_w"][j, k]) for k in range(4)] + [_row_vec(w["rg_conv_b"][j])]
            conv = _Op(f"rg_conv{j}", _make_conv_fn(4, False), (2, nb),
                       [(a, (1, 256), lambda g, c: (0, g)) for a in conv_p],
                       [(h, (bt, 256), lambda g, c: (c, CD_XR // 256 + g))],
                       [((t, 512), (bt, 256), lambda g, c: (c, g), F32)], [(8, 256)], dx_dtypes=[MM_DTYPE])
            gate = rowop(f"rg_gate{j}", _rg_gate_fn,
                         [_whole(_block_diag(w["rg_wa"][j])), _whole(_row_vec(w["rg_ba"][j])),
                          _whole(_block_diag(w["rg_wx"][j])), _whole(_row_vec(w["rg_bx"][j])),
                          _whole(_row_vec(w["rg_lambda"][j]))],
                         [rowblk(conv.ys[0], 512)], [512, 512])
            scan = _Op(f"rg_scan{j}", _rg_scan_fn, (2, t // bs), [],
                       [(gate.ys[0], (bs, 256), lambda g, c: (c, g)), (gate.ys[1], (bs, 256), lambda g, c: (c, g))],
                       [((t, 512), (bs, 256), lambda g, c: (c, g), F32)], [(8, 256)])
            post = rowop(f"cd_post{j}", _cd_post_fn, [],
                         [rowblk(swa.ys[0], 512), rowblk(scan.ys[0], 512), rowblk(h, 512, CD_GATE // 512)], [1024],
                         out_dtype=MM_DTYPE, dx_dtypes=[F32, F32, MM_DTYPE])
            rec.update(kind="cd", swa=swa, conv=conv, gate=gate, scan=scan, post=post)
        kind = rec["kind"]
        ycat = post.ys[0]
        m = _matmul(f"mix_out{layer}", ycat, comm.weight(kind + "_w_out", j), "nn", comm=comm)
        ln1 = rowop(f"ln_a{layer}", _ln_res_fn,
                    [_whole(_row_vec(w["ln_g"][layer, 0])), _whole(_row_vec(w["ln_b"][layer, 0]))],
                    [rowblk(x, 1024), rowblk(m, 1024)], [1024], dx_dtypes=[F32, MM_DTYPE])
        x1 = ln1.ys[0]
        hu = _matmul(f"ffn_up{layer}", x1, comm.weight("ffn_w_up", layer), "nn", comm=comm)
        n_fb = FFN_DIM // FFN_BLK
        taps = [_row_vec(w["ffn_conv_w"][layer, k]) for k in range(3)] + [_row_vec(w["ffn_conv_b"][layer])]
        act = _Op(f"ffn_act{layer}", _ffn_act_fn, (n_fb, nb),
                  [(a, (1, FFN_BLK), lambda g, c: (0, g)) for a in taps]
                  + [(a, (1, FFN_BLK), lambda g, c: (0, n_fb + g)) for a in taps],
                  [(hu, (bt, FFN_BLK), lambda g, c: (c, g)), (hu, (bt, FFN_BLK), lambda g, c: (c, n_fb + g))],
                  [((t, FFN_DIM), (bt, FFN_BLK), lambda g, c: (c, g), MM_DTYPE)], [(8, FFN_BLK)] * 2,
                  dx_dtypes=[MM_DTYPE, MM_DTYPE], comm=comm, bwd_fn=_ffn_act_bwd)
        a = act.ys[0]
        f = _matmul(f"ffn_down{layer}", a, comm.weight("ffn_w_down", layer), "nn", comm=comm)
        ln2 = rowop(f"ln_f{layer}", _ln_res_fn,
                    [_whole(_row_vec(w["ln_g"][layer, 1])), _whole(_row_vec(w["ln_b"][layer, 1]))],
                    [rowblk(x1, 1024), rowblk(f, 1024)], [1024], dx_dtypes=[F32, MM_DTYPE])
        rec.update(ycat=ycat, ln1=ln1, x1=x1, act=act, a=a, ln2=ln2)
        tape.append(rec)
        x = ln2.ys[0]

    dx, loss = _loss_kernel(x, target)

    d_lb = [jnp.zeros((1, 512), F32), jnp.zeros((1, 512), F32)]
    for layer in reversed(range(DEPTH)):
        j = layer // 2
        rec = tape[layer]
        (dg, db), (dx1_res, df) = rec["ln2"].bwd([dx])
        add_grad("ln_g", (layer, 1), dg[0]); add_grad("ln_b", (layer, 1), db[0])
        dw_matmul(f"ffn_down_dw{layer}", rec["a"], df, "ffn_w_down", layer, f"ffn_act{layer}_bwd")
        da = _matmul(f"ffn_down_dx{layer}", df, comm.weight("ffn_w_down", layer), "nt", comm=comm)
        dpa, (dhg, dhu) = rec["act"].bwd([da])
        halves = [jnp.concatenate([dpa[k][0, :FFN_DIM], dpa[4 + k][0, FFN_DIM:]]) for k in range(4)]
        add_grad("ffn_conv_w", layer, jnp.stack(halves[:3]))
        add_grad("ffn_conv_b", layer, halves[3])
        core_bwd = ("hg" if rec["kind"] == "ab" else "swa") + f"{j}_bwd"
        gate_cols, up_cols = (0, FFN_DIM), (FFN_DIM, FFN_DIM)
        dw_up = (_matmul(f"ffn_up_dw_g{layer}", rec["x1"], dhg, "tn", out_dtype=MM_DTYPE, comm=comm, b_cols=gate_cols),
                 _matmul(f"ffn_up_dw_u{layer}", rec["x1"], dhu, "tn", out_dtype=MM_DTYPE, comm=comm, b_cols=up_cols))
        comm.post(core_bwd, comm.grad_req("ffn_w_up", layer, dw_up))
        w_up = comm.weight("ffn_w_up", layer)
        dx1 = _matmul(f"ffn_up_dx_g{layer}", dhg, w_up, "nt", a_cols=gate_cols, b_cols=gate_cols, add=dx1_res,
                      comm=comm)
        dx1 = _matmul(f"ffn_up_dx_u{layer}", dhu, w_up, "nt", a_cols=up_cols, b_cols=up_cols, add=dx1, comm=comm)
        (dg, db), (dx_res, dm) = rec["ln1"].bwd([dx1])
        add_grad("ln_g", (layer, 0), dg[0]); add_grad("ln_b", (layer, 0), db[0])
        kind = rec["kind"]
        dw_matmul(f"mix_out_dw{layer}", rec["ycat"], dm, kind + "_w_out", j, f"{kind}_in_dw{j}")
        dycat = _matmul(f"mix_out_dx{layer}", dm, comm.weight(kind + "_w_out", j), "nt", comm=comm)
        if kind == "ab":
            (dnw_s, dnw_h), (dy_ssd, do_hg, dh) = rec["post"].bwd([dycat])
            add_grad("ssd_norm_w", j, dnw_s[0]); add_grad("hg_norm_w", j, dnw_h[0].reshape(4, LANES).sum(0))
            (dlb,), (dh,) = rec["hg"].bwd([do_hg], dx_into={0: dh})
            d_lb[j] = dlb
            (ddtb, dalog, ddsk), (dxbc_c, dh) = rec["ssd"].bwd([dy_ssd], dx_into={1: dh})
            add_grad("ssd_dt_bias", j, ddtb[0].reshape(8, 64).sum(-1))
            add_grad("ssd_a_log", j, dalog[0].reshape(8, 64).sum(-1))
            add_grad("ssd_d", j, ddsk[0].reshape(8, 64).sum(-1))
            dcp, (dh,) = rec["conv"].bwd([dxbc_c], dx_into={0: dh})
            add_grad("ssd_conv_w", j, jnp.stack([dcp[k][0] for k in range(4)]))
            add_grad("ssd_conv_b", j, dcp[4][0])
        else:
            _, (dyc, dhs, dh) = rec["post"].bwd([dycat])
            _, (da_s, du_s) = rec["scan"].bwd([dhs])
            (dwa, dba, dwx, dbx, dlam), (dxc,) = rec["gate"].bwd([da_s, du_s])
            add_grad("rg_wa", j, _block_diag_grad(dwa)); add_grad("rg_wx", j, _block_diag_grad(dwx))
            add_grad("rg_ba", j, dba[0]); add_grad("rg_bx", j, dbx[0]); add_grad("rg_lambda", j, dlam[0])
            dcp, (dh,) = rec["conv"].bwd([dxc], dx_into={0: dh})
            add_grad("rg_conv_w", j, jnp.stack([dcp[k][0] for k in range(4)]))
            add_grad("rg_conv_b", j, dcp[4][0])
            (dsink,), (dh,) = rec["swa"].bwd([dyc], dx_into={0: dh})
            add_grad("swa_sinks", j, dsink.sum(-1))
        dw_matmul(f"{kind}_in_dw{j}", rec["x_in"], dh, kind + "_w_in", j,
                  f"ffn_act{layer - 1}_bwd" if layer > 0 else f"{kind}_in_dx{j}")
        dx = _matmul(f"{kind}_in_dx{j}", dh, comm.weight(kind + "_w_in", j), "nt", add=dx_res, comm=comm)

    _, (dl0, dl1) = lb_op.bwd(d_lb)
    out = {"hg_lower": jnp.concatenate([dl0, dl1], axis=0)}
    for name, parts in grads.items():
        keys = sorted(parts)
        if isinstance(keys[0], tuple):
            out[name] = jnp.stack([jnp.stack([parts[(l, s)] for s in range(2)]) for l in range(DEPTH)])
        else:
            out[name] = jnp.stack([parts[k] for k in keys])
    return loss, dx, out


def _local_step_full(x, target, full):
    comm = _LocalComm(full)
    loss, dx, grads = _local_step(x, target, {n: a for n, a in full.items() if n not in BIG}, comm)
    for name in BIG:
        grads[name] = jnp.stack([comm.grads[name, l] for l in range(full[name].shape[0])])
    return loss, dx, grads


def kernel(x, ab_w_in, ssd_conv_w, ssd_conv_b, ssd_dt_bias, ssd_a_log, ssd_d, ssd_norm_w, hg_lower, hg_norm_w, ab_w_out, cd_w_in, swa_sinks, rg_conv_w, rg_conv_b, rg_wa, rg_ba, rg_wx, rg_bx, rg_lambda, cd_w_out, ffn_w_up, ffn_conv_w, ffn_conv_b, ffn_w_down, ln_g, ln_b, loss_target, m_ab_w_in, m_ssd_conv_w, m_ssd_conv_b, m_ssd_dt_bias, m_ssd_a_log, m_ssd_d, m_ssd_norm_w, m_hg_lower, m_hg_norm_w, m_ab_w_out, m_cd_w_in, m_swa_sinks, m_rg_conv_w, m_rg_conv_b, m_rg_wa, m_rg_ba, m_rg_wx, m_rg_bx, m_rg_lambda, m_cd_w_out, m_ffn_w_up, m_ffn_conv_w, m_ffn_conv_b, m_ffn_w_down, m_ln_g, m_ln_b, v_ab_w_in, v_ssd_conv_w, v_ssd_conv_b, v_ssd_dt_bias, v_ssd_a_log, v_ssd_d, v_ssd_norm_w, v_hg_lower, v_hg_norm_w, v_ab_w_out, v_cd_w_in, v_swa_sinks, v_rg_conv_w, v_rg_conv_b, v_rg_wa, v_rg_ba, v_rg_wx, v_rg_bx, v_rg_lambda, v_cd_w_out, v_ffn_w_up, v_ffn_conv_w, v_ffn_conv_b, v_ffn_w_down, v_ln_g, v_ln_b):
    args = dict(locals())
    wts = {n: args[n] for n in WEIGHTS}
    mom = {n: args["m_" + n] for n in WEIGHTS}
    var = {n: args["v_" + n] for n in WEIGHTS}
    axis = dict(SHARDED)
    small = [n for n, _ in SHARDED if n not in BIG]
    comm = _MeshComm(wts)

    def run(name, reqs):
        for (_, _, done), got in zip(reqs, _remote_copies(name, [(k, a) for k, a, _ in reqs])):
            done(got)

    full = {n: wts[n] for n in REPLICATED}

    def keep_small(n):
        def done(got):
            full[n] = _merge_shards(got.reshape((N_DEV,) + wts[n].shape), axis[n])
        return ("gather", _as2d(wts[n]), done)

    run("gather_first", [comm.gather_req("ab_w_in", 0), comm.gather_req("ab_w_out", 0)] + [keep_small(n) for n in small])

    loss, grad_x, grads = _local_step(x[0], loss_target[0], full, comm)
    loss = lax.psum(loss, ("x", "y", "c"))

    parts = {}

    def keep_parts(n, kind, arr):
        return (kind, arr, lambda got: parts.__setitem__(n, got))

    last = comm.take_all()
    last += [keep_parts(n, "exchange", _split_shards(grads[n], axis[n]).reshape((N_DEV,) + _as2d(wts[n]).shape))
             for n in small]
    last += [keep_parts(n, "gather", _as2d(grads[n])) for n in REPLICATED]
    run("exchange_last", last)

    new = {}
    for n in BIG:
        new[n] = _adamw_big("adamw_" + n, [comm.recv[n, l] for l in range(wts[n].shape[0])], wts[n], mom[n], var[n])
    names = small + REPLICATED
    res = _adamw_small("adamw_small", [(parts[n], _as2d(wts[n]), _as2d(mom[n]), _as2d(var[n])) for n in names])
    for n, r in zip(names, res):
        new[n] = [a.reshape(wts[n].shape) for a in r]

    outs = [loss, grad_x[None]]
    for kind in range(4):
        outs += [new[n][kind] for n in WEIGHTS]
    return tuple(outs)
```

```python
import math

import jax
import jax.numpy as jnp
from jax import lax
from jax.experimental import pallas as pl
from jax.experimental.pallas import tpu as pltpu

F32 = jnp.float32
BF16 = jnp.bfloat16
MM_DTYPE = BF16

DEPTH = 4
N_DEV = 8
LN_EPS = 1e-5
RMS_EPS = 1e-6
MASK_VALUE = -1e9
ALPHA = (2 * DEPTH) ** 0.25
RG_C = 8.0
FFN_DIM = 2816
SSD_CHUNK = 128
HG_STEP = 128
SWA_BLOCK = 128
LANES = 128
VMEM_LIMIT = 56 * 1024 * 1024

ADAM_LR, ADAM_B1, ADAM_B2, ADAM_EPS, ADAM_WD, ADAM_STEP = 0.001, 0.9, 0.999, 1e-08, 0.01, 10

AB_HEADS, AB_DT, AB_ZG, AB_XBC, AB_PAD = 0, 1536, 2048, 3072, 3840
CD_QKV, CD_GATE, CD_XR, CD_PAD = 0, 1024, 1536, 2048
FFN_BLK = 256
FFN_STRIP = 32


def _cols(x, lo, hi):
    n = x.shape[1]

    @jax.custom_vjp
    def f(x):
        return x[:, lo:hi]

    def bwd(_, g):
        parts = []
        if lo > 0:
            parts.append(jnp.zeros((g.shape[0], lo), g.dtype))
        parts.append(g)
        if hi < n:
            parts.append(jnp.zeros((g.shape[0], n - hi), g.dtype))
        return (jnp.concatenate(parts, axis=1) if len(parts) > 1 else g,)

    f.defvjp(lambda x: (f(x), None), bwd)
    return f(x)


def _split_cols(x, width):
    n = x.shape[1] // width

    @jax.custom_vjp
    def f(x):
        return tuple(x[:, width * i:width * (i + 1)] for i in range(n))

    f.defvjp(lambda x: (f(x), None), lambda _, gs: (jnp.concatenate(gs, axis=1),))
    return f(x)


def _rows(x, lo, hi):
    n = x.shape[0]

    @jax.custom_vjp
    def f(x):
        return x[lo:hi, :]

    def bwd(_, g):
        parts = []
        if lo > 0:
            parts.append(jnp.zeros((lo, g.shape[1]), g.dtype))
        parts.append(g)
        if hi < n:
            parts.append(jnp.zeros((n - hi, g.shape[1]), g.dtype))
        return (jnp.concatenate(parts, axis=0) if len(parts) > 1 else g,)

    f.defvjp(lambda x: (f(x), None), bwd)
    return f(x)


def _roll(x, shift, axis):
    n = x.shape[axis]
    shift = shift % n
    if shift == 0:
        return x

    @jax.custom_vjp
    def f(x):
        return pltpu.roll(x, shift, axis)

    f.defvjp(lambda x: (f(x), None), lambda _, g: (pltpu.roll(g, n - shift, axis),))
    return f(x)


def _dot(a, b):
    return lax.dot_general(a, b, (((1,), (0,)), ((), ())), preferred_element_type=F32)


def _dot_nt(a, b):
    return lax.dot_general(a, b, (((1,), (1,)), ((), ())), preferred_element_type=F32)


def _dot_tn(a, b):
    return lax.dot_general(a, b, (((0,), (0,)), ((), ())), preferred_element_type=F32)


def _split3(x):
    hi = x.astype(BF16)
    r = x - hi.astype(F32)
    mid = r.astype(BF16)
    return hi, mid, (r - mid.astype(F32)).astype(BF16)


def _sel_dot(sel, x):
    def run(mat, v, dims):
        n = v.shape[1]
        y = lax.dot_general(mat, jnp.concatenate(_split3(v), axis=1), dims, preferred_element_type=F32)
        return y[:, :n] + y[:, n:2 * n] + y[:, 2 * n:]

    @jax.custom_vjp
    def f(sel, x):
        return run(sel, x, (((1,), (0,)), ((), ())))

    def bwd(sel, g):
        return jnp.zeros_like(sel), run(sel, g, (((0,), (0,)), ((), ())))

    f.defvjp(lambda sel, x: (f(sel, x), sel), bwd)
    return f(sel, x)


def _dot_sel(x, sel):
    def run(v, mat, dims):
        m = v.shape[0]
        y = lax.dot_general(jnp.concatenate(_split3(v), axis=0), mat, dims, preferred_element_type=F32)
        return y[:m] + y[m:2 * m] + y[2 * m:]

    @jax.custom_vjp
    def f(x, sel):
        return run(x, sel, (((1,), (0,)), ((), ())))

    def bwd(sel, g):
        return run(g, sel, (((1,), (1,)), ((), ()))), jnp.zeros_like(sel)

    f.defvjp(lambda x, sel: (f(x, sel), sel), bwd)
    return f(x, sel)


def _sigmoid(x):
    return 0.5 * jnp.tanh(0.5 * x) + 0.5


def _silu(x):
    h = 0.5 * x
    return h + h * jnp.tanh(h)


def _softplus(x):
    return jnp.maximum(x, 0.0) + jnp.log(1.0 + jnp.exp(-jnp.abs(x)))


def _gelu_tanh(x):
    c = math.sqrt(2.0 / math.pi)
    return 0.5 * x * (1.0 + jnp.tanh(c * (x + 0.044715 * (x * x * x))))


def _iota(shape, axis):
    return lax.broadcasted_iota(jnp.int32, shape, axis)


def _lane_mask(lo, hi, width=LANES):
    lane = _iota((1, width), 1)
    return ((lane >= lo) & (lane < hi)).astype(F32)


def _mesh_pos():
    return lax.axis_index("x"), lax.axis_index("y"), lax.axis_index("c")


def _carry_shapes(carry):
    return [jax.ShapeDtypeStruct((N_DEV,) + a.shape if kind == "gather" else a.shape, a.dtype) for kind, a in carry]


def _carry_scratch(carry):
    n = len(carry)
    if n == 0:
        return []
    return [pltpu.SemaphoreType.DMA((n, N_DEV - 1)), pltpu.SemaphoreType.DMA((n, N_DEV - 1)),
            pltpu.SemaphoreType.DMA((n,))]


def _carry_run(start, kinds, in_refs, out_refs, send_sems, recv_sems, local_sems):
    x, y, cc = _mesh_pos()
    me = 4 * x + 2 * y + cc
    for i, kind in enumerate(kinds):
        mine = in_refs[i] if kind == "gather" else in_refs[i].at[me]
        local = pltpu.make_async_copy(mine, out_refs[i].at[me], local_sems.at[i])
        remote = []
        for k in range(1, N_DEV):
            px, py, pc = x ^ ((k >> 2) & 1), y ^ ((k >> 1) & 1), cc ^ (k & 1)
            src = in_refs[i] if kind == "gather" else in_refs[i].at[4 * px + 2 * py + pc]
            remote.append(pltpu.make_async_remote_copy(
                src_ref=src, dst_ref=out_refs[i].at[me],
                send_sem=send_sems.at[i, k - 1], recv_sem=recv_sems.at[i, k - 1],
                device_id=(px, py, pc), device_id_type=pl.DeviceIdType.MESH))
        if start:
            local.start()
            for cp in remote:
                cp.start()
        else:
            for cp in remote:
                cp.wait_recv()
            for cp in remote:
                cp.wait_send()
            local.wait()


def _remote_copies(name, carry):
    n = len(carry)
    kinds = [k for k, _ in carry]

    def body(*refs):
        sems = refs[2 * n:]
        _carry_run(True, kinds, refs[:n], refs[n:2 * n], *sems)
        _carry_run(False, kinds, refs[:n], refs[n:2 * n], *sems)

    return pl.pallas_call(
        body, name=name, out_shape=_carry_shapes(carry),
        in_specs=[pl.BlockSpec(memory_space=pl.ANY)] * n, out_specs=[pl.BlockSpec(memory_space=pl.ANY)] * n,
        scratch_shapes=_carry_scratch(carry),
    )(*[a for _, a in carry])


def _cparams(sem):
    return pltpu.CompilerParams(dimension_semantics=sem, vmem_limit_bytes=VMEM_LIMIT)


def _chunk_fwd(name, fn, grid, params, xs, outs, state_shapes, carry=()):
    n_g, n_c = grid
    n_p, n_x, n_o, n_s, n_r = len(params), len(xs), len(outs), len(state_shapes), len(carry)
    kinds = [k for k, _ in carry]

    def body(*refs):
        i = 0
        p_refs = refs[i:i + n_p]; i += n_p
        x_refs = refs[i:i + n_x]; i += n_x
        ci_refs = refs[i:i + n_r]; i += n_r
        o_refs = refs[i:i + n_o]; i += n_o
        sv_refs = refs[i:i + n_s]; i += n_s
        co_refs = refs[i:i + n_r]; i += n_r
        st_refs = refs[i:i + n_s]; i += n_s
        sems = refs[i:]
        g, c = pl.program_id(0), pl.program_id(1)

        if n_r:
            @pl.when((g == 0) & (c == 0))
            def _():
                _carry_run(True, kinds, ci_refs, co_refs, *sems)

        @pl.when(c == 0)
        def _():
            for s in st_refs:
                s[...] = jnp.zeros(s.shape, s.dtype)

        st = [s[...] for s in st_refs]
        ys, new_st = fn(c, [p[...] for p in p_refs], [x[...].astype(F32) for x in x_refs], st)
        for o, y in zip(o_refs, ys):
            o[...] = y.astype(o.dtype)
        for sv, s in zip(sv_refs, st):
            sv[0, 0] = s
        for s_ref, s in zip(st_refs, new_st):
            s_ref[...] = s

        if n_r:
            @pl.when((g == n_g - 1) & (c == n_c - 1))
            def _():
                _carry_run(False, kinds, ci_refs, co_refs, *sems)

    any_spec = pl.BlockSpec(memory_space=pl.ANY)
    in_specs = [pl.BlockSpec(b, m) for _, b, m in params] + [pl.BlockSpec(b, m) for _, b, m in xs] + [any_spec] * n_r
    out_specs = [pl.BlockSpec(b, m) for _, b, m, _ in outs]
    out_shape = [jax.ShapeDtypeStruct(s, d) for s, _, _, d in outs]
    for shp in state_shapes:
        out_specs.append(pl.BlockSpec((1, 1) + shp, lambda g, c, n=len(shp): (g, c) + (0,) * n))
        out_shape.append(jax.ShapeDtypeStruct((n_g, n_c) + shp, F32))
    out_specs += [any_spec] * n_r
    out_shape += _carry_shapes(carry)
    res = pl.pallas_call(
        body, name=name, grid=grid, in_specs=in_specs, out_specs=out_specs, out_shape=out_shape,
        scratch_shapes=[pltpu.VMEM(shp, F32) for shp in state_shapes] + _carry_scratch(carry),
        compiler_params=_cparams(("arbitrary", "arbitrary")),
    )(*[a for a, _, _ in params], *[a for a, _, _ in xs], *[a for _, a in carry])
    return list(res[:n_o]), list(res[n_o:n_o + n_s]), list(res[n_o + n_s:])


def _chunk_bwd(name, fn, grid, params, xs, saved, dys, state_shapes, dx_dtypes, dx_into, carry=(), bwd_fn=None):
    n_g, n_c = grid
    n_p, n_x, n_s, n_y, n_r = len(params), len(xs), len(state_shapes), len(dys), len(carry)
    kinds = [k for k, _ in carry]
    into = sorted(dx_into)
    n_a = len(into)

    def rev(m):
        return lambda g, c: m(g, n_c - 1 - c)

    def body(*refs):
        i = 0
        p_refs = refs[i:i + n_p]; i += n_p
        x_refs = refs[i:i + n_x]; i += n_x
        sv_refs = refs[i:i + n_s]; i += n_s
        dy_refs = refs[i:i + n_y]; i += n_y
        i += n_a
        ci_refs = refs[i:i + n_r]; i += n_r
        dp_refs = refs[i:i + n_p]; i += n_p
        dx_refs = refs[i:i + n_x]; i += n_x
        co_refs = refs[i:i + n_r]; i += n_r
        ds_refs = refs[i:i + n_s]; i += n_s
        sems = refs[i:]
        g, c = pl.program_id(0), pl.program_id(1)
        chunk = n_c - 1 - c

        if n_r:
            @pl.when((g == 0) & (c == 0))
            def _():
                _carry_run(True, kinds, ci_refs, co_refs, *sems)

        @pl.when(c == 0)
        def _():
            for s in ds_refs:
                s[...] = jnp.zeros(s.shape, s.dtype)
            for d in dp_refs:
                d[...] = jnp.zeros(d.shape, d.dtype)

        pv = [p[...] for p in p_refs]
        xv = [x[...].astype(F32) for x in x_refs]
        sv = [s[0, 0] for s in sv_refs]
        dyv, dsv = [d[...].astype(F32) for d in dy_refs], [s[...] for s in ds_refs]
        if bwd_fn is None:
            _, vjp = jax.vjp(lambda p, x, s: fn(chunk, p, x, s), pv, xv, sv)
            dp, dx, ds = vjp((dyv, dsv))
        else:
            dp, dx, ds = bwd_fn(chunk, pv, xv, sv, dyv, dsv)
        for r, v in zip(dp_refs, dp):
            r[...] += v
        for r, v in zip(dx_refs, dx):
            r[...] = v.astype(r.dtype)
        for r, v in zip(ds_refs, ds):
            r[...] = v

        if n_r:
            @pl.when((g == n_g - 1) & (c == n_c - 1))
            def _():
                _carry_run(False, kinds, ci_refs, co_refs, *sems)

    any_spec = pl.BlockSpec(memory_space=pl.ANY)
    in_specs = [pl.BlockSpec(b, rev(m)) for _, b, m in params] + [pl.BlockSpec(b, rev(m)) for _, b, m in xs]
    for shp in state_shapes:
        in_specs.append(pl.BlockSpec((1, 1) + shp, lambda g, c, n=len(shp): (g, n_c - 1 - c) + (0,) * n))
    in_specs += [pl.BlockSpec(b, rev(m)) for _, b, m in dys]
    in_specs += [any_spec] * (n_a + n_r)
    out_specs = [pl.BlockSpec(b, rev(m)) for _, b, m in params] + [pl.BlockSpec(b, rev(m)) for _, b, m in xs]
    out_specs += [any_spec] * n_r
    out_shape = [jax.ShapeDtypeStruct(a.shape, F32) for a, _, _ in params]
    out_shape += [jax.ShapeDtypeStruct(a.shape, d) for (a, _, _), d in zip(xs, dx_dtypes)]
    out_shape += _carry_shapes(carry)
    first_alias = n_p + n_x + n_s + n_y
    aliases = {first_alias + k: n_p + xi for k, xi in enumerate(into)}
    res = pl.pallas_call(
        body, name=name, grid=grid, in_specs=in_specs, out_specs=out_specs, out_shape=out_shape,
        scratch_shapes=[pltpu.VMEM(shp, F32) for shp in state_shapes] + _carry_scratch(carry),
        input_output_aliases=aliases,
        compiler_params=_cparams(("arbitrary", "arbitrary")),
    )(*[a for a, _, _ in params], *[a for a, _, _ in xs], *saved, *[a for a, _, _ in dys],
      *[dx_into[xi] for xi in into], *[a for _, a in carry])
    return list(res[:n_p]), list(res[n_p:n_p + n_x]), list(res[n_p + n_x:])


class _Op:
    def __init__(self, name, fn, grid, params, xs, outs, state_shapes=(), dx_dtypes=None, comm=None, bwd_fn=None):
        self.name, self.fn, self.grid, self.comm, self.bwd_fn = name, fn, grid, comm, bwd_fn
        self.params, self.xs, self.outs, self.state_shapes = params, xs, outs, list(state_shapes)
        self.dx_dtypes = dx_dtypes or [F32] * len(xs)
        reqs = comm.take(name + "_fwd") if comm is not None else []
        self.ys, self.saved, got = _chunk_fwd(name + "_fwd", fn, grid, params, xs, outs, self.state_shapes,
                                              carry=[(k, a) for k, a, _ in reqs])
        for (_, _, done), g in zip(reqs, got):
            done(g)

    def bwd(self, dys, dx_into=None):
        dy_defs = [(d, b, m) for d, (_, b, m, _) in zip(dys, self.outs)]
        reqs = self.comm.take(self.name + "_bwd") if self.comm is not None else []
        dps, dxs, got = _chunk_bwd(self.name + "_bwd", self.fn, self.grid, self.params, self.xs, self.saved, dy_defs,
                                   self.state_shapes, self.dx_dtypes, dx_into or {},
                                   carry=[(k, a) for k, a, _ in reqs], bwd_fn=self.bwd_fn)
        for (_, _, done), g in zip(reqs, got):
            done(g)
        return dps, dxs


def _whole(a):
    nd = a.ndim
    return (a, a.shape, lambda g, c: (0,) * nd)


def _pick(n, prefs):
    for p in prefs:
        if n % p == 0:
            return p
    return n


def _mm_blocks(mode, m, n, k):
    bn = _pick(n, (1408, 1280, 1024, 768, 512, 256, 128))
    if mode == "tn":
        return _pick(m, (1408, 1024, 768, 512, 256, 128)), bn, _pick(k, (2048, 1024, 512, 256, 128))
    bk = k if k <= 3840 else _pick(k, (2816, 1920, 1408, 1024, 512, 256, 128))
    return _pick(m, (1024, 512, 256, 128)), bn, bk


def _matmul(name, a, b, mode, *, add=None, out_dtype=F32, comm=None, a_cols=None, b_cols=None):
    a0, asize = a_cols if a_cols is not None else (0, a.shape[1])
    c0, csize = b_cols if b_cols is not None else (0, b.shape[1])
    if mode == "nn":
        (m, k), n = (a.shape[0], asize), csize
    elif mode == "nt":
        (m, k), n = (a.shape[0], asize), b.shape[0]
        assert k == csize
    else:
        (k, m), n = (a.shape[0], asize), csize
    bm, bn, bk = _mm_blocks(mode, m, n, k)
    assert c0 % (bk if mode == "nt" else bn) == 0 and a0 % (bm if mode == "tn" else bk) == 0
    j0, k0 = (0, c0 // bk) if mode == "nt" else (c0 // bn, 0)
    ia = a0 // (bm if mode == "tn" else bk)
    n_i, n_j, n_k = m // bm, n // bn, k // bk
    dims = {"nn": (((1,), (0,)), ((), ())), "nt": (((1,), (1,)), ((), ())), "tn": (((0,), (0,)), ((), ()))}[mode]
    has_add = add is not None
    reqs = comm.take(name) if comm is not None else []
    carry = [(kind, arr) for kind, arr, _ in reqs]
    kinds = [kind for kind, _ in carry]
    n_r = len(carry)

    def body(*refs):
        i = 2
        a_ref, b_ref = refs[0], refs[1]
        c_ref = refs[i] if has_add else None
        i += has_add
        ci_refs = refs[i:i + n_r]; i += n_r
        o_ref = refs[i]; i += 1
        co_refs = refs[i:i + n_r]; i += n_r
        acc = refs[i]; i += 1
        sems = refs[i:]
        ii, jj, kk = pl.program_id(0), pl.program_id(1), pl.program_id(2)

        if n_r:
            @pl.when((ii == 0) & (jj == 0) & (kk == 0))
            def _():
                _carry_run(True, kinds, ci_refs, co_refs, *sems)

        part = lax.dot_general(a_ref[...].astype(MM_DTYPE), b_ref[...].astype(MM_DTYPE), dims,
                               preferred_element_type=F32)

        def finish(r):
            if has_add:
                r = r + c_ref[...]
            o_ref[...] = r.astype(o_ref.dtype)

        if n_k == 1:
            finish(part)
        else:
            @pl.when(kk == 0)
            def _():
                acc[...] = part

            @pl.when((kk > 0) & (kk < n_k - 1))
            def _():
                acc[...] += part

            @pl.when(kk == n_k - 1)
            def _():
                finish(acc[...] + part)

        if n_r:
            @pl.when((ii == n_i - 1) & (jj == n_j - 1) & (kk == n_k - 1))
            def _():
                _carry_run(False, kinds, ci_refs, co_refs, *sems)

    if mode == "nn":
        a_spec = pl.BlockSpec((bm, bk), lambda i, j, kk: (i, ia + kk))
        b_spec = pl.BlockSpec((bk, bn), lambda i, j, kk: (kk, j0 + j))
    elif mode == "nt":
        a_spec = pl.BlockSpec((bm, bk), lambda i, j, kk: (i, ia + kk))
        b_spec = pl.BlockSpec((bn, bk), lambda i, j, kk: (j, k0 + kk))
    else:
        a_spec = pl.BlockSpec((bk, bm), lambda i, j, kk: (kk, ia + i))
        b_spec = pl.BlockSpec((bk, bn), lambda i, j, kk: (kk, j0 + j))
    any_spec = pl.BlockSpec(memory_space=pl.ANY)
    in_specs, args = [a_spec, b_spec], [a, b]
    if has_add:
        in_specs.append(pl.BlockSpec((bm, bn), lambda i, j, kk: (i, j)))
        args.append(add)
    res = pl.pallas_call(
        body, name=name, grid=(n_i, n_j, n_k), in_specs=in_specs + [any_spec] * n_r,
        out_specs=[pl.BlockSpec((bm, bn), lambda i, j, kk: (i, j))] + [any_spec] * n_r,
        out_shape=[jax.ShapeDtypeStruct((m, n), out_dtype)] + _carry_shapes(carry),
        scratch_shapes=[pltpu.VMEM((bm, bn) if n_k > 1 else (8, LANES), F32)] + _carry_scratch(carry),
        compiler_params=_cparams(("arbitrary", "arbitrary", "arbitrary")),
    )(*args, *[arr for _, arr in carry])
    for (_, _, done), g in zip(reqs, res[1:]):
        done(g)
    return res[0]


def _ln_res_fn(_, p, x, st):
    g, b = p
    xin, m = x
    pre = ALPHA * xin + m
    mu = jnp.mean(pre, -1, keepdims=True)
    d = pre - mu
    var = jnp.mean(d * d, -1, keepdims=True)
    return [d * lax.rsqrt(var + LN_EPS) * g + b], []


def _make_conv_fn(taps, act):
    def fn(_, p, x, st):
        ws, b = p[:taps], p[taps]
        (xin,), (prev,) = x, st
        n = xin.shape[0]
        ext = jnp.concatenate([prev, xin], axis=0)
        y = b
        for k in range(taps):
            y = y + ws[k] * _rows(_roll(ext, taps - 1 - k, 0), 8, 8 + n)
        if act:
            y = _silu(y)
        return [y], [_rows(xin, n - 8, n)]

    return fn


def _ffn_act_fn(_, p, x, st):
    n = x[0].shape[0]
    ys = []
    for half in range(2):
        ws, b = p[4 * half:4 * half + 3], p[4 * half + 3]
        ext = jnp.concatenate([st[half], x[half]], axis=0)
        y = b
        for k in range(3):
            y = y + ws[k] * _rows(_roll(ext, 2 - k, 0), 8, 8 + n)
        ys.append(y)
    return [_silu(ys[0]) * ys[1]], [_rows(x[0], n - 8, n), _rows(x[1], n - 8, n)]


def _ffn_act_bwd(_, p, x, st, dy, dst):
    (da,) = dy
    n, wd = x[0].shape
    rs = FFN_STRIP
    last = n // rs - 1
    zero8 = jnp.zeros((8, wd), F32)
    acc = [[zero8] * 4, [zero8] * 4]
    after = [zero8, zero8]
    strips = [[None] * (n // rs), [None] * (n // rs)]
    for i in reversed(range(n // rs)):
        r0 = rs * i
        taps, ys = [], []
        for half in range(2):
            w0, w1, w2, b = p[4 * half:4 * half + 4]
            xs = jnp.concatenate([st[half] if i == 0 else x[half][r0 - 8:r0], x[half][r0:r0 + rs]], axis=0)
            taps.append((pltpu.roll(xs, 2, 0)[8:], pltpu.roll(xs, 1, 0)[8:], xs[8:]))
            ys.append(b + w2 * taps[half][2] + w1 * taps[half][1] + w0 * taps[half][0])
        g, u = ys
        s = _sigmoid(g)
        d = da[r0:r0 + rs]
        dys = (d * u * (s * (1.0 + g * (1.0 - s))), d * (g * s))
        for half in range(2):
            w0, w1, w2, _ = p[4 * half:4 * half + 4]
            dyh = dys[half]
            for k, v in enumerate((dyh * taps[half][0], dyh * taps[half][1], dyh * taps[half][2], dyh)):
                for r in range(0, rs, 8):
                    acc[half][k] = acc[half][k] + v[r:r + 8]
            dyp = jnp.concatenate([dyh, after[half]], axis=0)
            dxs = w2 * dyh + w1 * pltpu.roll(dyp, rs + 8 - 1, 0)[:rs] + w0 * pltpu.roll(dyp, rs + 8 - 2, 0)[:rs]
            if i == last:
                dxs = jnp.concatenate([dxs[:rs - 8], dxs[rs - 8:] + dst[half]], axis=0)
            strips[half][i] = dxs
            after[half] = dyh[:8]
    dprev = []
    for half in range(2):
        w0, w1 = p[4 * half], p[4 * half + 1]
        head = jnp.concatenate([zero8, after[half]], axis=0)
        dprev.append((w1 * pltpu.roll(head, 16 - 1, 0) + w0 * pltpu.roll(head, 16 - 2, 0))[:8])
    dps = [jnp.sum(a, axis=0, keepdims=True) for half in range(2) for a in acc[half]]
    return dps, [jnp.concatenate(s_, axis=0) for s_ in strips], dprev


def _ssd_fn(_, p, x, st):
    dtb, alog, dsk = p
    xbc, dtr = x
    L = SSD_CHUNK
    tril = _iota((L, L), 0) >= _iota((L, L), 1)
    xs, bm, cm = _cols(xbc, 0, 512), _cols(xbc, 512, 640), _cols(xbc, 640, 768)
    dt = _softplus(dtr + dtb)
    da = dt * (-jnp.exp(alog))
    cs = _sel_dot(tril.astype(BF16), da)
    pick = ((_iota((LANES, 2 * LANES), 0) == 0) & (_iota((LANES, 2 * LANES), 1) < LANES)) | (
        (_iota((LANES, 2 * LANES), 0) == 64) & (_iota((LANES, 2 * LANES), 1) >= LANES))
    pick = pick.astype(BF16)
    tot = jnp.sum(da, axis=0, keepdims=True)
    xc = xs * dt
    xdec = xc * jnp.exp(tot - cs)
    cs_b, xc_b, xdec_b, ecs_b, etot_b, dsk_b, xs_b = (
        _split_cols(v, LANES) for v in (cs, xc, xdec, jnp.exp(cs), jnp.exp(tot), dsk, xs))
    ys, new_st = [], []
    for pr in range(4):
        grp = pr // 2
        c_g = cm * _lane_mask(64 * grp, 64 * grp + 64)
        gmat = _dot_nt(c_g, bm)
        cols2 = _split_cols(_dot_sel(cs_b[pr], pick), LANES)
        yd = jnp.zeros((L, LANES), F32)
        for half in range(2):
            col = cols2[half]
            diff = col - col.T
            dec = jnp.where(tril, jnp.exp(jnp.where(tril, diff, 0.0)), 0.0)
            yd = yd + _dot(gmat * dec, xc_b[pr]) * _lane_mask(64 * half, 64 * half + 64)
        s_in = st[pr]
        y_off = _dot(c_g, s_in) * ecs_b[pr]
        ys.append(yd + y_off + dsk_b[pr] * xs_b[pr])
        new_st.append(s_in * etot_b[pr] + _dot_tn(bm, xdec_b[pr]))
    return [jnp.concatenate(ys, axis=1)], new_st


def _hg_fn(_, p, x, st):
    (lb,) = p
    (xin,) = x
    L = HG_STEP
    n_lvl = L.bit_length() - 1
    hq, hf, hi = _split_cols(xin, 512)
    q = _silu(hq)
    logf = jnp.log(lb + (1.0 - lb) * _sigmoid(hf))
    k = (1.0 - lb) * _sigmoid(-hf)
    ti, si = _iota((L, L), 0), _iota((L, L), 1)
    bc = _sel_dot((ti >= si).astype(BF16), logf)
    tot = jnp.sum(logf, axis=0, keepdims=True)
    tn, sn = _iota((n_lvl * L, 1), 0), _iota((n_lvl * L, L), 1)
    row = tn & (L - 1)
    blk = L >> (tn >> n_lvl)
    piv = row - (row & (blk - 1)) + (blk >> 1)
    bcp_all = _sel_dot((sn == piv).astype(BF16), bc)
    t1 = _iota((L, 1), 0)
    qqs, kks, sames = [], [], []
    for lvl in range(n_lvl):
        size = L >> lvl
        upper = (t1 & (size - 1)) >= size // 2
        bcp = _rows(bcp_all, L * lvl, L * (lvl + 1))
        qqs.append(jnp.where(upper, q * jnp.exp(jnp.where(upper, bc - bcp, 0.0)), 0.0))
        kks.append(jnp.where(upper, 0.0, k * jnp.exp(jnp.where(upper, 0.0, bcp - bc))))
        sames.append((ti >> (n_lvl - lvl)) == (si >> (n_lvl - lvl)))
    qq_b = [_split_cols(v, LANES) for v in qqs]
    kk_b = [_split_cols(v, LANES) for v in kks]
    v_b, diag_b, q_in_b, k_out_b, etot_b = (
        _split_cols(v, LANES) for v in (hi, q * k, q * jnp.exp(bc), k * jnp.exp(tot - bc), jnp.exp(tot)))
    outs, new_st = [], []
    for h in range(4):
        attn = jnp.zeros((L, L), F32)
        for lvl in range(n_lvl):
            attn = attn + jnp.where(sames[lvl], _dot_nt(qq_b[lvl][h], kk_b[lvl][h]), 0.0)
        v = v_b[h]
        out = _dot(attn, v) + jnp.sum(diag_b[h], axis=-1, keepdims=True) * v
        outs.append(out + _dot_nt(q_in_b[h], st[h]))
        new_st.append(st[h] * etot_b[h] + _dot_tn(v, k_out_b[h]))
    return [jnp.concatenate(outs, axis=1)], new_st


def _swa_fn(chunk, p, x, st):
    (sinks,) = p
    (xin,) = x
    q, k, v = _cols(xin, 0, 512), _cols(xin, 512, 640), _cols(xin, 640, 768)
    kp, vp = st
    T = SWA_BLOCK
    kc = jnp.concatenate([kp, k], axis=0)
    vc = jnp.concatenate([vp, v], axis=0)
    qi, kj = _iota((T, 2 * T), 0), _iota((T, 2 * T), 1)
    rel = qi + T - kj
    mask = (rel >= 0) & (rel < T) & ((kj >= T) | (chunk > 0))
    srow = _iota((8, LANES), 0)
    q_b = _split_cols(q, LANES)
    outs = []
    for pr in range(4):
        grp = pr // 2
        gm = _lane_mask(64 * grp, 64 * grp + 64)
        km, vm = kc * gm, vc * gm
        q2 = q_b[pr]
        o2 = jnp.zeros((T, LANES), F32)
        for half in range(2):
            hm = _lane_mask(64 * half, 64 * half + 64)
            qh = q2 * hm
            if half != grp:
                qh = _roll(qh, 64, 1)
            s = _dot_nt(qh, km) * 0.125
            s = jnp.where(mask, s, MASK_VALUE)
            sink = jnp.mean(jnp.sum(jnp.where(srow == 2 * pr + half, sinks, 0.0), axis=0, keepdims=True),
                            axis=-1, keepdims=True)
            mx = lax.stop_gradient(jnp.maximum(jnp.max(s, axis=-1, keepdims=True), sink))
            e = jnp.exp(s - mx)
            den = jnp.sum(e, axis=-1, keepdims=True) + jnp.exp(sink - mx)
            o = _dot(e / den, vm)
            if half != grp:
                o = _roll(o, 64, 1)
            o2 = o2 + o * hm
        outs.append(o2)
    return [jnp.concatenate(outs, axis=1)], [k, v]


def _rg_gate_fn(_, p, x, st):
    wa, ba, wx, bx, lam = p
    (xc,) = x
    r = _sigmoid(_dot(xc, wa) + ba)
    i = _sigmoid(_dot(xc, wx) + bx)
    log_a = -RG_C * r * _softplus(-lam)
    a = jnp.exp(log_a)
    t = jnp.tanh(log_a)
    one_minus_a2 = -2.0 * t / (1.0 - t)
    u = jnp.sqrt(jnp.maximum(one_minus_a2, 0.0)) * (i * xc)
    return [a, u], []


def _rg_scan_fn(_, p, x, st):
    a, u = x
    (prev,) = st
    n = a.shape[0]
    row = _iota((n, 1), 0)
    s = 1
    while s < n:
        keep = row >= s
        a_s, u_s = _roll(a, s, 0), _roll(u, s, 0)
        u = jnp.where(keep, a * u_s + u, u)
        a = jnp.where(keep, a * a_s, a)
        s *= 2
    h_in = jnp.sum(jnp.where(_iota((8, 1), 0) == 7, prev, 0.0), axis=0, keepdims=True)
    h = u + a * h_in
    return [h], [_rows(h, n - 8, n)]


def _ab_post_fn(_, p, x, st):
    nw_ssd, nw_hg = p
    y, o, zg = x
    z, hgate = _split_cols(zg, 512)

    def rms(v, width):
        blocks = _split_cols(v, width)
        return jnp.concatenate([b * lax.rsqrt(jnp.mean(b * b, axis=-1, keepdims=True) + RMS_EPS) for b in blocks],
                               axis=1)

    ya = rms(y * _silu(z), 256) * nw_ssd
    yb = rms(o, 128) * nw_hg * _silu(hgate)
    return [jnp.concatenate([ya, yb], axis=1)], []


def _cd_post_fn(_, p, x, st):
    yc, h, gate = x
    return [jnp.concatenate([yc, h * _gelu_tanh(gate)], axis=1)], []


def _lb_fn(_, p, x, st):
    l0, l1 = x
    mx = lax.stop_gradient(jnp.maximum(l0, l1))
    e0, e1 = jnp.exp(l0 - mx), jnp.exp(l1 - mx)
    s0, s1 = e0 / (e0 + e1), e1 / (e0 + e1)
    return [jnp.clip(s0 - s0, 0.0, 1.0), jnp.clip((s0 + s1) - s0, 0.0, 1.0)], []


def _loss_kernel(y, target):
    t, d = y.shape
    bt = _pick(t, (512, 256, 128))

    def body(y_ref, t_ref, dy_ref, l_ref):
        @pl.when(pl.program_id(0) == 0)
        def _():
            l_ref[...] = jnp.zeros(l_ref.shape, F32)

        e = y_ref[...] - t_ref[...]
        dy_ref[...] = e * (1.0 / d)
        l_ref[...] += jnp.sum(e * e, axis=0, keepdims=True) * (0.5 / d)

    dy, part = pl.pallas_call(
        body, name="loss", grid=(t // bt,),
        in_specs=[pl.BlockSpec((bt, d), lambda i: (i, 0)), pl.BlockSpec((bt, d), lambda i: (i, 0))],
        out_specs=[pl.BlockSpec((bt, d), lambda i: (i, 0)), pl.BlockSpec((1, d), lambda i: (0, 0))],
        out_shape=[jax.ShapeDtypeStruct((t, d), F32), jax.ShapeDtypeStruct((1, d), F32)],
        compiler_params=_cparams(("arbitrary",)),
    )(y, target)
    return dy, jnp.sum(part)


def _adamw_math(parts, w_, m_, v_):
    c1 = 1.0 / (1.0 - ADAM_B1 ** ADAM_STEP)
    c2 = 1.0 / (1.0 - ADAM_B2 ** ADAM_STEP)
    g = parts[0].astype(F32)
    for s in range(1, N_DEV):
        g = g + parts[s].astype(F32)
    nm = ADAM_B1 * m_ + (1.0 - ADAM_B1) * g
    nv = ADAM_B2 * v_ + (1.0 - ADAM_B2) * (g * g)
    return g, -ADAM_LR * ((nm * c1) / (jnp.sqrt(nv * c2) + ADAM_EPS) + ADAM_WD * w_), nm, nv


def _adamw_big(name, parts, w, m, v):
    n_l, r, c = w.shape
    br = _pick(r, (256, 176, 128, 64, 32, 16, 8))

    def body(*refs):
        p_refs, (w_ref, m_ref, v_ref), outs = refs[:n_l], refs[n_l:n_l + 3], refs[n_l + 3:]
        for l in range(n_l):
            @pl.when(pl.program_id(0) == l)
            def _(p_ref=p_refs[l]):
                res = _adamw_math([p_ref[s] for s in range(N_DEV)], w_ref[...], m_ref[...], v_ref[...])
                for ref, val in zip(outs, res):
                    ref[...] = val

    blk = pl.BlockSpec((None, br, c), lambda l, i: (l, i, 0))
    p_specs = [pl.BlockSpec((N_DEV, br, c), lambda l, i, k=k: (0, jnp.where(l == k, i, 0), 0)) for k in range(n_l)]
    return pl.pallas_call(
        body, name=name, grid=(n_l, r // br), in_specs=p_specs + [blk, blk, blk],
        out_specs=[blk] * 4, out_shape=[jax.ShapeDtypeStruct(w.shape, F32)] * 4,
        compiler_params=_cparams(("arbitrary", "arbitrary")),
    )(*parts, w, m, v)


def _adamw_small(name, items):
    n = len(items)

    def body(*refs):
        ins, outs = refs[:4 * n], refs[4 * n:]
        for i in range(n):
            p_ref, w_ref, m_ref, v_ref = ins[4 * i:4 * i + 4]
            res = _adamw_math([p_ref[s] for s in range(N_DEV)], w_ref[...], m_ref[...], v_ref[...])
            for ref, val in zip(outs[4 * i:4 * i + 4], res):
                ref[...] = val

    flat = [a for it in items for a in it]
    out_shape = [jax.ShapeDtypeStruct(it[1].shape, F32) for it in items for _ in range(4)]
    res = pl.pallas_call(
        body, name=name, out_shape=out_shape,
        in_specs=[pl.BlockSpec(memory_space=pltpu.VMEM)] * len(flat),
        out_specs=[pl.BlockSpec(memory_space=pltpu.VMEM)] * len(out_shape),
        compiler_params=pltpu.CompilerParams(vmem_limit_bytes=VMEM_LIMIT),
    )(*flat)
    return [res[4 * i:4 * i + 4] for i in range(n)]


SHARDED = [("ab_w_in", 2), ("ab_w_out", 1), ("cd_w_in", 2), ("cd_w_out", 1), ("ffn_w_up", 2), ("ffn_w_down", 1),
           ("ssd_conv_w", 2), ("rg_conv_w", 2), ("rg_conv_b", 1), ("rg_ba", 1), ("rg_bx", 1), ("rg_lambda", 1),
           ("ffn_conv_w", 2), ("ln_g", 2), ("ln_b", 2)]
REPLICATED = ["ssd_conv_b", "ssd_dt_bias", "ssd_a_log", "ssd_d", "ssd_norm_w", "hg_lower", "hg_norm_w", "swa_sinks",
              "rg_wa", "rg_wx", "ffn_conv_b"]
WEIGHTS = ["ab_w_in", "ssd_conv_w", "ssd_conv_b", "ssd_dt_bias", "ssd_a_log", "ssd_d", "ssd_norm_w", "hg_lower",
           "hg_norm_w", "ab_w_out", "cd_w_in", "swa_sinks", "rg_conv_w", "rg_conv_b", "rg_wa", "rg_ba", "rg_wx",
           "rg_bx", "rg_lambda", "cd_w_out", "ffn_w_up", "ffn_conv_w", "ffn_conv_b", "ffn_w_down", "ln_g", "ln_b"]


def _as2d(a):
    return a.reshape(-1, a.shape[-1])


def _merge_shards(g, axis):
    g = jnp.moveaxis(g, 0, axis)
    shp = g.shape
    return g.reshape(shp[:axis] + (shp[axis] * shp[axis + 1],) + shp[axis + 2:])


def _split_shards(full, axis):
    shp = full.shape
    g = full.reshape(shp[:axis] + (N_DEV, shp[axis] // N_DEV) + shp[axis + 1:])
    return jnp.moveaxis(g, axis, 0)


def _ab_pad(w):
    z, xbc, dt = w[..., 0:512], w[..., 512:1280], w[..., 1280:1288]
    hqfi, hg = w[..., 1288:2824], w[..., 2824:3336]
    return jnp.concatenate([hqfi, jnp.repeat(dt, 64, axis=-1), z, hg, xbc], axis=-1)


def _ab_unpad(d):
    lead = d.shape[:-1]
    dt = d[..., AB_DT:AB_ZG].reshape(lead + (8, 64)).sum(-1)
    z, hg, xbc = d[..., AB_ZG:AB_ZG + 512], d[..., AB_ZG + 512:AB_XBC], d[..., AB_XBC:AB_PAD]
    return jnp.concatenate([z, xbc, dt, d[..., :AB_DT], hg], axis=-1)


def _cd_pad(w):
    return jnp.concatenate([w[..., :768], jnp.zeros(w.shape[:-1] + (256,), w.dtype), w[..., 768:]], axis=-1)


def _cd_unpad(d):
    return jnp.concatenate([d[..., :768], d[..., CD_GATE:CD_PAD]], axis=-1)


def _cat_halves(d):
    return jnp.concatenate(d, axis=1)


def _block_diag(w):
    eye = jnp.eye(8, dtype=w.dtype)
    return jnp.einsum("gij,gh->gihj", w, eye).reshape(512, 512)


def _block_diag_grad(d):
    return jnp.stack([d[64 * g:64 * g + 64, 64 * g:64 * g + 64] for g in range(8)])


def _same(a):
    return a


BIG = {"ab_w_in": (1, _ab_pad, _ab_unpad), "ab_w_out": (0, _same, _same), "cd_w_in": (1, _cd_pad, _cd_unpad),
       "cd_w_out": (0, _same, _same), "ffn_w_up": (1, _same, _cat_halves), "ffn_w_down": (0, _same, _same)}


class _MeshComm:
    def __init__(self, shards):
        self.shards, self.full, self.recv, self.posted = shards, {}, {}, {}

    def post(self, carrier, req):
        self.posted.setdefault(carrier, []).append(req)

    def take(self, carrier):
        return self.posted.pop(carrier, [])

    def take_all(self):
        reqs = [r for name in list(self.posted) for r in self.posted.pop(name)]
        return reqs

    def gather_req(self, name, layer):
        axis, prep, _ = BIG[name]

        def done(got):
            self.full[name, layer] = prep(_merge_shards(got, axis))

        return ("gather", self.shards[name][layer].astype(MM_DTYPE), done)

    def weight(self, name, layer):
        return self.full[name, layer]

    def grad_req(self, name, layer, d):
        axis, _, unprep = BIG[name]

        def done(got):
            self.recv[name, layer] = got

        return ("exchange", _split_shards(unprep(d), axis).astype(MM_DTYPE), done)


class _LocalComm:
    def __init__(self, full):
        self.full_w, self.grads = full, {}

    def post(self, carrier, req):
        pass

    def take(self, carrier):
        return []

    def gather_req(self, name, layer):
        return None

    def weight(self, name, layer):
        return BIG[name][1](self.full_w[name][layer].astype(MM_DTYPE))

    def grad_req(self, name, layer, d):
        self.grads[name, layer] = BIG[name][2](d)
        return None


def _row_vec(v):
    return v.reshape(1, -1)


def _heads64(v):
    return jnp.repeat(v, 64).reshape(1, 512)


def _local_step(x, target, w, comm):
    kinds = ["ab" if layer % 2 == 0 else "cd" for layer in range(DEPTH)]
    in_name = [f"{kinds[layer]}_in{layer // 2}" for layer in range(DEPTH)]
    core_name = [("hg" if layer % 2 == 0 else "swa") + f"{layer // 2}_fwd" for layer in range(DEPTH)]
    comm.post("ssd0_fwd", comm.gather_req("ffn_w_down", 0))
    comm.post(core_name[0], comm.gather_req("ffn_w_up", 0))
    for layer in range(DEPTH - 1):
        nxt, nj = kinds[layer + 1], (layer + 1) // 2
        if kinds[layer] == "ab":
            comm.post(in_name[layer], comm.gather_req(nxt + "_w_in", nj))
            comm.post(core_name[layer], comm.gather_req(nxt + "_w_out", nj))
            comm.post(core_name[layer], comm.gather_req("ffn_w_down", layer + 1))
        else:
            comm.post(in_name[layer], comm.gather_req(nxt + "_w_out", nj))
            comm.post(core_name[layer], comm.gather_req(nxt + "_w_in", nj))
            comm.post(f"ffn_up{layer}", comm.gather_req("ffn_w_down", layer + 1))
        comm.post(f"ffn_act{layer}_fwd", comm.gather_req("ffn_w_up", layer + 1))

    t = x.shape[0]
    bt = _pick(t, (512, 256, 128))
    nb = t // bt
    bs = _pick(t, (256, 128))

    def rowop(name, fn, params, xs, widths_out, out_dtype=F32, dx_dtypes=None):
        outs = [((t, wd), (bt, wd), lambda g, c: (c, 0), out_dtype) for wd in widths_out]
        return _Op(name, fn, (1, nb), params, xs, outs, dx_dtypes=dx_dtypes)

    def rowblk(arr, width, first=0):
        return (arr, (bt, width), lambda g, c: (c, first))

    one_row = lambda g, c: (0, 0)
    lb_op = _Op("hg_lb", _lb_fn, (1, 1), [],
                [(w["hg_lower"][0:1], (1, 512), one_row), (w["hg_lower"][1:2], (1, 512), one_row)],
                [((1, 512), (1, 512), one_row, F32)] * 2)
    lb_all = lb_op.ys

    tape = []
    grads = {}

    def add_grad(name, idx, val):
        grads.setdefault(name, {})[idx] = val

    def dw_matmul(name, a, b, wname, idx, carrier):
        d = _matmul(name, a, b, "tn", out_dtype=MM_DTYPE, comm=comm)
        comm.post(carrier, comm.grad_req(wname, idx, d))

    for layer in range(DEPTH):
        j = layer // 2
        rec = {"x_in": x}
        if layer % 2 == 0:
            h = _matmul(f"ab_in{j}", x, comm.weight("ab_w_in", j), "nn", comm=comm)
            conv_p = [_row_vec(w["ssd_conv_w"][j, k]) for k in range(4)] + [_row_vec(w["ssd_conv_b"][j])]
            conv = _Op(f"ssd_conv{j}", _make_conv_fn(4, True), (3, nb),
                       [(a, (1, 256), lambda g, c: (0, g)) for a in conv_p],
                       [(h, (bt, 256), lambda g, c: (c, AB_XBC // 256 + g))],
                       [((t, 768), (bt, 256), lambda g, c: (c, g), F32)], [(8, 256)], dx_dtypes=[MM_DTYPE])
            ssd = _Op(f"ssd{j}", _ssd_fn, (1, t // SSD_CHUNK),
                      [_whole(_heads64(w["ssd_dt_bias"][j])), _whole(_heads64(w["ssd_a_log"][j])),
                       _whole(_heads64(w["ssd_d"][j]))],
                      [(conv.ys[0], (SSD_CHUNK, 768), lambda g, c: (c, 0)),
                       (h, (SSD_CHUNK, 512), lambda g, c: (c, AB_DT // 512))],
                      [((t, 512), (SSD_CHUNK, 512), lambda g, c: (c, 0), F32)], [(LANES, LANES)] * 4,
                      dx_dtypes=[F32, MM_DTYPE], comm=comm)
            hg = _Op(f"hg{j}", _hg_fn, (1, t // HG_STEP), [_whole(lb_all[j])],
                     [(h, (HG_STEP, AB_DT), lambda g, c: (c, 0))],
                     [((t, 512), (HG_STEP, 512), lambda g, c: (c, 0), F32)], [(LANES, LANES)] * 4,
                     dx_dtypes=[MM_DTYPE], comm=comm)
            post = rowop(f"ab_post{j}", _ab_post_fn,
                         [_whole(_row_vec(w["ssd_norm_w"][j])), _whole(jnp.tile(_row_vec(w["hg_norm_w"][j]), (1, 4)))],
                         [rowblk(ssd.ys[0], 512), rowblk(hg.ys[0], 512), rowblk(h, 1024, AB_ZG // 1024)], [1024],
                         out_dtype=MM_DTYPE, dx_dtypes=[F32, F32, MM_DTYPE])
            rec.update(kind="ab", conv=conv, ssd=ssd, hg=hg, post=post)
        else:
            h = _matmul(f"cd_in{j}", x, comm.weight("cd_w_in", j), "nn", comm=comm)
            swa = _Op(f"swa{j}", _swa_fn, (1, t // SWA_BLOCK),
                      [_whole(jnp.tile(w["swa_sinks"][j].reshape(8, 1), (1, LANES)))],
                      [(h, (SWA_BLOCK, 1024), lambda g, c: (c, 0))],
                      [((t, 512), (SWA_BLOCK, 512), lambda g, c: (c, 0), F32)], [(SWA_BLOCK, LANES)] * 2,
                      dx_dtypes=[MM_DTYPE], comm=comm)
            conv_p = [_row_vec(w["rg_conv_w"][j, k]) for k in range(4)] + [_row_vec(w["rg_conv_b"][j])]
            conv = _Op(f"rg_conv{j}", _make_conv_fn(4, False), (2, nb),
                       [(a, (1, 256), lambda g, c: (0, g)) for a in conv_p],
                       [(h, (bt, 256), lambda g, c: (c, CD_XR // 256 + g))],
                       [((t, 512), (bt, 256), lambda g, c: (c, g), F32)], [(8, 256)], dx_dtypes=[MM_DTYPE])
            gate = rowop(f"rg_gate{j}", _rg_gate_fn,
                         [_whole(_block_diag(w["rg_wa"][j])), _whole(_row_vec(w["rg_ba"][j])),
                          _whole(_block_diag(w["rg_wx"][j])), _whole(_row_vec(w["rg_bx"][j])),
                          _whole(_row_vec(w["rg_lambda"][j]))],
                         [rowblk(conv.ys[0], 512)], [512, 512])
            scan = _Op(f"rg_scan{j}", _rg_scan_fn, (2, t // bs), [],
                       [(gate.ys[0], (bs, 256), lambda g, c: (c, g)), (gate.ys[1], (bs, 256), lambda g, c: (c, g))],
                       [((t, 512), (bs, 256), lambda g, c: (c, g), F32)], [(8, 256)])
            post = rowop(f"cd_post{j}", _cd_post_fn, [],
                         [rowblk(swa.ys[0], 512), rowblk(scan.ys[0], 512), rowblk(h, 512, CD_GATE // 512)], [1024],
                         out_dtype=MM_DTYPE, dx_dtypes=[F32, F32, MM_DTYPE])
            rec.update(kind="cd", swa=swa, conv=conv, gate=gate, scan=scan, post=post)
        kind = rec["kind"]
        ycat = post.ys[0]
        m = _matmul(f"mix_out{layer}", ycat, comm.weight(kind + "_w_out", j), "nn", comm=comm)
        ln1 = rowop(f"ln_a{layer}", _ln_res_fn,
                    [_whole(_row_vec(w["ln_g"][layer, 0])), _whole(_row_vec(w["ln_b"][layer, 0]))],
                    [rowblk(x, 1024), rowblk(m, 1024)], [1024], dx_dtypes=[F32, MM_DTYPE])
        x1 = ln1.ys[0]
        hu = _matmul(f"ffn_up{layer}", x1, comm.weight("ffn_w_up", layer), "nn", comm=comm)
        n_fb = FFN_DIM // FFN_BLK
        taps = [_row_vec(w["ffn_conv_w"][layer, k]) for k in range(3)] + [_row_vec(w["ffn_conv_b"][layer])]
        act = _Op(f"ffn_act{layer}", _ffn_act_fn, (n_fb, nb),
                  [(a, (1, FFN_BLK), lambda g, c: (0, g)) for a in taps]
                  + [(a, (1, FFN_BLK), lambda g, c: (0, n_fb + g)) for a in taps],
                  [(hu, (bt, FFN_BLK), lambda g, c: (c, g)), (hu, (bt, FFN_BLK), lambda g, c: (c, n_fb + g))],
                  [((t, FFN_DIM), (bt, FFN_BLK), lambda g, c: (c, g), MM_DTYPE)], [(8, FFN_BLK)] * 2,
                  dx_dtypes=[MM_DTYPE, MM_DTYPE], comm=comm, bwd_fn=_ffn_act_bwd)
        a = act.ys[0]
        f = _matmul(f"ffn_down{layer}", a, comm.weight("ffn_w_down", layer), "nn", comm=comm)
        ln2 = rowop(f"ln_f{layer}", _ln_res_fn,
                    [_whole(_row_vec(w["ln_g"][layer, 1])), _whole(_row_vec(w["ln_b"][layer, 1]))],
                    [rowblk(x1, 1024), rowblk(f, 1024)], [1024], dx_dtypes=[F32, MM_DTYPE])
        rec.update(ycat=ycat, ln1=ln1, x1=x1, act=act, a=a, ln2=ln2)
        tape.append(rec)
        x = ln2.ys[0]

    dx, loss = _loss_kernel(x, target)

    d_lb = [jnp.zeros((1, 512), F32), jnp.zeros((1, 512), F32)]
    for layer in reversed(range(DEPTH)):
        j = layer // 2
        rec = tape[layer]
        (dg, db), (dx1_res, df) = rec["ln2"].bwd([dx])
        add_grad("ln_g", (layer, 1), dg[0]); add_grad("ln_b", (layer, 1), db[0])
        dw_matmul(f"ffn_down_dw{layer}", rec["a"], df, "ffn_w_down", layer, f"ffn_act{layer}_bwd")
        da = _matmul(f"ffn_down_dx{layer}", df, comm.weight("ffn_w_down", layer), "nt", comm=comm)
        dpa, (dhg, dhu) = rec["act"].bwd([da])
        halves = [jnp.concatenate([dpa[k][0, :FFN_DIM], dpa[4 + k][0, FFN_DIM:]]) for k in range(4)]
        add_grad("ffn_conv_w", layer, jnp.stack(halves[:3]))
        add_grad("ffn_conv_b", layer, halves[3])
        core_bwd = ("hg" if rec["kind"] == "ab" else "swa") + f"{j}_bwd"
        gate_cols, up_cols = (0, FFN_DIM), (FFN_DIM, FFN_DIM)
        dw_up = (_matmul(f"ffn_up_dw_g{layer}", rec["x1"], dhg, "tn", out_dtype=MM_DTYPE, comm=comm, b_cols=gate_cols),
                 _matmul(f"ffn_up_dw_u{layer}", rec["x1"], dhu, "tn", out_dtype=MM_DTYPE, comm=comm, b_cols=up_cols))
        comm.post(core_bwd, comm.grad_req("ffn_w_up", layer, dw_up))
        w_up = comm.weight("ffn_w_up", layer)
        dx1 = _matmul(f"ffn_up_dx_g{layer}", dhg, w_up, "nt", a_cols=gate_cols, b_cols=gate_cols, add=dx1_res,
                      comm=comm)
        dx1 = _matmul(f"ffn_up_dx_u{layer}", dhu, w_up, "nt", a_cols=up_cols, b_cols=up_cols, add=dx1, comm=comm)
        (dg, db), (dx_res, dm) = rec["ln1"].bwd([dx1])
        add_grad("ln_g", (layer, 0), dg[0]); add_grad("ln_b", (layer, 0), db[0])
        kind = rec["kind"]
        dw_matmul(f"mix_out_dw{layer}", rec["ycat"], dm, kind + "_w_out", j, f"{kind}_in_dw{j}")
        dycat = _matmul(f"mix_out_dx{layer}", dm, comm.weight(kind + "_w_out", j), "nt", comm=comm)
        if kind == "ab":
            (dnw_s, dnw_h), (dy_ssd, do_hg, dh) = rec["post"].bwd([dycat])
            add_grad("ssd_norm_w", j, dnw_s[0]); add_grad("hg_norm_w", j, dnw_h[0].reshape(4, LANES).sum(0))
            (dlb,), (dh,) = rec["hg"].bwd([do_hg], dx_into={0: dh})
            d_lb[j] = dlb
            (ddtb, dalog, ddsk), (dxbc_c, dh) = rec["ssd"].bwd([dy_ssd], dx_into={1: dh})
            add_grad("ssd_dt_bias", j, ddtb[0].reshape(8, 64).sum(-1))
            add_grad("ssd_a_log", j, dalog[0].reshape(8, 64).sum(-1))
            add_grad("ssd_d", j, ddsk[0].reshape(8, 64).sum(-1))
            dcp, (dh,) = rec["conv"].bwd([dxbc_c], dx_into={0: dh})
            add_grad("ssd_conv_w", j, jnp.stack([dcp[k][0] for k in range(4)]))
            add_grad("ssd_conv_b", j, dcp[4][0])
        else:
            _, (dyc, dhs, dh) = rec["post"].bwd([dycat])
            _, (da_s, du_s) = rec["scan"].bwd([dhs])
            (dwa, dba, dwx, dbx, dlam), (dxc,) = rec["gate"].bwd([da_s, du_s])
            add_grad("rg_wa", j, _block_diag_grad(dwa)); add_grad("rg_wx", j, _block_diag_grad(dwx))
            add_grad("rg_ba", j, dba[0]); add_grad("rg_bx", j, dbx[0]); add_grad("rg_lambda", j, dlam[0])
            dcp, (dh,) = rec["conv"].bwd([dxc], dx_into={0: dh})
            add_grad("rg_conv_w", j, jnp.stack([dcp[k][0] for k in range(4)]))
            add_grad("rg_conv_b", j, dcp[4][0])
            (dsink,), (dh,) = rec["swa"].bwd([dyc], dx_into={0: dh})
            add_grad("swa_sinks", j, dsink.sum(-1))
        dw_matmul(f"{kind}_in_dw{j}", rec["x_in"], dh, kind + "_w_in", j,
                  f"ffn_act{layer - 1}_bwd" if layer > 0 else f"{kind}_in_dx{j}")
        dx = _matmul(f"{kind}_in_dx{j}", dh, comm.weight(kind + "_w_in", j), "nt", add=dx_res, comm=comm)

    _, (dl0, dl1) = lb_op.bwd(d_lb)
    out = {"hg_lower": jnp.concatenate([dl0, dl1], axis=0)}
    for name, parts in grads.items():
        keys = sorted(parts)
        if isinstance(keys[0], tuple):
            out[name] = jnp.stack([jnp.stack([parts[(l, s)] for s in range(2)]) for l in range(DEPTH)])
        else:
            out[name] = jnp.stack([parts[k] for k in keys])
    return loss, dx, out


def _local_step_full(x, target, full):
    comm = _LocalComm(full)
    loss, dx, grads = _local_step(x, target, {n: a for n, a in full.items() if n not in BIG}, comm)
    for name in BIG:
        grads[name] = jnp.stack([comm.grads[name, l] for l in range(full[name].shape[0])])
    return loss, dx, grads


def kernel(x, ab_w_in, ssd_conv_w, ssd_conv_b, ssd_dt_bias, ssd_a_log, ssd_d, ssd_norm_w, hg_lower, hg_norm_w, ab_w_out, cd_w_in, swa_sinks, rg_conv_w, rg_conv_b, rg_wa, rg_ba, rg_wx, rg_bx, rg_lambda, cd_w_out, ffn_w_up, ffn_conv_w, ffn_conv_b, ffn_w_down, ln_g, ln_b, loss_target, m_ab_w_in, m_ssd_conv_w, m_ssd_conv_b, m_ssd_dt_bias, m_ssd_a_log, m_ssd_d, m_ssd_norm_w, m_hg_lower, m_hg_norm_w, m_ab_w_out, m_cd_w_in, m_swa_sinks, m_rg_conv_w, m_rg_conv_b, m_rg_wa, m_rg_ba, m_rg_wx, m_rg_bx, m_rg_lambda, m_cd_w_out, m_ffn_w_up, m_ffn_conv_w, m_ffn_conv_b, m_ffn_w_down, m_ln_g, m_ln_b, v_ab_w_in, v_ssd_conv_w, v_ssd_conv_b, v_ssd_dt_bias, v_ssd_a_log, v_ssd_d, v_ssd_norm_w, v_hg_lower, v_hg_norm_w, v_ab_w_out, v_cd_w_in, v_swa_sinks, v_rg_conv_w, v_rg_conv_b, v_rg_wa, v_rg_ba, v_rg_wx, v_rg_bx, v_rg_lambda, v_cd_w_out, v_ffn_w_up, v_ffn_conv_w, v_ffn_conv_b, v_ffn_w_down, v_ln_g, v_ln_b):
    args = dict(locals())
    wts = {n: args[n] for n in WEIGHTS}
    mom = {n: args["m_" + n] for n in WEIGHTS}
    var = {n: args["v_" + n] for n in WEIGHTS}
    axis = dict(SHARDED)
    small = [n for n, _ in SHARDED if n not in BIG]
    comm = _MeshComm(wts)

    def run(name, reqs):
        for (_, _, done), got in zip(reqs, _remote_copies(name, [(k, a) for k, a, _ in reqs])):
            done(got)

    full = {n: wts[n] for n in REPLICATED}

    def keep_small(n):
        def done(got):
            full[n] = _merge_shards(got.reshape((N_DEV,) + wts[n].shape), axis[n])
        return ("gather", _as2d(wts[n]), done)

    run("gather_first", [comm.gather_req("ab_w_in", 0), keep_small("ssd_conv_w")])
    for req in [comm.gather_req("ab_w_out", 0)] + [keep_small(n) for n in small if n != "ssd_conv_w"]:
        comm.post("ab_in0", req)

    loss, grad_x, grads = _local_step(x[0], loss_target[0], full, comm)
    loss = lax.psum(loss, ("x", "y", "c"))

    parts = {}

    def keep_parts(n, kind, arr):
        return (kind, arr, lambda got: parts.__setitem__(n, got))

    last = comm.take_all()
    last += [keep_parts(n, "exchange", _split_shards(grads[n], axis[n]).reshape((N_DEV,) + _as2d(wts[n]).shape))
             for n in small]
    last += [keep_parts(n, "gather", _as2d(grads[n])) for n in REPLICATED]
    run("exchange_last", last)

    new = {}
    for n in BIG:
        new[n] = _adamw_big("adamw_" + n, [comm.recv[n, l] for l in range(wts[n].shape[0])], wts[n], mom[n], var[n])
    names = small + REPLICATED
    res = _adamw_small("adamw_small", [(parts[n], _as2d(wts[n]), _as2d(mom[n]), _as2d(var[n])) for n in names])
    for n, r in zip(names, res):
        new[n] = [a.reshape(wts[n].shape) for a in r]

    outs = [loss, grad_x[None]]
    for kind in range(4):
        outs += [new[n][kind] for n in WEIGHTS]
    return tuple(outs)
```

```python
import math

import jax
import jax.numpy as jnp
from jax import lax
from jax.experimental import pallas as pl
from jax.experimental.pallas import tpu as pltpu

F32 = jnp.float32
BF16 = jnp.bfloat16
MM_DTYPE = BF16

DEPTH = 4
N_DEV = 8
LN_EPS = 1e-5
RMS_EPS = 1e-6
MASK_VALUE = -1e9
ALPHA = (2 * DEPTH) ** 0.25
RG_C = 8.0
FFN_DIM = 2816
SSD_CHUNK = 128
HG_STEP = 128
SWA_BLOCK = 128
LANES = 128
VMEM_LIMIT = 56 * 1024 * 1024

ADAM_LR, ADAM_B1, ADAM_B2, ADAM_EPS, ADAM_WD, ADAM_STEP = 0.001, 0.9, 0.999, 1e-08, 0.01, 10

AB_HEADS, AB_DT, AB_ZG, AB_XBC, AB_PAD = 0, 1536, 2048, 3072, 3840
CD_QKV, CD_GATE, CD_XR, CD_PAD = 0, 1024, 1536, 2048
FFN_BLK = 256
FFN_ROWS = 1024
FFN_STRIP = 32


def _cols(x, lo, hi):
    n = x.shape[1]

    @jax.custom_vjp
    def f(x):
        return x[:, lo:hi]

    def bwd(_, g):
        parts = []
        if lo > 0:
            parts.append(jnp.zeros((g.shape[0], lo), g.dtype))
        parts.append(g)
        if hi < n:
            parts.append(jnp.zeros((g.shape[0], n - hi), g.dtype))
        return (jnp.concatenate(parts, axis=1) if len(parts) > 1 else g,)

    f.defvjp(lambda x: (f(x), None), bwd)
    return f(x)


def _split_cols(x, width):
    n = x.shape[1] // width

    @jax.custom_vjp
    def f(x):
        return tuple(x[:, width * i:width * (i + 1)] for i in range(n))

    f.defvjp(lambda x: (f(x), None), lambda _, gs: (jnp.concatenate(gs, axis=1),))
    return f(x)


def _rows(x, lo, hi):
    n = x.shape[0]

    @jax.custom_vjp
    def f(x):
        return x[lo:hi, :]

    def bwd(_, g):
        parts = []
        if lo > 0:
            parts.append(jnp.zeros((lo, g.shape[1]), g.dtype))
        parts.append(g)
        if hi < n:
            parts.append(jnp.zeros((n - hi, g.shape[1]), g.dtype))
        return (jnp.concatenate(parts, axis=0) if len(parts) > 1 else g,)

    f.defvjp(lambda x: (f(x), None), bwd)
    return f(x)


def _roll(x, shift, axis):
    n = x.shape[axis]
    shift = shift % n
    if shift == 0:
        return x

    @jax.custom_vjp
    def f(x):
        return pltpu.roll(x, shift, axis)

    f.defvjp(lambda x: (f(x), None), lambda _, g: (pltpu.roll(g, n - shift, axis),))
    return f(x)


def _dot(a, b):
    return lax.dot_general(a, b, (((1,), (0,)), ((), ())), preferred_element_type=F32)


def _dot_nt(a, b):
    return lax.dot_general(a, b, (((1,), (1,)), ((), ())), preferred_element_type=F32)


def _dot_tn(a, b):
    return lax.dot_general(a, b, (((0,), (0,)), ((), ())), preferred_element_type=F32)


def _split3(x):
    hi = x.astype(BF16)
    r = x - hi.astype(F32)
    mid = r.astype(BF16)
    return hi, mid, (r - mid.astype(F32)).astype(BF16)


def _sel_dot(sel, x):
    def run(mat, v, dims):
        n = v.shape[1]
        y = lax.dot_general(mat, jnp.concatenate(_split3(v), axis=1), dims, preferred_element_type=F32)
        return y[:, :n] + y[:, n:2 * n] + y[:, 2 * n:]

    @jax.custom_vjp
    def f(sel, x):
        return run(sel, x, (((1,), (0,)), ((), ())))

    def bwd(sel, g):
        return jnp.zeros_like(sel), run(sel, g, (((0,), (0,)), ((), ())))

    f.defvjp(lambda sel, x: (f(sel, x), sel), bwd)
    return f(sel, x)


def _dot_sel(x, sel):
    def run(v, mat, dims):
        m = v.shape[0]
        y = lax.dot_general(jnp.concatenate(_split3(v), axis=0), mat, dims, preferred_element_type=F32)
        return y[:m] + y[m:2 * m] + y[2 * m:]

    @jax.custom_vjp
    def f(x, sel):
        return run(x, sel, (((1,), (0,)), ((), ())))

    def bwd(sel, g):
        return run(g, sel, (((1,), (1,)), ((), ()))), jnp.zeros_like(sel)

    f.defvjp(lambda x, sel: (f(x, sel), sel), bwd)
    return f(x, sel)


def _sigmoid(x):
    return 0.5 * jnp.tanh(0.5 * x) + 0.5


def _silu(x):
    h = 0.5 * x
    return h + h * jnp.tanh(h)


def _softplus(x):
    return jnp.maximum(x, 0.0) + jnp.log(1.0 + jnp.exp(-jnp.abs(x)))


def _gelu_tanh(x):
    c = math.sqrt(2.0 / math.pi)
    return 0.5 * x * (1.0 + jnp.tanh(c * (x + 0.044715 * (x * x * x))))


def _iota(shape, axis):
    return lax.broadcasted_iota(jnp.int32, shape, axis)


def _lane_mask(lo, hi, width=LANES):
    lane = _iota((1, width), 1)
    return ((lane >= lo) & (lane < hi)).astype(F32)


def _mesh_pos():
    return lax.axis_index("x"), lax.axis_index("y"), lax.axis_index("c")


def _carry_shapes(carry):
    return [jax.ShapeDtypeStruct((N_DEV,) + a.shape if kind == "gather" else a.shape, a.dtype) for kind, a in carry]


def _carry_scratch(carry):
    n = len(carry)
    if n == 0:
        return []
    return [pltpu.SemaphoreType.DMA((n, N_DEV - 1)), pltpu.SemaphoreType.DMA((n, N_DEV - 1)),
            pltpu.SemaphoreType.DMA((n,))]


def _carry_run(start, kinds, in_refs, out_refs, send_sems, recv_sems, local_sems):
    x, y, cc = _mesh_pos()
    me = 4 * x + 2 * y + cc
    for i, kind in enumerate(kinds):
        mine = in_refs[i] if kind == "gather" else in_refs[i].at[me]
        local = pltpu.make_async_copy(mine, out_refs[i].at[me], local_sems.at[i])
        remote = []
        for k in range(1, N_DEV):
            px, py, pc = x ^ ((k >> 2) & 1), y ^ ((k >> 1) & 1), cc ^ (k & 1)
            src = in_refs[i] if kind == "gather" else in_refs[i].at[4 * px + 2 * py + pc]
            remote.append(pltpu.make_async_remote_copy(
                src_ref=src, dst_ref=out_refs[i].at[me],
                send_sem=send_sems.at[i, k - 1], recv_sem=recv_sems.at[i, k - 1],
                device_id=(px, py, pc), device_id_type=pl.DeviceIdType.MESH))
        if start:
            local.start()
            for cp in remote:
                cp.start()
        else:
            for cp in remote:
                cp.wait_recv()
            for cp in remote:
                cp.wait_send()
            local.wait()


def _remote_copies(name, carry):
    n = len(carry)
    kinds = [k for k, _ in carry]

    def body(*refs):
        sems = refs[2 * n:]
        _carry_run(True, kinds, refs[:n], refs[n:2 * n], *sems)
        _carry_run(False, kinds, refs[:n], refs[n:2 * n], *sems)

    return pl.pallas_call(
        body, name=name, out_shape=_carry_shapes(carry),
        in_specs=[pl.BlockSpec(memory_space=pl.ANY)] * n, out_specs=[pl.BlockSpec(memory_space=pl.ANY)] * n,
        scratch_shapes=_carry_scratch(carry),
    )(*[a for _, a in carry])


def _cparams(sem):
    return pltpu.CompilerParams(dimension_semantics=sem, vmem_limit_bytes=VMEM_LIMIT)


def _chunk_fwd(name, fn, grid, params, xs, outs, state_shapes, carry=()):
    n_g, n_c = grid
    n_p, n_x, n_o, n_s, n_r = len(params), len(xs), len(outs), len(state_shapes), len(carry)
    kinds = [k for k, _ in carry]

    def body(*refs):
        i = 0
        p_refs = refs[i:i + n_p]; i += n_p
        x_refs = refs[i:i + n_x]; i += n_x
        ci_refs = refs[i:i + n_r]; i += n_r
        o_refs = refs[i:i + n_o]; i += n_o
        sv_refs = refs[i:i + n_s]; i += n_s
        co_refs = refs[i:i + n_r]; i += n_r
        st_refs = refs[i:i + n_s]; i += n_s
        sems = refs[i:]
        g, c = pl.program_id(0), pl.program_id(1)

        if n_r:
            @pl.when((g == 0) & (c == 0))
            def _():
                _carry_run(True, kinds, ci_refs, co_refs, *sems)

        @pl.when(c == 0)
        def _():
            for s in st_refs:
                s[...] = jnp.zeros(s.shape, s.dtype)

        st = [s[...] for s in st_refs]
        ys, new_st = fn(c, [p[...] for p in p_refs], [x[...].astype(F32) for x in x_refs], st)
        for o, y in zip(o_refs, ys):
            o[...] = y.astype(o.dtype)
        for sv, s in zip(sv_refs, st):
            sv[0, 0] = s
        for s_ref, s in zip(st_refs, new_st):
            s_ref[...] = s

        if n_r:
            @pl.when((g == n_g - 1) & (c == n_c - 1))
            def _():
                _carry_run(False, kinds, ci_refs, co_refs, *sems)

    any_spec = pl.BlockSpec(memory_space=pl.ANY)
    in_specs = [pl.BlockSpec(b, m) for _, b, m in params] + [pl.BlockSpec(b, m) for _, b, m in xs] + [any_spec] * n_r
    out_specs = [pl.BlockSpec(b, m) for _, b, m, _ in outs]
    out_shape = [jax.ShapeDtypeStruct(s, d) for s, _, _, d in outs]
    for shp in state_shapes:
        out_specs.append(pl.BlockSpec((1, 1) + shp, lambda g, c, n=len(shp): (g, c) + (0,) * n))
        out_shape.append(jax.ShapeDtypeStruct((n_g, n_c) + shp, F32))
    out_specs += [any_spec] * n_r
    out_shape += _carry_shapes(carry)
    res = pl.pallas_call(
        body, name=name, grid=grid, in_specs=in_specs, out_specs=out_specs, out_shape=out_shape,
        scratch_shapes=[pltpu.VMEM(shp, F32) for shp in state_shapes] + _carry_scratch(carry),
        compiler_params=_cparams(("arbitrary", "arbitrary")),
    )(*[a for a, _, _ in params], *[a for a, _, _ in xs], *[a for _, a in carry])
    return list(res[:n_o]), list(res[n_o:n_o + n_s]), list(res[n_o + n_s:])


def _chunk_bwd(name, fn, grid, params, xs, saved, dys, state_shapes, dx_dtypes, dx_into, carry=(), bwd_fn=None):
    n_g, n_c = grid
    n_p, n_x, n_s, n_y, n_r = len(params), len(xs), len(state_shapes), len(dys), len(carry)
    kinds = [k for k, _ in carry]
    into = sorted(dx_into)
    n_a = len(into)

    def rev(m):
        return lambda g, c: m(g, n_c - 1 - c)

    def body(*refs):
        i = 0
        p_refs = refs[i:i + n_p]; i += n_p
        x_refs = refs[i:i + n_x]; i += n_x
        sv_refs = refs[i:i + n_s]; i += n_s
        dy_refs = refs[i:i + n_y]; i += n_y
        i += n_a
        ci_refs = refs[i:i + n_r]; i += n_r
        dp_refs = refs[i:i + n_p]; i += n_p
        dx_refs = refs[i:i + n_x]; i += n_x
        co_refs = refs[i:i + n_r]; i += n_r
        ds_refs = refs[i:i + n_s]; i += n_s
        sems = refs[i:]
        g, c = pl.program_id(0), pl.program_id(1)
        chunk = n_c - 1 - c

        if n_r:
            @pl.when((g == 0) & (c == 0))
            def _():
                _carry_run(True, kinds, ci_refs, co_refs, *sems)

        @pl.when(c == 0)
        def _():
            for s in ds_refs:
                s[...] = jnp.zeros(s.shape, s.dtype)
            for d in dp_refs:
                d[...] = jnp.zeros(d.shape, d.dtype)

        pv = [p[...] for p in p_refs]
        xv = [x[...].astype(F32) for x in x_refs]
        sv = [s[0, 0] for s in sv_refs]
        dyv, dsv = [d[...].astype(F32) for d in dy_refs], [s[...] for s in ds_refs]
        if bwd_fn is None:
            _, vjp = jax.vjp(lambda p, x, s: fn(chunk, p, x, s), pv, xv, sv)
            dp, dx, ds = vjp((dyv, dsv))
        else:
            dp, dx, ds = bwd_fn(chunk, pv, xv, sv, dyv, dsv)
        for r, v in zip(dp_refs, dp):
            r[...] += v
        for r, v in zip(dx_refs, dx):
            r[...] = v.astype(r.dtype)
        for r, v in zip(ds_refs, ds):
            r[...] = v

        if n_r:
            @pl.when((g == n_g - 1) & (c == n_c - 1))
            def _():
                _carry_run(False, kinds, ci_refs, co_refs, *sems)

    any_spec = pl.BlockSpec(memory_space=pl.ANY)
    in_specs = [pl.BlockSpec(b, rev(m)) for _, b, m in params] + [pl.BlockSpec(b, rev(m)) for _, b, m in xs]
    for shp in state_shapes:
        in_specs.append(pl.BlockSpec((1, 1) + shp, lambda g, c, n=len(shp): (g, n_c - 1 - c) + (0,) * n))
    in_specs += [pl.BlockSpec(b, rev(m)) for _, b, m in dys]
    in_specs += [any_spec] * (n_a + n_r)
    out_specs = [pl.BlockSpec(b, rev(m)) for _, b, m in params] + [pl.BlockSpec(b, rev(m)) for _, b, m in xs]
    out_specs += [any_spec] * n_r
    out_shape = [jax.ShapeDtypeStruct(a.shape, F32) for a, _, _ in params]
    out_shape += [jax.ShapeDtypeStruct(a.shape, d) for (a, _, _), d in zip(xs, dx_dtypes)]
    out_shape += _carry_shapes(carry)
    first_alias = n_p + n_x + n_s + n_y
    aliases = {first_alias + k: n_p + xi for k, xi in enumerate(into)}
    res = pl.pallas_call(
        body, name=name, grid=grid, in_specs=in_specs, out_specs=out_specs, out_shape=out_shape,
        scratch_shapes=[pltpu.VMEM(shp, F32) for shp in state_shapes] + _carry_scratch(carry),
        input_output_aliases=aliases,
        compiler_params=_cparams(("arbitrary", "arbitrary")),
    )(*[a for a, _, _ in params], *[a for a, _, _ in xs], *saved, *[a for a, _, _ in dys],
      *[dx_into[xi] for xi in into], *[a for _, a in carry])
    return list(res[:n_p]), list(res[n_p:n_p + n_x]), list(res[n_p + n_x:])


class _Op:
    def __init__(self, name, fn, grid, params, xs, outs, state_shapes=(), dx_dtypes=None, comm=None, bwd_fn=None):
        self.name, self.fn, self.grid, self.comm, self.bwd_fn = name, fn, grid, comm, bwd_fn
        self.params, self.xs, self.outs, self.state_shapes = params, xs, outs, list(state_shapes)
        self.dx_dtypes = dx_dtypes or [F32] * len(xs)
        reqs = comm.take(name + "_fwd") if comm is not None else []
        self.ys, self.saved, got = _chunk_fwd(name + "_fwd", fn, grid, params, xs, outs, self.state_shapes,
                                              carry=[(k, a) for k, a, _ in reqs])
        for (_, _, done), g in zip(reqs, got):
            done(g)

    def bwd(self, dys, dx_into=None):
        dy_defs = [(d, b, m) for d, (_, b, m, _) in zip(dys, self.outs)]
        reqs = self.comm.take(self.name + "_bwd") if self.comm is not None else []
        dps, dxs, got = _chunk_bwd(self.name + "_bwd", self.fn, self.grid, self.params, self.xs, self.saved, dy_defs,
                                   self.state_shapes, self.dx_dtypes, dx_into or {},
                                   carry=[(k, a) for k, a, _ in reqs], bwd_fn=self.bwd_fn)
        for (_, _, done), g in zip(reqs, got):
            done(g)
        return dps, dxs


def _whole(a):
    nd = a.ndim
    return (a, a.shape, lambda g, c: (0,) * nd)


def _pick(n, prefs):
    for p in prefs:
        if n % p == 0:
            return p
    return n


def _mm_blocks(mode, m, n, k):
    bn = _pick(n, (1408, 1280, 1024, 768, 512, 256, 128))
    if mode == "tn":
        return _pick(m, (1408, 1024, 768, 512, 256, 128)), bn, _pick(k, (2048, 1024, 512, 256, 128))
    bk = k if k <= 3840 else _pick(k, (2816, 1920, 1408, 1024, 512, 256, 128))
    return _pick(m, (1024, 512, 256, 128)), bn, bk


def _matmul(name, a, b, mode, *, add=None, out_dtype=F32, comm=None, a_cols=None, b_cols=None):
    a0, asize = a_cols if a_cols is not None else (0, a.shape[1])
    c0, csize = b_cols if b_cols is not None else (0, b.shape[1])
    if mode == "nn":
        (m, k), n = (a.shape[0], asize), csize
    elif mode == "nt":
        (m, k), n = (a.shape[0], asize), b.shape[0]
        assert k == csize
    else:
        (k, m), n = (a.shape[0], asize), csize
    bm, bn, bk = _mm_blocks(mode, m, n, k)
    assert c0 % (bk if mode == "nt" else bn) == 0 and a0 % (bm if mode == "tn" else bk) == 0
    j0, k0 = (0, c0 // bk) if mode == "nt" else (c0 // bn, 0)
    ia = a0 // (bm if mode == "tn" else bk)
    n_i, n_j, n_k = m // bm, n // bn, k // bk
    dims = {"nn": (((1,), (0,)), ((), ())), "nt": (((1,), (1,)), ((), ())), "tn": (((0,), (0,)), ((), ()))}[mode]
    has_add = add is not None
    reqs = comm.take(name) if comm is not None else []
    carry = [(kind, arr) for kind, arr, _ in reqs]
    kinds = [kind for kind, _ in carry]
    n_r = len(carry)

    def body(*refs):
        i = 2
        a_ref, b_ref = refs[0], refs[1]
        c_ref = refs[i] if has_add else None
        i += has_add
        ci_refs = refs[i:i + n_r]; i += n_r
        o_ref = refs[i]; i += 1
        co_refs = refs[i:i + n_r]; i += n_r
        acc = refs[i]; i += 1
        sems = refs[i:]
        ii, jj, kk = pl.program_id(0), pl.program_id(1), pl.program_id(2)

        if n_r:
            @pl.when((ii == 0) & (jj == 0) & (kk == 0))
            def _():
                _carry_run(True, kinds, ci_refs, co_refs, *sems)

        part = lax.dot_general(a_ref[...].astype(MM_DTYPE), b_ref[...].astype(MM_DTYPE), dims,
                               preferred_element_type=F32)

        def finish(r):
            if has_add:
                r = r + c_ref[...]
            o_ref[...] = r.astype(o_ref.dtype)

        if n_k == 1:
            finish(part)
        else:
            @pl.when(kk == 0)
            def _():
                acc[...] = part

            @pl.when((kk > 0) & (kk < n_k - 1))
            def _():
                acc[...] += part

            @pl.when(kk == n_k - 1)
            def _():
                finish(acc[...] + part)

        if n_r:
            @pl.when((ii == n_i - 1) & (jj == n_j - 1) & (kk == n_k - 1))
            def _():
                _carry_run(False, kinds, ci_refs, co_refs, *sems)

    if mode == "nn":
        a_spec = pl.BlockSpec((bm, bk), lambda i, j, kk: (i, ia + kk))
        b_spec = pl.BlockSpec((bk, bn), lambda i, j, kk: (kk, j0 + j))
    elif mode == "nt":
        a_spec = pl.BlockSpec((bm, bk), lambda i, j, kk: (i, ia + kk))
        b_spec = pl.BlockSpec((bn, bk), lambda i, j, kk: (j, k0 + kk))
    else:
        a_spec = pl.BlockSpec((bk, bm), lambda i, j, kk: (kk, ia + i))
        b_spec = pl.BlockSpec((bk, bn), lambda i, j, kk: (kk, j0 + j))
    any_spec = pl.BlockSpec(memory_space=pl.ANY)
    in_specs, args = [a_spec, b_spec], [a, b]
    if has_add:
        in_specs.append(pl.BlockSpec((bm, bn), lambda i, j, kk: (i, j)))
        args.append(add)
    res = pl.pallas_call(
        body, name=name, grid=(n_i, n_j, n_k), in_specs=in_specs + [any_spec] * n_r,
        out_specs=[pl.BlockSpec((bm, bn), lambda i, j, kk: (i, j))] + [any_spec] * n_r,
        out_shape=[jax.ShapeDtypeStruct((m, n), out_dtype)] + _carry_shapes(carry),
        scratch_shapes=[pltpu.VMEM((bm, bn) if n_k > 1 else (8, LANES), F32)] + _carry_scratch(carry),
        compiler_params=_cparams(("arbitrary", "arbitrary", "arbitrary")),
    )(*args, *[arr for _, arr in carry])
    for (_, _, done), g in zip(reqs, res[1:]):
        done(g)
    return res[0]


def _ln_res_fn(_, p, x, st):
    g, b = p
    xin, m = x
    pre = ALPHA * xin + m
    mu = jnp.mean(pre, -1, keepdims=True)
    d = pre - mu
    var = jnp.mean(d * d, -1, keepdims=True)
    return [d * lax.rsqrt(var + LN_EPS) * g + b], []


def _make_conv_fn(taps, act):
    def fn(_, p, x, st):
        ws, b = p[:taps], p[taps]
        (xin,), (prev,) = x, st
        n = xin.shape[0]
        ext = jnp.concatenate([prev, xin], axis=0)
        y = b
        for k in range(taps):
            y = y + ws[k] * _rows(_roll(ext, taps - 1 - k, 0), 8, 8 + n)
        if act:
            y = _silu(y)
        return [y], [_rows(xin, n - 8, n)]

    return fn


def _ffn_act_fn(_, p, x, st):
    n = x[0].shape[0]
    ys = []
    for half in range(2):
        ws, b = p[4 * half:4 * half + 3], p[4 * half + 3]
        ext = jnp.concatenate([st[half], x[half]], axis=0)
        y = b
        for k in range(3):
            y = y + ws[k] * _rows(_roll(ext, 2 - k, 0), 8, 8 + n)
        ys.append(y)
    return [_silu(ys[0]) * ys[1]], [_rows(x[0], n - 8, n), _rows(x[1], n - 8, n)]


def _ffn_act_bwd(_, p, x, st, dy, dst):
    (da,) = dy
    n, wd = x[0].shape
    rs = FFN_STRIP
    last = n // rs - 1
    zero8 = jnp.zeros((8, wd), F32)
    acc = [[zero8] * 4, [zero8] * 4]
    after = [zero8, zero8]
    strips = [[None] * (n // rs), [None] * (n // rs)]
    for i in reversed(range(n // rs)):
        r0 = rs * i
        taps, ys = [], []
        for half in range(2):
            w0, w1, w2, b = p[4 * half:4 * half + 4]
            xs = jnp.concatenate([st[half] if i == 0 else x[half][r0 - 8:r0], x[half][r0:r0 + rs]], axis=0)
            taps.append((pltpu.roll(xs, 2, 0)[8:], pltpu.roll(xs, 1, 0)[8:], xs[8:]))
            ys.append(b + w2 * taps[half][2] + w1 * taps[half][1] + w0 * taps[half][0])
        g, u = ys
        s = _sigmoid(g)
        d = da[r0:r0 + rs]
        dys = (d * u * (s * (1.0 + g * (1.0 - s))), d * (g * s))
        for half in range(2):
            w0, w1, w2, _ = p[4 * half:4 * half + 4]
            dyh = dys[half]
            for k, v in enumerate((dyh * taps[half][0], dyh * taps[half][1], dyh * taps[half][2], dyh)):
                for r in range(0, rs, 8):
                    acc[half][k] = acc[half][k] + v[r:r + 8]
            dyp = jnp.concatenate([dyh, after[half]], axis=0)
            dxs = w2 * dyh + w1 * pltpu.roll(dyp, rs + 8 - 1, 0)[:rs] + w0 * pltpu.roll(dyp, rs + 8 - 2, 0)[:rs]
            if i == last:
                dxs = jnp.concatenate([dxs[:rs - 8], dxs[rs - 8:] + dst[half]], axis=0)
            strips[half][i] = dxs
            after[half] = dyh[:8]
    dprev = []
    for half in range(2):
        w0, w1 = p[4 * half], p[4 * half + 1]
        head = jnp.concatenate([zero8, after[half]], axis=0)
        dprev.append((w1 * pltpu.roll(head, 16 - 1, 0) + w0 * pltpu.roll(head, 16 - 2, 0))[:8])
    dps = [jnp.sum(a, axis=0, keepdims=True) for half in range(2) for a in acc[half]]
    return dps, [jnp.concatenate(s_, axis=0) for s_ in strips], dprev


def _ssd_fn(_, p, x, st):
    dtb, alog, dsk = p
    xbc, dtr = x
    L = SSD_CHUNK
    tril = _iota((L, L), 0) >= _iota((L, L), 1)
    xs, bm, cm = _cols(xbc, 0, 512), _cols(xbc, 512, 640), _cols(xbc, 640, 768)
    dt = _softplus(dtr + dtb)
    da = dt * (-jnp.exp(alog))
    cs = _sel_dot(tril.astype(BF16), da)
    pick = ((_iota((LANES, 2 * LANES), 0) == 0) & (_iota((LANES, 2 * LANES), 1) < LANES)) | (
        (_iota((LANES, 2 * LANES), 0) == 64) & (_iota((LANES, 2 * LANES), 1) >= LANES))
    pick = pick.astype(BF16)
    tot = jnp.sum(da, axis=0, keepdims=True)
    xc = xs * dt
    xdec = xc * jnp.exp(tot - cs)
    cs_b, xc_b, xdec_b, ecs_b, etot_b, dsk_b, xs_b = (
        _split_cols(v, LANES) for v in (cs, xc, xdec, jnp.exp(cs), jnp.exp(tot), dsk, xs))
    ys, new_st = [], []
    for pr in range(4):
        grp = pr // 2
        c_g = cm * _lane_mask(64 * grp, 64 * grp + 64)
        gmat = _dot_nt(c_g, bm)
        cols2 = _split_cols(_dot_sel(cs_b[pr], pick), LANES)
        yd = jnp.zeros((L, LANES), F32)
        for half in range(2):
            col = cols2[half]
            diff = col - col.T
            dec = jnp.where(tril, jnp.exp(jnp.where(tril, diff, 0.0)), 0.0)
            yd = yd + _dot(gmat * dec, xc_b[pr]) * _lane_mask(64 * half, 64 * half + 64)
        s_in = st[pr]
        y_off = _dot(c_g, s_in) * ecs_b[pr]
        ys.append(yd + y_off + dsk_b[pr] * xs_b[pr])
        new_st.append(s_in * etot_b[pr] + _dot_tn(bm, xdec_b[pr]))
    return [jnp.concatenate(ys, axis=1)], new_st


def _hg_fn(_, p, x, st):
    (lb,) = p
    (xin,) = x
    L = HG_STEP
    n_lvl = L.bit_length() - 1
    hq, hf, hi = _split_cols(xin, 512)
    q = _silu(hq)
    logf = jnp.log(lb + (1.0 - lb) * _sigmoid(hf))
    k = (1.0 - lb) * _sigmoid(-hf)
    ti, si = _iota((L, L), 0), _iota((L, L), 1)
    bc = _sel_dot((ti >= si).astype(BF16), logf)
    tot = jnp.sum(logf, axis=0, keepdims=True)
    tn, sn = _iota((n_lvl * L, 1), 0), _iota((n_lvl * L, L), 1)
    row = tn & (L - 1)
    blk = L >> (tn >> n_lvl)
    piv = row - (row & (blk - 1)) + (blk >> 1)
    bcp_all = _sel_dot((sn == piv).astype(BF16), bc)
    t1 = _iota((L, 1), 0)
    qqs, kks, sames = [], [], []
    for lvl in range(n_lvl):
        size = L >> lvl
        upper = (t1 & (size - 1)) >= size // 2
        bcp = _rows(bcp_all, L * lvl, L * (lvl + 1))
        qqs.append(jnp.where(upper, q * jnp.exp(jnp.where(upper, bc - bcp, 0.0)), 0.0))
        kks.append(jnp.where(upper, 0.0, k * jnp.exp(jnp.where(upper, 0.0, bcp - bc))))
        sames.append((ti >> (n_lvl - lvl)) == (si >> (n_lvl - lvl)))
    qq_b = [_split_cols(v, LANES) for v in qqs]
    kk_b = [_split_cols(v, LANES) for v in kks]
    v_b, diag_b, q_in_b, k_out_b, etot_b = (
        _split_cols(v, LANES) for v in (hi, q * k, q * jnp.exp(bc), k * jnp.exp(tot - bc), jnp.exp(tot)))
    outs, new_st = [], []
    for h in range(4):
        attn = jnp.zeros((L, L), F32)
        for lvl in range(n_lvl):
            attn = attn + jnp.where(sames[lvl], _dot_nt(qq_b[lvl][h], kk_b[lvl][h]), 0.0)
        v = v_b[h]
        out = _dot(attn, v) + jnp.sum(diag_b[h], axis=-1, keepdims=True) * v
        outs.append(out + _dot_nt(q_in_b[h], st[h]))
        new_st.append(st[h] * etot_b[h] + _dot_tn(v, k_out_b[h]))
    return [jnp.concatenate(outs, axis=1)], new_st


def _swa_fn(chunk, p, x, st):
    (sinks,) = p
    (xin,) = x
    q, k, v = _cols(xin, 0, 512), _cols(xin, 512, 640), _cols(xin, 640, 768)
    kp, vp = st
    T = SWA_BLOCK
    kc = jnp.concatenate([kp, k], axis=0)
    vc = jnp.concatenate([vp, v], axis=0)
    qi, kj = _iota((T, 2 * T), 0), _iota((T, 2 * T), 1)
    rel = qi + T - kj
    mask = (rel >= 0) & (rel < T) & ((kj >= T) | (chunk > 0))
    srow = _iota((8, LANES), 0)
    q_b = _split_cols(q, LANES)
    outs = []
    for pr in range(4):
        grp = pr // 2
        gm = _lane_mask(64 * grp, 64 * grp + 64)
        km, vm = kc * gm, vc * gm
        q2 = q_b[pr]
        o2 = jnp.zeros((T, LANES), F32)
        for half in range(2):
            hm = _lane_mask(64 * half, 64 * half + 64)
            qh = q2 * hm
            if half != grp:
                qh = _roll(qh, 64, 1)
            s = _dot_nt(qh, km) * 0.125
            s = jnp.where(mask, s, MASK_VALUE)
            sink = jnp.mean(jnp.sum(jnp.where(srow == 2 * pr + half, sinks, 0.0), axis=0, keepdims=True),
                            axis=-1, keepdims=True)
            mx = lax.stop_gradient(jnp.maximum(jnp.max(s, axis=-1, keepdims=True), sink))
            e = jnp.exp(s - mx)
            den = jnp.sum(e, axis=-1, keepdims=True) + jnp.exp(sink - mx)
            o = _dot(e / den, vm)
            if half != grp:
                o = _roll(o, 64, 1)
            o2 = o2 + o * hm
        outs.append(o2)
    return [jnp.concatenate(outs, axis=1)], [k, v]


def _rg_gate_fn(_, p, x, st):
    wa, ba, wx, bx, lam = p
    (xc,) = x
    r = _sigmoid(_dot(xc, wa) + ba)
    i = _sigmoid(_dot(xc, wx) + bx)
    log_a = -RG_C * r * _softplus(-lam)
    a = jnp.exp(log_a)
    t = jnp.tanh(log_a)
    one_minus_a2 = -2.0 * t / (1.0 - t)
    u = jnp.sqrt(jnp.maximum(one_minus_a2, 0.0)) * (i * xc)
    return [a, u], []


def _rg_scan_fn(_, p, x, st):
    a, u = x
    (prev,) = st
    n = a.shape[0]
    row = _iota((n, 1), 0)
    s = 1
    while s < n:
        keep = row >= s
        a_s, u_s = _roll(a, s, 0), _roll(u, s, 0)
        u = jnp.where(keep, a * u_s + u, u)
        a = jnp.where(keep, a * a_s, a)
        s *= 2
    h_in = jnp.sum(jnp.where(_iota((8, 1), 0) == 7, prev, 0.0), axis=0, keepdims=True)
    h = u + a * h_in
    return [h], [_rows(h, n - 8, n)]


def _ab_post_fn(_, p, x, st):
    nw_ssd, nw_hg = p
    y, o, zg = x
    z, hgate = _split_cols(zg, 512)

    def rms(v, width):
        blocks = _split_cols(v, width)
        return jnp.concatenate([b * lax.rsqrt(jnp.mean(b * b, axis=-1, keepdims=True) + RMS_EPS) for b in blocks],
                               axis=1)

    ya = rms(y * _silu(z), 256) * nw_ssd
    yb = rms(o, 128) * nw_hg * _silu(hgate)
    return [jnp.concatenate([ya, yb], axis=1)], []


def _cd_post_fn(_, p, x, st):
    yc, h, gate = x
    return [jnp.concatenate([yc, h * _gelu_tanh(gate)], axis=1)], []


def _lb_fn(_, p, x, st):
    l0, l1 = x
    mx = lax.stop_gradient(jnp.maximum(l0, l1))
    e0, e1 = jnp.exp(l0 - mx), jnp.exp(l1 - mx)
    s0, s1 = e0 / (e0 + e1), e1 / (e0 + e1)
    return [jnp.clip(s0 - s0, 0.0, 1.0), jnp.clip((s0 + s1) - s0, 0.0, 1.0)], []


def _loss_kernel(y, target):
    t, d = y.shape
    bt = _pick(t, (512, 256, 128))

    def body(y_ref, t_ref, dy_ref, l_ref):
        @pl.when(pl.program_id(0) == 0)
        def _():
            l_ref[...] = jnp.zeros(l_ref.shape, F32)

        e = y_ref[...] - t_ref[...]
        dy_ref[...] = e * (1.0 / d)
        l_ref[...] += jnp.sum(e * e, axis=0, keepdims=True) * (0.5 / d)

    dy, part = pl.pallas_call(
        body, name="loss", grid=(t // bt,),
        in_specs=[pl.BlockSpec((bt, d), lambda i: (i, 0)), pl.BlockSpec((bt, d), lambda i: (i, 0))],
        out_specs=[pl.BlockSpec((bt, d), lambda i: (i, 0)), pl.BlockSpec((1, d), lambda i: (0, 0))],
        out_shape=[jax.ShapeDtypeStruct((t, d), F32), jax.ShapeDtypeStruct((1, d), F32)],
        compiler_params=_cparams(("arbitrary",)),
    )(y, target)
    return dy, jnp.sum(part)


def _adamw_math(parts, w_, m_, v_):
    c1 = 1.0 / (1.0 - ADAM_B1 ** ADAM_STEP)
    c2 = 1.0 / (1.0 - ADAM_B2 ** ADAM_STEP)
    g = parts[0].astype(F32)
    for s in range(1, N_DEV):
        g = g + parts[s].astype(F32)
    nm = ADAM_B1 * m_ + (1.0 - ADAM_B1) * g
    nv = ADAM_B2 * v_ + (1.0 - ADAM_B2) * (g * g)
    return g, -ADAM_LR * ((nm * c1) / (jnp.sqrt(nv * c2) + ADAM_EPS) + ADAM_WD * w_), nm, nv


def _adamw_big(name, parts, w, m, v):
    n_l, r, c = w.shape
    br = _pick(r, (256, 176, 128, 64, 32, 16, 8))

    def body(*refs):
        p_refs, (w_ref, m_ref, v_ref), outs = refs[:n_l], refs[n_l:n_l + 3], refs[n_l + 3:]
        for l in range(n_l):
            @pl.when(pl.program_id(0) == l)
            def _(p_ref=p_refs[l]):
                res = _adamw_math([p_ref[s] for s in range(N_DEV)], w_ref[...], m_ref[...], v_ref[...])
                for ref, val in zip(outs, res):
                    ref[...] = val

    blk = pl.BlockSpec((None, br, c), lambda l, i: (l, i, 0))
    p_specs = [pl.BlockSpec((N_DEV, br, c), lambda l, i, k=k: (0, jnp.where(l == k, i, 0), 0)) for k in range(n_l)]
    return pl.pallas_call(
        body, name=name, grid=(n_l, r // br), in_specs=p_specs + [blk, blk, blk],
        out_specs=[blk] * 4, out_shape=[jax.ShapeDtypeStruct(w.shape, F32)] * 4,
        compiler_params=_cparams(("arbitrary", "arbitrary")),
    )(*parts, w, m, v)


def _adamw_small(name, items):
    n = len(items)

    def body(*refs):
        ins, outs = refs[:4 * n], refs[4 * n:]
        for i in range(n):
            p_ref, w_ref, m_ref, v_ref = ins[4 * i:4 * i + 4]
            res = _adamw_math([p_ref[s] for s in range(N_DEV)], w_ref[...], m_ref[...], v_ref[...])
            for ref, val in zip(outs[4 * i:4 * i + 4], res):
                ref[...] = val

    flat = [a for it in items for a in it]
    out_shape = [jax.ShapeDtypeStruct(it[1].shape, F32) for it in items for _ in range(4)]
    res = pl.pallas_call(
        body, name=name, out_shape=out_shape,
        in_specs=[pl.BlockSpec(memory_space=pltpu.VMEM)] * len(flat),
        out_specs=[pl.BlockSpec(memory_space=pltpu.VMEM)] * len(out_shape),
        compiler_params=pltpu.CompilerParams(vmem_limit_bytes=VMEM_LIMIT),
    )(*flat)
    return [res[4 * i:4 * i + 4] for i in range(n)]


SHARDED = [("ab_w_in", 2), ("ab_w_out", 1), ("cd_w_in", 2), ("cd_w_out", 1), ("ffn_w_up", 2), ("ffn_w_down", 1),
           ("ssd_conv_w", 2), ("rg_conv_w", 2), ("rg_conv_b", 1), ("rg_ba", 1), ("rg_bx", 1), ("rg_lambda", 1),
           ("ffn_conv_w", 2), ("ln_g", 2), ("ln_b", 2)]
REPLICATED = ["ssd_conv_b", "ssd_dt_bias", "ssd_a_log", "ssd_d", "ssd_norm_w", "hg_lower", "hg_norm_w", "swa_sinks",
              "rg_wa", "rg_wx", "ffn_conv_b"]
WEIGHTS = ["ab_w_in", "ssd_conv_w", "ssd_conv_b", "ssd_dt_bias", "ssd_a_log", "ssd_d", "ssd_norm_w", "hg_lower",
           "hg_norm_w", "ab_w_out", "cd_w_in", "swa_sinks", "rg_conv_w", "rg_conv_b", "rg_wa", "rg_ba", "rg_wx",
           "rg_bx", "rg_lambda", "cd_w_out", "ffn_w_up", "ffn_conv_w", "ffn_conv_b", "ffn_w_down", "ln_g", "ln_b"]


def _as2d(a):
    return a.reshape(-1, a.shape[-1])


def _merge_shards(g, axis):
    g = jnp.moveaxis(g, 0, axis)
    shp = g.shape
    return g.reshape(shp[:axis] + (shp[axis] * shp[axis + 1],) + shp[axis + 2:])


def _split_shards(full, axis):
    shp = full.shape
    g = full.reshape(shp[:axis] + (N_DEV, shp[axis] // N_DEV) + shp[axis + 1:])
    return jnp.moveaxis(g, axis, 0)


def _ab_pad(w):
    z, xbc, dt = w[..., 0:512], w[..., 512:1280], w[..., 1280:1288]
    hqfi, hg = w[..., 1288:2824], w[..., 2824:3336]
    return jnp.concatenate([hqfi, jnp.repeat(dt, 64, axis=-1), z, hg, xbc], axis=-1)


def _ab_unpad(d):
    lead = d.shape[:-1]
    dt = d[..., AB_DT:AB_ZG].reshape(lead + (8, 64)).sum(-1)
    z, hg, xbc = d[..., AB_ZG:AB_ZG + 512], d[..., AB_ZG + 512:AB_XBC], d[..., AB_XBC:AB_PAD]
    return jnp.concatenate([z, xbc, dt, d[..., :AB_DT], hg], axis=-1)


def _cd_pad(w):
    return jnp.concatenate([w[..., :768], jnp.zeros(w.shape[:-1] + (256,), w.dtype), w[..., 768:]], axis=-1)


def _cd_unpad(d):
    return jnp.concatenate([d[..., :768], d[..., CD_GATE:CD_PAD]], axis=-1)


def _cat_halves(d):
    return jnp.concatenate(d, axis=1)


def _block_diag(w):
    eye = jnp.eye(8, dtype=w.dtype)
    return jnp.einsum("gij,gh->gihj", w, eye).reshape(512, 512)


def _block_diag_grad(d):
    return jnp.stack([d[64 * g:64 * g + 64, 64 * g:64 * g + 64] for g in range(8)])


def _same(a):
    return a


BIG = {"ab_w_in": (1, _ab_pad, _ab_unpad), "ab_w_out": (0, _same, _same), "cd_w_in": (1, _cd_pad, _cd_unpad),
       "cd_w_out": (0, _same, _same), "ffn_w_up": (1, _same, _cat_halves), "ffn_w_down": (0, _same, _same)}


class _MeshComm:
    def __init__(self, shards):
        self.shards, self.full, self.recv, self.posted = shards, {}, {}, {}

    def post(self, carrier, req):
        self.posted.setdefault(carrier, []).append(req)

    def take(self, carrier):
        return self.posted.pop(carrier, [])

    def take_all(self):
        reqs = [r for name in list(self.posted) for r in self.posted.pop(name)]
        return reqs

    def gather_req(self, name, layer):
        axis, prep, _ = BIG[name]

        def done(got):
            self.full[name, layer] = prep(_merge_shards(got, axis))

        return ("gather", self.shards[name][layer].astype(MM_DTYPE), done)

    def weight(self, name, layer):
        return self.full[name, layer]

    def grad_req(self, name, layer, d):
        axis, _, unprep = BIG[name]

        def done(got):
            self.recv[name, layer] = got

        return ("exchange", _split_shards(unprep(d), axis).astype(MM_DTYPE), done)


class _LocalComm:
    def __init__(self, full):
        self.full_w, self.grads = full, {}

    def post(self, carrier, req):
        pass

    def take(self, carrier):
        return []

    def gather_req(self, name, layer):
        return None

    def weight(self, name, layer):
        return BIG[name][1](self.full_w[name][layer].astype(MM_DTYPE))

    def grad_req(self, name, layer, d):
        self.grads[name, layer] = BIG[name][2](d)
        return None


def _row_vec(v):
    return v.reshape(1, -1)


def _heads64(v):
    return jnp.repeat(v, 64).reshape(1, 512)


def _local_step(x, target, w, comm):
    kinds = ["ab" if layer % 2 == 0 else "cd" for layer in range(DEPTH)]
    in_name = [f"{kinds[layer]}_in{layer // 2}" for layer in range(DEPTH)]
    core_name = [("hg" if layer % 2 == 0 else "swa") + f"{layer // 2}_fwd" for layer in range(DEPTH)]
    comm.post("ssd0_fwd", comm.gather_req("ffn_w_down", 0))
    comm.post(core_name[0], comm.gather_req("ffn_w_up", 0))
    for layer in range(DEPTH - 1):
        nxt, nj = kinds[layer + 1], (layer + 1) // 2
        if kinds[layer] == "ab":
            comm.post(in_name[layer], comm.gather_req(nxt + "_w_in", nj))
            comm.post(core_name[layer], comm.gather_req(nxt + "_w_out", nj))
            comm.post(core_name[layer], comm.gather_req("ffn_w_down", layer + 1))
        else:
            comm.post(in_name[layer], comm.gather_req(nxt + "_w_out", nj))
            comm.post(core_name[layer], comm.gather_req(nxt + "_w_in", nj))
            comm.post(f"ffn_up{layer}", comm.gather_req("ffn_w_down", layer + 1))
        comm.post(f"ffn_act{layer}_fwd", comm.gather_req("ffn_w_up", layer + 1))

    t = x.shape[0]
    bt = _pick(t, (512, 256, 128))
    nb = t // bt
    bs = _pick(t, (256, 128))

    def rowop(name, fn, params, xs, widths_out, out_dtype=F32, dx_dtypes=None):
        outs = [((t, wd), (bt, wd), lambda g, c: (c, 0), out_dtype) for wd in widths_out]
        return _Op(name, fn, (1, nb), params, xs, outs, dx_dtypes=dx_dtypes)

    def rowblk(arr, width, first=0):
        return (arr, (bt, width), lambda g, c: (c, first))

    one_row = lambda g, c: (0, 0)
    lb_op = _Op("hg_lb", _lb_fn, (1, 1), [],
                [(w["hg_lower"][0:1], (1, 512), one_row), (w["hg_lower"][1:2], (1, 512), one_row)],
                [((1, 512), (1, 512), one_row, F32)] * 2)
    lb_all = lb_op.ys

    tape = []
    grads = {}

    def add_grad(name, idx, val):
        grads.setdefault(name, {})[idx] = val

    def dw_matmul(name, a, b, wname, idx, carrier):
        d = _matmul(name, a, b, "tn", out_dtype=MM_DTYPE, comm=comm)
        comm.post(carrier, comm.grad_req(wname, idx, d))

    for layer in range(DEPTH):
        j = layer // 2
        rec = {"x_in": x}
        if layer % 2 == 0:
            h = _matmul(f"ab_in{j}", x, comm.weight("ab_w_in", j), "nn", comm=comm)
            conv_p = [_row_vec(w["ssd_conv_w"][j, k]) for k in range(4)] + [_row_vec(w["ssd_conv_b"][j])]
            conv = _Op(f"ssd_conv{j}", _make_conv_fn(4, True), (3, nb),
                       [(a, (1, 256), lambda g, c: (0, g)) for a in conv_p],
                       [(h, (bt, 256), lambda g, c: (c, AB_XBC // 256 + g))],
                       [((t, 768), (bt, 256), lambda g, c: (c, g), F32)], [(8, 256)], dx_dtypes=[MM_DTYPE])
            ssd = _Op(f"ssd{j}", _ssd_fn, (1, t // SSD_CHUNK),
                      [_whole(_heads64(w["ssd_dt_bias"][j])), _whole(_heads64(w["ssd_a_log"][j])),
                       _whole(_heads64(w["ssd_d"][j]))],
                      [(conv.ys[0], (SSD_CHUNK, 768), lambda g, c: (c, 0)),
                       (h, (SSD_CHUNK, 512), lambda g, c: (c, AB_DT // 512))],
                      [((t, 512), (SSD_CHUNK, 512), lambda g, c: (c, 0), F32)], [(LANES, LANES)] * 4,
                      dx_dtypes=[F32, MM_DTYPE], comm=comm)
            hg = _Op(f"hg{j}", _hg_fn, (1, t // HG_STEP), [_whole(lb_all[j])],
                     [(h, (HG_STEP, AB_DT), lambda g, c: (c, 0))],
                     [((t, 512), (HG_STEP, 512), lambda g, c: (c, 0), F32)], [(LANES, LANES)] * 4,
                     dx_dtypes=[MM_DTYPE], comm=comm)
            post = rowop(f"ab_post{j}", _ab_post_fn,
                         [_whole(_row_vec(w["ssd_norm_w"][j])), _whole(jnp.tile(_row_vec(w["hg_norm_w"][j]), (1, 4)))],
                         [rowblk(ssd.ys[0], 512), rowblk(hg.ys[0], 512), rowblk(h, 1024, AB_ZG // 1024)], [1024],
                         out_dtype=MM_DTYPE, dx_dtypes=[F32, F32, MM_DTYPE])
            rec.update(kind="ab", conv=conv, ssd=ssd, hg=hg, post=post)
        else:
            h = _matmul(f"cd_in{j}", x, comm.weight("cd_w_in", j), "nn", comm=comm)
            swa = _Op(f"swa{j}", _swa_fn, (1, t // SWA_BLOCK),
                      [_whole(jnp.tile(w["swa_sinks"][j].reshape(8, 1), (1, LANES)))],
                      [(h, (SWA_BLOCK, 1024), lambda g, c: (c, 0))],
                      [((t, 512), (SWA_BLOCK, 512), lambda g, c: (c, 0), F32)], [(SWA_BLOCK, LANES)] * 2,
                      dx_dtypes=[MM_DTYPE], comm=comm)
            conv_p = [_row_vec(w["rg_conv_w"][j, k]) for k in range(4)] + [_row_vec(w["rg_conv_b"][j])]
            conv = _Op(f"rg_conv{j}", _make_conv_fn(4, False), (2, nb),
                       [(a, (1, 256), lambda g, c: (0, g)) for a in conv_p],
                       [(h, (bt, 256), lambda g, c: (c, CD_XR // 256 + g))],
                       [((t, 512), (bt, 256), lambda g, c: (c, g), F32)], [(8, 256)], dx_dtypes=[MM_DTYPE])
            gate = rowop(f"rg_gate{j}", _rg_gate_fn,
                         [_whole(_block_diag(w["rg_wa"][j])), _whole(_row_vec(w["rg_ba"][j])),
                          _whole(_block_diag(w["rg_wx"][j])), _whole(_row_vec(w["rg_bx"][j])),
                          _whole(_row_vec(w["rg_lambda"][j]))],
                         [rowblk(conv.ys[0], 512)], [512, 512])
            scan = _Op(f"rg_scan{j}", _rg_scan_fn, (2, t // bs), [],
                       [(gate.ys[0], (bs, 256), lambda g, c: (c, g)), (gate.ys[1], (bs, 256), lambda g, c: (c, g))],
                       [((t, 512), (bs, 256), lambda g, c: (c, g), F32)], [(8, 256)])
            post = rowop(f"cd_post{j}", _cd_post_fn, [],
                         [rowblk(swa.ys[0], 512), rowblk(scan.ys[0], 512), rowblk(h, 512, CD_GATE // 512)], [1024],
                         out_dtype=MM_DTYPE, dx_dtypes=[F32, F32, MM_DTYPE])
            rec.update(kind="cd", swa=swa, conv=conv, gate=gate, scan=scan, post=post)
        kind = rec["kind"]
        ycat = post.ys[0]
        m = _matmul(f"mix_out{layer}", ycat, comm.weight(kind + "_w_out", j), "nn", comm=comm)
        ln1 = rowop(f"ln_a{layer}", _ln_res_fn,
                    [_whole(_row_vec(w["ln_g"][layer, 0])), _whole(_row_vec(w["ln_b"][layer, 0]))],
                    [rowblk(x, 1024), rowblk(m, 1024)], [1024], dx_dtypes=[F32, MM_DTYPE])
        x1 = ln1.ys[0]
        hu = _matmul(f"ffn_up{layer}", x1, comm.weight("ffn_w_up", layer), "nn", comm=comm)
        n_fb = FFN_DIM // FFN_BLK
        taps = [_row_vec(w["ffn_conv_w"][layer, k]) for k in range(3)] + [_row_vec(w["ffn_conv_b"][layer])]
        ba = _pick(t, (FFN_ROWS, 512, 256, 128))
        act = _Op(f"ffn_act{layer}", _ffn_act_fn, (n_fb, t // ba),
                  [(a, (1, FFN_BLK), lambda g, c: (0, g)) for a in taps]
                  + [(a, (1, FFN_BLK), lambda g, c: (0, n_fb + g)) for a in taps],
                  [(hu, (ba, FFN_BLK), lambda g, c: (c, g)), (hu, (ba, FFN_BLK), lambda g, c: (c, n_fb + g))],
                  [((t, FFN_DIM), (ba, FFN_BLK), lambda g, c: (c, g), MM_DTYPE)], [(8, FFN_BLK)] * 2,
                  dx_dtypes=[MM_DTYPE, MM_DTYPE], comm=comm, bwd_fn=_ffn_act_bwd)
        a = act.ys[0]
        f = _matmul(f"ffn_down{layer}", a, comm.weight("ffn_w_down", layer), "nn", comm=comm)
        ln2 = rowop(f"ln_f{layer}", _ln_res_fn,
                    [_whole(_row_vec(w["ln_g"][layer, 1])), _whole(_row_vec(w["ln_b"][layer, 1]))],
                    [rowblk(x1, 1024), rowblk(f, 1024)], [1024], dx_dtypes=[F32, MM_DTYPE])
        rec.update(ycat=ycat, ln1=ln1, x1=x1, act=act, a=a, ln2=ln2)
        tape.append(rec)
        x = ln2.ys[0]

    dx, loss = _loss_kernel(x, target)

    d_lb = [jnp.zeros((1, 512), F32), jnp.zeros((1, 512), F32)]
    for layer in reversed(range(DEPTH)):
        j = layer // 2
        rec = tape[layer]
        (dg, db), (dx1_res, df) = rec["ln2"].bwd([dx])
        add_grad("ln_g", (layer, 1), dg[0]); add_grad("ln_b", (layer, 1), db[0])
        dw_matmul(f"ffn_down_dw{layer}", rec["a"], df, "ffn_w_down", layer, f"ffn_act{layer}_bwd")
        da = _matmul(f"ffn_down_dx{layer}", df, comm.weight("ffn_w_down", layer), "nt", comm=comm)
        dpa, (dhg, dhu) = rec["act"].bwd([da])
        halves = [jnp.concatenate([dpa[k][0, :FFN_DIM], dpa[4 + k][0, FFN_DIM:]]) for k in range(4)]
        add_grad("ffn_conv_w", layer, jnp.stack(halves[:3]))
        add_grad("ffn_conv_b", layer, halves[3])
        core_bwd = ("hg" if rec["kind"] == "ab" else "swa") + f"{j}_bwd"
        gate_cols, up_cols = (0, FFN_DIM), (FFN_DIM, FFN_DIM)
        dw_up = (_matmul(f"ffn_up_dw_g{layer}", rec["x1"], dhg, "tn", out_dtype=MM_DTYPE, comm=comm, b_cols=gate_cols),
                 _matmul(f"ffn_up_dw_u{layer}", rec["x1"], dhu, "tn", out_dtype=MM_DTYPE, comm=comm, b_cols=up_cols))
        comm.post(core_bwd, comm.grad_req("ffn_w_up", layer, dw_up))
        w_up = comm.weight("ffn_w_up", layer)
        dx1 = _matmul(f"ffn_up_dx_g{layer}", dhg, w_up, "nt", a_cols=gate_cols, b_cols=gate_cols, add=dx1_res,
                      comm=comm)
        dx1 = _matmul(f"ffn_up_dx_u{layer}", dhu, w_up, "nt", a_cols=up_cols, b_cols=up_cols, add=dx1, comm=comm)
        (dg, db), (dx_res, dm) = rec["ln1"].bwd([dx1])
        add_grad("ln_g", (layer, 0), dg[0]); add_grad("ln_b", (layer, 0), db[0])
        kind = rec["kind"]
        dw_matmul(f"mix_out_dw{layer}", rec["ycat"], dm, kind + "_w_out", j, f"{kind}_in_dw{j}")
        dycat = _matmul(f"mix_out_dx{layer}", dm, comm.weight(kind + "_w_out", j), "nt", comm=comm)
        if kind == "ab":
            (dnw_s, dnw_h), (dy_ssd, do_hg, dh) = rec["post"].bwd([dycat])
            add_grad("ssd_norm_w", j, dnw_s[0]); add_grad("hg_norm_w", j, dnw_h[0].reshape(4, LANES).sum(0))
            (dlb,), (dh,) = rec["hg"].bwd([do_hg], dx_into={0: dh})
            d_lb[j] = dlb
            (ddtb, dalog, ddsk), (dxbc_c, dh) = rec["ssd"].bwd([dy_ssd], dx_into={1: dh})
            add_grad("ssd_dt_bias", j, ddtb[0].reshape(8, 64).sum(-1))
            add_grad("ssd_a_log", j, dalog[0].reshape(8, 64).sum(-1))
            add_grad("ssd_d", j, ddsk[0].reshape(8, 64).sum(-1))
            dcp, (dh,) = rec["conv"].bwd([dxbc_c], dx_into={0: dh})
            add_grad("ssd_conv_w", j, jnp.stack([dcp[k][0] for k in range(4)]))
            add_grad("ssd_conv_b", j, dcp[4][0])
        else:
            _, (dyc, dhs, dh) = rec["post"].bwd([dycat])
            _, (da_s, du_s) = rec["scan"].bwd([dhs])
            (dwa, dba, dwx, dbx, dlam), (dxc,) = rec["gate"].bwd([da_s, du_s])
            add_grad("rg_wa", j, _block_diag_grad(dwa)); add_grad("rg_wx", j, _block_diag_grad(dwx))
            add_grad("rg_ba", j, dba[0]); add_grad("rg_bx", j, dbx[0]); add_grad("rg_lambda", j, dlam[0])
            dcp, (dh,) = rec["conv"].bwd([dxc], dx_into={0: dh})
            add_grad("rg_conv_w", j, jnp.stack([dcp[k][0] for k in range(4)]))
            add_grad("rg_conv_b", j, dcp[4][0])
            (dsink,), (dh,) = rec["swa"].bwd([dyc], dx_into={0: dh})
            add_grad("swa_sinks", j, dsink.sum(-1))
        dw_matmul(f"{kind}_in_dw{j}", rec["x_in"], dh, kind + "_w_in", j,
                  f"ffn_act{layer - 1}_bwd" if layer > 0 else f"{kind}_in_dx{j}")
        dx = _matmul(f"{kind}_in_dx{j}", dh, comm.weight(kind + "_w_in", j), "nt", add=dx_res, comm=comm)

    _, (dl0, dl1) = lb_op.bwd(d_lb)
    out = {"hg_lower": jnp.concatenate([dl0, dl1], axis=0)}
    for name, parts in grads.items():
        keys = sorted(parts)
        if isinstance(keys[0], tuple):
            out[name] = jnp.stack([jnp.stack([parts[(l, s)] for s in range(2)]) for l in range(DEPTH)])
        else:
            out[name] = jnp.stack([parts[k] for k in keys])
    return loss, dx, out


def _local_step_full(x, target, full):
    comm = _LocalComm(full)
    loss, dx, grads = _local_step(x, target, {n: a for n, a in full.items() if n not in BIG}, comm)
    for name in BIG:
        grads[name] = jnp.stack([comm.grads[name, l] for l in range(full[name].shape[0])])
    return loss, dx, grads


def kernel(x, ab_w_in, ssd_conv_w, ssd_conv_b, ssd_dt_bias, ssd_a_log, ssd_d, ssd_norm_w, hg_lower, hg_norm_w, ab_w_out, cd_w_in, swa_sinks, rg_conv_w, rg_conv_b, rg_wa, rg_ba, rg_wx, rg_bx, rg_lambda, cd_w_out, ffn_w_up, ffn_conv_w, ffn_conv_b, ffn_w_down, ln_g, ln_b, loss_target, m_ab_w_in, m_ssd_conv_w, m_ssd_conv_b, m_ssd_dt_bias, m_ssd_a_log, m_ssd_d, m_ssd_norm_w, m_hg_lower, m_hg_norm_w, m_ab_w_out, m_cd_w_in, m_swa_sinks, m_rg_conv_w, m_rg_conv_b, m_rg_wa, m_rg_ba, m_rg_wx, m_rg_bx, m_rg_lambda, m_cd_w_out, m_ffn_w_up, m_ffn_conv_w, m_ffn_conv_b, m_ffn_w_down, m_ln_g, m_ln_b, v_ab_w_in, v_ssd_conv_w, v_ssd_conv_b, v_ssd_dt_bias, v_ssd_a_log, v_ssd_d, v_ssd_norm_w, v_hg_lower, v_hg_norm_w, v_ab_w_out, v_cd_w_in, v_swa_sinks, v_rg_conv_w, v_rg_conv_b, v_rg_wa, v_rg_ba, v_rg_wx, v_rg_bx, v_rg_lambda, v_cd_w_out, v_ffn_w_up, v_ffn_conv_w, v_ffn_conv_b, v_ffn_w_down, v_ln_g, v_ln_b):
    args = dict(locals())
    wts = {n: args[n] for n in WEIGHTS}
    mom = {n: args["m_" + n] for n in WEIGHTS}
    var = {n: args["v_" + n] for n in WEIGHTS}
    axis = dict(SHARDED)
    small = [n for n, _ in SHARDED if n not in BIG]
    comm = _MeshComm(wts)

    def run(name, reqs):
        for (_, _, done), got in zip(reqs, _remote_copies(name, [(k, a) for k, a, _ in reqs])):
            done(got)

    full = {n: wts[n] for n in REPLICATED}

    def keep_small(n):
        def done(got):
            full[n] = _merge_shards(got.reshape((N_DEV,) + wts[n].shape), axis[n])
        return ("gather", _as2d(wts[n]), done)

    run("gather_first", [comm.gather_req("ab_w_in", 0), keep_small("ssd_conv_w")])
    for req in [comm.gather_req("ab_w_out", 0)] + [keep_small(n) for n in small if n != "ssd_conv_w"]:
        comm.post("ab_in0", req)

    loss, grad_x, grads = _local_step(x[0], loss_target[0], full, comm)
    loss = lax.psum(loss, ("x", "y", "c"))

    parts = {}

    def keep_parts(n, kind, arr):
        return (kind, arr, lambda got: parts.__setitem__(n, got))

    last = comm.take_all()
    last += [keep_parts(n, "exchange", _split_shards(grads[n], axis[n]).reshape((N_DEV,) + _as2d(wts[n]).shape))
             for n in small]
    last += [keep_parts(n, "gather", _as2d(grads[n])) for n in REPLICATED]
    run("exchange_last", last)

    new = {}
    for n in BIG:
        new[n] = _adamw_big("adamw_" + n, [comm.recv[n, l] for l in range(wts[n].shape[0])], wts[n], mom[n], var[n])
    names = small + REPLICATED
    res = _adamw_small("adamw_small", [(parts[n], _as2d(wts[n]), _as2d(mom[n]), _as2d(var[n])) for n in names])
    for n, r in zip(names, res):
        new[n] = [a.reshape(wts[n].shape) for a in r]

    outs = [loss, grad_x[None]]
    for kind in range(4):
        outs += [new[n][kind] for n in WEIGHTS]
    return tuple(outs)
```

```python
import math

import jax
import jax.numpy as jnp
from jax import lax
from jax.experimental import pallas as pl
from jax.experimental.pallas import tpu as pltpu

F32 = jnp.float32
BF16 = jnp.bfloat16
MM_DTYPE = BF16

DEPTH = 4
N_DEV = 8
LN_EPS = 1e-5
RMS_EPS = 1e-6
MASK_VALUE = -1e9
ALPHA = (2 * DEPTH) ** 0.25
RG_C = 8.0
FFN_DIM = 2816
SSD_CHUNK = 128
HG_STEP = 128
SWA_BLOCK = 128
LANES = 128
VMEM_LIMIT = 56 * 1024 * 1024

ADAM_LR, ADAM_B1, ADAM_B2, ADAM_EPS, ADAM_WD, ADAM_STEP = 0.001, 0.9, 0.999, 1e-08, 0.01, 10

AB_HEADS, AB_DT, AB_ZG, AB_XBC, AB_PAD = 0, 1536, 2048, 3072, 3840
CD_QKV, CD_GATE, CD_XR, CD_PAD = 0, 1024, 1536, 2048
FFN_BLK = 256
FFN_ROWS = 2048
FFN_STRIP = 32


def _cols(x, lo, hi):
    n = x.shape[1]

    @jax.custom_vjp
    def f(x):
        return x[:, lo:hi]

    def bwd(_, g):
        parts = []
        if lo > 0:
            parts.append(jnp.zeros((g.shape[0], lo), g.dtype))
        parts.append(g)
        if hi < n:
            parts.append(jnp.zeros((g.shape[0], n - hi), g.dtype))
        return (jnp.concatenate(parts, axis=1) if len(parts) > 1 else g,)

    f.defvjp(lambda x: (f(x), None), bwd)
    return f(x)


def _split_cols(x, width):
    n = x.shape[1] // width

    @jax.custom_vjp
    def f(x):
        return tuple(x[:, width * i:width * (i + 1)] for i in range(n))

    f.defvjp(lambda x: (f(x), None), lambda _, gs: (jnp.concatenate(gs, axis=1),))
    return f(x)


def _rows(x, lo, hi):
    n = x.shape[0]

    @jax.custom_vjp
    def f(x):
        return x[lo:hi, :]

    def bwd(_, g):
        parts = []
        if lo > 0:
            parts.append(jnp.zeros((lo, g.shape[1]), g.dtype))
        parts.append(g)
        if hi < n:
            parts.append(jnp.zeros((n - hi, g.shape[1]), g.dtype))
        return (jnp.concatenate(parts, axis=0) if len(parts) > 1 else g,)

    f.defvjp(lambda x: (f(x), None), bwd)
    return f(x)


def _roll(x, shift, axis):
    n = x.shape[axis]
    shift = shift % n
    if shift == 0:
        return x

    @jax.custom_vjp
    def f(x):
        return pltpu.roll(x, shift, axis)

    f.defvjp(lambda x: (f(x), None), lambda _, g: (pltpu.roll(g, n - shift, axis),))
    return f(x)


def _dot(a, b):
    return lax.dot_general(a, b, (((1,), (0,)), ((), ())), preferred_element_type=F32)


def _dot_nt(a, b):
    return lax.dot_general(a, b, (((1,), (1,)), ((), ())), preferred_element_type=F32)


def _dot_tn(a, b):
    return lax.dot_general(a, b, (((0,), (0,)), ((), ())), preferred_element_type=F32)


def _split3(x):
    hi = x.astype(BF16)
    r = x - hi.astype(F32)
    mid = r.astype(BF16)
    return hi, mid, (r - mid.astype(F32)).astype(BF16)


def _sel_dot(sel, x):
    def run(mat, v, dims):
        n = v.shape[1]
        y = lax.dot_general(mat, jnp.concatenate(_split3(v), axis=1), dims, preferred_element_type=F32)
        return y[:, :n] + y[:, n:2 * n] + y[:, 2 * n:]

    @jax.custom_vjp
    def f(sel, x):
        return run(sel, x, (((1,), (0,)), ((), ())))

    def bwd(sel, g):
        return jnp.zeros_like(sel), run(sel, g, (((0,), (0,)), ((), ())))

    f.defvjp(lambda sel, x: (f(sel, x), sel), bwd)
    return f(sel, x)


def _dot_sel(x, sel):
    def run(v, mat, dims):
        m = v.shape[0]
        y = lax.dot_general(jnp.concatenate(_split3(v), axis=0), mat, dims, preferred_element_type=F32)
        return y[:m] + y[m:2 * m] + y[2 * m:]

    @jax.custom_vjp
    def f(x, sel):
        return run(x, sel, (((1,), (0,)), ((), ())))

    def bwd(sel, g):
        return run(g, sel, (((1,), (1,)), ((), ()))), jnp.zeros_like(sel)

    f.defvjp(lambda x, sel: (f(x, sel), sel), bwd)
    return f(x, sel)


def _sigmoid(x):
    return 0.5 * jnp.tanh(0.5 * x) + 0.5


def _silu(x):
    h = 0.5 * x
    return h + h * jnp.tanh(h)


def _softplus(x):
    return jnp.maximum(x, 0.0) + jnp.log(1.0 + jnp.exp(-jnp.abs(x)))


def _gelu_tanh(x):
    c = math.sqrt(2.0 / math.pi)
    return 0.5 * x * (1.0 + jnp.tanh(c * (x + 0.044715 * (x * x * x))))


def _iota(shape, axis):
    return lax.broadcasted_iota(jnp.int32, shape, axis)


def _lane_mask(lo, hi, width=LANES):
    lane = _iota((1, width), 1)
    return ((lane >= lo) & (lane < hi)).astype(F32)


def _mesh_pos():
    return lax.axis_index("x"), lax.axis_index("y"), lax.axis_index("c")


def _carry_shapes(carry):
    return [jax.ShapeDtypeStruct((N_DEV,) + a.shape if kind == "gather" else a.shape, a.dtype) for kind, a in carry]


def _carry_scratch(carry):
    n = len(carry)
    if n == 0:
        return []
    return [pltpu.SemaphoreType.DMA((n, N_DEV - 1)), pltpu.SemaphoreType.DMA((n, N_DEV - 1)),
            pltpu.SemaphoreType.DMA((n,))]


def _carry_run(start, kinds, in_refs, out_refs, send_sems, recv_sems, local_sems):
    x, y, cc = _mesh_pos()
    me = 4 * x + 2 * y + cc
    for i, kind in enumerate(kinds):
        mine = in_refs[i] if kind == "gather" else in_refs[i].at[me]
        local = pltpu.make_async_copy(mine, out_refs[i].at[me], local_sems.at[i])
        remote = []
        for k in range(1, N_DEV):
            px, py, pc = x ^ ((k >> 2) & 1), y ^ ((k >> 1) & 1), cc ^ (k & 1)
            src = in_refs[i] if kind == "gather" else in_refs[i].at[4 * px + 2 * py + pc]
            remote.append(pltpu.make_async_remote_copy(
                src_ref=src, dst_ref=out_refs[i].at[me],
                send_sem=send_sems.at[i, k - 1], recv_sem=recv_sems.at[i, k - 1],
                device_id=(px, py, pc), device_id_type=pl.DeviceIdType.MESH))
        if start:
            local.start()
            for cp in remote:
                cp.start()
        else:
            for cp in remote:
                cp.wait_recv()
            for cp in remote:
                cp.wait_send()
            local.wait()


def _remote_copies(name, carry):
    n = len(carry)
    kinds = [k for k, _ in carry]

    def body(*refs):
        sems = refs[2 * n:]
        _carry_run(True, kinds, refs[:n], refs[n:2 * n], *sems)
        _carry_run(False, kinds, refs[:n], refs[n:2 * n], *sems)

    return pl.pallas_call(
        body, name=name, out_shape=_carry_shapes(carry),
        in_specs=[pl.BlockSpec(memory_space=pl.ANY)] * n, out_specs=[pl.BlockSpec(memory_space=pl.ANY)] * n,
        scratch_shapes=_carry_scratch(carry),
    )(*[a for _, a in carry])


def _cparams(sem):
    return pltpu.CompilerParams(dimension_semantics=sem, vmem_limit_bytes=VMEM_LIMIT)


def _chunk_fwd(name, fn, grid, params, xs, outs, state_shapes, carry=()):
    n_g, n_c = grid
    n_p, n_x, n_o, n_s, n_r = len(params), len(xs), len(outs), len(state_shapes), len(carry)
    kinds = [k for k, _ in carry]

    def body(*refs):
        i = 0
        p_refs = refs[i:i + n_p]; i += n_p
        x_refs = refs[i:i + n_x]; i += n_x
        ci_refs = refs[i:i + n_r]; i += n_r
        o_refs = refs[i:i + n_o]; i += n_o
        sv_refs = refs[i:i + n_s]; i += n_s
        co_refs = refs[i:i + n_r]; i += n_r
        st_refs = refs[i:i + n_s]; i += n_s
        sems = refs[i:]
        g, c = pl.program_id(0), pl.program_id(1)

        if n_r:
            @pl.when((g == 0) & (c == 0))
            def _():
                _carry_run(True, kinds, ci_refs, co_refs, *sems)

        @pl.when(c == 0)
        def _():
            for s in st_refs:
                s[...] = jnp.zeros(s.shape, s.dtype)

        st = [s[...] for s in st_refs]
        ys, new_st = fn(c, [p[...] for p in p_refs], [x[...].astype(F32) for x in x_refs], st)
        for o, y in zip(o_refs, ys):
            o[...] = y.astype(o.dtype)
        for sv, s in zip(sv_refs, st):
            sv[0, 0] = s
        for s_ref, s in zip(st_refs, new_st):
            s_ref[...] = s

        if n_r:
            @pl.when((g == n_g - 1) & (c == n_c - 1))
            def _():
                _carry_run(False, kinds, ci_refs, co_refs, *sems)

    any_spec = pl.BlockSpec(memory_space=pl.ANY)
    in_specs = [pl.BlockSpec(b, m) for _, b, m in params] + [pl.BlockSpec(b, m) for _, b, m in xs] + [any_spec] * n_r
    out_specs = [pl.BlockSpec(b, m) for _, b, m, _ in outs]
    out_shape = [jax.ShapeDtypeStruct(s, d) for s, _, _, d in outs]
    for shp in state_shapes:
        out_specs.append(pl.BlockSpec((1, 1) + shp, lambda g, c, n=len(shp): (g, c) + (0,) * n))
        out_shape.append(jax.ShapeDtypeStruct((n_g, n_c) + shp, F32))
    out_specs += [any_spec] * n_r
    out_shape += _carry_shapes(carry)
    res = pl.pallas_call(
        body, name=name, grid=grid, in_specs=in_specs, out_specs=out_specs, out_shape=out_shape,
        scratch_shapes=[pltpu.VMEM(shp, F32) for shp in state_shapes] + _carry_scratch(carry),
        compiler_params=_cparams(("arbitrary", "arbitrary")),
    )(*[a for a, _, _ in params], *[a for a, _, _ in xs], *[a for _, a in carry])
    return list(res[:n_o]), list(res[n_o:n_o + n_s]), list(res[n_o + n_s:])


def _chunk_bwd(name, fn, grid, params, xs, saved, dys, state_shapes, dx_dtypes, dx_into, carry=(), bwd_fn=None):
    n_g, n_c = grid
    n_p, n_x, n_s, n_y, n_r = len(params), len(xs), len(state_shapes), len(dys), len(carry)
    kinds = [k for k, _ in carry]
    into = sorted(dx_into)
    n_a = len(into)

    def rev(m):
        return lambda g, c: m(g, n_c - 1 - c)

    def body(*refs):
        i = 0
        p_refs = refs[i:i + n_p]; i += n_p
        x_refs = refs[i:i + n_x]; i += n_x
        sv_refs = refs[i:i + n_s]; i += n_s
        dy_refs = refs[i:i + n_y]; i += n_y
        i += n_a
        ci_refs = refs[i:i + n_r]; i += n_r
        dp_refs = refs[i:i + n_p]; i += n_p
        dx_refs = refs[i:i + n_x]; i += n_x
        co_refs = refs[i:i + n_r]; i += n_r
        ds_refs = refs[i:i + n_s]; i += n_s
        sems = refs[i:]
        g, c = pl.program_id(0), pl.program_id(1)
        chunk = n_c - 1 - c

        if n_r:
            @pl.when((g == 0) & (c == 0))
            def _():
                _carry_run(True, kinds, ci_refs, co_refs, *sems)

        @pl.when(c == 0)
        def _():
            for s in ds_refs:
                s[...] = jnp.zeros(s.shape, s.dtype)
            for d in dp_refs:
                d[...] = jnp.zeros(d.shape, d.dtype)

        pv = [p[...] for p in p_refs]
        xv = [x[...].astype(F32) for x in x_refs]
        sv = [s[0, 0] for s in sv_refs]
        dyv, dsv = [d[...].astype(F32) for d in dy_refs], [s[...] for s in ds_refs]
        if bwd_fn is None:
            _, vjp = jax.vjp(lambda p, x, s: fn(chunk, p, x, s), pv, xv, sv)
            dp, dx, ds = vjp((dyv, dsv))
        else:
            dp, dx, ds = bwd_fn(chunk, pv, xv, sv, dyv, dsv)
        for r, v in zip(dp_refs, dp):
            r[...] += v
        for r, v in zip(dx_refs, dx):
            r[...] = v.astype(r.dtype)
        for r, v in zip(ds_refs, ds):
            r[...] = v

        if n_r:
            @pl.when((g == n_g - 1) & (c == n_c - 1))
            def _():
                _carry_run(False, kinds, ci_refs, co_refs, *sems)

    any_spec = pl.BlockSpec(memory_space=pl.ANY)
    in_specs = [pl.BlockSpec(b, rev(m)) for _, b, m in params] + [pl.BlockSpec(b, rev(m)) for _, b, m in xs]
    for shp in state_shapes:
        in_specs.append(pl.BlockSpec((1, 1) + shp, lambda g, c, n=len(shp): (g, n_c - 1 - c) + (0,) * n))
    in_specs += [pl.BlockSpec(b, rev(m)) for _, b, m in dys]
    in_specs += [any_spec] * (n_a + n_r)
    out_specs = [pl.BlockSpec(b, rev(m)) for _, b, m in params] + [pl.BlockSpec(b, rev(m)) for _, b, m in xs]
    out_specs += [any_spec] * n_r
    out_shape = [jax.ShapeDtypeStruct(a.shape, F32) for a, _, _ in params]
    out_shape += [jax.ShapeDtypeStruct(a.shape, d) for (a, _, _), d in zip(xs, dx_dtypes)]
    out_shape += _carry_shapes(carry)
    first_alias = n_p + n_x + n_s + n_y
    aliases = {first_alias + k: n_p + xi for k, xi in enumerate(into)}
    res = pl.pallas_call(
        body, name=name, grid=grid, in_specs=in_specs, out_specs=out_specs, out_shape=out_shape,
        scratch_shapes=[pltpu.VMEM(shp, F32) for shp in state_shapes] + _carry_scratch(carry),
        input_output_aliases=aliases,
        compiler_params=_cparams(("arbitrary", "arbitrary")),
    )(*[a for a, _, _ in params], *[a for a, _, _ in xs], *saved, *[a for a, _, _ in dys],
      *[dx_into[xi] for xi in into], *[a for _, a in carry])
    return list(res[:n_p]), list(res[n_p:n_p + n_x]), list(res[n_p + n_x:])


class _Op:
    def __init__(self, name, fn, grid, params, xs, outs, state_shapes=(), dx_dtypes=None, comm=None, bwd_fn=None):
        self.name, self.fn, self.grid, self.comm, self.bwd_fn = name, fn, grid, comm, bwd_fn
        self.params, self.xs, self.outs, self.state_shapes = params, xs, outs, list(state_shapes)
        self.dx_dtypes = dx_dtypes or [F32] * len(xs)
        reqs = comm.take(name + "_fwd") if comm is not None else []
        self.ys, self.saved, got = _chunk_fwd(name + "_fwd", fn, grid, params, xs, outs, self.state_shapes,
                                              carry=[(k, a) for k, a, _ in reqs])
        for (_, _, done), g in zip(reqs, got):
            done(g)

    def bwd(self, dys, dx_into=None):
        dy_defs = [(d, b, m) for d, (_, b, m, _) in zip(dys, self.outs)]
        reqs = self.comm.take(self.name + "_bwd") if self.comm is not None else []
        dps, dxs, got = _chunk_bwd(self.name + "_bwd", self.fn, self.grid, self.params, self.xs, self.saved, dy_defs,
                                   self.state_shapes, self.dx_dtypes, dx_into or {},
                                   carry=[(k, a) for k, a, _ in reqs], bwd_fn=self.bwd_fn)
        for (_, _, done), g in zip(reqs, got):
            done(g)
        return dps, dxs


def _whole(a):
    nd = a.ndim
    return (a, a.shape, lambda g, c: (0,) * nd)


def _pick(n, prefs):
    for p in prefs:
        if n % p == 0:
            return p
    return n


def _mm_blocks(mode, m, n, k):
    bn = _pick(n, (1408, 1280, 1024, 768, 512, 256, 128))
    if mode == "tn":
        return _pick(m, (1408, 1024, 768, 512, 256, 128)), bn, _pick(k, (2048, 1024, 512, 256, 128))
    bk = k if k <= 3840 else _pick(k, (2816, 1920, 1408, 1024, 512, 256, 128))
    return _pick(m, (1024, 512, 256, 128)), bn, bk


def _matmul(name, a, b, mode, *, add=None, out_dtype=F32, comm=None, a_cols=None, b_cols=None):
    a0, asize = a_cols if a_cols is not None else (0, a.shape[1])
    c0, csize = b_cols if b_cols is not None else (0, b.shape[1])
    if mode == "nn":
        (m, k), n = (a.shape[0], asize), csize
    elif mode == "nt":
        (m, k), n = (a.shape[0], asize), b.shape[0]
        assert k == csize
    else:
        (k, m), n = (a.shape[0], asize), csize
    bm, bn, bk = _mm_blocks(mode, m, n, k)
    assert c0 % (bk if mode == "nt" else bn) == 0 and a0 % (bm if mode == "tn" else bk) == 0
    j0, k0 = (0, c0 // bk) if mode == "nt" else (c0 // bn, 0)
    ia = a0 // (bm if mode == "tn" else bk)
    n_i, n_j, n_k = m // bm, n // bn, k // bk
    dims = {"nn": (((1,), (0,)), ((), ())), "nt": (((1,), (1,)), ((), ())), "tn": (((0,), (0,)), ((), ()))}[mode]
    has_add = add is not None
    reqs = comm.take(name) if comm is not None else []
    carry = [(kind, arr) for kind, arr, _ in reqs]
    kinds = [kind for kind, _ in carry]
    n_r = len(carry)

    def body(*refs):
        i = 2
        a_ref, b_ref = refs[0], refs[1]
        c_ref = refs[i] if has_add else None
        i += has_add
        ci_refs = refs[i:i + n_r]; i += n_r
        o_ref = refs[i]; i += 1
        co_refs = refs[i:i + n_r]; i += n_r
        acc = refs[i]; i += 1
        sems = refs[i:]
        ii, jj, kk = pl.program_id(0), pl.program_id(1), pl.program_id(2)

        if n_r:
            @pl.when((ii == 0) & (jj == 0) & (kk == 0))
            def _():
                _carry_run(True, kinds, ci_refs, co_refs, *sems)

        part = lax.dot_general(a_ref[...].astype(MM_DTYPE), b_ref[...].astype(MM_DTYPE), dims,
                               preferred_element_type=F32)

        def finish(r):
            if has_add:
                r = r + c_ref[...]
            o_ref[...] = r.astype(o_ref.dtype)

        if n_k == 1:
            finish(part)
        else:
            @pl.when(kk == 0)
            def _():
                acc[...] = part

            @pl.when((kk > 0) & (kk < n_k - 1))
            def _():
                acc[...] += part

            @pl.when(kk == n_k - 1)
            def _():
                finish(acc[...] + part)

        if n_r:
            @pl.when((ii == n_i - 1) & (jj == n_j - 1) & (kk == n_k - 1))
            def _():
                _carry_run(False, kinds, ci_refs, co_refs, *sems)

    if mode == "nn":
        a_spec = pl.BlockSpec((bm, bk), lambda i, j, kk: (i, ia + kk))
        b_spec = pl.BlockSpec((bk, bn), lambda i, j, kk: (kk, j0 + j))
    elif mode == "nt":
        a_spec = pl.BlockSpec((bm, bk), lambda i, j, kk: (i, ia + kk))
        b_spec = pl.BlockSpec((bn, bk), lambda i, j, kk: (j, k0 + kk))
    else:
        a_spec = pl.BlockSpec((bk, bm), lambda i, j, kk: (kk, ia + i))
        b_spec = pl.BlockSpec((bk, bn), lambda i, j, kk: (kk, j0 + j))
    any_spec = pl.BlockSpec(memory_space=pl.ANY)
    in_specs, args = [a_spec, b_spec], [a, b]
    if has_add:
        in_specs.append(pl.BlockSpec((bm, bn), lambda i, j, kk: (i, j)))
        args.append(add)
    res = pl.pallas_call(
        body, name=name, grid=(n_i, n_j, n_k), in_specs=in_specs + [any_spec] * n_r,
        out_specs=[pl.BlockSpec((bm, bn), lambda i, j, kk: (i, j))] + [any_spec] * n_r,
        out_shape=[jax.ShapeDtypeStruct((m, n), out_dtype)] + _carry_shapes(carry),
        scratch_shapes=[pltpu.VMEM((bm, bn) if n_k > 1 else (8, LANES), F32)] + _carry_scratch(carry),
        compiler_params=_cparams(("arbitrary", "arbitrary", "arbitrary")),
    )(*args, *[arr for _, arr in carry])
    for (_, _, done), g in zip(reqs, res[1:]):
        done(g)
    return res[0]


def _ln_res_fn(_, p, x, st):
    g, b = p
    xin, m = x
    pre = ALPHA * xin + m
    mu = jnp.mean(pre, -1, keepdims=True)
    d = pre - mu
    var = jnp.mean(d * d, -1, keepdims=True)
    return [d * lax.rsqrt(var + LN_EPS) * g + b], []


def _make_conv_fn(taps, act):
    def fn(_, p, x, st):
        ws, b = p[:taps], p[taps]
        (xin,), (prev,) = x, st
        n = xin.shape[0]
        ext = jnp.concatenate([prev, xin], axis=0)
        y = b
        for k in range(taps):
            y = y + ws[k] * _rows(_roll(ext, taps - 1 - k, 0), 8, 8 + n)
        if act:
            y = _silu(y)
        return [y], [_rows(xin, n - 8, n)]

    return fn


def _ffn_act_fn(_, p, x, st):
    n = x[0].shape[0]
    ys = []
    for half in range(2):
        ws, b = p[4 * half:4 * half + 3], p[4 * half + 3]
        ext = jnp.concatenate([st[half], x[half]], axis=0)
        y = b
        for k in range(3):
            y = y + ws[k] * _rows(_roll(ext, 2 - k, 0), 8, 8 + n)
        ys.append(y)
    return [_silu(ys[0]) * ys[1]], [_rows(x[0], n - 8, n), _rows(x[1], n - 8, n)]


def _ffn_act_bwd(_, p, x, st, dy, dst):
    (da,) = dy
    n, wd = x[0].shape
    rs = FFN_STRIP
    last = n // rs - 1
    zero8 = jnp.zeros((8, wd), F32)
    acc = [[zero8] * 4, [zero8] * 4]
    after = [zero8, zero8]
    strips = [[None] * (n // rs), [None] * (n // rs)]
    for i in reversed(range(n // rs)):
        r0 = rs * i
        taps, ys = [], []
        for half in range(2):
            w0, w1, w2, b = p[4 * half:4 * half + 4]
            xs = jnp.concatenate([st[half] if i == 0 else x[half][r0 - 8:r0], x[half][r0:r0 + rs]], axis=0)
            taps.append((pltpu.roll(xs, 2, 0)[8:], pltpu.roll(xs, 1, 0)[8:], xs[8:]))
            ys.append(b + w2 * taps[half][2] + w1 * taps[half][1] + w0 * taps[half][0])
        g, u = ys
        s = _sigmoid(g)
        d = da[r0:r0 + rs]
        dys = (d * u * (s * (1.0 + g * (1.0 - s))), d * (g * s))
        for half in range(2):
            w0, w1, w2, _ = p[4 * half:4 * half + 4]
            dyh = dys[half]
            for k, v in enumerate((dyh * taps[half][0], dyh * taps[half][1], dyh * taps[half][2], dyh)):
                for r in range(0, rs, 8):
                    acc[half][k] = acc[half][k] + v[r:r + 8]
            dyp = jnp.concatenate([dyh, after[half]], axis=0)
            dxs = w2 * dyh + w1 * pltpu.roll(dyp, rs + 8 - 1, 0)[:rs] + w0 * pltpu.roll(dyp, rs + 8 - 2, 0)[:rs]
            if i == last:
                dxs = jnp.concatenate([dxs[:rs - 8], dxs[rs - 8:] + dst[half]], axis=0)
            strips[half][i] = dxs
            after[half] = dyh[:8]
    dprev = []
    for half in range(2):
        w0, w1 = p[4 * half], p[4 * half + 1]
        head = jnp.concatenate([zero8, after[half]], axis=0)
        dprev.append((w1 * pltpu.roll(head, 16 - 1, 0) + w0 * pltpu.roll(head, 16 - 2, 0))[:8])
    dps = [jnp.sum(a, axis=0, keepdims=True) for half in range(2) for a in acc[half]]
    return dps, [jnp.concatenate(s_, axis=0) for s_ in strips], dprev


def _ssd_fn(_, p, x, st):
    dtb, alog, dsk = p
    xbc, dtr = x
    L = SSD_CHUNK
    tril = _iota((L, L), 0) >= _iota((L, L), 1)
    xs, bm, cm = _cols(xbc, 0, 512), _cols(xbc, 512, 640), _cols(xbc, 640, 768)
    dt = _softplus(dtr + dtb)
    da = dt * (-jnp.exp(alog))
    cs = _sel_dot(tril.astype(BF16), da)
    pick = ((_iota((LANES, 2 * LANES), 0) == 0) & (_iota((LANES, 2 * LANES), 1) < LANES)) | (
        (_iota((LANES, 2 * LANES), 0) == 64) & (_iota((LANES, 2 * LANES), 1) >= LANES))
    pick = pick.astype(BF16)
    tot = jnp.sum(da, axis=0, keepdims=True)
    xc = xs * dt
    xdec = xc * jnp.exp(tot - cs)
    cs_b, xc_b, xdec_b, ecs_b, etot_b, dsk_b, xs_b = (
        _split_cols(v, LANES) for v in (cs, xc, xdec, jnp.exp(cs), jnp.exp(tot), dsk, xs))
    ys, new_st = [], []
    for pr in range(4):
        grp = pr // 2
        c_g = cm * _lane_mask(64 * grp, 64 * grp + 64)
        gmat = _dot_nt(c_g, bm)
        cols2 = _split_cols(_dot_sel(cs_b[pr], pick), LANES)
        yd = jnp.zeros((L, LANES), F32)
        for half in range(2):
            col = cols2[half]
            diff = col - col.T
            dec = jnp.where(tril, jnp.exp(jnp.where(tril, diff, 0.0)), 0.0)
            yd = yd + _dot(gmat * dec, xc_b[pr]) * _lane_mask(64 * half, 64 * half + 64)
        s_in = st[pr]
        y_off = _dot(c_g, s_in) * ecs_b[pr]
        ys.append(yd + y_off + dsk_b[pr] * xs_b[pr])
        new_st.append(s_in * etot_b[pr] + _dot_tn(bm, xdec_b[pr]))
    return [jnp.concatenate(ys, axis=1)], new_st


def _hg_fn(_, p, x, st):
    (lb,) = p
    (xin,) = x
    L = HG_STEP
    n_lvl = L.bit_length() - 1
    hq, hf, hi = _split_cols(xin, 512)
    q = _silu(hq)
    logf = jnp.log(lb + (1.0 - lb) * _sigmoid(hf))
    k = (1.0 - lb) * _sigmoid(-hf)
    ti, si = _iota((L, L), 0), _iota((L, L), 1)
    bc = _sel_dot((ti >= si).astype(BF16), logf)
    tot = jnp.sum(logf, axis=0, keepdims=True)
    tn, sn = _iota((n_lvl * L, 1), 0), _iota((n_lvl * L, L), 1)
    row = tn & (L - 1)
    blk = L >> (tn >> n_lvl)
    piv = row - (row & (blk - 1)) + (blk >> 1)
    bcp_all = _sel_dot((sn == piv).astype(BF16), bc)
    t1 = _iota((L, 1), 0)
    qqs, kks, sames = [], [], []
    for lvl in range(n_lvl):
        size = L >> lvl
        upper = (t1 & (size - 1)) >= size // 2
        bcp = _rows(bcp_all, L * lvl, L * (lvl + 1))
        qqs.append(jnp.where(upper, q * jnp.exp(jnp.where(upper, bc - bcp, 0.0)), 0.0))
        kks.append(jnp.where(upper, 0.0, k * jnp.exp(jnp.where(upper, 0.0, bcp - bc))))
        sames.append((ti >> (n_lvl - lvl)) == (si >> (n_lvl - lvl)))
    qq_b = [_split_cols(v, LANES) for v in qqs]
    kk_b = [_split_cols(v, LANES) for v in kks]
    v_b, diag_b, q_in_b, k_out_b, etot_b = (
        _split_cols(v, LANES) for v in (hi, q * k, q * jnp.exp(bc), k * jnp.exp(tot - bc), jnp.exp(tot)))
    outs, new_st = [], []
    for h in range(4):
        attn = jnp.zeros((L, L), F32)
        for lvl in range(n_lvl):
            attn = attn + jnp.where(sames[lvl], _dot_nt(qq_b[lvl][h], kk_b[lvl][h]), 0.0)
        v = v_b[h]
        out = _dot(attn, v) + jnp.sum(diag_b[h], axis=-1, keepdims=True) * v
        outs.append(out + _dot_nt(q_in_b[h], st[h]))
        new_st.append(st[h] * etot_b[h] + _dot_tn(v, k_out_b[h]))
    return [jnp.concatenate(outs, axis=1)], new_st


def _swa_fn(chunk, p, x, st):
    (sinks,) = p
    (xin,) = x
    q, k, v = _cols(xin, 0, 512), _cols(xin, 512, 640), _cols(xin, 640, 768)
    kp, vp = st
    T = SWA_BLOCK
    kc = jnp.concatenate([kp, k], axis=0)
    vc = jnp.concatenate([vp, v], axis=0)
    qi, kj = _iota((T, 2 * T), 0), _iota((T, 2 * T), 1)
    rel = qi + T - kj
    mask = (rel >= 0) & (rel < T) & ((kj >= T) | (chunk > 0))
    srow = _iota((8, LANES), 0)
    q_b = _split_cols(q, LANES)
    outs = []
    for pr in range(4):
        grp = pr // 2
        gm = _lane_mask(64 * grp, 64 * grp + 64)
        km, vm = kc * gm, vc * gm
        q2 = q_b[pr]
        o2 = jnp.zeros((T, LANES), F32)
        for half in range(2):
            hm = _lane_mask(64 * half, 64 * half + 64)
            qh = q2 * hm
            if half != grp:
                qh = _roll(qh, 64, 1)
            s = _dot_nt(qh, km) * 0.125
            s = jnp.where(mask, s, MASK_VALUE)
            sink = jnp.mean(jnp.sum(jnp.where(srow == 2 * pr + half, sinks, 0.0), axis=0, keepdims=True),
                            axis=-1, keepdims=True)
            mx = lax.stop_gradient(jnp.maximum(jnp.max(s, axis=-1, keepdims=True), sink))
            e = jnp.exp(s - mx)
            den = jnp.sum(e, axis=-1, keepdims=True) + jnp.exp(sink - mx)
            o = _dot(e / den, vm)
            if half != grp:
                o = _roll(o, 64, 1)
            o2 = o2 + o * hm
        outs.append(o2)
    return [jnp.concatenate(outs, axis=1)], [k, v]


def _rg_gate_fn(_, p, x, st):
    wa, ba, wx, bx, lam = p
    (xc,) = x
    r = _sigmoid(_dot(xc, wa) + ba)
    i = _sigmoid(_dot(xc, wx) + bx)
    log_a = -RG_C * r * _softplus(-lam)
    a = jnp.exp(log_a)
    t = jnp.tanh(log_a)
    one_minus_a2 = -2.0 * t / (1.0 - t)
    u = jnp.sqrt(jnp.maximum(one_minus_a2, 0.0)) * (i * xc)
    return [a, u], []


def _rg_scan_fn(_, p, x, st):
    a, u = x
    (prev,) = st
    n = a.shape[0]
    row = _iota((n, 1), 0)
    s = 1
    while s < n:
        keep = row >= s
        a_s, u_s = _roll(a, s, 0), _roll(u, s, 0)
        u = jnp.where(keep, a * u_s + u, u)
        a = jnp.where(keep, a * a_s, a)
        s *= 2
    h_in = jnp.sum(jnp.where(_iota((8, 1), 0) == 7, prev, 0.0), axis=0, keepdims=True)
    h = u + a * h_in
    return [h], [_rows(h, n - 8, n)]


def _ab_post_fn(_, p, x, st):
    nw_ssd, nw_hg = p
    y, o, zg = x
    z, hgate = _split_cols(zg, 512)

    def rms(v, width):
        blocks = _split_cols(v, width)
        return jnp.concatenate([b * lax.rsqrt(jnp.mean(b * b, axis=-1, keepdims=True) + RMS_EPS) for b in blocks],
                               axis=1)

    ya = rms(y * _silu(z), 256) * nw_ssd
    yb = rms(o, 128) * nw_hg * _silu(hgate)
    return [jnp.concatenate([ya, yb], axis=1)], []


def _cd_post_fn(_, p, x, st):
    yc, h, gate = x
    return [jnp.concatenate([yc, h * _gelu_tanh(gate)], axis=1)], []


def _lb_fn(_, p, x, st):
    l0, l1 = x
    mx = lax.stop_gradient(jnp.maximum(l0, l1))
    e0, e1 = jnp.exp(l0 - mx), jnp.exp(l1 - mx)
    s0, s1 = e0 / (e0 + e1), e1 / (e0 + e1)
    return [jnp.clip(s0 - s0, 0.0, 1.0), jnp.clip((s0 + s1) - s0, 0.0, 1.0)], []


def _loss_kernel(y, target):
    t, d = y.shape
    bt = _pick(t, (512, 256, 128))

    def body(y_ref, t_ref, dy_ref, l_ref):
        @pl.when(pl.program_id(0) == 0)
        def _():
            l_ref[...] = jnp.zeros(l_ref.shape, F32)

        e = y_ref[...] - t_ref[...]
        dy_ref[...] = e * (1.0 / d)
        l_ref[...] += jnp.sum(e * e, axis=0, keepdims=True) * (0.5 / d)

    dy, part = pl.pallas_call(
        body, name="loss", grid=(t // bt,),
        in_specs=[pl.BlockSpec((bt, d), lambda i: (i, 0)), pl.BlockSpec((bt, d), lambda i: (i, 0))],
        out_specs=[pl.BlockSpec((bt, d), lambda i: (i, 0)), pl.BlockSpec((1, d), lambda i: (0, 0))],
        out_shape=[jax.ShapeDtypeStruct((t, d), F32), jax.ShapeDtypeStruct((1, d), F32)],
        compiler_params=_cparams(("arbitrary",)),
    )(y, target)
    return dy, jnp.sum(part)


def _adamw_math(parts, w_, m_, v_):
    c1 = 1.0 / (1.0 - ADAM_B1 ** ADAM_STEP)
    c2 = 1.0 / (1.0 - ADAM_B2 ** ADAM_STEP)
    g = parts[0].astype(F32)
    for s in range(1, N_DEV):
        g = g + parts[s].astype(F32)
    nm = ADAM_B1 * m_ + (1.0 - ADAM_B1) * g
    nv = ADAM_B2 * v_ + (1.0 - ADAM_B2) * (g * g)
    return g, -ADAM_LR * ((nm * c1) / (jnp.sqrt(nv * c2) + ADAM_EPS) + ADAM_WD * w_), nm, nv


def _adamw_big(name, parts, w, m, v):
    n_l, r, c = w.shape
    br = _pick(r, (256, 176, 128, 64, 32, 16, 8))

    def body(*refs):
        p_refs, (w_ref, m_ref, v_ref), outs = refs[:n_l], refs[n_l:n_l + 3], refs[n_l + 3:]
        for l in range(n_l):
            @pl.when(pl.program_id(0) == l)
            def _(p_ref=p_refs[l]):
                res = _adamw_math([p_ref[s] for s in range(N_DEV)], w_ref[...], m_ref[...], v_ref[...])
                for ref, val in zip(outs, res):
                    ref[...] = val

    blk = pl.BlockSpec((None, br, c), lambda l, i: (l, i, 0))
    p_specs = [pl.BlockSpec((N_DEV, br, c), lambda l, i, k=k: (0, jnp.where(l == k, i, 0), 0)) for k in range(n_l)]
    return pl.pallas_call(
        body, name=name, grid=(n_l, r // br), in_specs=p_specs + [blk, blk, blk],
        out_specs=[blk] * 4, out_shape=[jax.ShapeDtypeStruct(w.shape, F32)] * 4,
        compiler_params=_cparams(("arbitrary", "arbitrary")),
    )(*parts, w, m, v)


def _adamw_small(name, items):
    n = len(items)

    def body(*refs):
        ins, outs = refs[:4 * n], refs[4 * n:]
        for i in range(n):
            p_ref, w_ref, m_ref, v_ref = ins[4 * i:4 * i + 4]
            res = _adamw_math([p_ref[s] for s in range(N_DEV)], w_ref[...], m_ref[...], v_ref[...])
            for ref, val in zip(outs[4 * i:4 * i + 4], res):
                ref[...] = val

    flat = [a for it in items for a in it]
    out_shape = [jax.ShapeDtypeStruct(it[1].shape, F32) for it in items for _ in range(4)]
    res = pl.pallas_call(
        body, name=name, out_shape=out_shape,
        in_specs=[pl.BlockSpec(memory_space=pltpu.VMEM)] * len(flat),
        out_specs=[pl.BlockSpec(memory_space=pltpu.VMEM)] * len(out_shape),
        compiler_params=pltpu.CompilerParams(vmem_limit_bytes=VMEM_LIMIT),
    )(*flat)
    return [res[4 * i:4 * i + 4] for i in range(n)]


SHARDED = [("ab_w_in", 2), ("ab_w_out", 1), ("cd_w_in", 2), ("cd_w_out", 1), ("ffn_w_up", 2), ("ffn_w_down", 1),
           ("ssd_conv_w", 2), ("rg_conv_w", 2), ("rg_conv_b", 1), ("rg_ba", 1), ("rg_bx", 1), ("rg_lambda", 1),
           ("ffn_conv_w", 2), ("ln_g", 2), ("ln_b", 2)]
REPLICATED = ["ssd_conv_b", "ssd_dt_bias", "ssd_a_log", "ssd_d", "ssd_norm_w", "hg_lower", "hg_norm_w", "swa_sinks",
              "rg_wa", "rg_wx", "ffn_conv_b"]
WEIGHTS = ["ab_w_in", "ssd_conv_w", "ssd_conv_b", "ssd_dt_bias", "ssd_a_log", "ssd_d", "ssd_norm_w", "hg_lower",
           "hg_norm_w", "ab_w_out", "cd_w_in", "swa_sinks", "rg_conv_w", "rg_conv_b", "rg_wa", "rg_ba", "rg_wx",
           "rg_bx", "rg_lambda", "cd_w_out", "ffn_w_up", "ffn_conv_w", "ffn_conv_b", "ffn_w_down", "ln_g", "ln_b"]


def _as2d(a):
    return a.reshape(-1, a.shape[-1])


def _merge_shards(g, axis):
    g = jnp.moveaxis(g, 0, axis)
    shp = g.shape
    return g.reshape(shp[:axis] + (shp[axis] * shp[axis + 1],) + shp[axis + 2:])


def _split_shards(full, axis):
    shp = full.shape
    g = full.reshape(shp[:axis] + (N_DEV, shp[axis] // N_DEV) + shp[axis + 1:])
    return jnp.moveaxis(g, axis, 0)


def _ab_pad(w):
    z, xbc, dt = w[..., 0:512], w[..., 512:1280], w[..., 1280:1288]
    hqfi, hg = w[..., 1288:2824], w[..., 2824:3336]
    return jnp.concatenate([hqfi, jnp.repeat(dt, 64, axis=-1), z, hg, xbc], axis=-1)


def _ab_unpad(d):
    lead = d.shape[:-1]
    dt = d[..., AB_DT:AB_ZG].reshape(lead + (8, 64)).sum(-1)
    z, hg, xbc = d[..., AB_ZG:AB_ZG + 512], d[..., AB_ZG + 512:AB_XBC], d[..., AB_XBC:AB_PAD]
    return jnp.concatenate([z, xbc, dt, d[..., :AB_DT], hg], axis=-1)


def _cd_pad(w):
    return jnp.concatenate([w[..., :768], jnp.zeros(w.shape[:-1] + (256,), w.dtype), w[..., 768:]], axis=-1)


def _cd_unpad(d):
    return jnp.concatenate([d[..., :768], d[..., CD_GATE:CD_PAD]], axis=-1)


def _cat_halves(d):
    return jnp.concatenate(d, axis=1)


def _block_diag(w):
    eye = jnp.eye(8, dtype=w.dtype)
    return jnp.einsum("gij,gh->gihj", w, eye).reshape(512, 512)


def _block_diag_grad(d):
    return jnp.stack([d[64 * g:64 * g + 64, 64 * g:64 * g + 64] for g in range(8)])


def _same(a):
    return a


BIG = {"ab_w_in": (1, _ab_pad, _ab_unpad), "ab_w_out": (0, _same, _same), "cd_w_in": (1, _cd_pad, _cd_unpad),
       "cd_w_out": (0, _same, _same), "ffn_w_up": (1, _same, _cat_halves), "ffn_w_down": (0, _same, _same)}


class _MeshComm:
    def __init__(self, shards):
        self.shards, self.full, self.recv, self.posted = shards, {}, {}, {}

    def post(self, carrier, req):
        self.posted.setdefault(carrier, []).append(req)

    def take(self, carrier):
        return self.posted.pop(carrier, [])

    def take_all(self):
        reqs = [r for name in list(self.posted) for r in self.posted.pop(name)]
        return reqs

    def gather_req(self, name, layer):
        axis, prep, _ = BIG[name]

        def done(got):
            self.full[name, layer] = prep(_merge_shards(got, axis))

        return ("gather", self.shards[name][layer].astype(MM_DTYPE), done)

    def weight(self, name, layer):
        return self.full[name, layer]

    def grad_req(self, name, layer, d):
        axis, _, unprep = BIG[name]

        def done(got):
            self.recv[name, layer] = got

        return ("exchange", _split_shards(unprep(d), axis).astype(MM_DTYPE), done)


class _LocalComm:
    def __init__(self, full):
        self.full_w, self.grads = full, {}

    def post(self, carrier, req):
        pass

    def take(self, carrier):
        return []

    def gather_req(self, name, layer):
        return None

    def weight(self, name, layer):
        return BIG[name][1](self.full_w[name][layer].astype(MM_DTYPE))

    def grad_req(self, name, layer, d):
        self.grads[name, layer] = BIG[name][2](d)
        return None


def _row_vec(v):
    return v.reshape(1, -1)


def _heads64(v):
    return jnp.repeat(v, 64).reshape(1, 512)


def _local_step(x, target, w, comm):
    kinds = ["ab" if layer % 2 == 0 else "cd" for layer in range(DEPTH)]
    in_name = [f"{kinds[layer]}_in{layer // 2}" for layer in range(DEPTH)]
    core_name = [("hg" if layer % 2 == 0 else "swa") + f"{layer // 2}_fwd" for layer in range(DEPTH)]
    comm.post("ssd0_fwd", comm.gather_req("ffn_w_down", 0))
    comm.post(core_name[0], comm.gather_req("ffn_w_up", 0))
    for layer in range(DEPTH - 1):
        nxt, nj = kinds[layer + 1], (layer + 1) // 2
        if kinds[layer] == "ab":
            comm.post(in_name[layer], comm.gather_req(nxt + "_w_in", nj))
            comm.post(core_name[layer], comm.gather_req(nxt + "_w_out", nj))
            comm.post(core_name[layer], comm.gather_req("ffn_w_down", layer + 1))
        else:
            comm.post(in_name[layer], comm.gather_req(nxt + "_w_out", nj))
            comm.post(core_name[layer], comm.gather_req(nxt + "_w_in", nj))
            comm.post(f"ffn_up{layer}", comm.gather_req("ffn_w_down", layer + 1))
        comm.post(f"ffn_act{layer}_fwd", comm.gather_req("ffn_w_up", layer + 1))

    t = x.shape[0]
    bt = _pick(t, (512, 256, 128))
    nb = t // bt
    bs = _pick(t, (256, 128))

    def rowop(name, fn, params, xs, widths_out, out_dtype=F32, dx_dtypes=None):
        outs = [((t, wd), (bt, wd), lambda g, c: (c, 0), out_dtype) for wd in widths_out]
        return _Op(name, fn, (1, nb), params, xs, outs, dx_dtypes=dx_dtypes)

    def rowblk(arr, width, first=0):
        return (arr, (bt, width), lambda g, c: (c, first))

    one_row = lambda g, c: (0, 0)
    lb_op = _Op("hg_lb", _lb_fn, (1, 1), [],
                [(w["hg_lower"][0:1], (1, 512), one_row), (w["hg_lower"][1:2], (1, 512), one_row)],
                [((1, 512), (1, 512), one_row, F32)] * 2)
    lb_all = lb_op.ys

    tape = []
    grads = {}

    def add_grad(name, idx, val):
        grads.setdefault(name, {})[idx] = val

    def dw_matmul(name, a, b, wname, idx, carrier):
        d = _matmul(name, a, b, "tn", out_dtype=MM_DTYPE, comm=comm)
        comm.post(carrier, comm.grad_req(wname, idx, d))

    for layer in range(DEPTH):
        j = layer // 2
        rec = {"x_in": x}
        if layer % 2 == 0:
            h = _matmul(f"ab_in{j}", x, comm.weight("ab_w_in", j), "nn", comm=comm)
            conv_p = [_row_vec(w["ssd_conv_w"][j, k]) for k in range(4)] + [_row_vec(w["ssd_conv_b"][j])]
            conv = _Op(f"ssd_conv{j}", _make_conv_fn(4, True), (3, nb),
                       [(a, (1, 256), lambda g, c: (0, g)) for a in conv_p],
                       [(h, (bt, 256), lambda g, c: (c, AB_XBC // 256 + g))],
                       [((t, 768), (bt, 256), lambda g, c: (c, g), F32)], [(8, 256)], dx_dtypes=[MM_DTYPE])
            ssd = _Op(f"ssd{j}", _ssd_fn, (1, t // SSD_CHUNK),
                      [_whole(_heads64(w["ssd_dt_bias"][j])), _whole(_heads64(w["ssd_a_log"][j])),
                       _whole(_heads64(w["ssd_d"][j]))],
                      [(conv.ys[0], (SSD_CHUNK, 768), lambda g, c: (c, 0)),
                       (h, (SSD_CHUNK, 512), lambda g, c: (c, AB_DT // 512))],
                      [((t, 512), (SSD_CHUNK, 512), lambda g, c: (c, 0), F32)], [(LANES, LANES)] * 4,
                      dx_dtypes=[F32, MM_DTYPE], comm=comm)
            hg = _Op(f"hg{j}", _hg_fn, (1, t // HG_STEP), [_whole(lb_all[j])],
                     [(h, (HG_STEP, AB_DT), lambda g, c: (c, 0))],
                     [((t, 512), (HG_STEP, 512), lambda g, c: (c, 0), F32)], [(LANES, LANES)] * 4,
                     dx_dtypes=[MM_DTYPE], comm=comm)
            post = rowop(f"ab_post{j}", _ab_post_fn,
                         [_whole(_row_vec(w["ssd_norm_w"][j])), _whole(jnp.tile(_row_vec(w["hg_norm_w"][j]), (1, 4)))],
                         [rowblk(ssd.ys[0], 512), rowblk(hg.ys[0], 512), rowblk(h, 1024, AB_ZG // 1024)], [1024],
                         out_dtype=MM_DTYPE, dx_dtypes=[F32, F32, MM_DTYPE])
            rec.update(kind="ab", conv=conv, ssd=ssd, hg=hg, post=post)
        else:
            h = _matmul(f"cd_in{j}", x, comm.weight("cd_w_in", j), "nn", comm=comm)
            swa = _Op(f"swa{j}", _swa_fn, (1, t // SWA_BLOCK),
                      [_whole(jnp.tile(w["swa_sinks"][j].reshape(8, 1), (1, LANES)))],
                      [(h, (SWA_BLOCK, 1024), lambda g, c: (c, 0))],
                      [((t, 512), (SWA_BLOCK, 512), lambda g, c: (c, 0), F32)], [(SWA_BLOCK, LANES)] * 2,
                      dx_dtypes=[MM_DTYPE], comm=comm)
            conv_p = [_row_vec(w["rg_conv_w"][j, k]) for k in range(4)] + [_row_vec(w["rg_conv_b"][j])]
            conv = _Op(f"rg_conv{j}", _make_conv_fn(4, False), (2, nb),
                       [(a, (1, 256), lambda g, c: (0, g)) for a in conv_p],
                       [(h, (bt, 256), lambda g, c: (c, CD_XR // 256 + g))],
                       [((t, 512), (bt, 256), lambda g, c: (c, g), F32)], [(8, 256)], dx_dtypes=[MM_DTYPE])
            gate = rowop(f"rg_gate{j}", _rg_gate_fn,
                         [_whole(_block_diag(w["rg_wa"][j])), _whole(_row_vec(w["rg_ba"][j])),
                          _whole(_block_diag(w["rg_wx"][j])), _whole(_row_vec(w["rg_bx"][j])),
                          _whole(_row_vec(w["rg_lambda"][j]))],
                         [rowblk(conv.ys[0], 512)], [512, 512])
            scan = _Op(f"rg_scan{j}", _rg_scan_fn, (2, t // bs), [],
                       [(gate.ys[0], (bs, 256), lambda g, c: (c, g)), (gate.ys[1], (bs, 256), lambda g, c: (c, g))],
                       [((t, 512), (bs, 256), lambda g, c: (c, g), F32)], [(8, 256)])
            post = rowop(f"cd_post{j}", _cd_post_fn, [],
                         [rowblk(swa.ys[0], 512), rowblk(scan.ys[0], 512), rowblk(h, 512, CD_GATE // 512)], [1024],
                         out_dtype=MM_DTYPE, dx_dtypes=[F32, F32, MM_DTYPE])
            rec.update(kind="cd", swa=swa, conv=conv, gate=gate, scan=scan, post=post)
        kind = rec["kind"]
        ycat = post.ys[0]
        m = _matmul(f"mix_out{layer}", ycat, comm.weight(kind + "_w_out", j), "nn", comm=comm)
        ln1 = rowop(f"ln_a{layer}", _ln_res_fn,
                    [_whole(_row_vec(w["ln_g"][layer, 0])), _whole(_row_vec(w["ln_b"][layer, 0]))],
                    [rowblk(x, 1024), rowblk(m, 1024)], [1024], dx_dtypes=[F32, MM_DTYPE])
        x1 = ln1.ys[0]
        hu = _matmul(f"ffn_up{layer}", x1, comm.weight("ffn_w_up", layer), "nn", comm=comm)
        n_fb = FFN_DIM // FFN_BLK
        taps = [_row_vec(w["ffn_conv_w"][layer, k]) for k in range(3)] + [_row_vec(w["ffn_conv_b"][layer])]
        ba = _pick(t, (FFN_ROWS, 512, 256, 128))
        act = _Op(f"ffn_act{layer}", _ffn_act_fn, (n_fb, t // ba),
                  [(a, (1, FFN_BLK), lambda g, c: (0, g)) for a in taps]
                  + [(a, (1, FFN_BLK), lambda g, c: (0, n_fb + g)) for a in taps],
                  [(hu, (ba, FFN_BLK), lambda g, c: (c, g)), (hu, (ba, FFN_BLK), lambda g, c: (c, n_fb + g))],
                  [((t, FFN_DIM), (ba, FFN_BLK), lambda g, c: (c, g), MM_DTYPE)], [(8, FFN_BLK)] * 2,
                  dx_dtypes=[MM_DTYPE, MM_DTYPE], comm=comm, bwd_fn=_ffn_act_bwd)
        a = act.ys[0]
        f = _matmul(f"ffn_down{layer}", a, comm.weight("ffn_w_down", layer), "nn", comm=comm)
        ln2 = rowop(f"ln_f{layer}", _ln_res_fn,
                    [_whole(_row_vec(w["ln_g"][layer, 1])), _whole(_row_vec(w["ln_b"][layer, 1]))],
                    [rowblk(x1, 1024), rowblk(f, 1024)], [1024], dx_dtypes=[F32, MM_DTYPE])
        rec.update(ycat=ycat, ln1=ln1, x1=x1, act=act, a=a, ln2=ln2)
        tape.append(rec)
        x = ln2.ys[0]

    dx, loss = _loss_kernel(x, target)

    d_lb = [jnp.zeros((1, 512), F32), jnp.zeros((1, 512), F32)]
    for layer in reversed(range(DEPTH)):
        j = layer // 2
        rec = tape[layer]
        (dg, db), (dx1_res, df) = rec["ln2"].bwd([dx])
        add_grad("ln_g", (layer, 1), dg[0]); add_grad("ln_b", (layer, 1), db[0])
        dw_matmul(f"ffn_down_dw{layer}", rec["a"], df, "ffn_w_down", layer, f"ffn_act{layer}_bwd")
        da = _matmul(f"ffn_down_dx{layer}", df, comm.weight("ffn_w_down", layer), "nt", comm=comm)
        dpa, (dhg, dhu) = rec["act"].bwd([da])
        halves = [jnp.concatenate([dpa[k][0, :FFN_DIM], dpa[4 + k][0, FFN_DIM:]]) for k in range(4)]
        add_grad("ffn_conv_w", layer, jnp.stack(halves[:3]))
        add_grad("ffn_conv_b", layer, halves[3])
        core_bwd = ("hg" if rec["kind"] == "ab" else "swa") + f"{j}_bwd"
        gate_cols, up_cols = (0, FFN_DIM), (FFN_DIM, FFN_DIM)
        dw_up = (_matmul(f"ffn_up_dw_g{layer}", rec["x1"], dhg, "tn", out_dtype=MM_DTYPE, comm=comm, b_cols=gate_cols),
                 _matmul(f"ffn_up_dw_u{layer}", rec["x1"], dhu, "tn", out_dtype=MM_DTYPE, comm=comm, b_cols=up_cols))
        comm.post(core_bwd, comm.grad_req("ffn_w_up", layer, dw_up))
        w_up = comm.weight("ffn_w_up", layer)
        dx1 = _matmul(f"ffn_up_dx_g{layer}", dhg, w_up, "nt", a_cols=gate_cols, b_cols=gate_cols, add=dx1_res,
                      comm=comm)
        dx1 = _matmul(f"ffn_up_dx_u{layer}", dhu, w_up, "nt", a_cols=up_cols, b_cols=up_cols, add=dx1, comm=comm)
        (dg, db), (dx_res, dm) = rec["ln1"].bwd([dx1])
        add_grad("ln_g", (layer, 0), dg[0]); add_grad("ln_b", (layer, 0), db[0])
        kind = rec["kind"]
        dw_matmul(f"mix_out_dw{layer}", rec["ycat"], dm, kind + "_w_out", j, f"{kind}_in_dw{j}")
        dycat = _matmul(f"mix_out_dx{layer}", dm, comm.weight(kind + "_w_out", j), "nt", comm=comm)
        if kind == "ab":
            (dnw_s, dnw_h), (dy_ssd, do_hg, dh) = rec["post"].bwd([dycat])
            add_grad("ssd_norm_w", j, dnw_s[0]); add_grad("hg_norm_w", j, dnw_h[0].reshape(4, LANES).sum(0))
            (dlb,), (dh,) = rec["hg"].bwd([do_hg], dx_into={0: dh})
            d_lb[j] = dlb
            (ddtb, dalog, ddsk), (dxbc_c, dh) = rec["ssd"].bwd([dy_ssd], dx_into={1: dh})
            add_grad("ssd_dt_bias", j, ddtb[0].reshape(8, 64).sum(-1))
            add_grad("ssd_a_log", j, dalog[0].reshape(8, 64).sum(-1))
            add_grad("ssd_d", j, ddsk[0].reshape(8, 64).sum(-1))
            dcp, (dh,) = rec["conv"].bwd([dxbc_c], dx_into={0: dh})
            add_grad("ssd_conv_w", j, jnp.stack([dcp[k][0] for k in range(4)]))
            add_grad("ssd_conv_b", j, dcp[4][0])
        else:
            _, (dyc, dhs, dh) = rec["post"].bwd([dycat])
            _, (da_s, du_s) = rec["scan"].bwd([dhs])
            (dwa, dba, dwx, dbx, dlam), (dxc,) = rec["gate"].bwd([da_s, du_s])
            add_grad("rg_wa", j, _block_diag_grad(dwa)); add_grad("rg_wx", j, _block_diag_grad(dwx))
            add_grad("rg_ba", j, dba[0]); add_grad("rg_bx", j, dbx[0]); add_grad("rg_lambda", j, dlam[0])
            dcp, (dh,) = rec["conv"].bwd([dxc], dx_into={0: dh})
            add_grad("rg_conv_w", j, jnp.stack([dcp[k][0] for k in range(4)]))
            add_grad("rg_conv_b", j, dcp[4][0])
            (dsink,), (dh,) = rec["swa"].bwd([dyc], dx_into={0: dh})
            add_grad("swa_sinks", j, dsink.sum(-1))
        dw_matmul(f"{kind}_in_dw{j}", rec["x_in"], dh, kind + "_w_in", j,
                  f"ffn_act{layer - 1}_bwd" if layer > 0 else f"{kind}_in_dx{j}")
        dx = _matmul(f"{kind}_in_dx{j}", dh, comm.weight(kind + "_w_in", j), "nt", add=dx_res, comm=comm)

    _, (dl0, dl1) = lb_op.bwd(d_lb)
    out = {"hg_lower": jnp.concatenate([dl0, dl1], axis=0)}
    for name, parts in grads.items():
        keys = sorted(parts)
        if isinstance(keys[0], tuple):
            out[name] = jnp.stack([jnp.stack([parts[(l, s)] for s in range(2)]) for l in range(DEPTH)])
        else:
            out[name] = jnp.stack([parts[k] for k in keys])
    return loss, dx, out


def _local_step_full(x, target, full):
    comm = _LocalComm(full)
    loss, dx, grads = _local_step(x, target, {n: a for n, a in full.items() if n not in BIG}, comm)
    for name in BIG:
        grads[name] = jnp.stack([comm.grads[name, l] for l in range(full[name].shape[0])])
    return loss, dx, grads


def kernel(x, ab_w_in, ssd_conv_w, ssd_conv_b, ssd_dt_bias, ssd_a_log, ssd_d, ssd_norm_w, hg_lower, hg_norm_w, ab_w_out, cd_w_in, swa_sinks, rg_conv_w, rg_conv_b, rg_wa, rg_ba, rg_wx, rg_bx, rg_lambda, cd_w_out, ffn_w_up, ffn_conv_w, ffn_conv_b, ffn_w_down, ln_g, ln_b, loss_target, m_ab_w_in, m_ssd_conv_w, m_ssd_conv_b, m_ssd_dt_bias, m_ssd_a_log, m_ssd_d, m_ssd_norm_w, m_hg_lower, m_hg_norm_w, m_ab_w_out, m_cd_w_in, m_swa_sinks, m_rg_conv_w, m_rg_conv_b, m_rg_wa, m_rg_ba, m_rg_wx, m_rg_bx, m_rg_lambda, m_cd_w_out, m_ffn_w_up, m_ffn_conv_w, m_ffn_conv_b, m_ffn_w_down, m_ln_g, m_ln_b, v_ab_w_in, v_ssd_conv_w, v_ssd_conv_b, v_ssd_dt_bias, v_ssd_a_log, v_ssd_d, v_ssd_norm_w, v_hg_lower, v_hg_norm_w, v_ab_w_out, v_cd_w_in, v_swa_sinks, v_rg_conv_w, v_rg_conv_b, v_rg_wa, v_rg_ba, v_rg_wx, v_rg_bx, v_rg_lambda, v_cd_w_out, v_ffn_w_up, v_ffn_conv_w, v_ffn_conv_b, v_ffn_w_down, v_ln_g, v_ln_b):
    args = dict(locals())
    wts = {n: args[n] for n in WEIGHTS}
    mom = {n: args["m_" + n] for n in WEIGHTS}
    var = {n: args["v_" + n] for n in WEIGHTS}
    axis = dict(SHARDED)
    small = [n for n, _ in SHARDED if n not in BIG]
    comm = _MeshComm(wts)

    def run(name, reqs):
        for (_, _, done), got in zip(reqs, _remote_copies(name, [(k, a) for k, a, _ in reqs])):
            done(got)

    full = {n: wts[n] for n in REPLICATED}

    def keep_small(n):
        def done(got):
            full[n] = _merge_shards(got.reshape((N_DEV,) + wts[n].shape), axis[n])
        return ("gather", _as2d(wts[n]), done)

    run("gather_first", [comm.gather_req("ab_w_in", 0), keep_small("ssd_conv_w")])
    for req in [comm.gather_req("ab_w_out", 0)] + [keep_small(n) for n in small if n != "ssd_conv_w"]:
        comm.post("ab_in0", req)

    loss, grad_x, grads = _local_step(x[0], loss_target[0], full, comm)
    loss = lax.psum(loss, ("x", "y", "c"))

    parts = {}

    def keep_parts(n, kind, arr):
        return (kind, arr, lambda got: parts.__setitem__(n, got))

    last = comm.take_all()
    last += [keep_parts(n, "exchange", _split_shards(grads[n], axis[n]).reshape((N_DEV,) + _as2d(wts[n]).shape))
             for n in small]
    last += [keep_parts(n, "gather", _as2d(grads[n])) for n in REPLICATED]
    run("exchange_last", last)

    new = {}
    for n in BIG:
        new[n] = _adamw_big("adamw_" + n, [comm.recv[n, l] for l in range(wts[n].shape[0])], wts[n], mom[n], var[n])
    names = small + REPLICATED
    res = _adamw_small("adamw_small", [(parts[n], _as2d(wts[n]), _as2d(mom[n]), _as2d(var[n])) for n in names])
    for n, r in zip(names, res):
        new[n] = [a.reshape(wts[n].shape) for a in r]

    outs = [loss, grad_x[None]]
    for kind in range(4):
        outs += [new[n][kind] for n in WEIGHTS]
    return tuple(outs)
```

```python
import math

import jax
import jax.numpy as jnp
from jax import lax
from jax.experimental import pallas as pl
from jax.experimental.pallas import tpu as pltpu

F32 = jnp.float32
BF16 = jnp.bfloat16
MM_DTYPE = BF16

DEPTH = 4
N_DEV = 8
LN_EPS = 1e-5
RMS_EPS = 1e-6
MASK_VALUE = -1e9
ALPHA = (2 * DEPTH) ** 0.25
RG_C = 8.0
FFN_DIM = 2816
SSD_CHUNK = 128
HG_STEP = 128
SWA_BLOCK = 128
LANES = 128
VMEM_LIMIT = 56 * 1024 * 1024

ADAM_LR, ADAM_B1, ADAM_B2, ADAM_EPS, ADAM_WD, ADAM_STEP = 0.001, 0.9, 0.999, 1e-08, 0.01, 10

AB_HEADS, AB_DT, AB_ZG, AB_XBC, AB_PAD = 0, 1536, 2048, 3072, 3840
CD_QKV, CD_GATE, CD_XR, CD_PAD = 0, 1024, 1536, 2048
FFN_BLK = 256
FFN_ROWS = 1024
FFN_STRIP = 32


def _cols(x, lo, hi):
    n = x.shape[1]

    @jax.custom_vjp
    def f(x):
        return x[:, lo:hi]

    def bwd(_, g):
        parts = []
        if lo > 0:
            parts.append(jnp.zeros((g.shape[0], lo), g.dtype))
        parts.append(g)
        if hi < n:
            parts.append(jnp.zeros((g.shape[0], n - hi), g.dtype))
        return (jnp.concatenate(parts, axis=1) if len(parts) > 1 else g,)

    f.defvjp(lambda x: (f(x), None), bwd)
    return f(x)


def _split_cols(x, width):
    n = x.shape[1] // width

    @jax.custom_vjp
    def f(x):
        return tuple(x[:, width * i:width * (i + 1)] for i in range(n))

    f.defvjp(lambda x: (f(x), None), lambda _, gs: (jnp.concatenate(gs, axis=1),))
    return f(x)


def _rows(x, lo, hi):
    n = x.shape[0]

    @jax.custom_vjp
    def f(x):
        return x[lo:hi, :]

    def bwd(_, g):
        parts = []
        if lo > 0:
            parts.append(jnp.zeros((lo, g.shape[1]), g.dtype))
        parts.append(g)
        if hi < n:
            parts.append(jnp.zeros((n - hi, g.shape[1]), g.dtype))
        return (jnp.concatenate(parts, axis=0) if len(parts) > 1 else g,)

    f.defvjp(lambda x: (f(x), None), bwd)
    return f(x)


def _roll(x, shift, axis):
    n = x.shape[axis]
    shift = shift % n
    if shift == 0:
        return x

    @jax.custom_vjp
    def f(x):
        return pltpu.roll(x, shift, axis)

    f.defvjp(lambda x: (f(x), None), lambda _, g: (pltpu.roll(g, n - shift, axis),))
    return f(x)


def _dot(a, b):
    return lax.dot_general(a, b, (((1,), (0,)), ((), ())), preferred_element_type=F32)


def _dot_nt(a, b):
    return lax.dot_general(a, b, (((1,), (1,)), ((), ())), preferred_element_type=F32)


def _dot_tn(a, b):
    return lax.dot_general(a, b, (((0,), (0,)), ((), ())), preferred_element_type=F32)


def _split3(x):
    hi = x.astype(BF16)
    r = x - hi.astype(F32)
    mid = r.astype(BF16)
    return hi, mid, (r - mid.astype(F32)).astype(BF16)


def _sel_dot(sel, x):
    def run(mat, v, dims):
        n = v.shape[1]
        y = lax.dot_general(mat, jnp.concatenate(_split3(v), axis=1), dims, preferred_element_type=F32)
        return y[:, :n] + y[:, n:2 * n] + y[:, 2 * n:]

    @jax.custom_vjp
    def f(sel, x):
        return run(sel, x, (((1,), (0,)), ((), ())))

    def bwd(sel, g):
        return jnp.zeros_like(sel), run(sel, g, (((0,), (0,)), ((), ())))

    f.defvjp(lambda sel, x: (f(sel, x), sel), bwd)
    return f(sel, x)


def _dot_sel(x, sel):
    def run(v, mat, dims):
        m = v.shape[0]
        y = lax.dot_general(jnp.concatenate(_split3(v), axis=0), mat, dims, preferred_element_type=F32)
        return y[:m] + y[m:2 * m] + y[2 * m:]

    @jax.custom_vjp
    def f(x, sel):
        return run(x, sel, (((1,), (0,)), ((), ())))

    def bwd(sel, g):
        return run(g, sel, (((1,), (1,)), ((), ()))), jnp.zeros_like(sel)

    f.defvjp(lambda x, sel: (f(x, sel), sel), bwd)
    return f(x, sel)


def _sigmoid(x):
    return 0.5 * jnp.tanh(0.5 * x) + 0.5


def _silu(x):
    h = 0.5 * x
    return h + h * jnp.tanh(h)


def _softplus(x):
    return jnp.maximum(x, 0.0) + jnp.log(1.0 + jnp.exp(-jnp.abs(x)))


def _gelu_tanh(x):
    c = math.sqrt(2.0 / math.pi)
    return 0.5 * x * (1.0 + jnp.tanh(c * (x + 0.044715 * (x * x * x))))


def _iota(shape, axis):
    return lax.broadcasted_iota(jnp.int32, shape, axis)


def _lane_mask(lo, hi, width=LANES):
    lane = _iota((1, width), 1)
    return ((lane >= lo) & (lane < hi)).astype(F32)


def _mesh_pos():
    return lax.axis_index("x"), lax.axis_index("y"), lax.axis_index("c")


def _carry_shapes(carry):
    return [jax.ShapeDtypeStruct((N_DEV,) + a.shape if kind == "gather" else a.shape, a.dtype) for kind, a in carry]


def _carry_scratch(carry):
    n = len(carry)
    if n == 0:
        return []
    return [pltpu.SemaphoreType.DMA((n, N_DEV - 1)), pltpu.SemaphoreType.DMA((n, N_DEV - 1)),
            pltpu.SemaphoreType.DMA((n,))]


def _carry_run(start, kinds, in_refs, out_refs, send_sems, recv_sems, local_sems):
    x, y, cc = _mesh_pos()
    me = 4 * x + 2 * y + cc
    for i, kind in enumerate(kinds):
        mine = in_refs[i] if kind == "gather" else in_refs[i].at[me]
        local = pltpu.make_async_copy(mine, out_refs[i].at[me], local_sems.at[i])
        remote = []
        for k in range(1, N_DEV):
            px, py, pc = x ^ ((k >> 2) & 1), y ^ ((k >> 1) & 1), cc ^ (k & 1)
            src = in_refs[i] if kind == "gather" else in_refs[i].at[4 * px + 2 * py + pc]
            remote.append(pltpu.make_async_remote_copy(
                src_ref=src, dst_ref=out_refs[i].at[me],
                send_sem=send_sems.at[i, k - 1], recv_sem=recv_sems.at[i, k - 1],
                device_id=(px, py, pc), device_id_type=pl.DeviceIdType.MESH))
        if start:
            local.start()
            for cp in remote:
                cp.start()
        else:
            for cp in remote:
                cp.wait_recv()
            for cp in remote:
                cp.wait_send()
            local.wait()


def _remote_copies(name, carry):
    n = len(carry)
    kinds = [k for k, _ in carry]

    def body(*refs):
        sems = refs[2 * n:]
        _carry_run(True, kinds, refs[:n], refs[n:2 * n], *sems)
        _carry_run(False, kinds, refs[:n], refs[n:2 * n], *sems)

    return pl.pallas_call(
        body, name=name, out_shape=_carry_shapes(carry),
        in_specs=[pl.BlockSpec(memory_space=pl.ANY)] * n, out_specs=[pl.BlockSpec(memory_space=pl.ANY)] * n,
        scratch_shapes=_carry_scratch(carry),
    )(*[a for _, a in carry])


def _cparams(sem):
    return pltpu.CompilerParams(dimension_semantics=sem, vmem_limit_bytes=VMEM_LIMIT)


def _chunk_fwd(name, fn, grid, params, xs, outs, state_shapes, carry=()):
    n_g, n_c = grid
    n_p, n_x, n_o, n_s, n_r = len(params), len(xs), len(outs), len(state_shapes), len(carry)
    kinds = [k for k, _ in carry]

    def body(*refs):
        i = 0
        p_refs = refs[i:i + n_p]; i += n_p
        x_refs = refs[i:i + n_x]; i += n_x
        ci_refs = refs[i:i + n_r]; i += n_r
        o_refs = refs[i:i + n_o]; i += n_o
        sv_refs = refs[i:i + n_s]; i += n_s
        co_refs = refs[i:i + n_r]; i += n_r
        st_refs = refs[i:i + n_s]; i += n_s
        sems = refs[i:]
        g, c = pl.program_id(0), pl.program_id(1)

        if n_r:
            @pl.when((g == 0) & (c == 0))
            def _():
                _carry_run(True, kinds, ci_refs, co_refs, *sems)

        @pl.when(c == 0)
        def _():
            for s in st_refs:
                s[...] = jnp.zeros(s.shape, s.dtype)

        st = [s[...] for s in st_refs]
        ys, new_st = fn(c, [p[...] for p in p_refs], [x[...].astype(F32) for x in x_refs], st)
        for o, y in zip(o_refs, ys):
            o[...] = y.astype(o.dtype)
        for sv, s in zip(sv_refs, st):
            sv[0, 0] = s
        for s_ref, s in zip(st_refs, new_st):
            s_ref[...] = s

        if n_r:
            @pl.when((g == n_g - 1) & (c == n_c - 1))
            def _():
                _carry_run(False, kinds, ci_refs, co_refs, *sems)

    any_spec = pl.BlockSpec(memory_space=pl.ANY)
    in_specs = [pl.BlockSpec(b, m) for _, b, m in params] + [pl.BlockSpec(b, m) for _, b, m in xs] + [any_spec] * n_r
    out_specs = [pl.BlockSpec(b, m) for _, b, m, _ in outs]
    out_shape = [jax.ShapeDtypeStruct(s, d) for s, _, _, d in outs]
    for shp in state_shapes:
        out_specs.append(pl.BlockSpec((1, 1) + shp, lambda g, c, n=len(shp): (g, c) + (0,) * n))
        out_shape.append(jax.ShapeDtypeStruct((n_g, n_c) + shp, F32))
    out_specs += [any_spec] * n_r
    out_shape += _carry_shapes(carry)
    res = pl.pallas_call(
        body, name=name, grid=grid, in_specs=in_specs, out_specs=out_specs, out_shape=out_shape,
        scratch_shapes=[pltpu.VMEM(shp, F32) for shp in state_shapes] + _carry_scratch(carry),
        compiler_params=_cparams(("arbitrary", "arbitrary")),
    )(*[a for a, _, _ in params], *[a for a, _, _ in xs], *[a for _, a in carry])
    return list(res[:n_o]), list(res[n_o:n_o + n_s]), list(res[n_o + n_s:])


def _chunk_bwd(name, fn, grid, params, xs, saved, dys, state_shapes, dx_dtypes, dx_into, carry=(), bwd_fn=None):
    n_g, n_c = grid
    n_p, n_x, n_s, n_y, n_r = len(params), len(xs), len(state_shapes), len(dys), len(carry)
    kinds = [k for k, _ in carry]
    into = sorted(dx_into)
    n_a = len(into)

    def rev(m):
        return lambda g, c: m(g, n_c - 1 - c)

    def body(*refs):
        i = 0
        p_refs = refs[i:i + n_p]; i += n_p
        x_refs = refs[i:i + n_x]; i += n_x
        sv_refs = refs[i:i + n_s]; i += n_s
        dy_refs = refs[i:i + n_y]; i += n_y
        i += n_a
        ci_refs = refs[i:i + n_r]; i += n_r
        dp_refs = refs[i:i + n_p]; i += n_p
        dx_refs = refs[i:i + n_x]; i += n_x
        co_refs = refs[i:i + n_r]; i += n_r
        ds_refs = refs[i:i + n_s]; i += n_s
        sems = refs[i:]
        g, c = pl.program_id(0), pl.program_id(1)
        chunk = n_c - 1 - c

        if n_r:
            @pl.when((g == 0) & (c == 0))
            def _():
                _carry_run(True, kinds, ci_refs, co_refs, *sems)

        @pl.when(c == 0)
        def _():
            for s in ds_refs:
                s[...] = jnp.zeros(s.shape, s.dtype)
            for d in dp_refs:
                d[...] = jnp.zeros(d.shape, d.dtype)

        pv = [p[...] for p in p_refs]
        xv = [x[...].astype(F32) for x in x_refs]
        sv = [s[0, 0] for s in sv_refs]
        dyv, dsv = [d[...].astype(F32) for d in dy_refs], [s[...] for s in ds_refs]
        if bwd_fn is None:
            _, vjp = jax.vjp(lambda p, x, s: fn(chunk, p, x, s), pv, xv, sv)
            dp, dx, ds = vjp((dyv, dsv))
        else:
            dp, dx, ds = bwd_fn(chunk, pv, xv, sv, dyv, dsv)
        for r, v in zip(dp_refs, dp):
            r[...] += v
        for r, v in zip(dx_refs, dx):
            r[...] = v.astype(r.dtype)
        for r, v in zip(ds_refs, ds):
            r[...] = v

        if n_r:
            @pl.when((g == n_g - 1) & (c == n_c - 1))
            def _():
                _carry_run(False, kinds, ci_refs, co_refs, *sems)

    any_spec = pl.BlockSpec(memory_space=pl.ANY)
    in_specs = [pl.BlockSpec(b, rev(m)) for _, b, m in params] + [pl.BlockSpec(b, rev(m)) for _, b, m in xs]
    for shp in state_shapes:
        in_specs.append(pl.BlockSpec((1, 1) + shp, lambda g, c, n=len(shp): (g, n_c - 1 - c) + (0,) * n))
    in_specs += [pl.BlockSpec(b, rev(m)) for _, b, m in dys]
    in_specs += [any_spec] * (n_a + n_r)
    out_specs = [pl.BlockSpec(b, rev(m)) for _, b, m in params] + [pl.BlockSpec(b, rev(m)) for _, b, m in xs]
    out_specs += [any_spec] * n_r
    out_shape = [jax.ShapeDtypeStruct(a.shape, F32) for a, _, _ in params]
    out_shape += [jax.ShapeDtypeStruct(a.shape, d) for (a, _, _), d in zip(xs, dx_dtypes)]
    out_shape += _carry_shapes(carry)
    first_alias = n_p + n_x + n_s + n_y
    aliases = {first_alias + k: n_p + xi for k, xi in enumerate(into)}
    res = pl.pallas_call(
        body, name=name, grid=grid, in_specs=in_specs, out_specs=out_specs, out_shape=out_shape,
        scratch_shapes=[pltpu.VMEM(shp, F32) for shp in state_shapes] + _carry_scratch(carry),
        input_output_aliases=aliases,
        compiler_params=_cparams(("arbitrary", "arbitrary")),
    )(*[a for a, _, _ in params], *[a for a, _, _ in xs], *saved, *[a for a, _, _ in dys],
      *[dx_into[xi] for xi in into], *[a for _, a in carry])
    return list(res[:n_p]), list(res[n_p:n_p + n_x]), list(res[n_p + n_x:])


class _Op:
    def __init__(self, name, fn, grid, params, xs, outs, state_shapes=(), dx_dtypes=None, comm=None, bwd_fn=None):
        self.name, self.fn, self.grid, self.comm, self.bwd_fn = name, fn, grid, comm, bwd_fn
        self.params, self.xs, self.outs, self.state_shapes = params, xs, outs, list(state_shapes)
        self.dx_dtypes = dx_dtypes or [F32] * len(xs)
        reqs = comm.take(name + "_fwd") if comm is not None else []
        self.ys, self.saved, got = _chunk_fwd(name + "_fwd", fn, grid, params, xs, outs, self.state_shapes,
                                              carry=[(k, a) for k, a, _ in reqs])
        for (_, _, done), g in zip(reqs, got):
            done(g)

    def bwd(self, dys, dx_into=None):
        dy_defs = [(d, b, m) for d, (_, b, m, _) in zip(dys, self.outs)]
        reqs = self.comm.take(self.name + "_bwd") if self.comm is not None else []
        dps, dxs, got = _chunk_bwd(self.name + "_bwd", self.fn, self.grid, self.params, self.xs, self.saved, dy_defs,
                                   self.state_shapes, self.dx_dtypes, dx_into or {},
                                   carry=[(k, a) for k, a, _ in reqs], bwd_fn=self.bwd_fn)
        for (_, _, done), g in zip(reqs, got):
            done(g)
        return dps, dxs


def _whole(a):
    nd = a.ndim
    return (a, a.shape, lambda g, c: (0,) * nd)


def _pick(n, prefs):
    for p in prefs:
        if n % p == 0:
            return p
    return n


def _mm_blocks(mode, m, n, k):
    bn = _pick(n, (1408, 1280, 1024, 768, 512, 256, 128))
    if mode == "tn":
        return _pick(m, (1408, 1024, 768, 512, 256, 128)), bn, _pick(k, (2048, 1024, 512, 256, 128))
    bk = k if k <= 3840 else _pick(k, (2816, 1920, 1408, 1024, 512, 256, 128))
    return _pick(m, (1024, 512, 256, 128)), bn, bk


def _matmul(name, a, b, mode, *, add=None, out_dtype=F32, comm=None, a_cols=None, b_cols=None):
    a0, asize = a_cols if a_cols is not None else (0, a.shape[1])
    c0, csize = b_cols if b_cols is not None else (0, b.shape[1])
    if mode == "nn":
        (m, k), n = (a.shape[0], asize), csize
    elif mode == "nt":
        (m, k), n = (a.shape[0], asize), b.shape[0]
        assert k == csize
    else:
        (k, m), n = (a.shape[0], asize), csize
    bm, bn, bk = _mm_blocks(mode, m, n, k)
    assert c0 % (bk if mode == "nt" else bn) == 0 and a0 % (bm if mode == "tn" else bk) == 0
    j0, k0 = (0, c0 // bk) if mode == "nt" else (c0 // bn, 0)
    ia = a0 // (bm if mode == "tn" else bk)
    n_i, n_j, n_k = m // bm, n // bn, k // bk
    dims = {"nn": (((1,), (0,)), ((), ())), "nt": (((1,), (1,)), ((), ())), "tn": (((0,), (0,)), ((), ()))}[mode]
    has_add = add is not None
    reqs = comm.take(name) if comm is not None else []
    carry = [(kind, arr) for kind, arr, _ in reqs]
    kinds = [kind for kind, _ in carry]
    n_r = len(carry)

    def body(*refs):
        i = 2
        a_ref, b_ref = refs[0], refs[1]
        c_ref = refs[i] if has_add else None
        i += has_add
        ci_refs = refs[i:i + n_r]; i += n_r
        o_ref = refs[i]; i += 1
        co_refs = refs[i:i + n_r]; i += n_r
        acc = refs[i]; i += 1
        sems = refs[i:]
        ii, jj, kk = pl.program_id(0), pl.program_id(1), pl.program_id(2)

        if n_r:
            @pl.when((ii == 0) & (jj == 0) & (kk == 0))
            def _():
                _carry_run(True, kinds, ci_refs, co_refs, *sems)

        part = lax.dot_general(a_ref[...].astype(MM_DTYPE), b_ref[...].astype(MM_DTYPE), dims,
                               preferred_element_type=F32)

        def finish(r):
            if has_add:
                r = r + c_ref[...]
            o_ref[...] = r.astype(o_ref.dtype)

        if n_k == 1:
            finish(part)
        else:
            @pl.when(kk == 0)
            def _():
                acc[...] = part

            @pl.when((kk > 0) & (kk < n_k - 1))
            def _():
                acc[...] += part

            @pl.when(kk == n_k - 1)
            def _():
                finish(acc[...] + part)

        if n_r:
            @pl.when((ii == n_i - 1) & (jj == n_j - 1) & (kk == n_k - 1))
            def _():
                _carry_run(False, kinds, ci_refs, co_refs, *sems)

    if mode == "nn":
        a_spec = pl.BlockSpec((bm, bk), lambda i, j, kk: (i, ia + kk))
        b_spec = pl.BlockSpec((bk, bn), lambda i, j, kk: (kk, j0 + j))
    elif mode == "nt":
        a_spec = pl.BlockSpec((bm, bk), lambda i, j, kk: (i, ia + kk))
        b_spec = pl.BlockSpec((bn, bk), lambda i, j, kk: (j, k0 + kk))
    else:
        a_spec = pl.BlockSpec((bk, bm), lambda i, j, kk: (kk, ia + i))
        b_spec = pl.BlockSpec((bk, bn), lambda i, j, kk: (kk, j0 + j))
    any_spec = pl.BlockSpec(memory_space=pl.ANY)
    in_specs, args = [a_spec, b_spec], [a, b]
    if has_add:
        in_specs.append(pl.BlockSpec((bm, bn), lambda i, j, kk: (i, j)))
        args.append(add)
    res = pl.pallas_call(
        body, name=name, grid=(n_i, n_j, n_k), in_specs=in_specs + [any_spec] * n_r,
        out_specs=[pl.BlockSpec((bm, bn), lambda i, j, kk: (i, j))] + [any_spec] * n_r,
        out_shape=[jax.ShapeDtypeStruct((m, n), out_dtype)] + _carry_shapes(carry),
        scratch_shapes=[pltpu.VMEM((bm, bn) if n_k > 1 else (8, LANES), F32)] + _carry_scratch(carry),
        compiler_params=_cparams(("arbitrary", "arbitrary", "arbitrary")),
    )(*args, *[arr for _, arr in carry])
    for (_, _, done), g in zip(reqs, res[1:]):
        done(g)
    return res[0]


def _ln_res_fn(_, p, x, st):
    g, b = p
    xin, m = x
    pre = ALPHA * xin + m
    mu = jnp.mean(pre, -1, keepdims=True)
    d = pre - mu
    var = jnp.mean(d * d, -1, keepdims=True)
    return [d * lax.rsqrt(var + LN_EPS) * g + b], []


def _make_conv_fn(taps, act):
    def fn(_, p, x, st):
        ws, b = p[:taps], p[taps]
        (xin,), (prev,) = x, st
        n = xin.shape[0]
        ext = jnp.concatenate([prev, xin], axis=0)
        y = b
        for k in range(taps):
            y = y + ws[k] * _rows(_roll(ext, taps - 1 - k, 0), 8, 8 + n)
        if act:
            y = _silu(y)
        return [y], [_rows(xin, n - 8, n)]

    return fn


def _ffn_act_fn(_, p, x, st):
    n = x[0].shape[0]
    ys = []
    for half in range(2):
        ws, b = p[4 * half:4 * half + 3], p[4 * half + 3]
        ext = jnp.concatenate([st[half], x[half]], axis=0)
        y = b
        for k in range(3):
            y = y + ws[k] * _rows(_roll(ext, 2 - k, 0), 8, 8 + n)
        ys.append(y)
    return [_silu(ys[0]) * ys[1]], [_rows(x[0], n - 8, n), _rows(x[1], n - 8, n)]


def _ffn_act_bwd(_, p, x, st, dy, dst):
    (da,) = dy
    n, wd = x[0].shape
    rs = FFN_STRIP
    last = n // rs - 1
    zero8 = jnp.zeros((8, wd), F32)
    acc = [[zero8] * 4, [zero8] * 4]
    after = [zero8, zero8]
    strips = [[None] * (n // rs), [None] * (n // rs)]
    for i in reversed(range(n // rs)):
        r0 = rs * i
        taps, ys = [], []
        for half in range(2):
            w0, w1, w2, b = p[4 * half:4 * half + 4]
            xs = jnp.concatenate([st[half] if i == 0 else x[half][r0 - 8:r0], x[half][r0:r0 + rs]], axis=0)
            taps.append((pltpu.roll(xs, 2, 0)[8:], pltpu.roll(xs, 1, 0)[8:], xs[8:]))
            ys.append(b + w2 * taps[half][2] + w1 * taps[half][1] + w0 * taps[half][0])
        g, u = ys
        s = _sigmoid(g)
        d = da[r0:r0 + rs]
        dys = (d * u * (s * (1.0 + g * (1.0 - s))), d * (g * s))
        for half in range(2):
            w0, w1, w2, _ = p[4 * half:4 * half + 4]
            dyh = dys[half]
            for k, v in enumerate((dyh * taps[half][0], dyh * taps[half][1], dyh * taps[half][2], dyh)):
                for r in range(0, rs, 8):
                    acc[half][k] = acc[half][k] + v[r:r + 8]
            dyp = jnp.concatenate([dyh, after[half]], axis=0)
            dxs = w2 * dyh + w1 * pltpu.roll(dyp, rs + 8 - 1, 0)[:rs] + w0 * pltpu.roll(dyp, rs + 8 - 2, 0)[:rs]
            if i == last:
                dxs = jnp.concatenate([dxs[:rs - 8], dxs[rs - 8:] + dst[half]], axis=0)
            strips[half][i] = dxs
            after[half] = dyh[:8]
    dprev = []
    for half in range(2):
        w0, w1 = p[4 * half], p[4 * half + 1]
        head = jnp.concatenate([zero8, after[half]], axis=0)
        dprev.append((w1 * pltpu.roll(head, 16 - 1, 0) + w0 * pltpu.roll(head, 16 - 2, 0))[:8])
    dps = [jnp.sum(a, axis=0, keepdims=True) for half in range(2) for a in acc[half]]
    return dps, [jnp.concatenate(s_, axis=0) for s_ in strips], dprev


def _ssd_fn(_, p, x, st):
    dtb, alog, dsk = p
    xbc, dtr = x
    L = SSD_CHUNK
    tril = _iota((L, L), 0) >= _iota((L, L), 1)
    xs, bm, cm = _cols(xbc, 0, 512), _cols(xbc, 512, 640), _cols(xbc, 640, 768)
    dt = _softplus(dtr + dtb)
    da = dt * (-jnp.exp(alog))
    cs = _sel_dot(tril.astype(BF16), da)
    pick = ((_iota((LANES, 2 * LANES), 0) == 0) & (_iota((LANES, 2 * LANES), 1) < LANES)) | (
        (_iota((LANES, 2 * LANES), 0) == 64) & (_iota((LANES, 2 * LANES), 1) >= LANES))
    pick = pick.astype(BF16)
    tot = jnp.sum(da, axis=0, keepdims=True)
    xc = xs * dt
    xdec = xc * jnp.exp(tot - cs)
    cs_b, xc_b, xdec_b, ecs_b, etot_b, dsk_b, xs_b = (
        _split_cols(v, LANES) for v in (cs, xc, xdec, jnp.exp(cs), jnp.exp(tot), dsk, xs))
    ys, new_st = [], []
    for pr in range(4):
        grp = pr // 2
        c_g = cm * _lane_mask(64 * grp, 64 * grp + 64)
        gmat = _dot_nt(c_g, bm)
        cols2 = _split_cols(_dot_sel(cs_b[pr], pick), LANES)
        yd = jnp.zeros((L, LANES), F32)
        for half in range(2):
            col = cols2[half]
            diff = col - col.T
            dec = jnp.where(tril, jnp.exp(jnp.where(tril, diff, 0.0)), 0.0)
            yd = yd + _dot(gmat * dec, xc_b[pr]) * _lane_mask(64 * half, 64 * half + 64)
        s_in = st[pr]
        y_off = _dot(c_g, s_in) * ecs_b[pr]
        ys.append(yd + y_off + dsk_b[pr] * xs_b[pr])
        new_st.append(s_in * etot_b[pr] + _dot_tn(bm, xdec_b[pr]))
    return [jnp.concatenate(ys, axis=1)], new_st


def _hg_fn(_, p, x, st):
    (lb,) = p
    (xin,) = x
    L = HG_STEP
    n_lvl = L.bit_length() - 1
    hq, hf, hi = _split_cols(xin, 512)
    q = _silu(hq)
    logf = jnp.log(lb + (1.0 - lb) * _sigmoid(hf))
    k = (1.0 - lb) * _sigmoid(-hf)
    ti, si = _iota((L, L), 0), _iota((L, L), 1)
    bc = _sel_dot((ti >= si).astype(BF16), logf)
    tot = jnp.sum(logf, axis=0, keepdims=True)
    tn, sn = _iota((n_lvl * L, 1), 0), _iota((n_lvl * L, L), 1)
    row = tn & (L - 1)
    blk = L >> (tn >> n_lvl)
    piv = row - (row & (blk - 1)) + (blk >> 1)
    bcp_all = _sel_dot((sn == piv).astype(BF16), bc)
    t1 = _iota((L, 1), 0)
    qqs, kks, sames = [], [], []
    for lvl in range(n_lvl):
        size = L >> lvl
        upper = (t1 & (size - 1)) >= size // 2
        bcp = _rows(bcp_all, L * lvl, L * (lvl + 1))
        qqs.append(jnp.where(upper, q * jnp.exp(jnp.where(upper, bc - bcp, 0.0)), 0.0))
        kks.append(jnp.where(upper, 0.0, k * jnp.exp(jnp.where(upper, 0.0, bcp - bc))))
        sames.append((ti >> (n_lvl - lvl)) == (si >> (n_lvl - lvl)))
    qq_b = [_split_cols(v, LANES) for v in qqs]
    kk_b = [_split_cols(v, LANES) for v in kks]
    v_b, diag_b, q_in_b, k_out_b, etot_b = (
        _split_cols(v, LANES) for v in (hi, q * k, q * jnp.exp(bc), k * jnp.exp(tot - bc), jnp.exp(tot)))
    outs, new_st = [], []
    for h in range(4):
        attn = jnp.zeros((L, L), F32)
        for lvl in range(n_lvl):
            attn = attn + jnp.where(sames[lvl], _dot_nt(qq_b[lvl][h], kk_b[lvl][h]), 0.0)
        v = v_b[h]
        out = _dot(attn, v) + jnp.sum(diag_b[h], axis=-1, keepdims=True) * v
        outs.append(out + _dot_nt(q_in_b[h], st[h]))
        new_st.append(st[h] * etot_b[h] + _dot_tn(v, k_out_b[h]))
    return [jnp.concatenate(outs, axis=1)], new_st


def _swa_fn(chunk, p, x, st):
    (sinks,) = p
    (xin,) = x
    q, k, v = _cols(xin, 0, 512), _cols(xin, 512, 640), _cols(xin, 640, 768)
    kp, vp = st
    T = SWA_BLOCK
    kc = jnp.concatenate([kp, k], axis=0)
    vc = jnp.concatenate([vp, v], axis=0)
    qi, kj = _iota((T, 2 * T), 0), _iota((T, 2 * T), 1)
    rel = qi + T - kj
    mask = (rel >= 0) & (rel < T) & ((kj >= T) | (chunk > 0))
    srow = _iota((8, LANES), 0)
    q_b = _split_cols(q, LANES)
    outs = []
    for pr in range(4):
        grp = pr // 2
        gm = _lane_mask(64 * grp, 64 * grp + 64)
        km, vm = kc * gm, vc * gm
        q2 = q_b[pr]
        o2 = jnp.zeros((T, LANES), F32)
        for half in range(2):
            hm = _lane_mask(64 * half, 64 * half + 64)
            qh = q2 * hm
            if half != grp:
                qh = _roll(qh, 64, 1)
            s = _dot_nt(qh, km) * 0.125
            s = jnp.where(mask, s, MASK_VALUE)
            sink = jnp.mean(jnp.sum(jnp.where(srow == 2 * pr + half, sinks, 0.0), axis=0, keepdims=True),
                            axis=-1, keepdims=True)
            mx = lax.stop_gradient(jnp.maximum(jnp.max(s, axis=-1, keepdims=True), sink))
            e = jnp.exp(s - mx)
            den = jnp.sum(e, axis=-1, keepdims=True) + jnp.exp(sink - mx)
            o = _dot(e / den, vm)
            if half != grp:
                o = _roll(o, 64, 1)
            o2 = o2 + o * hm
        outs.append(o2)
    return [jnp.concatenate(outs, axis=1)], [k, v]


def _rg_gate_fn(_, p, x, st):
    wa, ba, wx, bx, lam = p
    (xc,) = x
    r = _sigmoid(_dot(xc, wa) + ba)
    i = _sigmoid(_dot(xc, wx) + bx)
    log_a = -RG_C * r * _softplus(-lam)
    a = jnp.exp(log_a)
    t = jnp.tanh(log_a)
    one_minus_a2 = -2.0 * t / (1.0 - t)
    u = jnp.sqrt(jnp.maximum(one_minus_a2, 0.0)) * (i * xc)
    return [a, u], []


def _rg_scan_fn(_, p, x, st):
    a, u = x
    (prev,) = st
    n = a.shape[0]
    row = _iota((n, 1), 0)
    s = 1
    while s < n:
        keep = row >= s
        a_s, u_s = _roll(a, s, 0), _roll(u, s, 0)
        u = jnp.where(keep, a * u_s + u, u)
        a = jnp.where(keep, a * a_s, a)
        s *= 2
    h_in = jnp.sum(jnp.where(_iota((8, 1), 0) == 7, prev, 0.0), axis=0, keepdims=True)
    h = u + a * h_in
    return [h], [_rows(h, n - 8, n)]


def _ab_post_fn(_, p, x, st):
    nw_ssd, nw_hg = p
    y, o, zg = x
    z, hgate = _split_cols(zg, 512)

    def rms(v, width):
        blocks = _split_cols(v, width)
        return jnp.concatenate([b * lax.rsqrt(jnp.mean(b * b, axis=-1, keepdims=True) + RMS_EPS) for b in blocks],
                               axis=1)

    ya = rms(y * _silu(z), 256) * nw_ssd
    yb = rms(o, 128) * nw_hg * _silu(hgate)
    return [jnp.concatenate([ya, yb], axis=1)], []


def _cd_post_fn(_, p, x, st):
    yc, h, gate = x
    return [jnp.concatenate([yc, h * _gelu_tanh(gate)], axis=1)], []


def _lb_fn(_, p, x, st):
    l0, l1 = x
    mx = lax.stop_gradient(jnp.maximum(l0, l1))
    e0, e1 = jnp.exp(l0 - mx), jnp.exp(l1 - mx)
    s0, s1 = e0 / (e0 + e1), e1 / (e0 + e1)
    return [jnp.clip(s0 - s0, 0.0, 1.0), jnp.clip((s0 + s1) - s0, 0.0, 1.0)], []


def _loss_kernel(y, target):
    t, d = y.shape
    bt = _pick(t, (512, 256, 128))

    def body(y_ref, t_ref, dy_ref, l_ref):
        @pl.when(pl.program_id(0) == 0)
        def _():
            l_ref[...] = jnp.zeros(l_ref.shape, F32)

        e = y_ref[...] - t_ref[...]
        dy_ref[...] = e * (1.0 / d)
        l_ref[...] += jnp.sum(e * e, axis=0, keepdims=True) * (0.5 / d)

    dy, part = pl.pallas_call(
        body, name="loss", grid=(t // bt,),
        in_specs=[pl.BlockSpec((bt, d), lambda i: (i, 0)), pl.BlockSpec((bt, d), lambda i: (i, 0))],
        out_specs=[pl.BlockSpec((bt, d), lambda i: (i, 0)), pl.BlockSpec((1, d), lambda i: (0, 0))],
        out_shape=[jax.ShapeDtypeStruct((t, d), F32), jax.ShapeDtypeStruct((1, d), F32)],
        compiler_params=_cparams(("arbitrary",)),
    )(y, target)
    return dy, jnp.sum(part)


def _adamw_math(parts, w_, m_, v_):
    c1 = 1.0 / (1.0 - ADAM_B1 ** ADAM_STEP)
    c2 = 1.0 / (1.0 - ADAM_B2 ** ADAM_STEP)
    g = parts[0].astype(F32)
    for s in range(1, N_DEV):
        g = g + parts[s].astype(F32)
    nm = ADAM_B1 * m_ + (1.0 - ADAM_B1) * g
    nv = ADAM_B2 * v_ + (1.0 - ADAM_B2) * (g * g)
    return g, -ADAM_LR * ((nm * c1) / (jnp.sqrt(nv * c2) + ADAM_EPS) + ADAM_WD * w_), nm, nv


def _adamw_big(name, parts, w, m, v):
    n_l, r, c = w.shape
    br = _pick(r, (256, 176, 128, 64, 32, 16, 8))

    def body(*refs):
        p_refs, (w_ref, m_ref, v_ref), outs = refs[:n_l], refs[n_l:n_l + 3], refs[n_l + 3:]
        for l in range(n_l):
            @pl.when(pl.program_id(0) == l)
            def _(p_ref=p_refs[l]):
                res = _adamw_math([p_ref[s] for s in range(N_DEV)], w_ref[...], m_ref[...], v_ref[...])
                for ref, val in zip(outs, res):
                    ref[...] = val

    blk = pl.BlockSpec((None, br, c), lambda l, i: (l, i, 0))
    p_specs = [pl.BlockSpec((N_DEV, br, c), lambda l, i, k=k: (0, jnp.where(l == k, i, 0), 0)) for k in range(n_l)]
    return pl.pallas_call(
        body, name=name, grid=(n_l, r // br), in_specs=p_specs + [blk, blk, blk],
        out_specs=[blk] * 4, out_shape=[jax.ShapeDtypeStruct(w.shape, F32)] * 4,
        compiler_params=_cparams(("arbitrary", "arbitrary")),
    )(*parts, w, m, v)


def _adamw_small(name, items):
    n = len(items)

    def body(*refs):
        ins, outs = refs[:4 * n], refs[4 * n:]
        for i in range(n):
            p_ref, w_ref, m_ref, v_ref = ins[4 * i:4 * i + 4]
            res = _adamw_math([p_ref[s] for s in range(N_DEV)], w_ref[...], m_ref[...], v_ref[...])
            for ref, val in zip(outs[4 * i:4 * i + 4], res):
                ref[...] = val

    flat = [a for it in items for a in it]
    out_shape = [jax.ShapeDtypeStruct(it[1].shape, F32) for it in items for _ in range(4)]
    res = pl.pallas_call(
        body, name=name, out_shape=out_shape,
        in_specs=[pl.BlockSpec(memory_space=pltpu.VMEM)] * len(flat),
        out_specs=[pl.BlockSpec(memory_space=pltpu.VMEM)] * len(out_shape),
        compiler_params=pltpu.CompilerParams(vmem_limit_bytes=VMEM_LIMIT),
    )(*flat)
    return [res[4 * i:4 * i + 4] for i in range(n)]


SHARDED = [("ab_w_in", 2), ("ab_w_out", 1), ("cd_w_in", 2), ("cd_w_out", 1), ("ffn_w_up", 2), ("ffn_w_down", 1),
           ("ssd_conv_w", 2), ("rg_conv_w", 2), ("rg_conv_b", 1), ("rg_ba", 1), ("rg_bx", 1), ("rg_lambda", 1),
           ("ffn_conv_w", 2), ("ln_g", 2), ("ln_b", 2)]
REPLICATED = ["ssd_conv_b", "ssd_dt_bias", "ssd_a_log", "ssd_d", "ssd_norm_w", "hg_lower", "hg_norm_w", "swa_sinks",
              "rg_wa", "rg_wx", "ffn_conv_b"]
WEIGHTS = ["ab_w_in", "ssd_conv_w", "ssd_conv_b", "ssd_dt_bias", "ssd_a_log", "ssd_d", "ssd_norm_w", "hg_lower",
           "hg_norm_w", "ab_w_out", "cd_w_in", "swa_sinks", "rg_conv_w", "rg_conv_b", "rg_wa", "rg_ba", "rg_wx",
           "rg_bx", "rg_lambda", "cd_w_out", "ffn_w_up", "ffn_conv_w", "ffn_conv_b", "ffn_w_down", "ln_g", "ln_b"]


def _as2d(a):
    return a.reshape(-1, a.shape[-1])


def _merge_shards(g, axis):
    g = jnp.moveaxis(g, 0, axis)
    shp = g.shape
    return g.reshape(shp[:axis] + (shp[axis] * shp[axis + 1],) + shp[axis + 2:])


def _split_shards(full, axis):
    shp = full.shape
    g = full.reshape(shp[:axis] + (N_DEV, shp[axis] // N_DEV) + shp[axis + 1:])
    return jnp.moveaxis(g, axis, 0)


def _ab_pad(w):
    z, xbc, dt = w[..., 0:512], w[..., 512:1280], w[..., 1280:1288]
    hqfi, hg = w[..., 1288:2824], w[..., 2824:3336]
    return jnp.concatenate([hqfi, jnp.repeat(dt, 64, axis=-1), z, hg, xbc], axis=-1)


def _ab_unpad(d):
    lead = d.shape[:-1]
    dt = d[..., AB_DT:AB_ZG].reshape(lead + (8, 64)).sum(-1)
    z, hg, xbc = d[..., AB_ZG:AB_ZG + 512], d[..., AB_ZG + 512:AB_XBC], d[..., AB_XBC:AB_PAD]
    return jnp.concatenate([z, xbc, dt, d[..., :AB_DT], hg], axis=-1)


def _cd_pad(w):
    return jnp.concatenate([w[..., :768], jnp.zeros(w.shape[:-1] + (256,), w.dtype), w[..., 768:]], axis=-1)


def _cd_unpad(d):
    return jnp.concatenate([d[..., :768], d[..., CD_GATE:CD_PAD]], axis=-1)


def _cat_halves(d):
    return jnp.concatenate(d, axis=1)


def _block_diag(w):
    eye = jnp.eye(8, dtype=w.dtype)
    return jnp.einsum("gij,gh->gihj", w, eye).reshape(512, 512)


def _block_diag_grad(d):
    return jnp.stack([d[64 * g:64 * g + 64, 64 * g:64 * g + 64] for g in range(8)])


def _same(a):
    return a


BIG = {"ab_w_in": (1, _ab_pad, _ab_unpad), "ab_w_out": (0, _same, _same), "cd_w_in": (1, _cd_pad, _cd_unpad),
       "cd_w_out": (0, _same, _same), "ffn_w_up": (1, _same, _cat_halves), "ffn_w_down": (0, _same, _same)}


class _MeshComm:
    def __init__(self, shards):
        self.shards, self.full, self.recv, self.posted = shards, {}, {}, {}

    def post(self, carrier, req):
        self.posted.setdefault(carrier, []).append(req)

    def take(self, carrier):
        return self.posted.pop(carrier, [])

    def take_all(self):
        reqs = [r for name in list(self.posted) for r in self.posted.pop(name)]
        return reqs

    def gather_req(self, name, layer):
        axis, prep, _ = BIG[name]

        def done(got):
            self.full[name, layer] = prep(_merge_shards(got, axis))

        return ("gather", self.shards[name][layer].astype(MM_DTYPE), done)

    def weight(self, name, layer):
        return self.full[name, layer]

    def grad_req(self, name, layer, d):
        axis, _, unprep = BIG[name]

        def done(got):
            self.recv[name, layer] = got

        return ("exchange", _split_shards(unprep(d), axis).astype(MM_DTYPE), done)


class _LocalComm:
    def __init__(self, full):
        self.full_w, self.grads = full, {}

    def post(self, carrier, req):
        pass

    def take(self, carrier):
        return []

    def gather_req(self, name, layer):
        return None

    def weight(self, name, layer):
        return BIG[name][1](self.full_w[name][layer].astype(MM_DTYPE))

    def grad_req(self, name, layer, d):
        self.grads[name, layer] = BIG[name][2](d)
        return None


def _row_vec(v):
    return v.reshape(1, -1)


def _heads64(v):
    return jnp.repeat(v, 64).reshape(1, 512)


def _local_step(x, target, w, comm):
    kinds = ["ab" if layer % 2 == 0 else "cd" for layer in range(DEPTH)]
    in_name = [f"{kinds[layer]}_in{layer // 2}" for layer in range(DEPTH)]
    core_name = [("hg" if layer % 2 == 0 else "swa") + f"{layer // 2}_fwd" for layer in range(DEPTH)]
    comm.post("ssd0_fwd", comm.gather_req("ffn_w_down", 0))
    comm.post(core_name[0], comm.gather_req("ffn_w_up", 0))
    for layer in range(DEPTH - 1):
        nxt, nj = kinds[layer + 1], (layer + 1) // 2
        if kinds[layer] == "ab":
            comm.post(in_name[layer], comm.gather_req(nxt + "_w_in", nj))
            comm.post(core_name[layer], comm.gather_req(nxt + "_w_out", nj))
            comm.post(core_name[layer], comm.gather_req("ffn_w_down", layer + 1))
        else:
            comm.post(in_name[layer], comm.gather_req(nxt + "_w_out", nj))
            comm.post(core_name[layer], comm.gather_req(nxt + "_w_in", nj))
            comm.post(f"ffn_up{layer}", comm.gather_req("ffn_w_down", layer + 1))
        comm.post(f"ffn_act{layer}_fwd", comm.gather_req("ffn_w_up", layer + 1))

    t = x.shape[0]
    bt = _pick(t, (512, 256, 128))
    nb = t // bt
    bs = _pick(t, (512, 256, 128))
    bc = _pick(t, (FFN_ROWS, 512, 256, 128))

    def rowop(name, fn, params, xs, widths_out, out_dtype=F32, dx_dtypes=None):
        outs = [((t, wd), (bt, wd), lambda g, c: (c, 0), out_dtype) for wd in widths_out]
        return _Op(name, fn, (1, nb), params, xs, outs, dx_dtypes=dx_dtypes)

    def rowblk(arr, width, first=0):
        return (arr, (bt, width), lambda g, c: (c, first))

    one_row = lambda g, c: (0, 0)
    lb_op = _Op("hg_lb", _lb_fn, (1, 1), [],
                [(w["hg_lower"][0:1], (1, 512), one_row), (w["hg_lower"][1:2], (1, 512), one_row)],
                [((1, 512), (1, 512), one_row, F32)] * 2)
    lb_all = lb_op.ys

    tape = []
    grads = {}

    def add_grad(name, idx, val):
        grads.setdefault(name, {})[idx] = val

    def dw_matmul(name, a, b, wname, idx, carrier):
        d = _matmul(name, a, b, "tn", out_dtype=MM_DTYPE, comm=comm)
        comm.post(carrier, comm.grad_req(wname, idx, d))

    for layer in range(DEPTH):
        j = layer // 2
        rec = {"x_in": x}
        if layer % 2 == 0:
            h = _matmul(f"ab_in{j}", x, comm.weight("ab_w_in", j), "nn", comm=comm)
            conv_p = [_row_vec(w["ssd_conv_w"][j, k]) for k in range(4)] + [_row_vec(w["ssd_conv_b"][j])]
            conv = _Op(f"ssd_conv{j}", _make_conv_fn(4, True), (3, t // bc),
                       [(a, (1, 256), lambda g, c: (0, g)) for a in conv_p],
                       [(h, (bc, 256), lambda g, c: (c, AB_XBC // 256 + g))],
                       [((t, 768), (bc, 256), lambda g, c: (c, g), F32)], [(8, 256)], dx_dtypes=[MM_DTYPE])
            ssd = _Op(f"ssd{j}", _ssd_fn, (1, t // SSD_CHUNK),
                      [_whole(_heads64(w["ssd_dt_bias"][j])), _whole(_heads64(w["ssd_a_log"][j])),
                       _whole(_heads64(w["ssd_d"][j]))],
                      [(conv.ys[0], (SSD_CHUNK, 768), lambda g, c: (c, 0)),
                       (h, (SSD_CHUNK, 512), lambda g, c: (c, AB_DT // 512))],
                      [((t, 512), (SSD_CHUNK, 512), lambda g, c: (c, 0), F32)], [(LANES, LANES)] * 4,
                      dx_dtypes=[F32, MM_DTYPE], comm=comm)
            hg = _Op(f"hg{j}", _hg_fn, (1, t // HG_STEP), [_whole(lb_all[j])],
                     [(h, (HG_STEP, AB_DT), lambda g, c: (c, 0))],
                     [((t, 512), (HG_STEP, 512), lambda g, c: (c, 0), F32)], [(LANES, LANES)] * 4,
                     dx_dtypes=[MM_DTYPE], comm=comm)
            post = rowop(f"ab_post{j}", _ab_post_fn,
                         [_whole(_row_vec(w["ssd_norm_w"][j])), _whole(jnp.tile(_row_vec(w["hg_norm_w"][j]), (1, 4)))],
                         [rowblk(ssd.ys[0], 512), rowblk(hg.ys[0], 512), rowblk(h, 1024, AB_ZG // 1024)], [1024],
                         out_dtype=MM_DTYPE, dx_dtypes=[F32, F32, MM_DTYPE])
            rec.update(kind="ab", conv=conv, ssd=ssd, hg=hg, post=post)
        else:
            h = _matmul(f"cd_in{j}", x, comm.weight("cd_w_in", j), "nn", comm=comm)
            swa = _Op(f"swa{j}", _swa_fn, (1, t // SWA_BLOCK),
                      [_whole(jnp.tile(w["swa_sinks"][j].reshape(8, 1), (1, LANES)))],
                      [(h, (SWA_BLOCK, 1024), lambda g, c: (c, 0))],
                      [((t, 512), (SWA_BLOCK, 512), lambda g, c: (c, 0), F32)], [(SWA_BLOCK, LANES)] * 2,
                      dx_dtypes=[MM_DTYPE], comm=comm)
            conv_p = [_row_vec(w["rg_conv_w"][j, k]) for k in range(4)] + [_row_vec(w["rg_conv_b"][j])]
            conv = _Op(f"rg_conv{j}", _make_conv_fn(4, False), (2, t // bc),
                       [(a, (1, 256), lambda g, c: (0, g)) for a in conv_p],
                       [(h, (bc, 256), lambda g, c: (c, CD_XR // 256 + g))],
                       [((t, 512), (bc, 256), lambda g, c: (c, g), F32)], [(8, 256)], dx_dtypes=[MM_DTYPE])
            gate = rowop(f"rg_gate{j}", _rg_gate_fn,
                         [_whole(_block_diag(w["rg_wa"][j])), _whole(_row_vec(w["rg_ba"][j])),
                          _whole(_block_diag(w["rg_wx"][j])), _whole(_row_vec(w["rg_bx"][j])),
                          _whole(_row_vec(w["rg_lambda"][j]))],
                         [rowblk(conv.ys[0], 512)], [512, 512])
            scan = _Op(f"rg_scan{j}", _rg_scan_fn, (2, t // bs), [],
                       [(gate.ys[0], (bs, 256), lambda g, c: (c, g)), (gate.ys[1], (bs, 256), lambda g, c: (c, g))],
                       [((t, 512), (bs, 256), lambda g, c: (c, g), F32)], [(8, 256)])
            post = rowop(f"cd_post{j}", _cd_post_fn, [],
                         [rowblk(swa.ys[0], 512), rowblk(scan.ys[0], 512), rowblk(h, 512, CD_GATE // 512)], [1024],
                         out_dtype=MM_DTYPE, dx_dtypes=[F32, F32, MM_DTYPE])
            rec.update(kind="cd", swa=swa, conv=conv, gate=gate, scan=scan, post=post)
        kind = rec["kind"]
        ycat = post.ys[0]
        m = _matmul(f"mix_out{layer}", ycat, comm.weight(kind + "_w_out", j), "nn", comm=comm)
        ln1 = rowop(f"ln_a{layer}", _ln_res_fn,
                    [_whole(_row_vec(w["ln_g"][layer, 0])), _whole(_row_vec(w["ln_b"][layer, 0]))],
                    [rowblk(x, 1024), rowblk(m, 1024)], [1024], dx_dtypes=[F32, MM_DTYPE])
        x1 = ln1.ys[0]
        hu = _matmul(f"ffn_up{layer}", x1, comm.weight("ffn_w_up", layer), "nn", comm=comm)
        n_fb = FFN_DIM // FFN_BLK
        taps = [_row_vec(w["ffn_conv_w"][layer, k]) for k in range(3)] + [_row_vec(w["ffn_conv_b"][layer])]
        ba = _pick(t, (FFN_ROWS, 512, 256, 128))
        act = _Op(f"ffn_act{layer}", _ffn_act_fn, (n_fb, t // ba),
                  [(a, (1, FFN_BLK), lambda g, c: (0, g)) for a in taps]
                  + [(a, (1, FFN_BLK), lambda g, c: (0, n_fb + g)) for a in taps],
                  [(hu, (ba, FFN_BLK), lambda g, c: (c, g)), (hu, (ba, FFN_BLK), lambda g, c: (c, n_fb + g))],
                  [((t, FFN_DIM), (ba, FFN_BLK), lambda g, c: (c, g), MM_DTYPE)], [(8, FFN_BLK)] * 2,
                  dx_dtypes=[MM_DTYPE, MM_DTYPE], comm=comm, bwd_fn=_ffn_act_bwd)
        a = act.ys[0]
        f = _matmul(f"ffn_down{layer}", a, comm.weight("ffn_w_down", layer), "nn", comm=comm)
        ln2 = rowop(f"ln_f{layer}", _ln_res_fn,
                    [_whole(_row_vec(w["ln_g"][layer, 1])), _whole(_row_vec(w["ln_b"][layer, 1]))],
                    [rowblk(x1, 1024), rowblk(f, 1024)], [1024], dx_dtypes=[F32, MM_DTYPE])
        rec.update(ycat=ycat, ln1=ln1, x1=x1, act=act, a=a, ln2=ln2)
        tape.append(rec)
        x = ln2.ys[0]

    dx, loss = _loss_kernel(x, target)

    d_lb = [jnp.zeros((1, 512), F32), jnp.zeros((1, 512), F32)]
    for layer in reversed(range(DEPTH)):
        j = layer // 2
        rec = tape[layer]
        (dg, db), (dx1_res, df) = rec["ln2"].bwd([dx])
        add_grad("ln_g", (layer, 1), dg[0]); add_grad("ln_b", (layer, 1), db[0])
        dw_matmul(f"ffn_down_dw{layer}", rec["a"], df, "ffn_w_down", layer, f"ffn_act{layer}_bwd")
        da = _matmul(f"ffn_down_dx{layer}", df, comm.weight("ffn_w_down", layer), "nt", comm=comm)
        dpa, (dhg, dhu) = rec["act"].bwd([da])
        halves = [jnp.concatenate([dpa[k][0, :FFN_DIM], dpa[4 + k][0, FFN_DIM:]]) for k in range(4)]
        add_grad("ffn_conv_w", layer, jnp.stack(halves[:3]))
        add_grad("ffn_conv_b", layer, halves[3])
        core_bwd = ("hg" if rec["kind"] == "ab" else "swa") + f"{j}_bwd"
        gate_cols, up_cols = (0, FFN_DIM), (FFN_DIM, FFN_DIM)
        dw_up = (_matmul(f"ffn_up_dw_g{layer}", rec["x1"], dhg, "tn", out_dtype=MM_DTYPE, comm=comm, b_cols=gate_cols),
                 _matmul(f"ffn_up_dw_u{layer}", rec["x1"], dhu, "tn", out_dtype=MM_DTYPE, comm=comm, b_cols=up_cols))
        comm.post(core_bwd, comm.grad_req("ffn_w_up", layer, dw_up))
        w_up = comm.weight("ffn_w_up", layer)
        dx1 = _matmul(f"ffn_up_dx_g{layer}", dhg, w_up, "nt", a_cols=gate_cols, b_cols=gate_cols, add=dx1_res,
                      comm=comm)
        dx1 = _matmul(f"ffn_up_dx_u{layer}", dhu, w_up, "nt", a_cols=up_cols, b_cols=up_cols, add=dx1, comm=comm)
        (dg, db), (dx_res, dm) = rec["ln1"].bwd([dx1])
        add_grad("ln_g", (layer, 0), dg[0]); add_grad("ln_b", (layer, 0), db[0])
        kind = rec["kind"]
        dw_matmul(f"mix_out_dw{layer}", rec["ycat"], dm, kind + "_w_out", j, f"{kind}_in_dw{j}")
        dycat = _matmul(f"mix_out_dx{layer}", dm, comm.weight(kind + "_w_out", j), "nt", comm=comm)
        if kind == "ab":
            (dnw_s, dnw_h), (dy_ssd, do_hg, dh) = rec["post"].bwd([dycat])
            add_grad("ssd_norm_w", j, dnw_s[0]); add_grad("hg_norm_w", j, dnw_h[0].reshape(4, LANES).sum(0))
            (dlb,), (dh,) = rec["hg"].bwd([do_hg], dx_into={0: dh})
            d_lb[j] = dlb
            (ddtb, dalog, ddsk), (dxbc_c, dh) = rec["ssd"].bwd([dy_ssd], dx_into={1: dh})
            add_grad("ssd_dt_bias", j, ddtb[0].reshape(8, 64).sum(-1))
            add_grad("ssd_a_log", j, dalog[0].reshape(8, 64).sum(-1))
            add_grad("ssd_d", j, ddsk[0].reshape(8, 64).sum(-1))
            dcp, (dh,) = rec["conv"].bwd([dxbc_c], dx_into={0: dh})
            add_grad("ssd_conv_w", j, jnp.stack([dcp[k][0] for k in range(4)]))
            add_grad("ssd_conv_b", j, dcp[4][0])
        else:
            _, (dyc, dhs, dh) = rec["post"].bwd([dycat])
            _, (da_s, du_s) = rec["scan"].bwd([dhs])
            (dwa, dba, dwx, dbx, dlam), (dxc,) = rec["gate"].bwd([da_s, du_s])
            add_grad("rg_wa", j, _block_diag_grad(dwa)); add_grad("rg_wx", j, _block_diag_grad(dwx))
            add_grad("rg_ba", j, dba[0]); add_grad("rg_bx", j, dbx[0]); add_grad("rg_lambda", j, dlam[0])
            dcp, (dh,) = rec["conv"].bwd([dxc], dx_into={0: dh})
            add_grad("rg_conv_w", j, jnp.stack([dcp[k][0] for k in range(4)]))
            add_grad("rg_conv_b", j, dcp[4][0])
            (dsink,), (dh,) = rec["swa"].bwd([dyc], dx_into={0: dh})
            add_grad("swa_sinks", j, dsink.sum(-1))
        dw_matmul(f"{kind}_in_dw{j}", rec["x_in"], dh, kind + "_w_in", j,
                  f"ffn_act{layer - 1}_bwd" if layer > 0 else f"{kind}_in_dx{j}")
        dx = _matmul(f"{kind}_in_dx{j}", dh, comm.weight(kind + "_w_in", j), "nt", add=dx_res, comm=comm)

    _, (dl0, dl1) = lb_op.bwd(d_lb)
    out = {"hg_lower": jnp.concatenate([dl0, dl1], axis=0)}
    for name, parts in grads.items():
        keys = sorted(parts)
        if isinstance(keys[0], tuple):
            out[name] = jnp.stack([jnp.stack([parts[(l, s)] for s in range(2)]) for l in range(DEPTH)])
        else:
            out[name] = jnp.stack([parts[k] for k in keys])
    return loss, dx, out


def _local_step_full(x, target, full):
    comm = _LocalComm(full)
    loss, dx, grads = _local_step(x, target, {n: a for n, a in full.items() if n not in BIG}, comm)
    for name in BIG:
        grads[name] = jnp.stack([comm.grads[name, l] for l in range(full[name].shape[0])])
    return loss, dx, grads


def kernel(x, ab_w_in, ssd_conv_w, ssd_conv_b, ssd_dt_bias, ssd_a_log, ssd_d, ssd_norm_w, hg_lower, hg_norm_w, ab_w_out, cd_w_in, swa_sinks, rg_conv_w, rg_conv_b, rg_wa, rg_ba, rg_wx, rg_bx, rg_lambda, cd_w_out, ffn_w_up, ffn_conv_w, ffn_conv_b, ffn_w_down, ln_g, ln_b, loss_target, m_ab_w_in, m_ssd_conv_w, m_ssd_conv_b, m_ssd_dt_bias, m_ssd_a_log, m_ssd_d, m_ssd_norm_w, m_hg_lower, m_hg_norm_w, m_ab_w_out, m_cd_w_in, m_swa_sinks, m_rg_conv_w, m_rg_conv_b, m_rg_wa, m_rg_ba, m_rg_wx, m_rg_bx, m_rg_lambda, m_cd_w_out, m_ffn_w_up, m_ffn_conv_w, m_ffn_conv_b, m_ffn_w_down, m_ln_g, m_ln_b, v_ab_w_in, v_ssd_conv_w, v_ssd_conv_b, v_ssd_dt_bias, v_ssd_a_log, v_ssd_d, v_ssd_norm_w, v_hg_lower, v_hg_norm_w, v_ab_w_out, v_cd_w_in, v_swa_sinks, v_rg_conv_w, v_rg_conv_b, v_rg_wa, v_rg_ba, v_rg_wx, v_rg_bx, v_rg_lambda, v_cd_w_out, v_ffn_w_up, v_ffn_conv_w, v_ffn_conv_b, v_ffn_w_down, v_ln_g, v_ln_b):
    args = dict(locals())
    wts = {n: args[n] for n in WEIGHTS}
    mom = {n: args["m_" + n] for n in WEIGHTS}
    var = {n: args["v_" + n] for n in WEIGHTS}
    axis = dict(SHARDED)
    small = [n for n, _ in SHARDED if n not in BIG]
    comm = _MeshComm(wts)

    def run(name, reqs):
        for (_, _, done), got in zip(reqs, _remote_copies(name, [(k, a) for k, a, _ in reqs])):
            done(got)

    full = {n: wts[n] for n in REPLICATED}

    def keep_small(n):
        def done(got):
            full[n] = _merge_shards(got.reshape((N_DEV,) + wts[n].shape), axis[n])
        return ("gather", _as2d(wts[n]), done)

    run("gather_first", [comm.gather_req("ab_w_in", 0), keep_small("ssd_conv_w")])
    for req in [comm.gather_req("ab_w_out", 0)] + [keep_small(n) for n in small if n != "ssd_conv_w"]:
        comm.post("ab_in0", req)

    loss, grad_x, grads = _local_step(x[0], loss_target[0], full, comm)
    loss = lax.psum(loss, ("x", "y", "c"))

    parts = {}

    def keep_parts(n, kind, arr):
        return (kind, arr, lambda got: parts.__setitem__(n, got))

    last = comm.take_all()
    last += [keep_parts(n, "exchange", _split_shards(grads[n], axis[n]).reshape((N_DEV,) + _as2d(wts[n]).shape))
             for n in small]
    last += [keep_parts(n, "gather", _as2d(grads[n])) for n in REPLICATED]
    run("exchange_last", last)

    new = {}
    for n in BIG:
        new[n] = _adamw_big("adamw_" + n, [comm.recv[n, l] for l in range(wts[n].shape[0])], wts[n], mom[n], var[n])
    names = small + REPLICATED
    res = _adamw_small("adamw_small", [(parts[n], _as2d(wts[n]), _as2d(mom[n]), _as2d(var[n])) for n in names])
    for n, r in zip(names, res):
        new[n] = [a.reshape(wts[n].shape) for a in r]

    outs = [loss, grad_x[None]]
    for kind in range(4):
        outs += [new[n][kind] for n in WEIGHTS]
    return tuple(outs)
```

```python
import math

import jax
import jax.numpy as jnp
from jax import lax
from jax.experimental import pallas as pl
from jax.experimental.pallas import tpu as pltpu

F32 = jnp.float32
BF16 = jnp.bfloat16
MM_DTYPE = BF16

DEPTH = 4
N_DEV = 8
LN_EPS = 1e-5
RMS_EPS = 1e-6
MASK_VALUE = -1e9
ALPHA = (2 * DEPTH) ** 0.25
RG_C = 8.0
FFN_DIM = 2816
SSD_CHUNK = 128
SSD_STEP = 256
SWA_STEP = 256
HG_STEP = 128
SWA_BLOCK = 128
LANES = 128
VMEM_LIMIT = 56 * 1024 * 1024

ADAM_LR, ADAM_B1, ADAM_B2, ADAM_EPS, ADAM_WD, ADAM_STEP = 0.001, 0.9, 0.999, 1e-08, 0.01, 10

AB_HEADS, AB_DT, AB_ZG, AB_XBC, AB_PAD = 0, 1536, 2048, 3072, 3840
CD_QKV, CD_GATE, CD_XR, CD_PAD = 0, 1024, 1536, 2048
FFN_BLK = 256
FFN_ROWS = 1024
FFN_STRIP = 32


def _cols(x, lo, hi):
    n = x.shape[1]

    @jax.custom_vjp
    def f(x):
        return x[:, lo:hi]

    def bwd(_, g):
        parts = []
        if lo > 0:
            parts.append(jnp.zeros((g.shape[0], lo), g.dtype))
        parts.append(g)
        if hi < n:
            parts.append(jnp.zeros((g.shape[0], n - hi), g.dtype))
        return (jnp.concatenate(parts, axis=1) if len(parts) > 1 else g,)

    f.defvjp(lambda x: (f(x), None), bwd)
    return f(x)


def _split_cols(x, width):
    n = x.shape[1] // width

    @jax.custom_vjp
    def f(x):
        return tuple(x[:, width * i:width * (i + 1)] for i in range(n))

    f.defvjp(lambda x: (f(x), None), lambda _, gs: (jnp.concatenate(gs, axis=1),))
    return f(x)


def _rows(x, lo, hi):
    n = x.shape[0]

    @jax.custom_vjp
    def f(x):
        return x[lo:hi, :]

    def bwd(_, g):
        parts = []
        if lo > 0:
            parts.append(jnp.zeros((lo, g.shape[1]), g.dtype))
        parts.append(g)
        if hi < n:
            parts.append(jnp.zeros((n - hi, g.shape[1]), g.dtype))
        return (jnp.concatenate(parts, axis=0) if len(parts) > 1 else g,)

    f.defvjp(lambda x: (f(x), None), bwd)
    return f(x)


def _roll(x, shift, axis):
    n = x.shape[axis]
    shift = shift % n
    if shift == 0:
        return x

    @jax.custom_vjp
    def f(x):
        return pltpu.roll(x, shift, axis)

    f.defvjp(lambda x: (f(x), None), lambda _, g: (pltpu.roll(g, n - shift, axis),))
    return f(x)


def _dot(a, b):
    return lax.dot_general(a, b, (((1,), (0,)), ((), ())), preferred_element_type=F32)


def _dot_nt(a, b):
    return lax.dot_general(a, b, (((1,), (1,)), ((), ())), preferred_element_type=F32)


def _dot_tn(a, b):
    return lax.dot_general(a, b, (((0,), (0,)), ((), ())), preferred_element_type=F32)


def _split3(x):
    hi = x.astype(BF16)
    r = x - hi.astype(F32)
    mid = r.astype(BF16)
    return hi, mid, (r - mid.astype(F32)).astype(BF16)


def _sel_dot(sel, x):
    def run(mat, v, dims):
        n = v.shape[1]
        y = lax.dot_general(mat, jnp.concatenate(_split3(v), axis=1), dims, preferred_element_type=F32)
        return y[:, :n] + y[:, n:2 * n] + y[:, 2 * n:]

    @jax.custom_vjp
    def f(sel, x):
        return run(sel, x, (((1,), (0,)), ((), ())))

    def bwd(sel, g):
        return jnp.zeros_like(sel), run(sel, g, (((0,), (0,)), ((), ())))

    f.defvjp(lambda sel, x: (f(sel, x), sel), bwd)
    return f(sel, x)


def _dot_sel(x, sel):
    def run(v, mat, dims):
        m = v.shape[0]
        y = lax.dot_general(jnp.concatenate(_split3(v), axis=0), mat, dims, preferred_element_type=F32)
        return y[:m] + y[m:2 * m] + y[2 * m:]

    @jax.custom_vjp
    def f(x, sel):
        return run(x, sel, (((1,), (0,)), ((), ())))

    def bwd(sel, g):
        return run(g, sel, (((1,), (1,)), ((), ()))), jnp.zeros_like(sel)

    f.defvjp(lambda x, sel: (f(x, sel), sel), bwd)
    return f(x, sel)


def _sigmoid(x):
    return 0.5 * jnp.tanh(0.5 * x) + 0.5


def _silu(x):
    h = 0.5 * x
    return h + h * jnp.tanh(h)


def _softplus(x):
    return jnp.maximum(x, 0.0) + jnp.log(1.0 + jnp.exp(-jnp.abs(x)))


def _gelu_tanh(x):
    c = math.sqrt(2.0 / math.pi)
    return 0.5 * x * (1.0 + jnp.tanh(c * (x + 0.044715 * (x * x * x))))


def _iota(shape, axis):
    return lax.broadcasted_iota(jnp.int32, shape, axis)


def _lane_mask(lo, hi, width=LANES):
    lane = _iota((1, width), 1)
    return ((lane >= lo) & (lane < hi)).astype(F32)


def _mesh_pos():
    return lax.axis_index("x"), lax.axis_index("y"), lax.axis_index("c")


def _carry_shapes(carry):
    return [jax.ShapeDtypeStruct((N_DEV,) + a.shape if kind == "gather" else a.shape, a.dtype) for kind, a in carry]


def _carry_scratch(carry):
    n = len(carry)
    if n == 0:
        return []
    return [pltpu.SemaphoreType.DMA((n, N_DEV - 1)), pltpu.SemaphoreType.DMA((n, N_DEV - 1)),
            pltpu.SemaphoreType.DMA((n,))]


def _carry_run(start, kinds, in_refs, out_refs, send_sems, recv_sems, local_sems):
    x, y, cc = _mesh_pos()
    me = 4 * x + 2 * y + cc
    for i, kind in enumerate(kinds):
        mine = in_refs[i] if kind == "gather" else in_refs[i].at[me]
        local = pltpu.make_async_copy(mine, out_refs[i].at[me], local_sems.at[i])
        remote = []
        for k in range(1, N_DEV):
            px, py, pc = x ^ ((k >> 2) & 1), y ^ ((k >> 1) & 1), cc ^ (k & 1)
            src = in_refs[i] if kind == "gather" else in_refs[i].at[4 * px + 2 * py + pc]
            remote.append(pltpu.make_async_remote_copy(
                src_ref=src, dst_ref=out_refs[i].at[me],
                send_sem=send_sems.at[i, k - 1], recv_sem=recv_sems.at[i, k - 1],
                device_id=(px, py, pc), device_id_type=pl.DeviceIdType.MESH))
        if start:
            local.start()
            for cp in remote:
                cp.start()
        else:
            for cp in remote:
                cp.wait_recv()
            for cp in remote:
                cp.wait_send()
            local.wait()


def _remote_copies(name, carry):
    n = len(carry)
    kinds = [k for k, _ in carry]

    def body(*refs):
        sems = refs[2 * n:]
        _carry_run(True, kinds, refs[:n], refs[n:2 * n], *sems)
        _carry_run(False, kinds, refs[:n], refs[n:2 * n], *sems)

    return pl.pallas_call(
        body, name=name, out_shape=_carry_shapes(carry),
        in_specs=[pl.BlockSpec(memory_space=pl.ANY)] * n, out_specs=[pl.BlockSpec(memory_space=pl.ANY)] * n,
        scratch_shapes=_carry_scratch(carry),
    )(*[a for _, a in carry])


def _cparams(sem):
    return pltpu.CompilerParams(dimension_semantics=sem, vmem_limit_bytes=VMEM_LIMIT)


def _chunk_fwd(name, fn, grid, params, xs, outs, state_shapes, carry=()):
    n_g, n_c = grid
    n_p, n_x, n_o, n_s, n_r = len(params), len(xs), len(outs), len(state_shapes), len(carry)
    kinds = [k for k, _ in carry]

    def body(*refs):
        i = 0
        p_refs = refs[i:i + n_p]; i += n_p
        x_refs = refs[i:i + n_x]; i += n_x
        ci_refs = refs[i:i + n_r]; i += n_r
        o_refs = refs[i:i + n_o]; i += n_o
        sv_refs = refs[i:i + n_s]; i += n_s
        co_refs = refs[i:i + n_r]; i += n_r
        st_refs = refs[i:i + n_s]; i += n_s
        sems = refs[i:]
        g, c = pl.program_id(0), pl.program_id(1)

        if n_r:
            @pl.when((g == 0) & (c == 0))
            def _():
                _carry_run(True, kinds, ci_refs, co_refs, *sems)

        @pl.when(c == 0)
        def _():
            for s in st_refs:
                s[...] = jnp.zeros(s.shape, s.dtype)

        st = [s[...] for s in st_refs]
        ys, new_st = fn(c, [p[...] for p in p_refs], [x[...].astype(F32) for x in x_refs], st)
        for o, y in zip(o_refs, ys):
            o[...] = y.astype(o.dtype)
        for sv, s in zip(sv_refs, st):
            sv[0, 0] = s
        for s_ref, s in zip(st_refs, new_st):
            s_ref[...] = s

        if n_r:
            @pl.when((g == n_g - 1) & (c == n_c - 1))
            def _():
                _carry_run(False, kinds, ci_refs, co_refs, *sems)

    any_spec = pl.BlockSpec(memory_space=pl.ANY)
    in_specs = [pl.BlockSpec(b, m) for _, b, m in params] + [pl.BlockSpec(b, m) for _, b, m in xs] + [any_spec] * n_r
    out_specs = [pl.BlockSpec(b, m) for _, b, m, _ in outs]
    out_shape = [jax.ShapeDtypeStruct(s, d) for s, _, _, d in outs]
    for shp in state_shapes:
        out_specs.append(pl.BlockSpec((1, 1) + shp, lambda g, c, n=len(shp): (g, c) + (0,) * n))
        out_shape.append(jax.ShapeDtypeStruct((n_g, n_c) + shp, F32))
    out_specs += [any_spec] * n_r
    out_shape += _carry_shapes(carry)
    res = pl.pallas_call(
        body, name=name, grid=grid, in_specs=in_specs, out_specs=out_specs, out_shape=out_shape,
        scratch_shapes=[pltpu.VMEM(shp, F32) for shp in state_shapes] + _carry_scratch(carry),
        compiler_params=_cparams(("arbitrary", "arbitrary")),
    )(*[a for a, _, _ in params], *[a for a, _, _ in xs], *[a for _, a in carry])
    return list(res[:n_o]), list(res[n_o:n_o + n_s]), list(res[n_o + n_s:])


def _chunk_bwd(name, fn, grid, params, xs, saved, dys, state_shapes, dx_dtypes, dx_into, carry=(), bwd_fn=None):
    n_g, n_c = grid
    n_p, n_x, n_s, n_y, n_r = len(params), len(xs), len(state_shapes), len(dys), len(carry)
    kinds = [k for k, _ in carry]
    into = sorted(dx_into)
    n_a = len(into)

    def rev(m):
        return lambda g, c: m(g, n_c - 1 - c)

    def body(*refs):
        i = 0
        p_refs = refs[i:i + n_p]; i += n_p
        x_refs = refs[i:i + n_x]; i += n_x
        sv_refs = refs[i:i + n_s]; i += n_s
        dy_refs = refs[i:i + n_y]; i += n_y
        i += n_a
        ci_refs = refs[i:i + n_r]; i += n_r
        dp_refs = refs[i:i + n_p]; i += n_p
        dx_refs = refs[i:i + n_x]; i += n_x
        co_refs = refs[i:i + n_r]; i += n_r
        ds_refs = refs[i:i + n_s]; i += n_s
        sems = refs[i:]
        g, c = pl.program_id(0), pl.program_id(1)
        chunk = n_c - 1 - c

        if n_r:
            @pl.when((g == 0) & (c == 0))
            def _():
                _carry_run(True, kinds, ci_refs, co_refs, *sems)

        @pl.when(c == 0)
        def _():
            for s in ds_refs:
                s[...] = jnp.zeros(s.shape, s.dtype)
            for d in dp_refs:
                d[...] = jnp.zeros(d.shape, d.dtype)

        pv = [p[...] for p in p_refs]
        xv = [x[...].astype(F32) for x in x_refs]
        sv = [s[0, 0] for s in sv_refs]
        dyv, dsv = [d[...].astype(F32) for d in dy_refs], [s[...] for s in ds_refs]
        if bwd_fn is None:
            _, vjp = jax.vjp(lambda p, x, s: fn(chunk, p, x, s), pv, xv, sv)
            dp, dx, ds = vjp((dyv, dsv))
        else:
            dp, dx, ds = bwd_fn(chunk, pv, xv, sv, dyv, dsv)
        for r, v in zip(dp_refs, dp):
            r[...] += v
        for r, v in zip(dx_refs, dx):
            r[...] = v.astype(r.dtype)
        for r, v in zip(ds_refs, ds):
            r[...] = v

        if n_r:
            @pl.when((g == n_g - 1) & (c == n_c - 1))
            def _():
                _carry_run(False, kinds, ci_refs, co_refs, *sems)

    any_spec = pl.BlockSpec(memory_space=pl.ANY)
    in_specs = [pl.BlockSpec(b, rev(m)) for _, b, m in params] + [pl.BlockSpec(b, rev(m)) for _, b, m in xs]
    for shp in state_shapes:
        in_specs.append(pl.BlockSpec((1, 1) + shp, lambda g, c, n=len(shp): (g, n_c - 1 - c) + (0,) * n))
    in_specs += [pl.BlockSpec(b, rev(m)) for _, b, m in dys]
    in_specs += [any_spec] * (n_a + n_r)
    out_specs = [pl.BlockSpec(b, rev(m)) for _, b, m in params] + [pl.BlockSpec(b, rev(m)) for _, b, m in xs]
    out_specs += [any_spec] * n_r
    out_shape = [jax.ShapeDtypeStruct(a.shape, F32) for a, _, _ in params]
    out_shape += [jax.ShapeDtypeStruct(a.shape, d) for (a, _, _), d in zip(xs, dx_dtypes)]
    out_shape += _carry_shapes(carry)
    first_alias = n_p + n_x + n_s + n_y
    aliases = {first_alias + k: n_p + xi for k, xi in enumerate(into)}
    res = pl.pallas_call(
        body, name=name, grid=grid, in_specs=in_specs, out_specs=out_specs, out_shape=out_shape,
        scratch_shapes=[pltpu.VMEM(shp, F32) for shp in state_shapes] + _carry_scratch(carry),
        input_output_aliases=aliases,
        compiler_params=_cparams(("arbitrary", "arbitrary")),
    )(*[a for a, _, _ in params], *[a for a, _, _ in xs], *saved, *[a for a, _, _ in dys],
      *[dx_into[xi] for xi in into], *[a for _, a in carry])
    return list(res[:n_p]), list(res[n_p:n_p + n_x]), list(res[n_p + n_x:])


class _Op:
    def __init__(self, name, fn, grid, params, xs, outs, state_shapes=(), dx_dtypes=None, comm=None, bwd_fn=None):
        self.name, self.fn, self.grid, self.comm, self.bwd_fn = name, fn, grid, comm, bwd_fn
        self.params, self.xs, self.outs, self.state_shapes = params, xs, outs, list(state_shapes)
        self.dx_dtypes = dx_dtypes or [F32] * len(xs)
        reqs = comm.take(name + "_fwd") if comm is not None else []
        self.ys, self.saved, got = _chunk_fwd(name + "_fwd", fn, grid, params, xs, outs, self.state_shapes,
                                              carry=[(k, a) for k, a, _ in reqs])
        for (_, _, done), g in zip(reqs, got):
            done(g)

    def bwd(self, dys, dx_into=None):
        dy_defs = [(d, b, m) for d, (_, b, m, _) in zip(dys, self.outs)]
        reqs = self.comm.take(self.name + "_bwd") if self.comm is not None else []
        dps, dxs, got = _chunk_bwd(self.name + "_bwd", self.fn, self.grid, self.params, self.xs, self.saved, dy_defs,
                                   self.state_shapes, self.dx_dtypes, dx_into or {},
                                   carry=[(k, a) for k, a, _ in reqs], bwd_fn=self.bwd_fn)
        for (_, _, done), g in zip(reqs, got):
            done(g)
        return dps, dxs


def _whole(a):
    nd = a.ndim
    return (a, a.shape, lambda g, c: (0,) * nd)


def _pick(n, prefs):
    for p in prefs:
        if n % p == 0:
            return p
    return n


def _mm_blocks(mode, m, n, k):
    bn = _pick(n, (1408, 1280, 1024, 768, 512, 256, 128))
    if mode == "tn":
        return _pick(m, (1408, 1024, 768, 512, 256, 128)), bn, _pick(k, (2048, 1024, 512, 256, 128))
    bk = k if k <= 3840 else _pick(k, (2816, 1920, 1408, 1024, 512, 256, 128))
    return _pick(m, (1024, 512, 256, 128)), bn, bk


def _matmul(name, a, b, mode, *, add=None, out_dtype=F32, comm=None, a_cols=None, b_cols=None):
    a0, asize = a_cols if a_cols is not None else (0, a.shape[1])
    c0, csize = b_cols if b_cols is not None else (0, b.shape[1])
    if mode == "nn":
        (m, k), n = (a.shape[0], asize), csize
    elif mode == "nt":
        (m, k), n = (a.shape[0], asize), b.shape[0]
        assert k == csize
    else:
        (k, m), n = (a.shape[0], asize), csize
    bm, bn, bk = _mm_blocks(mode, m, n, k)
    assert c0 % (bk if mode == "nt" else bn) == 0 and a0 % (bm if mode == "tn" else bk) == 0
    j0, k0 = (0, c0 // bk) if mode == "nt" else (c0 // bn, 0)
    ia = a0 // (bm if mode == "tn" else bk)
    n_i, n_j, n_k = m // bm, n // bn, k // bk
    dims = {"nn": (((1,), (0,)), ((), ())), "nt": (((1,), (1,)), ((), ())), "tn": (((0,), (0,)), ((), ()))}[mode]
    has_add = add is not None
    reqs = comm.take(name) if comm is not None else []
    carry = [(kind, arr) for kind, arr, _ in reqs]
    kinds = [kind for kind, _ in carry]
    n_r = len(carry)

    def body(*refs):
        i = 2
        a_ref, b_ref = refs[0], refs[1]
        c_ref = refs[i] if has_add else None
        i += has_add
        ci_refs = refs[i:i + n_r]; i += n_r
        o_ref = refs[i]; i += 1
        co_refs = refs[i:i + n_r]; i += n_r
        acc = refs[i]; i += 1
        sems = refs[i:]
        ii, jj, kk = pl.program_id(0), pl.program_id(1), pl.program_id(2)

        if n_r:
            @pl.when((ii == 0) & (jj == 0) & (kk == 0))
            def _():
                _carry_run(True, kinds, ci_refs, co_refs, *sems)

        part = lax.dot_general(a_ref[...].astype(MM_DTYPE), b_ref[...].astype(MM_DTYPE), dims,
                               preferred_element_type=F32)

        def finish(r):
            if has_add:
                r = r + c_ref[...]
            o_ref[...] = r.astype(o_ref.dtype)

        if n_k == 1:
            finish(part)
        else:
            @pl.when(kk == 0)
            def _():
                acc[...] = part

            @pl.when((kk > 0) & (kk < n_k - 1))
            def _():
                acc[...] += part

            @pl.when(kk == n_k - 1)
            def _():
                finish(acc[...] + part)

        if n_r:
            @pl.when((ii == n_i - 1) & (jj == n_j - 1) & (kk == n_k - 1))
            def _():
                _carry_run(False, kinds, ci_refs, co_refs, *sems)

    if mode == "nn":
        a_spec = pl.BlockSpec((bm, bk), lambda i, j, kk: (i, ia + kk))
        b_spec = pl.BlockSpec((bk, bn), lambda i, j, kk: (kk, j0 + j))
    elif mode == "nt":
        a_spec = pl.BlockSpec((bm, bk), lambda i, j, kk: (i, ia + kk))
        b_spec = pl.BlockSpec((bn, bk), lambda i, j, kk: (j, k0 + kk))
    else:
        a_spec = pl.BlockSpec((bk, bm), lambda i, j, kk: (kk, ia + i))
        b_spec = pl.BlockSpec((bk, bn), lambda i, j, kk: (kk, j0 + j))
    any_spec = pl.BlockSpec(memory_space=pl.ANY)
    in_specs, args = [a_spec, b_spec], [a, b]
    if has_add:
        in_specs.append(pl.BlockSpec((bm, bn), lambda i, j, kk: (i, j)))
        args.append(add)
    res = pl.pallas_call(
        body, name=name, grid=(n_i, n_j, n_k), in_specs=in_specs + [any_spec] * n_r,
        out_specs=[pl.BlockSpec((bm, bn), lambda i, j, kk: (i, j))] + [any_spec] * n_r,
        out_shape=[jax.ShapeDtypeStruct((m, n), out_dtype)] + _carry_shapes(carry),
        scratch_shapes=[pltpu.VMEM((bm, bn) if n_k > 1 else (8, LANES), F32)] + _carry_scratch(carry),
        compiler_params=_cparams(("arbitrary", "arbitrary", "arbitrary")),
    )(*args, *[arr for _, arr in carry])
    for (_, _, done), g in zip(reqs, res[1:]):
        done(g)
    return res[0]


def _ln_res_fn(_, p, x, st):
    g, b = p
    xin, m = x
    pre = ALPHA * xin + m
    mu = jnp.mean(pre, -1, keepdims=True)
    d = pre - mu
    var = jnp.mean(d * d, -1, keepdims=True)
    return [d * lax.rsqrt(var + LN_EPS) * g + b], []


def _make_conv_fn(taps, act):
    def fn(_, p, x, st):
        ws, b = p[:taps], p[taps]
        (xin,), (prev,) = x, st
        n = xin.shape[0]
        ext = jnp.concatenate([prev, xin], axis=0)
        y = b
        for k in range(taps):
            y = y + ws[k] * _rows(_roll(ext, taps - 1 - k, 0), 8, 8 + n)
        if act:
            y = _silu(y)
        return [y], [_rows(xin, n - 8, n)]

    return fn


def _ffn_act_fn(_, p, x, st):
    n = x[0].shape[0]
    ys = []
    for half in range(2):
        ws, b = p[4 * half:4 * half + 3], p[4 * half + 3]
        ext = jnp.concatenate([st[half], x[half]], axis=0)
        y = b
        for k in range(3):
            y = y + ws[k] * _rows(_roll(ext, 2 - k, 0), 8, 8 + n)
        ys.append(y)
    return [_silu(ys[0]) * ys[1]], [_rows(x[0], n - 8, n), _rows(x[1], n - 8, n)]


def _ffn_act_bwd(_, p, x, st, dy, dst):
    (da,) = dy
    n, wd = x[0].shape
    rs = FFN_STRIP
    last = n // rs - 1
    zero8 = jnp.zeros((8, wd), F32)
    acc = [[zero8] * 4, [zero8] * 4]
    after = [zero8, zero8]
    strips = [[None] * (n // rs), [None] * (n // rs)]
    for i in reversed(range(n // rs)):
        r0 = rs * i
        taps, ys = [], []
        for half in range(2):
            w0, w1, w2, b = p[4 * half:4 * half + 4]
            xs = jnp.concatenate([st[half] if i == 0 else x[half][r0 - 8:r0], x[half][r0:r0 + rs]], axis=0)
            taps.append((pltpu.roll(xs, 2, 0)[8:], pltpu.roll(xs, 1, 0)[8:], xs[8:]))
            ys.append(b + w2 * taps[half][2] + w1 * taps[half][1] + w0 * taps[half][0])
        g, u = ys
        s = _sigmoid(g)
        d = da[r0:r0 + rs]
        dys = (d * u * (s * (1.0 + g * (1.0 - s))), d * (g * s))
        for half in range(2):
            w0, w1, w2, _ = p[4 * half:4 * half + 4]
            dyh = dys[half]
            for k, v in enumerate((dyh * taps[half][0], dyh * taps[half][1], dyh * taps[half][2], dyh)):
                for r in range(0, rs, 8):
                    acc[half][k] = acc[half][k] + v[r:r + 8]
            dyp = jnp.concatenate([dyh, after[half]], axis=0)
            dxs = w2 * dyh + w1 * pltpu.roll(dyp, rs + 8 - 1, 0)[:rs] + w0 * pltpu.roll(dyp, rs + 8 - 2, 0)[:rs]
            if i == last:
                dxs = jnp.concatenate([dxs[:rs - 8], dxs[rs - 8:] + dst[half]], axis=0)
            strips[half][i] = dxs
            after[half] = dyh[:8]
    dprev = []
    for half in range(2):
        w0, w1 = p[4 * half], p[4 * half + 1]
        head = jnp.concatenate([zero8, after[half]], axis=0)
        dprev.append((w1 * pltpu.roll(head, 16 - 1, 0) + w0 * pltpu.roll(head, 16 - 2, 0))[:8])
    dps = [jnp.sum(a, axis=0, keepdims=True) for half in range(2) for a in acc[half]]
    return dps, [jnp.concatenate(s_, axis=0) for s_ in strips], dprev


def _ssd_fn(_, p, x, st):
    ys = []
    for r in range(0, SSD_STEP, SSD_CHUNK):
        (y,), st = _ssd_chunk(p, [_rows(v, r, r + SSD_CHUNK) for v in x], st)
        ys.append(y)
    return [jnp.concatenate(ys, axis=0)], st


def _ssd_chunk(p, x, st):
    dtb, alog, dsk = p
    xbc, dtr = x
    L = SSD_CHUNK
    tril = _iota((L, L), 0) >= _iota((L, L), 1)
    xs, bm, cm = _cols(xbc, 0, 512), _cols(xbc, 512, 640), _cols(xbc, 640, 768)
    dt = _softplus(dtr + dtb)
    da = dt * (-jnp.exp(alog))
    cs = _sel_dot(tril.astype(BF16), da)
    pick = ((_iota((LANES, 2 * LANES), 0) == 0) & (_iota((LANES, 2 * LANES), 1) < LANES)) | (
        (_iota((LANES, 2 * LANES), 0) == 64) & (_iota((LANES, 2 * LANES), 1) >= LANES))
    pick = pick.astype(BF16)
    tot = jnp.sum(da, axis=0, keepdims=True)
    xc = xs * dt
    xdec = xc * jnp.exp(tot - cs)
    cs_b, xc_b, xdec_b, ecs_b, etot_b, dsk_b, xs_b = (
        _split_cols(v, LANES) for v in (cs, xc, xdec, jnp.exp(cs), jnp.exp(tot), dsk, xs))
    ys, new_st = [], []
    for pr in range(4):
        grp = pr // 2
        c_g = cm * _lane_mask(64 * grp, 64 * grp + 64)
        gmat = _dot_nt(c_g, bm)
        cols2 = _split_cols(_dot_sel(cs_b[pr], pick), LANES)
        yd = jnp.zeros((L, LANES), F32)
        for half in range(2):
            col = cols2[half]
            diff = col - col.T
            dec = jnp.where(tril, jnp.exp(jnp.where(tril, diff, 0.0)), 0.0)
            yd = yd + _dot(gmat * dec, xc_b[pr]) * _lane_mask(64 * half, 64 * half + 64)
        s_in = st[pr]
        y_off = _dot(c_g, s_in) * ecs_b[pr]
        ys.append(yd + y_off + dsk_b[pr] * xs_b[pr])
        new_st.append(s_in * etot_b[pr] + _dot_tn(bm, xdec_b[pr]))
    return [jnp.concatenate(ys, axis=1)], new_st


def _hg_fn(_, p, x, st):
    (lb,) = p
    (xin,) = x
    L = HG_STEP
    n_lvl = L.bit_length() - 1
    hq, hf, hi = _split_cols(xin, 512)
    q = _silu(hq)
    logf = jnp.log(lb + (1.0 - lb) * _sigmoid(hf))
    k = (1.0 - lb) * _sigmoid(-hf)
    ti, si = _iota((L, L), 0), _iota((L, L), 1)
    bc = _sel_dot((ti >= si).astype(BF16), logf)
    tot = jnp.sum(logf, axis=0, keepdims=True)
    tn, sn = _iota((n_lvl * L, 1), 0), _iota((n_lvl * L, L), 1)
    row = tn & (L - 1)
    blk = L >> (tn >> n_lvl)
    piv = row - (row & (blk - 1)) + (blk >> 1)
    bcp_all = _sel_dot((sn == piv).astype(BF16), bc)
    t1 = _iota((L, 1), 0)
    qqs, kks, sames = [], [], []
    for lvl in range(n_lvl):
        size = L >> lvl
        upper = (t1 & (size - 1)) >= size // 2
        bcp = _rows(bcp_all, L * lvl, L * (lvl + 1))
        qqs.append(jnp.where(upper, q * jnp.exp(jnp.where(upper, bc - bcp, 0.0)), 0.0))
        kks.append(jnp.where(upper, 0.0, k * jnp.exp(jnp.where(upper, 0.0, bcp - bc))))
        sames.append((ti >> (n_lvl - lvl)) == (si >> (n_lvl - lvl)))
    qq_b = [_split_cols(v, LANES) for v in qqs]
    kk_b = [_split_cols(v, LANES) for v in kks]
    v_b, diag_b, q_in_b, k_out_b, etot_b = (
        _split_cols(v, LANES) for v in (hi, q * k, q * jnp.exp(bc), k * jnp.exp(tot - bc), jnp.exp(tot)))
    outs, new_st = [], []
    for h in range(4):
        attn = jnp.zeros((L, L), F32)
        for lvl in range(n_lvl):
            attn = attn + jnp.where(sames[lvl], _dot_nt(qq_b[lvl][h], kk_b[lvl][h]), 0.0)
        v = v_b[h]
        out = _dot(attn, v) + jnp.sum(diag_b[h], axis=-1, keepdims=True) * v
        outs.append(out + _dot_nt(q_in_b[h], st[h]))
        new_st.append(st[h] * etot_b[h] + _dot_tn(v, k_out_b[h]))
    return [jnp.concatenate(outs, axis=1)], new_st


def _swa_fn(chunk, p, x, st):
    n_sub = SWA_STEP // SWA_BLOCK
    ys = []
    for sub in range(n_sub):
        (y,), st = _swa_block(n_sub * chunk + sub, p, [_rows(x[0], SWA_BLOCK * sub, SWA_BLOCK * (sub + 1))], st)
        ys.append(y)
    return [jnp.concatenate(ys, axis=0)], st


def _swa_block(block, p, x, st):
    (sinks,) = p
    (xin,) = x
    q, k, v = _cols(xin, 0, 512), _cols(xin, 512, 640), _cols(xin, 640, 768)
    kp, vp = st
    T = SWA_BLOCK
    kc = jnp.concatenate([kp, k], axis=0)
    vc = jnp.concatenate([vp, v], axis=0)
    qi, kj = _iota((T, 2 * T), 0), _iota((T, 2 * T), 1)
    rel = qi + T - kj
    mask = (rel >= 0) & (rel < T) & ((kj >= T) | (block > 0))
    srow = _iota((8, LANES), 0)
    q_b = _split_cols(q, LANES)
    outs = []
    for pr in range(4):
        grp = pr // 2
        gm = _lane_mask(64 * grp, 64 * grp + 64)
        km, vm = kc * gm, vc * gm
        q2 = q_b[pr]
        o2 = jnp.zeros((T, LANES), F32)
        for half in range(2):
            hm = _lane_mask(64 * half, 64 * half + 64)
            qh = q2 * hm
            if half != grp:
                qh = _roll(qh, 64, 1)
            s = _dot_nt(qh, km) * 0.125
            s = jnp.where(mask, s, MASK_VALUE)
            sink = jnp.mean(jnp.sum(jnp.where(srow == 2 * pr + half, sinks, 0.0), axis=0, keepdims=True),
                            axis=-1, keepdims=True)
            mx = lax.stop_gradient(jnp.maximum(jnp.max(s, axis=-1, keepdims=True), sink))
            e = jnp.exp(s - mx)
            den = jnp.sum(e, axis=-1, keepdims=True) + jnp.exp(sink - mx)
            o = _dot(e / den, vm)
            if half != grp:
                o = _roll(o, 64, 1)
            o2 = o2 + o * hm
        outs.append(o2)
    return [jnp.concatenate(outs, axis=1)], [k, v]


def _rg_gate_fn(_, p, x, st):
    wa, ba, wx, bx, lam = p
    (xc,) = x
    r = _sigmoid(_dot(xc, wa) + ba)
    i = _sigmoid(_dot(xc, wx) + bx)
    log_a = -RG_C * r * _softplus(-lam)
    a = jnp.exp(log_a)
    t = jnp.tanh(log_a)
    one_minus_a2 = -2.0 * t / (1.0 - t)
    u = jnp.sqrt(jnp.maximum(one_minus_a2, 0.0)) * (i * xc)
    return [a, u], []


def _rg_scan_fn(_, p, x, st):
    a, u = x
    (prev,) = st
    n = a.shape[0]
    row = _iota((n, 1), 0)
    s = 1
    while s < n:
        keep = row >= s
        a_s, u_s = _roll(a, s, 0), _roll(u, s, 0)
        u = jnp.where(keep, a * u_s + u, u)
        a = jnp.where(keep, a * a_s, a)
        s *= 2
    h_in = jnp.sum(jnp.where(_iota((8, 1), 0) == 7, prev, 0.0), axis=0, keepdims=True)
    h = u + a * h_in
    return [h], [_rows(h, n - 8, n)]


def _ab_post_fn(_, p, x, st):
    nw_ssd, nw_hg = p
    y, o, zg = x
    z, hgate = _split_cols(zg, 512)

    def rms(v, width):
        blocks = _split_cols(v, width)
        return jnp.concatenate([b * lax.rsqrt(jnp.mean(b * b, axis=-1, keepdims=True) + RMS_EPS) for b in blocks],
                               axis=1)

    ya = rms(y * _silu(z), 256) * nw_ssd
    yb = rms(o, 128) * nw_hg * _silu(hgate)
    return [jnp.concatenate([ya, yb], axis=1)], []


def _cd_post_fn(_, p, x, st):
    yc, h, gate = x
    return [jnp.concatenate([yc, h * _gelu_tanh(gate)], axis=1)], []


def _lb_fn(_, p, x, st):
    l0, l1 = x
    mx = lax.stop_gradient(jnp.maximum(l0, l1))
    e0, e1 = jnp.exp(l0 - mx), jnp.exp(l1 - mx)
    s0, s1 = e0 / (e0 + e1), e1 / (e0 + e1)
    return [jnp.clip(s0 - s0, 0.0, 1.0), jnp.clip((s0 + s1) - s0, 0.0, 1.0)], []


def _loss_kernel(y, target):
    t, d = y.shape
    bt = _pick(t, (512, 256, 128))

    def body(y_ref, t_ref, dy_ref, l_ref):
        @pl.when(pl.program_id(0) == 0)
        def _():
            l_ref[...] = jnp.zeros(l_ref.shape, F32)

        e = y_ref[...] - t_ref[...]
        dy_ref[...] = e * (1.0 / d)
        l_ref[...] += jnp.sum(e * e, axis=0, keepdims=True) * (0.5 / d)

    dy, part = pl.pallas_call(
        body, name="loss", grid=(t // bt,),
        in_specs=[pl.BlockSpec((bt, d), lambda i: (i, 0)), pl.BlockSpec((bt, d), lambda i: (i, 0))],
        out_specs=[pl.BlockSpec((bt, d), lambda i: (i, 0)), pl.BlockSpec((1, d), lambda i: (0, 0))],
        out_shape=[jax.ShapeDtypeStruct((t, d), F32), jax.ShapeDtypeStruct((1, d), F32)],
        compiler_params=_cparams(("arbitrary",)),
    )(y, target)
    return dy, jnp.sum(part)


def _adamw_math(parts, w_, m_, v_):
    c1 = 1.0 / (1.0 - ADAM_B1 ** ADAM_STEP)
    c2 = 1.0 / (1.0 - ADAM_B2 ** ADAM_STEP)
    g = parts[0].astype(F32)
    for s in range(1, N_DEV):
        g = g + parts[s].astype(F32)
    nm = ADAM_B1 * m_ + (1.0 - ADAM_B1) * g
    nv = ADAM_B2 * v_ + (1.0 - ADAM_B2) * (g * g)
    return g, -ADAM_LR * ((nm * c1) / (jnp.sqrt(nv * c2) + ADAM_EPS) + ADAM_WD * w_), nm, nv


def _adamw_big(name, parts, w, m, v):
    n_l, r, c = w.shape
    br = _pick(r, (256, 176, 128, 64, 32, 16, 8))

    def body(*refs):
        p_refs, (w_ref, m_ref, v_ref), outs = refs[:n_l], refs[n_l:n_l + 3], refs[n_l + 3:]
        for l in range(n_l):
            @pl.when(pl.program_id(0) == l)
            def _(p_ref=p_refs[l]):
                res = _adamw_math([p_ref[s] for s in range(N_DEV)], w_ref[...], m_ref[...], v_ref[...])
                for ref, val in zip(outs, res):
                    ref[...] = val

    blk = pl.BlockSpec((None, br, c), lambda l, i: (l, i, 0))
    p_specs = [pl.BlockSpec((N_DEV, br, c), lambda l, i, k=k: (0, jnp.where(l == k, i, 0), 0)) for k in range(n_l)]
    return pl.pallas_call(
        body, name=name, grid=(n_l, r // br), in_specs=p_specs + [blk, blk, blk],
        out_specs=[blk] * 4, out_shape=[jax.ShapeDtypeStruct(w.shape, F32)] * 4,
        compiler_params=_cparams(("arbitrary", "arbitrary")),
    )(*parts, w, m, v)


def _adamw_small(name, items):
    n = len(items)

    def body(*refs):
        ins, outs = refs[:4 * n], refs[4 * n:]
        for i in range(n):
            p_ref, w_ref, m_ref, v_ref = ins[4 * i:4 * i + 4]
            res = _adamw_math([p_ref[s] for s in range(N_DEV)], w_ref[...], m_ref[...], v_ref[...])
            for ref, val in zip(outs[4 * i:4 * i + 4], res):
                ref[...] = val

    flat = [a for it in items for a in it]
    out_shape = [jax.ShapeDtypeStruct(it[1].shape, F32) for it in items for _ in range(4)]
    res = pl.pallas_call(
        body, name=name, out_shape=out_shape,
        in_specs=[pl.BlockSpec(memory_space=pltpu.VMEM)] * len(flat),
        out_specs=[pl.BlockSpec(memory_space=pltpu.VMEM)] * len(out_shape),
        compiler_params=pltpu.CompilerParams(vmem_limit_bytes=VMEM_LIMIT),
    )(*flat)
    return [res[4 * i:4 * i + 4] for i in range(n)]


SHARDED = [("ab_w_in", 2), ("ab_w_out", 1), ("cd_w_in", 2), ("cd_w_out", 1), ("ffn_w_up", 2), ("ffn_w_down", 1),
           ("ssd_conv_w", 2), ("rg_conv_w", 2), ("rg_conv_b", 1), ("rg_ba", 1), ("rg_bx", 1), ("rg_lambda", 1),
           ("ffn_conv_w", 2), ("ln_g", 2), ("ln_b", 2)]
REPLICATED = ["ssd_conv_b", "ssd_dt_bias", "ssd_a_log", "ssd_d", "ssd_norm_w", "hg_lower", "hg_norm_w", "swa_sinks",
              "rg_wa", "rg_wx", "ffn_conv_b"]
WEIGHTS = ["ab_w_in", "ssd_conv_w", "ssd_conv_b", "ssd_dt_bias", "ssd_a_log", "ssd_d", "ssd_norm_w", "hg_lower",
           "hg_norm_w", "ab_w_out", "cd_w_in", "swa_sinks", "rg_conv_w", "rg_conv_b", "rg_wa", "rg_ba", "rg_wx",
           "rg_bx", "rg_lambda", "cd_w_out", "ffn_w_up", "ffn_conv_w", "ffn_conv_b", "ffn_w_down", "ln_g", "ln_b"]


def _as2d(a):
    return a.reshape(-1, a.shape[-1])


def _merge_shards(g, axis):
    g = jnp.moveaxis(g, 0, axis)
    shp = g.shape
    return g.reshape(shp[:axis] + (shp[axis] * shp[axis + 1],) + shp[axis + 2:])


def _split_shards(full, axis):
    shp = full.shape
    g = full.reshape(shp[:axis] + (N_DEV, shp[axis] // N_DEV) + shp[axis + 1:])
    return jnp.moveaxis(g, axis, 0)


def _ab_pad(w):
    z, xbc, dt = w[..., 0:512], w[..., 512:1280], w[..., 1280:1288]
    hqfi, hg = w[..., 1288:2824], w[..., 2824:3336]
    return jnp.concatenate([hqfi, jnp.repeat(dt, 64, axis=-1), z, hg, xbc], axis=-1)


def _ab_unpad(d):
    lead = d.shape[:-1]
    dt = d[..., AB_DT:AB_ZG].reshape(lead + (8, 64)).sum(-1)
    z, hg, xbc = d[..., AB_ZG:AB_ZG + 512], d[..., AB_ZG + 512:AB_XBC], d[..., AB_XBC:AB_PAD]
    return jnp.concatenate([z, xbc, dt, d[..., :AB_DT], hg], axis=-1)


def _cd_pad(w):
    return jnp.concatenate([w[..., :768], jnp.zeros(w.shape[:-1] + (256,), w.dtype), w[..., 768:]], axis=-1)


def _cd_unpad(d):
    return jnp.concatenate([d[..., :768], d[..., CD_GATE:CD_PAD]], axis=-1)


def _cat_halves(d):
    return jnp.concatenate(d, axis=1)


def _block_diag(w):
    eye = jnp.eye(8, dtype=w.dtype)
    return jnp.einsum("gij,gh->gihj", w, eye).reshape(512, 512)


def _block_diag_grad(d):
    return jnp.stack([d[64 * g:64 * g + 64, 64 * g:64 * g + 64] for g in range(8)])


def _same(a):
    return a


BIG = {"ab_w_in": (1, _ab_pad, _ab_unpad), "ab_w_out": (0, _same, _same), "cd_w_in": (1, _cd_pad, _cd_unpad),
       "cd_w_out": (0, _same, _same), "ffn_w_up": (1, _same, _cat_halves), "ffn_w_down": (0, _same, _same)}


class _MeshComm:
    def __init__(self, shards):
        self.shards, self.full, self.recv, self.posted = shards, {}, {}, {}

    def post(self, carrier, req):
        self.posted.setdefault(carrier, []).append(req)

    def take(self, carrier):
        return self.posted.pop(carrier, [])

    def take_all(self):
        reqs = [r for name in list(self.posted) for r in self.posted.pop(name)]
        return reqs

    def gather_req(self, name, layer):
        axis, prep, _ = BIG[name]

        def done(got):
            self.full[name, layer] = prep(_merge_shards(got, axis))

        return ("gather", self.shards[name][layer].astype(MM_DTYPE), done)

    def weight(self, name, layer):
        return self.full[name, layer]

    def grad_req(self, name, layer, d):
        axis, _, unprep = BIG[name]

        def done(got):
            self.recv[name, layer] = got

        return ("exchange", _split_shards(unprep(d), axis).astype(MM_DTYPE), done)


class _LocalComm:
    def __init__(self, full):
        self.full_w, self.grads = full, {}

    def post(self, carrier, req):
        pass

    def take(self, carrier):
        return []

    def gather_req(self, name, layer):
        return None

    def weight(self, name, layer):
        return BIG[name][1](self.full_w[name][layer].astype(MM_DTYPE))

    def grad_req(self, name, layer, d):
        self.grads[name, layer] = BIG[name][2](d)
        return None


def _row_vec(v):
    return v.reshape(1, -1)


def _heads64(v):
    return jnp.repeat(v, 64).reshape(1, 512)


def _local_step(x, target, w, comm):
    kinds = ["ab" if layer % 2 == 0 else "cd" for layer in range(DEPTH)]
    in_name = [f"{kinds[layer]}_in{layer // 2}" for layer in range(DEPTH)]
    core_name = [("hg" if layer % 2 == 0 else "swa") + f"{layer // 2}_fwd" for layer in range(DEPTH)]
    comm.post("ssd0_fwd", comm.gather_req("ffn_w_down", 0))
    comm.post(core_name[0], comm.gather_req("ffn_w_up", 0))
    for layer in range(DEPTH - 1):
        nxt, nj = kinds[layer + 1], (layer + 1) // 2
        if kinds[layer] == "ab":
            comm.post(in_name[layer], comm.gather_req(nxt + "_w_in", nj))
            comm.post(core_name[layer], comm.gather_req(nxt + "_w_out", nj))
            comm.post(core_name[layer], comm.gather_req("ffn_w_down", layer + 1))
        else:
            comm.post(in_name[layer], comm.gather_req(nxt + "_w_out", nj))
            comm.post(core_name[layer], comm.gather_req(nxt + "_w_in", nj))
            comm.post(f"ffn_up{layer}", comm.gather_req("ffn_w_down", layer + 1))
        comm.post(f"ffn_act{layer}_fwd", comm.gather_req("ffn_w_up", layer + 1))

    t = x.shape[0]
    bt = _pick(t, (512, 256, 128))
    nb = t // bt
    bs = _pick(t, (512, 256, 128))
    bc = _pick(t, (FFN_ROWS, 512, 256, 128))

    def rowop(name, fn, params, xs, widths_out, out_dtype=F32, dx_dtypes=None):
        outs = [((t, wd), (bt, wd), lambda g, c: (c, 0), out_dtype) for wd in widths_out]
        return _Op(name, fn, (1, nb), params, xs, outs, dx_dtypes=dx_dtypes)

    def rowblk(arr, width, first=0):
        return (arr, (bt, width), lambda g, c: (c, first))

    one_row = lambda g, c: (0, 0)
    lb_op = _Op("hg_lb", _lb_fn, (1, 1), [],
                [(w["hg_lower"][0:1], (1, 512), one_row), (w["hg_lower"][1:2], (1, 512), one_row)],
                [((1, 512), (1, 512), one_row, F32)] * 2)
    lb_all = lb_op.ys

    tape = []
    grads = {}

    def add_grad(name, idx, val):
        grads.setdefault(name, {})[idx] = val

    def dw_matmul(name, a, b, wname, idx, carrier):
        d = _matmul(name, a, b, "tn", out_dtype=MM_DTYPE, comm=comm)
        comm.post(carrier, comm.grad_req(wname, idx, d))

    for layer in range(DEPTH):
        j = layer // 2
        rec = {"x_in": x}
        if layer % 2 == 0:
            h = _matmul(f"ab_in{j}", x, comm.weight("ab_w_in", j), "nn", comm=comm)
            conv_p = [_row_vec(w["ssd_conv_w"][j, k]) for k in range(4)] + [_row_vec(w["ssd_conv_b"][j])]
            conv = _Op(f"ssd_conv{j}", _make_conv_fn(4, True), (3, t // bc),
                       [(a, (1, 256), lambda g, c: (0, g)) for a in conv_p],
                       [(h, (bc, 256), lambda g, c: (c, AB_XBC // 256 + g))],
                       [((t, 768), (bc, 256), lambda g, c: (c, g), F32)], [(8, 256)], dx_dtypes=[MM_DTYPE])
            ssd = _Op(f"ssd{j}", _ssd_fn, (1, t // SSD_STEP),
                      [_whole(_heads64(w["ssd_dt_bias"][j])), _whole(_heads64(w["ssd_a_log"][j])),
                       _whole(_heads64(w["ssd_d"][j]))],
                      [(conv.ys[0], (SSD_STEP, 768), lambda g, c: (c, 0)),
                       (h, (SSD_STEP, 512), lambda g, c: (c, AB_DT // 512))],
                      [((t, 512), (SSD_STEP, 512), lambda g, c: (c, 0), F32)], [(LANES, LANES)] * 4,
                      dx_dtypes=[F32, MM_DTYPE], comm=comm)
            hg = _Op(f"hg{j}", _hg_fn, (1, t // HG_STEP), [_whole(lb_all[j])],
                     [(h, (HG_STEP, AB_DT), lambda g, c: (c, 0))],
                     [((t, 512), (HG_STEP, 512), lambda g, c: (c, 0), F32)], [(LANES, LANES)] * 4,
                     dx_dtypes=[MM_DTYPE], comm=comm)
            post = rowop(f"ab_post{j}", _ab_post_fn,
                         [_whole(_row_vec(w["ssd_norm_w"][j])), _whole(jnp.tile(_row_vec(w["hg_norm_w"][j]), (1, 4)))],
                         [rowblk(ssd.ys[0], 512), rowblk(hg.ys[0], 512), rowblk(h, 1024, AB_ZG // 1024)], [1024],
                         out_dtype=MM_DTYPE, dx_dtypes=[F32, F32, MM_DTYPE])
            rec.update(kind="ab", conv=conv, ssd=ssd, hg=hg, post=post)
        else:
            h = _matmul(f"cd_in{j}", x, comm.weight("cd_w_in", j), "nn", comm=comm)
            swa = _Op(f"swa{j}", _swa_fn, (1, t // SWA_STEP),
                      [_whole(jnp.tile(w["swa_sinks"][j].reshape(8, 1), (1, LANES)))],
                      [(h, (SWA_STEP, 1024), lambda g, c: (c, 0))],
                      [((t, 512), (SWA_STEP, 512), lambda g, c: (c, 0), F32)], [(SWA_BLOCK, LANES)] * 2,
                      dx_dtypes=[MM_DTYPE], comm=comm)
            conv_p = [_row_vec(w["rg_conv_w"][j, k]) for k in range(4)] + [_row_vec(w["rg_conv_b"][j])]
            conv = _Op(f"rg_conv{j}", _make_conv_fn(4, False), (2, t // bc),
                       [(a, (1, 256), lambda g, c: (0, g)) for a in conv_p],
                       [(h, (bc, 256), lambda g, c: (c, CD_XR // 256 + g))],
                       [((t, 512), (bc, 256), lambda g, c: (c, g), F32)], [(8, 256)], dx_dtypes=[MM_DTYPE])
            gate = rowop(f"rg_gate{j}", _rg_gate_fn,
                         [_whole(_block_diag(w["rg_wa"][j])), _whole(_row_vec(w["rg_ba"][j])),
                          _whole(_block_diag(w["rg_wx"][j])), _whole(_row_vec(w["rg_bx"][j])),
                          _whole(_row_vec(w["rg_lambda"][j]))],
                         [rowblk(conv.ys[0], 512)], [512, 512])
            scan = _Op(f"rg_scan{j}", _rg_scan_fn, (2, t // bs), [],
                       [(gate.ys[0], (bs, 256), lambda g, c: (c, g)), (gate.ys[1], (bs, 256), lambda g, c: (c, g))],
                       [((t, 512), (bs, 256), lambda g, c: (c, g), F32)], [(8, 256)])
            post = rowop(f"cd_post{j}", _cd_post_fn, [],
                         [rowblk(swa.ys[0], 512), rowblk(scan.ys[0], 512), rowblk(h, 512, CD_GATE // 512)], [1024],
                         out_dtype=MM_DTYPE, dx_dtypes=[F32, F32, MM_DTYPE])
            rec.update(kind="cd", swa=swa, conv=conv, gate=gate, scan=scan, post=post)
        kind = rec["kind"]
        ycat = post.ys[0]
        m = _matmul(f"mix_out{layer}", ycat, comm.weight(kind + "_w_out", j), "nn", comm=comm)
        ln1 = rowop(f"ln_a{layer}", _ln_res_fn,
                    [_whole(_row_vec(w["ln_g"][layer, 0])), _whole(_row_vec(w["ln_b"][layer, 0]))],
                    [rowblk(x, 1024), rowblk(m, 1024)], [1024], dx_dtypes=[F32, MM_DTYPE])
        x1 = ln1.ys[0]
        hu = _matmul(f"ffn_up{layer}", x1, comm.weight("ffn_w_up", layer), "nn", comm=comm)
        n_fb = FFN_DIM // FFN_BLK
        taps = [_row_vec(w["ffn_conv_w"][layer, k]) for k in range(3)] + [_row_vec(w["ffn_conv_b"][layer])]
        ba = _pick(t, (FFN_ROWS, 512, 256, 128))
        act = _Op(f"ffn_act{layer}", _ffn_act_fn, (n_fb, t // ba),
                  [(a, (1, FFN_BLK), lambda g, c: (0, g)) for a in taps]
                  + [(a, (1, FFN_BLK), lambda g, c: (0, n_fb + g)) for a in taps],
                  [(hu, (ba, FFN_BLK), lambda g, c: (c, g)), (hu, (ba, FFN_BLK), lambda g, c: (c, n_fb + g))],
                  [((t, FFN_DIM), (ba, FFN_BLK), lambda g, c: (c, g), MM_DTYPE)], [(8, FFN_BLK)] * 2,
                  dx_dtypes=[MM_DTYPE, MM_DTYPE], comm=comm, bwd_fn=_ffn_act_bwd)
        a = act.ys[0]
        f = _matmul(f"ffn_down{layer}", a, comm.weight("ffn_w_down", layer), "nn", comm=comm)
        ln2 = rowop(f"ln_f{layer}", _ln_res_fn,
                    [_whole(_row_vec(w["ln_g"][layer, 1])), _whole(_row_vec(w["ln_b"][layer, 1]))],
                    [rowblk(x1, 1024), rowblk(f, 1024)], [1024], dx_dtypes=[F32, MM_DTYPE])
        rec.update(ycat=ycat, ln1=ln1, x1=x1, act=act, a=a, ln2=ln2)
        tape.append(rec)
        x = ln2.ys[0]

    dx, loss = _loss_kernel(x, target)

    d_lb = [jnp.zeros((1, 512), F32), jnp.zeros((1, 512), F32)]
    for layer in reversed(range(DEPTH)):
        j = layer // 2
        rec = tape[layer]
        (dg, db), (dx1_res, df) = rec["ln2"].bwd([dx])
        add_grad("ln_g", (layer, 1), dg[0]); add_grad("ln_b", (layer, 1), db[0])
        dw_matmul(f"ffn_down_dw{layer}", rec["a"], df, "ffn_w_down", layer, f"ffn_act{layer}_bwd")
        da = _matmul(f"ffn_down_dx{layer}", df, comm.weight("ffn_w_down", layer), "nt", comm=comm)
        dpa, (dhg, dhu) = rec["act"].bwd([da])
        halves = [jnp.concatenate([dpa[k][0, :FFN_DIM], dpa[4 + k][0, FFN_DIM:]]) for k in range(4)]
        add_grad("ffn_conv_w", layer, jnp.stack(halves[:3]))
        add_grad("ffn_conv_b", layer, halves[3])
        core_bwd = ("hg" if rec["kind"] == "ab" else "swa") + f"{j}_bwd"
        gate_cols, up_cols = (0, FFN_DIM), (FFN_DIM, FFN_DIM)
        dw_up = (_matmul(f"ffn_up_dw_g{layer}", rec["x1"], dhg, "tn", out_dtype=MM_DTYPE, comm=comm, b_cols=gate_cols),
                 _matmul(f"ffn_up_dw_u{layer}", rec["x1"], dhu, "tn", out_dtype=MM_DTYPE, comm=comm, b_cols=up_cols))
        comm.post(core_bwd, comm.grad_req("ffn_w_up", layer, dw_up))
        w_up = comm.weight("ffn_w_up", layer)
        dx1 = _matmul(f"ffn_up_dx_g{layer}", dhg, w_up, "nt", a_cols=gate_cols, b_cols=gate_cols, add=dx1_res,
                      comm=comm)
        dx1 = _matmul(f"ffn_up_dx_u{layer}", dhu, w_up, "nt", a_cols=up_cols, b_cols=up_cols, add=dx1, comm=comm)
        (dg, db), (dx_res, dm) = rec["ln1"].bwd([dx1])
        add_grad("ln_g", (layer, 0), dg[0]); add_grad("ln_b", (layer, 0), db[0])
        kind = rec["kind"]
        dw_matmul(f"mix_out_dw{layer}", rec["ycat"], dm, kind + "_w_out", j, f"{kind}_in_dw{j}")
        dycat = _matmul(f"mix_out_dx{layer}", dm, comm.weight(kind + "_w_out", j), "nt", comm=comm)
        if kind == "ab":
            (dnw_s, dnw_h), (dy_ssd, do_hg, dh) = rec["post"].bwd([dycat])
            add_grad("ssd_norm_w", j, dnw_s[0]); add_grad("hg_norm_w", j, dnw_h[0].reshape(4, LANES).sum(0))
            (dlb,), (dh,) = rec["hg"].bwd([do_hg], dx_into={0: dh})
            d_lb[j] = dlb
            (ddtb, dalog, ddsk), (dxbc_c, dh) = rec["ssd"].bwd([dy_ssd], dx_into={1: dh})
            add_grad("ssd_dt_bias", j, ddtb[0].reshape(8, 64).sum(-1))
            add_grad("ssd_a_log", j, dalog[0].reshape(8, 64).sum(-1))
            add_grad("ssd_d", j, ddsk[0].reshape(8, 64).sum(-1))
            dcp, (dh,) = rec["conv"].bwd([dxbc_c], dx_into={0: dh})
            add_grad("ssd_conv_w", j, jnp.stack([dcp[k][0] for k in range(4)]))
            add_grad("ssd_conv_b", j, dcp[4][0])
        else:
            _, (dyc, dhs, dh) = rec["post"].bwd([dycat])
            _, (da_s, du_s) = rec["scan"].bwd([dhs])
            (dwa, dba, dwx, dbx, dlam), (dxc,) = rec["gate"].bwd([da_s, du_s])
            add_grad("rg_wa", j, _block_diag_grad(dwa)); add_grad("rg_wx", j, _block_diag_grad(dwx))
            add_grad("rg_ba", j, dba[0]); add_grad("rg_bx", j, dbx[0]); add_grad("rg_lambda", j, dlam[0])
            dcp, (dh,) = rec["conv"].bwd([dxc], dx_into={0: dh})
            add_grad("rg_conv_w", j, jnp.stack([dcp[k][0] for k in range(4)]))
            add_grad("rg_conv_b", j, dcp[4][0])
            (dsink,), (dh,) = rec["swa"].bwd([dyc], dx_into={0: dh})
            add_grad("swa_sinks", j, dsink.sum(-1))
        dw_matmul(f"{kind}_in_dw{j}", rec["x_in"], dh, kind + "_w_in", j,
                  f"ffn_act{layer - 1}_bwd" if layer > 0 else f"{kind}_in_dx{j}")
        dx = _matmul(f"{kind}_in_dx{j}", dh, comm.weight(kind + "_w_in", j), "nt", add=dx_res, comm=comm)

    _, (dl0, dl1) = lb_op.bwd(d_lb)
    out = {"hg_lower": jnp.concatenate([dl0, dl1], axis=0)}
    for name, parts in grads.items():
        keys = sorted(parts)
        if isinstance(keys[0], tuple):
            out[name] = jnp.stack([jnp.stack([parts[(l, s)] for s in range(2)]) for l in range(DEPTH)])
        else:
            out[name] = jnp.stack([parts[k] for k in keys])
    return loss, dx, out


def _local_step_full(x, target, full):
    comm = _LocalComm(full)
    loss, dx, grads = _local_step(x, target, {n: a for n, a in full.items() if n not in BIG}, comm)
    for name in BIG:
        grads[name] = jnp.stack([comm.grads[name, l] for l in range(full[name].shape[0])])
    return loss, dx, grads


def kernel(x, ab_w_in, ssd_conv_w, ssd_conv_b, ssd_dt_bias, ssd_a_log, ssd_d, ssd_norm_w, hg_lower, hg_norm_w, ab_w_out, cd_w_in, swa_sinks, rg_conv_w, rg_conv_b, rg_wa, rg_ba, rg_wx, rg_bx, rg_lambda, cd_w_out, ffn_w_up, ffn_conv_w, ffn_conv_b, ffn_w_down, ln_g, ln_b, loss_target, m_ab_w_in, m_ssd_conv_w, m_ssd_conv_b, m_ssd_dt_bias, m_ssd_a_log, m_ssd_d, m_ssd_norm_w, m_hg_lower, m_hg_norm_w, m_ab_w_out, m_cd_w_in, m_swa_sinks, m_rg_conv_w, m_rg_conv_b, m_rg_wa, m_rg_ba, m_rg_wx, m_rg_bx, m_rg_lambda, m_cd_w_out, m_ffn_w_up, m_ffn_conv_w, m_ffn_conv_b, m_ffn_w_down, m_ln_g, m_ln_b, v_ab_w_in, v_ssd_conv_w, v_ssd_conv_b, v_ssd_dt_bias, v_ssd_a_log, v_ssd_d, v_ssd_norm_w, v_hg_lower, v_hg_norm_w, v_ab_w_out, v_cd_w_in, v_swa_sinks, v_rg_conv_w, v_rg_conv_b, v_rg_wa, v_rg_ba, v_rg_wx, v_rg_bx, v_rg_lambda, v_cd_w_out, v_ffn_w_up, v_ffn_conv_w, v_ffn_conv_b, v_ffn_w_down, v_ln_g, v_ln_b):
    args = dict(locals())
    wts = {n: args[n] for n in WEIGHTS}
    mom = {n: args["m_" + n] for n in WEIGHTS}
    var = {n: args["v_" + n] for n in WEIGHTS}
    axis = dict(SHARDED)
    small = [n for n, _ in SHARDED if n not in BIG]
    comm = _MeshComm(wts)

    def run(name, reqs):
        for (_, _, done), got in zip(reqs, _remote_copies(name, [(k, a) for k, a, _ in reqs])):
            done(got)

    full = {n: wts[n] for n in REPLICATED}

    def keep_small(n):
        def done(got):
            full[n] = _merge_shards(got.reshape((N_DEV,) + wts[n].shape), axis[n])
        return ("gather", _as2d(wts[n]), done)

    run("gather_first", [comm.gather_req("ab_w_in", 0), keep_small("ssd_conv_w")])
    for req in [comm.gather_req("ab_w_out", 0)] + [keep_small(n) for n in small if n != "ssd_conv_w"]:
        comm.post("ab_in0", req)

    loss, grad_x, grads = _local_step(x[0], loss_target[0], full, comm)
    loss = lax.psum(loss, ("x", "y", "c"))

    parts = {}

    def keep_parts(n, kind, arr):
        return (kind, arr, lambda got: parts.__setitem__(n, got))

    last = comm.take_all()
    last += [keep_parts(n, "exchange", _split_shards(grads[n], axis[n]).reshape((N_DEV,) + _as2d(wts[n]).shape))
             for n in small]
    last += [keep_parts(n, "gather", _as2d(grads[n])) for n in REPLICATED]
    run("exchange_last", last)

    new = {}
    for n in BIG:
        new[n] = _adamw_big("adamw_" + n, [comm.recv[n, l] for l in range(wts[n].shape[0])], wts[n], mom[n], var[n])
    names = small + REPLICATED
    res = _adamw_small("adamw_small", [(parts[n], _as2d(wts[n]), _as2d(mom[n]), _as2d(var[n])) for n in names])
    for n, r in zip(names, res):
        new[n] = [a.reshape(wts[n].shape) for a in r]

    outs = [loss, grad_x[None]]
    for kind in range(4):
        outs += [new[n][kind] for n in WEIGHTS]
    return tuple(outs)
```

```python
import math

import jax
import jax.numpy as jnp
from jax import lax
from jax.experimental import pallas as pl
from jax.experimental.pallas import tpu as pltpu

F32 = jnp.float32
BF16 = jnp.bfloat16
MM_DTYPE = BF16

DEPTH = 4
N_DEV = 8
LN_EPS = 1e-5
RMS_EPS = 1e-6
MASK_VALUE = -1e9
ALPHA = (2 * DEPTH) ** 0.25
RG_C = 8.0
FFN_DIM = 2816
SSD_CHUNK = 128
SSD_STEP = 256
SWA_STEP = 256
HG_CHUNK = 128
HG_STEP = 256
SWA_BLOCK = 128
LANES = 128
VMEM_LIMIT = 56 * 1024 * 1024

ADAM_LR, ADAM_B1, ADAM_B2, ADAM_EPS, ADAM_WD, ADAM_STEP = 0.001, 0.9, 0.999, 1e-08, 0.01, 10

AB_HEADS, AB_DT, AB_ZG, AB_XBC, AB_PAD = 0, 1536, 2048, 3072, 3840
CD_QKV, CD_GATE, CD_XR, CD_PAD = 0, 1024, 1536, 2048
FFN_BLK = 256
FFN_ROWS = 1024
FFN_STRIP = 32


def _cols(x, lo, hi):
    n = x.shape[1]

    @jax.custom_vjp
    def f(x):
        return x[:, lo:hi]

    def bwd(_, g):
        parts = []
        if lo > 0:
            parts.append(jnp.zeros((g.shape[0], lo), g.dtype))
        parts.append(g)
        if hi < n:
            parts.append(jnp.zeros((g.shape[0], n - hi), g.dtype))
        return (jnp.concatenate(parts, axis=1) if len(parts) > 1 else g,)

    f.defvjp(lambda x: (f(x), None), bwd)
    return f(x)


def _split_cols(x, width):
    n = x.shape[1] // width

    @jax.custom_vjp
    def f(x):
        return tuple(x[:, width * i:width * (i + 1)] for i in range(n))

    f.defvjp(lambda x: (f(x), None), lambda _, gs: (jnp.concatenate(gs, axis=1),))
    return f(x)


def _rows(x, lo, hi):
    n = x.shape[0]

    @jax.custom_vjp
    def f(x):
        return x[lo:hi, :]

    def bwd(_, g):
        parts = []
        if lo > 0:
            parts.append(jnp.zeros((lo, g.shape[1]), g.dtype))
        parts.append(g)
        if hi < n:
            parts.append(jnp.zeros((n - hi, g.shape[1]), g.dtype))
        return (jnp.concatenate(parts, axis=0) if len(parts) > 1 else g,)

    f.defvjp(lambda x: (f(x), None), bwd)
    return f(x)


def _roll(x, shift, axis):
    n = x.shape[axis]
    shift = shift % n
    if shift == 0:
        return x

    @jax.custom_vjp
    def f(x):
        return pltpu.roll(x, shift, axis)

    f.defvjp(lambda x: (f(x), None), lambda _, g: (pltpu.roll(g, n - shift, axis),))
    return f(x)


def _dot(a, b):
    return lax.dot_general(a, b, (((1,), (0,)), ((), ())), preferred_element_type=F32)


def _dot_nt(a, b):
    return lax.dot_general(a, b, (((1,), (1,)), ((), ())), preferred_element_type=F32)


def _dot_tn(a, b):
    return lax.dot_general(a, b, (((0,), (0,)), ((), ())), preferred_element_type=F32)


def _split3(x):
    hi = x.astype(BF16)
    r = x - hi.astype(F32)
    mid = r.astype(BF16)
    return hi, mid, (r - mid.astype(F32)).astype(BF16)


def _sel_dot(sel, x):
    def run(mat, v, dims):
        n = v.shape[1]
        y = lax.dot_general(mat, jnp.concatenate(_split3(v), axis=1), dims, preferred_element_type=F32)
        return y[:, :n] + y[:, n:2 * n] + y[:, 2 * n:]

    @jax.custom_vjp
    def f(sel, x):
        return run(sel, x, (((1,), (0,)), ((), ())))

    def bwd(sel, g):
        return jnp.zeros_like(sel), run(sel, g, (((0,), (0,)), ((), ())))

    f.defvjp(lambda sel, x: (f(sel, x), sel), bwd)
    return f(sel, x)


def _dot_sel(x, sel):
    def run(v, mat, dims):
        m = v.shape[0]
        y = lax.dot_general(jnp.concatenate(_split3(v), axis=0), mat, dims, preferred_element_type=F32)
        return y[:m] + y[m:2 * m] + y[2 * m:]

    @jax.custom_vjp
    def f(x, sel):
        return run(x, sel, (((1,), (0,)), ((), ())))

    def bwd(sel, g):
        return run(g, sel, (((1,), (1,)), ((), ()))), jnp.zeros_like(sel)

    f.defvjp(lambda x, sel: (f(x, sel), sel), bwd)
    return f(x, sel)


def _sigmoid(x):
    return 0.5 * jnp.tanh(0.5 * x) + 0.5


def _silu(x):
    h = 0.5 * x
    return h + h * jnp.tanh(h)


def _softplus(x):
    return jnp.maximum(x, 0.0) + jnp.log(1.0 + jnp.exp(-jnp.abs(x)))


def _gelu_tanh(x):
    c = math.sqrt(2.0 / math.pi)
    return 0.5 * x * (1.0 + jnp.tanh(c * (x + 0.044715 * (x * x * x))))


def _iota(shape, axis):
    return lax.broadcasted_iota(jnp.int32, shape, axis)


def _lane_mask(lo, hi, width=LANES):
    lane = _iota((1, width), 1)
    return ((lane >= lo) & (lane < hi)).astype(F32)


def _mesh_pos():
    return lax.axis_index("x"), lax.axis_index("y"), lax.axis_index("c")


def _carry_shapes(carry):
    return [jax.ShapeDtypeStruct((N_DEV,) + a.shape if kind == "gather" else a.shape, a.dtype) for kind, a in carry]


def _carry_scratch(carry):
    n = len(carry)
    if n == 0:
        return []
    return [pltpu.SemaphoreType.DMA((n, N_DEV - 1)), pltpu.SemaphoreType.DMA((n, N_DEV - 1)),
            pltpu.SemaphoreType.DMA((n,))]


def _carry_run(start, kinds, in_refs, out_refs, send_sems, recv_sems, local_sems):
    x, y, cc = _mesh_pos()
    me = 4 * x + 2 * y + cc
    for i, kind in enumerate(kinds):
        mine = in_refs[i] if kind == "gather" else in_refs[i].at[me]
        local = pltpu.make_async_copy(mine, out_refs[i].at[me], local_sems.at[i])
        remote = []
        for k in range(1, N_DEV):
            px, py, pc = x ^ ((k >> 2) & 1), y ^ ((k >> 1) & 1), cc ^ (k & 1)
            src = in_refs[i] if kind == "gather" else in_refs[i].at[4 * px + 2 * py + pc]
            remote.append(pltpu.make_async_remote_copy(
                src_ref=src, dst_ref=out_refs[i].at[me],
                send_sem=send_sems.at[i, k - 1], recv_sem=recv_sems.at[i, k - 1],
                device_id=(px, py, pc), device_id_type=pl.DeviceIdType.MESH))
        if start:
            local.start()
            for cp in remote:
                cp.start()
        else:
            for cp in remote:
                cp.wait_recv()
            for cp in remote:
                cp.wait_send()
            local.wait()


def _remote_copies(name, carry):
    n = len(carry)
    kinds = [k for k, _ in carry]

    def body(*refs):
        sems = refs[2 * n:]
        _carry_run(True, kinds, refs[:n], refs[n:2 * n], *sems)
        _carry_run(False, kinds, refs[:n], refs[n:2 * n], *sems)

    return pl.pallas_call(
        body, name=name, out_shape=_carry_shapes(carry),
        in_specs=[pl.BlockSpec(memory_space=pl.ANY)] * n, out_specs=[pl.BlockSpec(memory_space=pl.ANY)] * n,
        scratch_shapes=_carry_scratch(carry),
    )(*[a for _, a in carry])


def _cparams(sem):
    return pltpu.CompilerParams(dimension_semantics=sem, vmem_limit_bytes=VMEM_LIMIT)


def _chunk_fwd(name, fn, grid, params, xs, outs, state_shapes, carry=()):
    n_g, n_c = grid
    n_p, n_x, n_o, n_s, n_r = len(params), len(xs), len(outs), len(state_shapes), len(carry)
    kinds = [k for k, _ in carry]

    def body(*refs):
        i = 0
        p_refs = refs[i:i + n_p]; i += n_p
        x_refs = refs[i:i + n_x]; i += n_x
        ci_refs = refs[i:i + n_r]; i += n_r
        o_refs = refs[i:i + n_o]; i += n_o
        sv_refs = refs[i:i + n_s]; i += n_s
        co_refs = refs[i:i + n_r]; i += n_r
        st_refs = refs[i:i + n_s]; i += n_s
        sems = refs[i:]
        g, c = pl.program_id(0), pl.program_id(1)

        if n_r:
            @pl.when((g == 0) & (c == 0))
            def _():
                _carry_run(True, kinds, ci_refs, co_refs, *sems)

        @pl.when(c == 0)
        def _():
            for s in st_refs:
                s[...] = jnp.zeros(s.shape, s.dtype)

        st = [s[...] for s in st_refs]
        ys, new_st = fn(c, [p[...] for p in p_refs], [x[...].astype(F32) for x in x_refs], st)
        for o, y in zip(o_refs, ys):
            o[...] = y.astype(o.dtype)
        for sv, s in zip(sv_refs, st):
            sv[0, 0] = s
        for s_ref, s in zip(st_refs, new_st):
            s_ref[...] = s

        if n_r:
            @pl.when((g == n_g - 1) & (c == n_c - 1))
            def _():
                _carry_run(False, kinds, ci_refs, co_refs, *sems)

    any_spec = pl.BlockSpec(memory_space=pl.ANY)
    in_specs = [pl.BlockSpec(b, m) for _, b, m in params] + [pl.BlockSpec(b, m) for _, b, m in xs] + [any_spec] * n_r
    out_specs = [pl.BlockSpec(b, m) for _, b, m, _ in outs]
    out_shape = [jax.ShapeDtypeStruct(s, d) for s, _, _, d in outs]
    for shp in state_shapes:
        out_specs.append(pl.BlockSpec((1, 1) + shp, lambda g, c, n=len(shp): (g, c) + (0,) * n))
        out_shape.append(jax.ShapeDtypeStruct((n_g, n_c) + shp, F32))
    out_specs += [any_spec] * n_r
    out_shape += _carry_shapes(carry)
    res = pl.pallas_call(
        body, name=name, grid=grid, in_specs=in_specs, out_specs=out_specs, out_shape=out_shape,
        scratch_shapes=[pltpu.VMEM(shp, F32) for shp in state_shapes] + _carry_scratch(carry),
        compiler_params=_cparams(("arbitrary", "arbitrary")),
    )(*[a for a, _, _ in params], *[a for a, _, _ in xs], *[a for _, a in carry])
    return list(res[:n_o]), list(res[n_o:n_o + n_s]), list(res[n_o + n_s:])


def _chunk_bwd(name, fn, grid, params, xs, saved, dys, state_shapes, dx_dtypes, dx_into, carry=(), bwd_fn=None):
    n_g, n_c = grid
    n_p, n_x, n_s, n_y, n_r = len(params), len(xs), len(state_shapes), len(dys), len(carry)
    kinds = [k for k, _ in carry]
    into = sorted(dx_into)
    n_a = len(into)

    def rev(m):
        return lambda g, c: m(g, n_c - 1 - c)

    def body(*refs):
        i = 0
        p_refs = refs[i:i + n_p]; i += n_p
        x_refs = refs[i:i + n_x]; i += n_x
        sv_refs = refs[i:i + n_s]; i += n_s
        dy_refs = refs[i:i + n_y]; i += n_y
        i += n_a
        ci_refs = refs[i:i + n_r]; i += n_r
        dp_refs = refs[i:i + n_p]; i += n_p
        dx_refs = refs[i:i + n_x]; i += n_x
        co_refs = refs[i:i + n_r]; i += n_r
        ds_refs = refs[i:i + n_s]; i += n_s
        sems = refs[i:]
        g, c = pl.program_id(0), pl.program_id(1)
        chunk = n_c - 1 - c

        if n_r:
            @pl.when((g == 0) & (c == 0))
            def _():
                _carry_run(True, kinds, ci_refs, co_refs, *sems)

        @pl.when(c == 0)
        def _():
            for s in ds_refs:
                s[...] = jnp.zeros(s.shape, s.dtype)
            for d in dp_refs:
                d[...] = jnp.zeros(d.shape, d.dtype)

        pv = [p[...] for p in p_refs]
        xv = [x[...].astype(F32) for x in x_refs]
        sv = [s[0, 0] for s in sv_refs]
        dyv, dsv = [d[...].astype(F32) for d in dy_refs], [s[...] for s in ds_refs]
        if bwd_fn is None:
            _, vjp = jax.vjp(lambda p, x, s: fn(chunk, p, x, s), pv, xv, sv)
            dp, dx, ds = vjp((dyv, dsv))
        else:
            dp, dx, ds = bwd_fn(chunk, pv, xv, sv, dyv, dsv)
        for r, v in zip(dp_refs, dp):
            r[...] += v
        for r, v in zip(dx_refs, dx):
            r[...] = v.astype(r.dtype)
        for r, v in zip(ds_refs, ds):
            r[...] = v

        if n_r:
            @pl.when((g == n_g - 1) & (c == n_c - 1))
            def _():
                _carry_run(False, kinds, ci_refs, co_refs, *sems)

    any_spec = pl.BlockSpec(memory_space=pl.ANY)
    in_specs = [pl.BlockSpec(b, rev(m)) for _, b, m in params] + [pl.BlockSpec(b, rev(m)) for _, b, m in xs]
    for shp in state_shapes:
        in_specs.append(pl.BlockSpec((1, 1) + shp, lambda g, c, n=len(shp): (g, n_c - 1 - c) + (0,) * n))
    in_specs += [pl.BlockSpec(b, rev(m)) for _, b, m in dys]
    in_specs += [any_spec] * (n_a + n_r)
    out_specs = [pl.BlockSpec(b, rev(m)) for _, b, m in params] + [pl.BlockSpec(b, rev(m)) for _, b, m in xs]
    out_specs += [any_spec] * n_r
    out_shape = [jax.ShapeDtypeStruct(a.shape, F32) for a, _, _ in params]
    out_shape += [jax.ShapeDtypeStruct(a.shape, d) for (a, _, _), d in zip(xs, dx_dtypes)]
    out_shape += _carry_shapes(carry)
    first_alias = n_p + n_x + n_s + n_y
    aliases = {first_alias + k: n_p + xi for k, xi in enumerate(into)}
    res = pl.pallas_call(
        body, name=name, grid=grid, in_specs=in_specs, out_specs=out_specs, out_shape=out_shape,
        scratch_shapes=[pltpu.VMEM(shp, F32) for shp in state_shapes] + _carry_scratch(carry),
        input_output_aliases=aliases,
        compiler_params=_cparams(("arbitrary", "arbitrary")),
    )(*[a for a, _, _ in params], *[a for a, _, _ in xs], *saved, *[a for a, _, _ in dys],
      *[dx_into[xi] for xi in into], *[a for _, a in carry])
    return list(res[:n_p]), list(res[n_p:n_p + n_x]), list(res[n_p + n_x:])


class _Op:
    def __init__(self, name, fn, grid, params, xs, outs, state_shapes=(), dx_dtypes=None, comm=None, bwd_fn=None):
        self.name, self.fn, self.grid, self.comm, self.bwd_fn = name, fn, grid, comm, bwd_fn
        self.params, self.xs, self.outs, self.state_shapes = params, xs, outs, list(state_shapes)
        self.dx_dtypes = dx_dtypes or [F32] * len(xs)
        reqs = comm.take(name + "_fwd") if comm is not None else []
        self.ys, self.saved, got = _chunk_fwd(name + "_fwd", fn, grid, params, xs, outs, self.state_shapes,
                                              carry=[(k, a) for k, a, _ in reqs])
        for (_, _, done), g in zip(reqs, got):
            done(g)

    def bwd(self, dys, dx_into=None):
        dy_defs = [(d, b, m) for d, (_, b, m, _) in zip(dys, self.outs)]
        reqs = self.comm.take(self.name + "_bwd") if self.comm is not None else []
        dps, dxs, got = _chunk_bwd(self.name + "_bwd", self.fn, self.grid, self.params, self.xs, self.saved, dy_defs,
                                   self.state_shapes, self.dx_dtypes, dx_into or {},
                                   carry=[(k, a) for k, a, _ in reqs], bwd_fn=self.bwd_fn)
        for (_, _, done), g in zip(reqs, got):
            done(g)
        return dps, dxs


def _whole(a):
    nd = a.ndim
    return (a, a.shape, lambda g, c: (0,) * nd)


def _pick(n, prefs):
    for p in prefs:
        if n % p == 0:
            return p
    return n


def _mm_blocks(mode, m, n, k):
    bn = _pick(n, (1408, 1280, 1024, 768, 512, 256, 128))
    if mode == "tn":
        return _pick(m, (1408, 1024, 768, 512, 256, 128)), bn, _pick(k, (2048, 1024, 512, 256, 128))
    bk = k if k <= 3840 else _pick(k, (2816, 1920, 1408, 1024, 512, 256, 128))
    return _pick(m, (1024, 512, 256, 128)), bn, bk


def _matmul(name, a, b, mode, *, add=None, out_dtype=F32, comm=None, a_cols=None, b_cols=None):
    a0, asize = a_cols if a_cols is not None else (0, a.shape[1])
    c0, csize = b_cols if b_cols is not None else (0, b.shape[1])
    if mode == "nn":
        (m, k), n = (a.shape[0], asize), csize
    elif mode == "nt":
        (m, k), n = (a.shape[0], asize), b.shape[0]
        assert k == csize
    else:
        (k, m), n = (a.shape[0], asize), csize
    bm, bn, bk = _mm_blocks(mode, m, n, k)
    assert c0 % (bk if mode == "nt" else bn) == 0 and a0 % (bm if mode == "tn" else bk) == 0
    j0, k0 = (0, c0 // bk) if mode == "nt" else (c0 // bn, 0)
    ia = a0 // (bm if mode == "tn" else bk)
    n_i, n_j, n_k = m // bm, n // bn, k // bk
    dims = {"nn": (((1,), (0,)), ((), ())), "nt": (((1,), (1,)), ((), ())), "tn": (((0,), (0,)), ((), ()))}[mode]
    has_add = add is not None
    reqs = comm.take(name) if comm is not None else []
    carry = [(kind, arr) for kind, arr, _ in reqs]
    kinds = [kind for kind, _ in carry]
    n_r = len(carry)

    def body(*refs):
        i = 2
        a_ref, b_ref = refs[0], refs[1]
        c_ref = refs[i] if has_add else None
        i += has_add
        ci_refs = refs[i:i + n_r]; i += n_r
        o_ref = refs[i]; i += 1
        co_refs = refs[i:i + n_r]; i += n_r
        acc = refs[i]; i += 1
        sems = refs[i:]
        ii, jj, kk = pl.program_id(0), pl.program_id(1), pl.program_id(2)

        if n_r:
            @pl.when((ii == 0) & (jj == 0) & (kk == 0))
            def _():
                _carry_run(True, kinds, ci_refs, co_refs, *sems)

        part = lax.dot_general(a_ref[...].astype(MM_DTYPE), b_ref[...].astype(MM_DTYPE), dims,
                               preferred_element_type=F32)

        def finish(r):
            if has_add:
                r = r + c_ref[...]
            o_ref[...] = r.astype(o_ref.dtype)

        if n_k == 1:
            finish(part)
        else:
            @pl.when(kk == 0)
            def _():
                acc[...] = part

            @pl.when((kk > 0) & (kk < n_k - 1))
            def _():
                acc[...] += part

            @pl.when(kk == n_k - 1)
            def _():
                finish(acc[...] + part)

        if n_r:
            @pl.when((ii == n_i - 1) & (jj == n_j - 1) & (kk == n_k - 1))
            def _():
                _carry_run(False, kinds, ci_refs, co_refs, *sems)

    if mode == "nn":
        a_spec = pl.BlockSpec((bm, bk), lambda i, j, kk: (i, ia + kk))
        b_spec = pl.BlockSpec((bk, bn), lambda i, j, kk: (kk, j0 + j))
    elif mode == "nt":
        a_spec = pl.BlockSpec((bm, bk), lambda i, j, kk: (i, ia + kk))
        b_spec = pl.BlockSpec((bn, bk), lambda i, j, kk: (j, k0 + kk))
    else:
        a_spec = pl.BlockSpec((bk, bm), lambda i, j, kk: (kk, ia + i))
        b_spec = pl.BlockSpec((bk, bn), lambda i, j, kk: (kk, j0 + j))
    any_spec = pl.BlockSpec(memory_space=pl.ANY)
    in_specs, args = [a_spec, b_spec], [a, b]
    if has_add:
        in_specs.append(pl.BlockSpec((bm, bn), lambda i, j, kk: (i, j)))
        args.append(add)
    res = pl.pallas_call(
        body, name=name, grid=(n_i, n_j, n_k), in_specs=in_specs + [any_spec] * n_r,
        out_specs=[pl.BlockSpec((bm, bn), lambda i, j, kk: (i, j))] + [any_spec] * n_r,
        out_shape=[jax.ShapeDtypeStruct((m, n), out_dtype)] + _carry_shapes(carry),
        scratch_shapes=[pltpu.VMEM((bm, bn) if n_k > 1 else (8, LANES), F32)] + _carry_scratch(carry),
        compiler_params=_cparams(("arbitrary", "arbitrary", "arbitrary")),
    )(*args, *[arr for _, arr in carry])
    for (_, _, done), g in zip(reqs, res[1:]):
        done(g)
    return res[0]


def _ln_res_fn(_, p, x, st):
    g, b = p
    xin, m = x
    pre = ALPHA * xin + m
    mu = jnp.mean(pre, -1, keepdims=True)
    d = pre - mu
    var = jnp.mean(d * d, -1, keepdims=True)
    return [d * lax.rsqrt(var + LN_EPS) * g + b], []


def _make_conv_fn(taps, act):
    def fn(_, p, x, st):
        ws, b = p[:taps], p[taps]
        (xin,), (prev,) = x, st
        n = xin.shape[0]
        ext = jnp.concatenate([prev, xin], axis=0)
        y = b
        for k in range(taps):
            y = y + ws[k] * _rows(_roll(ext, taps - 1 - k, 0), 8, 8 + n)
        if act:
            y = _silu(y)
        return [y], [_rows(xin, n - 8, n)]

    return fn


def _ffn_act_fn(_, p, x, st):
    n = x[0].shape[0]
    ys = []
    for half in range(2):
        ws, b = p[4 * half:4 * half + 3], p[4 * half + 3]
        ext = jnp.concatenate([st[half], x[half]], axis=0)
        y = b
        for k in range(3):
            y = y + ws[k] * _rows(_roll(ext, 2 - k, 0), 8, 8 + n)
        ys.append(y)
    return [_silu(ys[0]) * ys[1]], [_rows(x[0], n - 8, n), _rows(x[1], n - 8, n)]


def _ffn_act_bwd(_, p, x, st, dy, dst):
    (da,) = dy
    n, wd = x[0].shape
    rs = FFN_STRIP
    last = n // rs - 1
    zero8 = jnp.zeros((8, wd), F32)
    acc = [[zero8] * 4, [zero8] * 4]
    after = [zero8, zero8]
    strips = [[None] * (n // rs), [None] * (n // rs)]
    for i in reversed(range(n // rs)):
        r0 = rs * i
        taps, ys = [], []
        for half in range(2):
            w0, w1, w2, b = p[4 * half:4 * half + 4]
            xs = jnp.concatenate([st[half] if i == 0 else x[half][r0 - 8:r0], x[half][r0:r0 + rs]], axis=0)
            taps.append((pltpu.roll(xs, 2, 0)[8:], pltpu.roll(xs, 1, 0)[8:], xs[8:]))
            ys.append(b + w2 * taps[half][2] + w1 * taps[half][1] + w0 * taps[half][0])
        g, u = ys
        s = _sigmoid(g)
        d = da[r0:r0 + rs]
        dys = (d * u * (s * (1.0 + g * (1.0 - s))), d * (g * s))
        for half in range(2):
            w0, w1, w2, _ = p[4 * half:4 * half + 4]
            dyh = dys[half]
            for k, v in enumerate((dyh * taps[half][0], dyh * taps[half][1], dyh * taps[half][2], dyh)):
                for r in range(0, rs, 8):
                    acc[half][k] = acc[half][k] + v[r:r + 8]
            dyp = jnp.concatenate([dyh, after[half]], axis=0)
            dxs = w2 * dyh + w1 * pltpu.roll(dyp, rs + 8 - 1, 0)[:rs] + w0 * pltpu.roll(dyp, rs + 8 - 2, 0)[:rs]
            if i == last:
                dxs = jnp.concatenate([dxs[:rs - 8], dxs[rs - 8:] + dst[half]], axis=0)
            strips[half][i] = dxs
            after[half] = dyh[:8]
    dprev = []
    for half in range(2):
        w0, w1 = p[4 * half], p[4 * half + 1]
        head = jnp.concatenate([zero8, after[half]], axis=0)
        dprev.append((w1 * pltpu.roll(head, 16 - 1, 0) + w0 * pltpu.roll(head, 16 - 2, 0))[:8])
    dps = [jnp.sum(a, axis=0, keepdims=True) for half in range(2) for a in acc[half]]
    return dps, [jnp.concatenate(s_, axis=0) for s_ in strips], dprev


def _ssd_fn(_, p, x, st):
    ys = []
    for r in range(0, SSD_STEP, SSD_CHUNK):
        (y,), st = _ssd_chunk(p, [_rows(v, r, r + SSD_CHUNK) for v in x], st)
        ys.append(y)
    return [jnp.concatenate(ys, axis=0)], st


def _ssd_chunk(p, x, st):
    dtb, alog, dsk = p
    xbc, dtr = x
    L = SSD_CHUNK
    tril = _iota((L, L), 0) >= _iota((L, L), 1)
    xs, bm, cm = _cols(xbc, 0, 512), _cols(xbc, 512, 640), _cols(xbc, 640, 768)
    dt = _softplus(dtr + dtb)
    da = dt * (-jnp.exp(alog))
    cs = _sel_dot(tril.astype(BF16), da)
    pick = ((_iota((LANES, 2 * LANES), 0) == 0) & (_iota((LANES, 2 * LANES), 1) < LANES)) | (
        (_iota((LANES, 2 * LANES), 0) == 64) & (_iota((LANES, 2 * LANES), 1) >= LANES))
    pick = pick.astype(BF16)
    tot = jnp.sum(da, axis=0, keepdims=True)
    xc = xs * dt
    xdec = xc * jnp.exp(tot - cs)
    cs_b, xc_b, xdec_b, ecs_b, etot_b, dsk_b, xs_b = (
        _split_cols(v, LANES) for v in (cs, xc, xdec, jnp.exp(cs), jnp.exp(tot), dsk, xs))
    ys, new_st = [], []
    for pr in range(4):
        grp = pr // 2
        c_g = cm * _lane_mask(64 * grp, 64 * grp + 64)
        gmat = _dot_nt(c_g, bm)
        cols2 = _split_cols(_dot_sel(cs_b[pr], pick), LANES)
        yd = jnp.zeros((L, LANES), F32)
        for half in range(2):
            col = cols2[half]
            diff = col - col.T
            dec = jnp.where(tril, jnp.exp(jnp.where(tril, diff, 0.0)), 0.0)
            yd = yd + _dot(gmat * dec, xc_b[pr]) * _lane_mask(64 * half, 64 * half + 64)
        s_in = st[pr]
        y_off = _dot(c_g, s_in) * ecs_b[pr]
        ys.append(yd + y_off + dsk_b[pr] * xs_b[pr])
        new_st.append(s_in * etot_b[pr] + _dot_tn(bm, xdec_b[pr]))
    return [jnp.concatenate(ys, axis=1)], new_st


def _hg_fn(_, p, x, st):
    ys = []
    for r in range(0, HG_STEP, HG_CHUNK):
        (y,), st = _hg_chunk(p, [_rows(x[0], r, r + HG_CHUNK)], st)
        ys.append(y)
    return [jnp.concatenate(ys, axis=0)], st


def _hg_chunk(p, x, st):
    (lb,) = p
    (xin,) = x
    L = HG_CHUNK
    n_lvl = L.bit_length() - 1
    hq, hf, hi = _split_cols(xin, 512)
    q = _silu(hq)
    logf = jnp.log(lb + (1.0 - lb) * _sigmoid(hf))
    k = (1.0 - lb) * _sigmoid(-hf)
    ti, si = _iota((L, L), 0), _iota((L, L), 1)
    bc = _sel_dot((ti >= si).astype(BF16), logf)
    tot = jnp.sum(logf, axis=0, keepdims=True)
    tn, sn = _iota((n_lvl * L, 1), 0), _iota((n_lvl * L, L), 1)
    row = tn & (L - 1)
    blk = L >> (tn >> n_lvl)
    piv = row - (row & (blk - 1)) + (blk >> 1)
    bcp_all = _sel_dot((sn == piv).astype(BF16), bc)
    t1 = _iota((L, 1), 0)
    qqs, kks, sames = [], [], []
    for lvl in range(n_lvl):
        size = L >> lvl
        upper = (t1 & (size - 1)) >= size // 2
        bcp = _rows(bcp_all, L * lvl, L * (lvl + 1))
        qqs.append(jnp.where(upper, q * jnp.exp(jnp.where(upper, bc - bcp, 0.0)), 0.0))
        kks.append(jnp.where(upper, 0.0, k * jnp.exp(jnp.where(upper, 0.0, bcp - bc))))
        sames.append((ti >> (n_lvl - lvl)) == (si >> (n_lvl - lvl)))
    qq_b = [_split_cols(v, LANES) for v in qqs]
    kk_b = [_split_cols(v, LANES) for v in kks]
    v_b, diag_b, q_in_b, k_out_b, etot_b = (
        _split_cols(v, LANES) for v in (hi, q * k, q * jnp.exp(bc), k * jnp.exp(tot - bc), jnp.exp(tot)))
    outs, new_st = [], []
    for h in range(4):
        attn = jnp.zeros((L, L), F32)
        for lvl in range(n_lvl):
            attn = attn + jnp.where(sames[lvl], _dot_nt(qq_b[lvl][h], kk_b[lvl][h]), 0.0)
        v = v_b[h]
        out = _dot(attn, v) + jnp.sum(diag_b[h], axis=-1, keepdims=True) * v
        outs.append(out + _dot_nt(q_in_b[h], st[h]))
        new_st.append(st[h] * etot_b[h] + _dot_tn(v, k_out_b[h]))
    return [jnp.concatenate(outs, axis=1)], new_st


def _swa_fn(chunk, p, x, st):
    n_sub = SWA_STEP // SWA_BLOCK
    ys = []
    for sub in range(n_sub):
        (y,), st = _swa_block(n_sub * chunk + sub, p, [_rows(x[0], SWA_BLOCK * sub, SWA_BLOCK * (sub + 1))], st)
        ys.append(y)
    return [jnp.concatenate(ys, axis=0)], st


def _swa_block(block, p, x, st):
    (sinks,) = p
    (xin,) = x
    q, k, v = _cols(xin, 0, 512), _cols(xin, 512, 640), _cols(xin, 640, 768)
    kp, vp = st
    T = SWA_BLOCK
    kc = jnp.concatenate([kp, k], axis=0)
    vc = jnp.concatenate([vp, v], axis=0)
    qi, kj = _iota((T, 2 * T), 0), _iota((T, 2 * T), 1)
    rel = qi + T - kj
    mask = (rel >= 0) & (rel < T) & ((kj >= T) | (block > 0))
    srow = _iota((8, LANES), 0)
    q_b = _split_cols(q, LANES)
    outs = []
    for pr in range(4):
        grp = pr // 2
        gm = _lane_mask(64 * grp, 64 * grp + 64)
        km, vm = kc * gm, vc * gm
        q2 = q_b[pr]
        o2 = jnp.zeros((T, LANES), F32)
        for half in range(2):
            hm = _lane_mask(64 * half, 64 * half + 64)
            qh = q2 * hm
            if half != grp:
                qh = _roll(qh, 64, 1)
            s = _dot_nt(qh, km) * 0.125
            s = jnp.where(mask, s, MASK_VALUE)
            sink = jnp.mean(jnp.sum(jnp.where(srow == 2 * pr + half, sinks, 0.0), axis=0, keepdims=True),
                            axis=-1, keepdims=True)
            mx = lax.stop_gradient(jnp.maximum(jnp.max(s, axis=-1, keepdims=True), sink))
            e = jnp.exp(s - mx)
            den = jnp.sum(e, axis=-1, keepdims=True) + jnp.exp(sink - mx)
            o = _dot(e / den, vm)
            if half != grp:
                o = _roll(o, 64, 1)
            o2 = o2 + o * hm
        outs.append(o2)
    return [jnp.concatenate(outs, axis=1)], [k, v]


def _rg_gate_fn(_, p, x, st):
    wa, ba, wx, bx, lam = p
    (xc,) = x
    r = _sigmoid(_dot(xc, wa) + ba)
    i = _sigmoid(_dot(xc, wx) + bx)
    log_a = -RG_C * r * _softplus(-lam)
    a = jnp.exp(log_a)
    t = jnp.tanh(log_a)
    one_minus_a2 = -2.0 * t / (1.0 - t)
    u = jnp.sqrt(jnp.maximum(one_minus_a2, 0.0)) * (i * xc)
    return [a, u], []


def _rg_scan_fn(_, p, x, st):
    a, u = x
    (prev,) = st
    n = a.shape[0]
    row = _iota((n, 1), 0)
    s = 1
    while s < n:
        keep = row >= s
        a_s, u_s = _roll(a, s, 0), _roll(u, s, 0)
        u = jnp.where(keep, a * u_s + u, u)
        a = jnp.where(keep, a * a_s, a)
        s *= 2
    h_in = jnp.sum(jnp.where(_iota((8, 1), 0) == 7, prev, 0.0), axis=0, keepdims=True)
    h = u + a * h_in
    return [h], [_rows(h, n - 8, n)]


def _ab_post_fn(_, p, x, st):
    nw_ssd, nw_hg = p
    y, o, zg = x
    z, hgate = _split_cols(zg, 512)

    def rms(v, width):
        blocks = _split_cols(v, width)
        return jnp.concatenate([b * lax.rsqrt(jnp.mean(b * b, axis=-1, keepdims=True) + RMS_EPS) for b in blocks],
                               axis=1)

    ya = rms(y * _silu(z), 256) * nw_ssd
    yb = rms(o, 128) * nw_hg * _silu(hgate)
    return [jnp.concatenate([ya, yb], axis=1)], []


def _cd_post_fn(_, p, x, st):
    yc, h, gate = x
    return [jnp.concatenate([yc, h * _gelu_tanh(gate)], axis=1)], []


def _lb_fn(_, p, x, st):
    l0, l1 = x
    mx = lax.stop_gradient(jnp.maximum(l0, l1))
    e0, e1 = jnp.exp(l0 - mx), jnp.exp(l1 - mx)
    s0, s1 = e0 / (e0 + e1), e1 / (e0 + e1)
    return [jnp.clip(s0 - s0, 0.0, 1.0), jnp.clip((s0 + s1) - s0, 0.0, 1.0)], []


def _loss_kernel(y, target):
    t, d = y.shape
    bt = _pick(t, (512, 256, 128))

    def body(y_ref, t_ref, dy_ref, l_ref):
        @pl.when(pl.program_id(0) == 0)
        def _():
            l_ref[...] = jnp.zeros(l_ref.shape, F32)

        e = y_ref[...] - t_ref[...]
        dy_ref[...] = e * (1.0 / d)
        l_ref[...] += jnp.sum(e * e, axis=0, keepdims=True) * (0.5 / d)

    dy, part = pl.pallas_call(
        body, name="loss", grid=(t // bt,),
        in_specs=[pl.BlockSpec((bt, d), lambda i: (i, 0)), pl.BlockSpec((bt, d), lambda i: (i, 0))],
        out_specs=[pl.BlockSpec((bt, d), lambda i: (i, 0)), pl.BlockSpec((1, d), lambda i: (0, 0))],
        out_shape=[jax.ShapeDtypeStruct((t, d), F32), jax.ShapeDtypeStruct((1, d), F32)],
        compiler_params=_cparams(("arbitrary",)),
    )(y, target)
    return dy, jnp.sum(part)


def _adamw_math(parts, w_, m_, v_):
    c1 = 1.0 / (1.0 - ADAM_B1 ** ADAM_STEP)
    c2 = 1.0 / (1.0 - ADAM_B2 ** ADAM_STEP)
    g = parts[0].astype(F32)
    for s in range(1, N_DEV):
        g = g + parts[s].astype(F32)
    nm = ADAM_B1 * m_ + (1.0 - ADAM_B1) * g
    nv = ADAM_B2 * v_ + (1.0 - ADAM_B2) * (g * g)
    return g, -ADAM_LR * ((nm * c1) / (jnp.sqrt(nv * c2) + ADAM_EPS) + ADAM_WD * w_), nm, nv


def _adamw_big(name, parts, w, m, v):
    n_l, r, c = w.shape
    br = _pick(r, (256, 176, 128, 64, 32, 16, 8))

    def body(*refs):
        p_refs, (w_ref, m_ref, v_ref), outs = refs[:n_l], refs[n_l:n_l + 3], refs[n_l + 3:]
        for l in range(n_l):
            @pl.when(pl.program_id(0) == l)
            def _(p_ref=p_refs[l]):
                res = _adamw_math([p_ref[s] for s in range(N_DEV)], w_ref[...], m_ref[...], v_ref[...])
                for ref, val in zip(outs, res):
                    ref[...] = val

    blk = pl.BlockSpec((None, br, c), lambda l, i: (l, i, 0))
    p_specs = [pl.BlockSpec((N_DEV, br, c), lambda l, i, k=k: (0, jnp.where(l == k, i, 0), 0)) for k in range(n_l)]
    return pl.pallas_call(
        body, name=name, grid=(n_l, r // br), in_specs=p_specs + [blk, blk, blk],
        out_specs=[blk] * 4, out_shape=[jax.ShapeDtypeStruct(w.shape, F32)] * 4,
        compiler_params=_cparams(("arbitrary", "arbitrary")),
    )(*parts, w, m, v)


def _adamw_small(name, items):
    n = len(items)

    def body(*refs):
        ins, outs = refs[:4 * n], refs[4 * n:]
        for i in range(n):
            p_ref, w_ref, m_ref, v_ref = ins[4 * i:4 * i + 4]
            res = _adamw_math([p_ref[s] for s in range(N_DEV)], w_ref[...], m_ref[...], v_ref[...])
            for ref, val in zip(outs[4 * i:4 * i + 4], res):
                ref[...] = val

    flat = [a for it in items for a in it]
    out_shape = [jax.ShapeDtypeStruct(it[1].shape, F32) for it in items for _ in range(4)]
    res = pl.pallas_call(
        body, name=name, out_shape=out_shape,
        in_specs=[pl.BlockSpec(memory_space=pltpu.VMEM)] * len(flat),
        out_specs=[pl.BlockSpec(memory_space=pltpu.VMEM)] * len(out_shape),
        compiler_params=pltpu.CompilerParams(vmem_limit_bytes=VMEM_LIMIT),
    )(*flat)
    return [res[4 * i:4 * i + 4] for i in range(n)]


SHARDED = [("ab_w_in", 2), ("ab_w_out", 1), ("cd_w_in", 2), ("cd_w_out", 1), ("ffn_w_up", 2), ("ffn_w_down", 1),
           ("ssd_conv_w", 2), ("rg_conv_w", 2), ("rg_conv_b", 1), ("rg_ba", 1), ("rg_bx", 1), ("rg_lambda", 1),
           ("ffn_conv_w", 2), ("ln_g", 2), ("ln_b", 2)]
REPLICATED = ["ssd_conv_b", "ssd_dt_bias", "ssd_a_log", "ssd_d", "ssd_norm_w", "hg_lower", "hg_norm_w", "swa_sinks",
              "rg_wa", "rg_wx", "ffn_conv_b"]
WEIGHTS = ["ab_w_in", "ssd_conv_w", "ssd_conv_b", "ssd_dt_bias", "ssd_a_log", "ssd_d", "ssd_norm_w", "hg_lower",
           "hg_norm_w", "ab_w_out", "cd_w_in", "swa_sinks", "rg_conv_w", "rg_conv_b", "rg_wa", "rg_ba", "rg_wx",
           "rg_bx", "rg_lambda", "cd_w_out", "ffn_w_up", "ffn_conv_w", "ffn_conv_b", "ffn_w_down", "ln_g", "ln_b"]


def _as2d(a):
    return a.reshape(-1, a.shape[-1])


def _merge_shards(g, axis):
    g = jnp.moveaxis(g, 0, axis)
    shp = g.shape
    return g.reshape(shp[:axis] + (shp[axis] * shp[axis + 1],) + shp[axis + 2:])


def _split_shards(full, axis):
    shp = full.shape
    g = full.reshape(shp[:axis] + (N_DEV, shp[axis] // N_DEV) + shp[axis + 1:])
    return jnp.moveaxis(g, axis, 0)


def _ab_pad(w):
    z, xbc, dt = w[..., 0:512], w[..., 512:1280], w[..., 1280:1288]
    hqfi, hg = w[..., 1288:2824], w[..., 2824:3336]
    return jnp.concatenate([hqfi, jnp.repeat(dt, 64, axis=-1), z, hg, xbc], axis=-1)


def _ab_unpad(d):
    lead = d.shape[:-1]
    dt = d[..., AB_DT:AB_ZG].reshape(lead + (8, 64)).sum(-1)
    z, hg, xbc = d[..., AB_ZG:AB_ZG + 512], d[..., AB_ZG + 512:AB_XBC], d[..., AB_XBC:AB_PAD]
    return jnp.concatenate([z, xbc, dt, d[..., :AB_DT], hg], axis=-1)


def _cd_pad(w):
    return jnp.concatenate([w[..., :768], jnp.zeros(w.shape[:-1] + (256,), w.dtype), w[..., 768:]], axis=-1)


def _cd_unpad(d):
    return jnp.concatenate([d[..., :768], d[..., CD_GATE:CD_PAD]], axis=-1)


def _cat_halves(d):
    return jnp.concatenate(d, axis=1)


def _block_diag(w):
    eye = jnp.eye(8, dtype=w.dtype)
    return jnp.einsum("gij,gh->gihj", w, eye).reshape(512, 512)


def _block_diag_grad(d):
    return jnp.stack([d[64 * g:64 * g + 64, 64 * g:64 * g + 64] for g in range(8)])


def _same(a):
    return a


BIG = {"ab_w_in": (1, _ab_pad, _ab_unpad), "ab_w_out": (0, _same, _same), "cd_w_in": (1, _cd_pad, _cd_unpad),
       "cd_w_out": (0, _same, _same), "ffn_w_up": (1, _same, _cat_halves), "ffn_w_down": (0, _same, _same)}


class _MeshComm:
    def __init__(self, shards):
        self.shards, self.full, self.recv, self.posted = shards, {}, {}, {}

    def post(self, carrier, req):
        self.posted.setdefault(carrier, []).append(req)

    def take(self, carrier):
        return self.posted.pop(carrier, [])

    def take_all(self):
        reqs = [r for name in list(self.posted) for r in self.posted.pop(name)]
        return reqs

    def gather_req(self, name, layer):
        axis, prep, _ = BIG[name]

        def done(got):
            self.full[name, layer] = prep(_merge_shards(got, axis))

        return ("gather", self.shards[name][layer].astype(MM_DTYPE), done)

    def weight(self, name, layer):
        return self.full[name, layer]

    def grad_req(self, name, layer, d):
        axis, _, unprep = BIG[name]

        def done(got):
            self.recv[name, layer] = got

        return ("exchange", _split_shards(unprep(d), axis).astype(MM_DTYPE), done)


class _LocalComm:
    def __init__(self, full):
        self.full_w, self.grads = full, {}

    def post(self, carrier, req):
        pass

    def take(self, carrier):
        return []

    def gather_req(self, name, layer):
        return None

    def weight(self, name, layer):
        return BIG[name][1](self.full_w[name][layer].astype(MM_DTYPE))

    def grad_req(self, name, layer, d):
        self.grads[name, layer] = BIG[name][2](d)
        return None


def _row_vec(v):
    return v.reshape(1, -1)


def _heads64(v):
    return jnp.repeat(v, 64).reshape(1, 512)


def _local_step(x, target, w, comm):
    kinds = ["ab" if layer % 2 == 0 else "cd" for layer in range(DEPTH)]
    in_name = [f"{kinds[layer]}_in{layer // 2}" for layer in range(DEPTH)]
    core_name = [("hg" if layer % 2 == 0 else "swa") + f"{layer // 2}_fwd" for layer in range(DEPTH)]
    comm.post("ssd0_fwd", comm.gather_req("ffn_w_down", 0))
    comm.post(core_name[0], comm.gather_req("ffn_w_up", 0))
    for layer in range(DEPTH - 1):
        nxt, nj = kinds[layer + 1], (layer + 1) // 2
        if kinds[layer] == "ab":
            comm.post(in_name[layer], comm.gather_req(nxt + "_w_in", nj))
            comm.post(core_name[layer], comm.gather_req(nxt + "_w_out", nj))
            comm.post(core_name[layer], comm.gather_req("ffn_w_down", layer + 1))
        else:
            comm.post(in_name[layer], comm.gather_req(nxt + "_w_out", nj))
            comm.post(core_name[layer], comm.gather_req(nxt + "_w_in", nj))
            comm.post(f"ffn_up{layer}", comm.gather_req("ffn_w_down", layer + 1))
        comm.post(f"ffn_act{layer}_fwd", comm.gather_req("ffn_w_up", layer + 1))

    t = x.shape[0]
    bt = _pick(t, (512, 256, 128))
    nb = t // bt
    bs = _pick(t, (512, 256, 128))
    bc = _pick(t, (FFN_ROWS, 512, 256, 128))

    def rowop(name, fn, params, xs, widths_out, out_dtype=F32, dx_dtypes=None):
        outs = [((t, wd), (bt, wd), lambda g, c: (c, 0), out_dtype) for wd in widths_out]
        return _Op(name, fn, (1, nb), params, xs, outs, dx_dtypes=dx_dtypes)

    def rowblk(arr, width, first=0):
        return (arr, (bt, width), lambda g, c: (c, first))

    one_row = lambda g, c: (0, 0)
    lb_op = _Op("hg_lb", _lb_fn, (1, 1), [],
                [(w["hg_lower"][0:1], (1, 512), one_row), (w["hg_lower"][1:2], (1, 512), one_row)],
                [((1, 512), (1, 512), one_row, F32)] * 2)
    lb_all = lb_op.ys

    tape = []
    grads = {}

    def add_grad(name, idx, val):
        grads.setdefault(name, {})[idx] = val

    def dw_matmul(name, a, b, wname, idx, carrier):
        d = _matmul(name, a, b, "tn", out_dtype=MM_DTYPE, comm=comm)
        comm.post(carrier, comm.grad_req(wname, idx, d))

    for layer in range(DEPTH):
        j = layer // 2
        rec = {"x_in": x}
        if layer % 2 == 0:
            h = _matmul(f"ab_in{j}", x, comm.weight("ab_w_in", j), "nn", comm=comm)
            conv_p = [_row_vec(w["ssd_conv_w"][j, k]) for k in range(4)] + [_row_vec(w["ssd_conv_b"][j])]
            conv = _Op(f"ssd_conv{j}", _make_conv_fn(4, True), (3, t // bc),
                       [(a, (1, 256), lambda g, c: (0, g)) for a in conv_p],
                       [(h, (bc, 256), lambda g, c: (c, AB_XBC // 256 + g))],
                       [((t, 768), (bc, 256), lambda g, c: (c, g), F32)], [(8, 256)], dx_dtypes=[MM_DTYPE])
            ssd = _Op(f"ssd{j}", _ssd_fn, (1, t // SSD_STEP),
                      [_whole(_heads64(w["ssd_dt_bias"][j])), _whole(_heads64(w["ssd_a_log"][j])),
                       _whole(_heads64(w["ssd_d"][j]))],
                      [(conv.ys[0], (SSD_STEP, 768), lambda g, c: (c, 0)),
                       (h, (SSD_STEP, 512), lambda g, c: (c, AB_DT // 512))],
                      [((t, 512), (SSD_STEP, 512), lambda g, c: (c, 0), F32)], [(LANES, LANES)] * 4,
                      dx_dtypes=[F32, MM_DTYPE], comm=comm)
            hg = _Op(f"hg{j}", _hg_fn, (1, t // HG_STEP), [_whole(lb_all[j])],
                     [(h, (HG_STEP, AB_DT), lambda g, c: (c, 0))],
                     [((t, 512), (HG_STEP, 512), lambda g, c: (c, 0), F32)], [(LANES, LANES)] * 4,
                     dx_dtypes=[MM_DTYPE], comm=comm)
            post = rowop(f"ab_post{j}", _ab_post_fn,
                         [_whole(_row_vec(w["ssd_norm_w"][j])), _whole(jnp.tile(_row_vec(w["hg_norm_w"][j]), (1, 4)))],
                         [rowblk(ssd.ys[0], 512), rowblk(hg.ys[0], 512), rowblk(h, 1024, AB_ZG // 1024)], [1024],
                         out_dtype=MM_DTYPE, dx_dtypes=[F32, F32, MM_DTYPE])
            rec.update(kind="ab", conv=conv, ssd=ssd, hg=hg, post=post)
        else:
            h = _matmul(f"cd_in{j}", x, comm.weight("cd_w_in", j), "nn", comm=comm)
            swa = _Op(f"swa{j}", _swa_fn, (1, t // SWA_STEP),
                      [_whole(jnp.tile(w["swa_sinks"][j].reshape(8, 1), (1, LANES)))],
                      [(h, (SWA_STEP, 1024), lambda g, c: (c, 0))],
                      [((t, 512), (SWA_STEP, 512), lambda g, c: (c, 0), F32)], [(SWA_BLOCK, LANES)] * 2,
                      dx_dtypes=[MM_DTYPE], comm=comm)
            conv_p = [_row_vec(w["rg_conv_w"][j, k]) for k in range(4)] + [_row_vec(w["rg_conv_b"][j])]
            conv = _Op(f"rg_conv{j}", _make_conv_fn(4, False), (2, t // bc),
                       [(a, (1, 256), lambda g, c: (0, g)) for a in conv_p],
                       [(h, (bc, 256), lambda g, c: (c, CD_XR // 256 + g))],
                       [((t, 512), (bc, 256), lambda g, c: (c, g), F32)], [(8, 256)], dx_dtypes=[MM_DTYPE])
            gate = rowop(f"rg_gate{j}", _rg_gate_fn,
                         [_whole(_block_diag(w["rg_wa"][j])), _whole(_row_vec(w["rg_ba"][j])),
                          _whole(_block_diag(w["rg_wx"][j])), _whole(_row_vec(w["rg_bx"][j])),
                          _whole(_row_vec(w["rg_lambda"][j]))],
                         [rowblk(conv.ys[0], 512)], [512, 512])
            scan = _Op(f"rg_scan{j}", _rg_scan_fn, (2, t // bs), [],
                       [(gate.ys[0], (bs, 256), lambda g, c: (c, g)), (gate.ys[1], (bs, 256), lambda g, c: (c, g))],
                       [((t, 512), (bs, 256), lambda g, c: (c, g), F32)], [(8, 256)])
            post = rowop(f"cd_post{j}", _cd_post_fn, [],
                         [rowblk(swa.ys[0], 512), rowblk(scan.ys[0], 512), rowblk(h, 512, CD_GATE // 512)], [1024],
                         out_dtype=MM_DTYPE, dx_dtypes=[F32, F32, MM_DTYPE])
            rec.update(kind="cd", swa=swa, conv=conv, gate=gate, scan=scan, post=post)
        kind = rec["kind"]
        ycat = post.ys[0]
        m = _matmul(f"mix_out{layer}", ycat, comm.weight(kind + "_w_out", j), "nn", comm=comm)
        ln1 = rowop(f"ln_a{layer}", _ln_res_fn,
                    [_whole(_row_vec(w["ln_g"][layer, 0])), _whole(_row_vec(w["ln_b"][layer, 0]))],
                    [rowblk(x, 1024), rowblk(m, 1024)], [1024], dx_dtypes=[F32, MM_DTYPE])
        x1 = ln1.ys[0]
        hu = _matmul(f"ffn_up{layer}", x1, comm.weight("ffn_w_up", layer), "nn", comm=comm)
        n_fb = FFN_DIM // FFN_BLK
        taps = [_row_vec(w["ffn_conv_w"][layer, k]) for k in range(3)] + [_row_vec(w["ffn_conv_b"][layer])]
        ba = _pick(t, (FFN_ROWS, 512, 256, 128))
        act = _Op(f"ffn_act{layer}", _ffn_act_fn, (n_fb, t // ba),
                  [(a, (1, FFN_BLK), lambda g, c: (0, g)) for a in taps]
                  + [(a, (1, FFN_BLK), lambda g, c: (0, n_fb + g)) for a in taps],
                  [(hu, (ba, FFN_BLK), lambda g, c: (c, g)), (hu, (ba, FFN_BLK), lambda g, c: (c, n_fb + g))],
                  [((t, FFN_DIM), (ba, FFN_BLK), lambda g, c: (c, g), MM_DTYPE)], [(8, FFN_BLK)] * 2,
                  dx_dtypes=[MM_DTYPE, MM_DTYPE], comm=comm, bwd_fn=_ffn_act_bwd)
        a = act.ys[0]
        f = _matmul(f"ffn_down{layer}", a, comm.weight("ffn_w_down", layer), "nn", comm=comm)
        ln2 = rowop(f"ln_f{layer}", _ln_res_fn,
                    [_whole(_row_vec(w["ln_g"][layer, 1])), _whole(_row_vec(w["ln_b"][layer, 1]))],
                    [rowblk(x1, 1024), rowblk(f, 1024)], [1024], dx_dtypes=[F32, MM_DTYPE])
        rec.update(ycat=ycat, ln1=ln1, x1=x1, act=act, a=a, ln2=ln2)
        tape.append(rec)
        x = ln2.ys[0]

    dx, loss = _loss_kernel(x, target)

    d_lb = [jnp.zeros((1, 512), F32), jnp.zeros((1, 512), F32)]
    for layer in reversed(range(DEPTH)):
        j = layer // 2
        rec = tape[layer]
        (dg, db), (dx1_res, df) = rec["ln2"].bwd([dx])
        add_grad("ln_g", (layer, 1), dg[0]); add_grad("ln_b", (layer, 1), db[0])
        dw_matmul(f"ffn_down_dw{layer}", rec["a"], df, "ffn_w_down", layer, f"ffn_act{layer}_bwd")
        da = _matmul(f"ffn_down_dx{layer}", df, comm.weight("ffn_w_down", layer), "nt", comm=comm)
        dpa, (dhg, dhu) = rec["act"].bwd([da])
        halves = [jnp.concatenate([dpa[k][0, :FFN_DIM], dpa[4 + k][0, FFN_DIM:]]) for k in range(4)]
        add_grad("ffn_conv_w", layer, jnp.stack(halves[:3]))
        add_grad("ffn_conv_b", layer, halves[3])
        core_bwd = ("hg" if rec["kind"] == "ab" else "swa") + f"{j}_bwd"
        gate_cols, up_cols = (0, FFN_DIM), (FFN_DIM, FFN_DIM)
        dw_up = (_matmul(f"ffn_up_dw_g{layer}", rec["x1"], dhg, "tn", out_dtype=MM_DTYPE, comm=comm, b_cols=gate_cols),
                 _matmul(f"ffn_up_dw_u{layer}", rec["x1"], dhu, "tn", out_dtype=MM_DTYPE, comm=comm, b_cols=up_cols))
        comm.post(core_bwd, comm.grad_req("ffn_w_up", layer, dw_up))
        w_up = comm.weight("ffn_w_up", layer)
        dx1 = _matmul(f"ffn_up_dx_g{layer}", dhg, w_up, "nt", a_cols=gate_cols, b_cols=gate_cols, add=dx1_res,
                      comm=comm)
        dx1 = _matmul(f"ffn_up_dx_u{layer}", dhu, w_up, "nt", a_cols=up_cols, b_cols=up_cols, add=dx1, comm=comm)
        (dg, db), (dx_res, dm) = rec["ln1"].bwd([dx1])
        add_grad("ln_g", (layer, 0), dg[0]); add_grad("ln_b", (layer, 0), db[0])
        kind = rec["kind"]
        dw_matmul(f"mix_out_dw{layer}", rec["ycat"], dm, kind + "_w_out", j, f"{kind}_in_dw{j}")
        dycat = _matmul(f"mix_out_dx{layer}", dm, comm.weight(kind + "_w_out", j), "nt", comm=comm)
        if kind == "ab":
            (dnw_s, dnw_h), (dy_ssd, do_hg, dh) = rec["post"].bwd([dycat])
            add_grad("ssd_norm_w", j, dnw_s[0]); add_grad("hg_norm_w", j, dnw_h[0].reshape(4, LANES).sum(0))
            (dlb,), (dh,) = rec["hg"].bwd([do_hg], dx_into={0: dh})
            d_lb[j] = dlb
            (ddtb, dalog, ddsk), (dxbc_c, dh) = rec["ssd"].bwd([dy_ssd], dx_into={1: dh})
            add_grad("ssd_dt_bias", j, ddtb[0].reshape(8, 64).sum(-1))
            add_grad("ssd_a_log", j, dalog[0].reshape(8, 64).sum(-1))
            add_grad("ssd_d", j, ddsk[0].reshape(8, 64).sum(-1))
            dcp, (dh,) = rec["conv"].bwd([dxbc_c], dx_into={0: dh})
            add_grad("ssd_conv_w", j, jnp.stack([dcp[k][0] for k in range(4)]))
            add_grad("ssd_conv_b", j, dcp[4][0])
        else:
            _, (dyc, dhs, dh) = rec["post"].bwd([dycat])
            _, (da_s, du_s) = rec["scan"].bwd([dhs])
            (dwa, dba, dwx, dbx, dlam), (dxc,) = rec["gate"].bwd([da_s, du_s])
            add_grad("rg_wa", j, _block_diag_grad(dwa)); add_grad("rg_wx", j, _block_diag_grad(dwx))
            add_grad("rg_ba", j, dba[0]); add_grad("rg_bx", j, dbx[0]); add_grad("rg_lambda", j, dlam[0])
            dcp, (dh,) = rec["conv"].bwd([dxc], dx_into={0: dh})
            add_grad("rg_conv_w", j, jnp.stack([dcp[k][0] for k in range(4)]))
            add_grad("rg_conv_b", j, dcp[4][0])
            (dsink,), (dh,) = rec["swa"].bwd([dyc], dx_into={0: dh})
            add_grad("swa_sinks", j, dsink.sum(-1))
        dw_matmul(f"{kind}_in_dw{j}", rec["x_in"], dh, kind + "_w_in", j,
                  f"ffn_act{layer - 1}_bwd" if layer > 0 else f"{kind}_in_dx{j}")
        dx = _matmul(f"{kind}_in_dx{j}", dh, comm.weight(kind + "_w_in", j), "nt", add=dx_res, comm=comm)

    _, (dl0, dl1) = lb_op.bwd(d_lb)
    out = {"hg_lower": jnp.concatenate([dl0, dl1], axis=0)}
    for name, parts in grads.items():
        keys = sorted(parts)
        if isinstance(keys[0], tuple):
            out[name] = jnp.stack([jnp.stack([parts[(l, s)] for s in range(2)]) for l in range(DEPTH)])
        else:
            out[name] = jnp.stack([parts[k] for k in keys])
    return loss, dx, out


def _local_step_full(x, target, full):
    comm = _LocalComm(full)
    loss, dx, grads = _local_step(x, target, {n: a for n, a in full.items() if n not in BIG}, comm)
    for name in BIG:
        grads[name] = jnp.stack([comm.grads[name, l] for l in range(full[name].shape[0])])
    return loss, dx, grads


def kernel(x, ab_w_in, ssd_conv_w, ssd_conv_b, ssd_dt_bias, ssd_a_log, ssd_d, ssd_norm_w, hg_lower, hg_norm_w, ab_w_out, cd_w_in, swa_sinks, rg_conv_w, rg_conv_b, rg_wa, rg_ba, rg_wx, rg_bx, rg_lambda, cd_w_out, ffn_w_up, ffn_conv_w, ffn_conv_b, ffn_w_down, ln_g, ln_b, loss_target, m_ab_w_in, m_ssd_conv_w, m_ssd_conv_b, m_ssd_dt_bias, m_ssd_a_log, m_ssd_d, m_ssd_norm_w, m_hg_lower, m_hg_norm_w, m_ab_w_out, m_cd_w_in, m_swa_sinks, m_rg_conv_w, m_rg_conv_b, m_rg_wa, m_rg_ba, m_rg_wx, m_rg_bx, m_rg_lambda, m_cd_w_out, m_ffn_w_up, m_ffn_conv_w, m_ffn_conv_b, m_ffn_w_down, m_ln_g, m_ln_b, v_ab_w_in, v_ssd_conv_w, v_ssd_conv_b, v_ssd_dt_bias, v_ssd_a_log, v_ssd_d, v_ssd_norm_w, v_hg_lower, v_hg_norm_w, v_ab_w_out, v_cd_w_in, v_swa_sinks, v_rg_conv_w, v_rg_conv_b, v_rg_wa, v_rg_ba, v_rg_wx, v_rg_bx, v_rg_lambda, v_cd_w_out, v_ffn_w_up, v_ffn_conv_w, v_ffn_conv_b, v_ffn_w_down, v_ln_g, v_ln_b):
    args = dict(locals())
    wts = {n: args[n] for n in WEIGHTS}
    mom = {n: args["m_" + n] for n in WEIGHTS}
    var = {n: args["v_" + n] for n in WEIGHTS}
    axis = dict(SHARDED)
    small = [n for n, _ in SHARDED if n not in BIG]
    comm = _MeshComm(wts)

    def run(name, reqs):
        for (_, _, done), got in zip(reqs, _remote_copies(name, [(k, a) for k, a, _ in reqs])):
            done(got)

    full = {n: wts[n] for n in REPLICATED}

    def keep_small(n):
        def done(got):
            full[n] = _merge_shards(got.reshape((N_DEV,) + wts[n].shape), axis[n])
        return ("gather", _as2d(wts[n]), done)

    run("gather_first", [comm.gather_req("ab_w_in", 0), keep_small("ssd_conv_w")])
    for req in [comm.gather_req("ab_w_out", 0)] + [keep_small(n) for n in small if n != "ssd_conv_w"]:
        comm.post("ab_in0", req)

    loss, grad_x, grads = _local_step(x[0], loss_target[0], full, comm)
    loss = lax.psum(loss, ("x", "y", "c"))

    parts = {}

    def keep_parts(n, kind, arr):
        return (kind, arr, lambda got: parts.__setitem__(n, got))

    last = comm.take_all()
    last += [keep_parts(n, "exchange", _split_shards(grads[n], axis[n]).reshape((N_DEV,) + _as2d(wts[n]).shape))
             for n in small]
    last += [keep_parts(n, "gather", _as2d(grads[n])) for n in REPLICATED]
    run("exchange_last", last)

    new = {}
    for n in BIG:
        new[n] = _adamw_big("adamw_" + n, [comm.recv[n, l] for l in range(wts[n].shape[0])], wts[n], mom[n], var[n])
    names = small + REPLICATED
    res = _adamw_small("adamw_small", [(parts[n], _as2d(wts[n]), _as2d(mom[n]), _as2d(var[n])) for n in names])
    for n, r in zip(names, res):
        new[n] = [a.reshape(wts[n].shape) for a in r]

    outs = [loss, grad_x[None]]
    for kind in range(4):
        outs += [new[n][kind] for n in WEIGHTS]
    return tuple(outs)
```
